```python
import jax
import jax.numpy as jnp
from jax import lax
import numpy as np

D_MODEL = 1024
BATCH = 8
SEQ = 2048
DEPTH = 2

GRID_W = 64
CTX_LEN = 256
QBLK = 128
ROPE_THETA = 10000.0
EPS = 1e-6

POOL_GROUPS = 4
POOL_CH = 64
POOL_DIM = POOL_GROUPS * POOL_CH
POOL_WINDOWS = (2, 4, 8, 16)

HEAD_DIM = 64
GQA_HEADS = 12
GQA_KV_HEADS = 4
GQA_GROUP = GQA_HEADS // GQA_KV_HEADS
GQA_Q_DIM = GQA_HEADS * HEAD_DIM
GQA_KV_DIM = GQA_KV_HEADS * HEAD_DIM
L0_IN = POOL_DIM + GQA_Q_DIM + 2 * GQA_KV_DIM
L0_MIX = POOL_DIM + GQA_Q_DIM

MLA_HEADS = 8
MLA_NOPE = 64
MLA_ROPE = 32
MLA_QK = MLA_NOPE + MLA_ROPE
MLA_V = 64
MLA_Q_LORA = 384
MLA_KV_LORA = 256

NA_HEADS = 8
NA_HEAD_DIM = 64
NA_DIM = NA_HEADS * NA_HEAD_DIM
NA_ROWS = 8
NA_COLS = 16

L1_IN = MLA_Q_LORA + MLA_KV_LORA + MLA_ROPE + 3 * NA_DIM
L1_MIX = MLA_HEADS * MLA_V + NA_DIM

FFN_DIM = -(-8 * D_MODEL // (3 * 256)) * 256

kernel_name = "hybrid_pool_gqa_mla_na_dit_trunk"


def rmsnorm(x, g):
    xf = x.astype(jnp.float32)
    y = xf * lax.rsqrt(jnp.mean(xf * xf, axis=-1, keepdims=True) + EPS)
    return y.astype(x.dtype) * g


def rope_1d(x, pos):
    d = x.shape[-1]
    half = d // 2
    inv = (ROPE_THETA ** (-np.arange(half, dtype=np.float32) * 2.0 / d)).astype(np.float32)
    ang = pos.astype(jnp.float32)[:, None] * inv[None, :]
    shape = (1, x.shape[1]) + (1,) * (x.ndim - 3) + (half,)
    cos = jnp.cos(ang).reshape(shape)
    sin = jnp.sin(ang).reshape(shape)
    xf = x.astype(jnp.float32)
    x1, x2 = xf[..., :half], xf[..., half:]
    return jnp.concatenate([x1 * cos - x2 * sin, x1 * sin + x2 * cos], axis=-1).astype(x.dtype)


def axial_rope(x, row, col):
    half = x.shape[-1] // 2
    return jnp.concatenate([rope_1d(x[..., :half], row), rope_1d(x[..., half:], col)], axis=-1)


def split_cols(p, sizes):
    return jnp.split(p, [int(o) for o in np.cumsum(sizes)[:-1]], axis=-1)


def block_attention(q, k, v, scale):
    B, T = q.shape[:2]
    qb = jnp.moveaxis(q.reshape((B, T // QBLK, QBLK) + q.shape[2:]), 1, 0)

    def one_block(q_blk):
        s = jnp.einsum("bqhgd,bkhd->bhgqk", q_blk, k).astype(jnp.float32) * scale
        p = jax.nn.softmax(s, axis=-1).astype(v.dtype)
        return jnp.einsum("bhgqk,bkhd->bqhgd", p, v)

    o = lax.map(one_block, qb)
    return jnp.moveaxis(o, 0, 1).reshape((B, T) + o.shape[3:])


def multiscale_pool(u, w_grp, ch_scale):
    B, T, _ = u.shape
    ug = u.reshape(B, T, POOL_GROUPS, POOL_CH).astype(jnp.float32)
    csum = jnp.concatenate([jnp.zeros((B, 1, POOL_GROUPS, POOL_CH), jnp.float32),
                            jnp.cumsum(ug, axis=1)], axis=1)
    t = np.arange(T)
    lo = np.stack([np.clip(t - w // 2, 0, T) for w in POOL_WINDOWS], axis=-1).astype(np.int32)
    hi = np.stack([np.clip(t - w // 2 + w, 0, T) for w in POOL_WINDOWS], axis=-1).astype(np.int32)
    grp = np.arange(POOL_GROUPS, dtype=np.int32)[None, :]
    win_sum = csum[:, hi, grp] - csum[:, lo, grp]
    count = jnp.asarray((hi - lo).astype(np.float32))[None, :, :, None]
    d = (win_sum / count - ug).astype(u.dtype)
    y = jnp.einsum("btgc,gcd->btgd", d, w_grp).reshape(B, T, POOL_DIM)
    return y * ch_scale


def neighbourhood_attention(q, k, v, k_ctx, v_ctx, rpb, scale):
    B, S, H, dh = q.shape
    rows = S // GRID_W
    kr = min(NA_ROWS, rows)
    kc = NA_COLS
    cols = np.arange(GRID_W)
    key_col = (np.clip(cols - kc // 2, 0, GRID_W - kc)[:, None] + np.arange(kc)[None, :]).astype(np.int32)
    dc_idx = (key_col - cols[:, None] + (NA_COLS - 1)).astype(np.int32)
    n_win = kr * kc
    q_rows = jnp.moveaxis(q.reshape(B, rows, GRID_W, H, dh), 1, 0)

    def one_row(args):
        r, q_r = args
        r0 = jnp.clip(r - kr // 2, 0, rows - kr)
        key_row = r0 + jnp.arange(kr, dtype=jnp.int32)
        idx = (key_row[None, :, None] * GRID_W + key_col[:, None, :]).reshape(GRID_W, n_win)
        k_w = jnp.take(k, idx, axis=1)
        v_w = jnp.take(v, idx, axis=1)
        bias = rpb[:, key_row - r + (NA_ROWS - 1)][:, :, dc_idx]
        bias = jnp.transpose(bias, (0, 2, 1, 3)).reshape(H, GRID_W, n_win)
        s_win = jnp.einsum("bqhd,bqkhd->bhqk", q_r, k_w).astype(jnp.float32) * scale + bias.astype(jnp.float32)
        s_ctx = jnp.einsum("bqhd,bchd->bhqc", q_r, k_ctx).astype(jnp.float32) * scale
        p = jax.nn.softmax(jnp.concatenate([s_win, s_ctx], axis=-1), axis=-1).astype(v.dtype)
        return (jnp.einsum("bhqk,bqkhd->bqhd", p[..., :n_win], v_w)
                + jnp.einsum("bhqc,bchd->bqhd", p[..., n_win:], v_ctx))

    o = lax.map(one_row, (jnp.arange(rows, dtype=jnp.int32), q_rows))
    return jnp.moveaxis(o, 0, 1).reshape(B, S, H, dh)


def adaln(cond, w, b):
    d = w.shape[0]
    m = jax.nn.silu(cond) @ w + b
    return [m[:, None, i * d:(i + 1) * d] for i in range(6)]


def modulate(x, g, shift, scale):
    return rmsnorm(x, g) * (1.0 + scale) + shift


def swiglu(h, wg, wu, wd):
    return (jax.nn.silu(h @ wg) * (h @ wu)) @ wd


def mixer_pool_gqa(h, hc, row, col, need_ctx, w_in, pool_w, pool_scale, q_gain, k_gain, w_out):
    B, S, _ = h.shape
    Tc = hc.shape[1]
    sizes = (POOL_DIM, GQA_Q_DIM, GQA_KV_DIM, GQA_KV_DIM)
    a, q, k, v = split_cols(h @ w_in, sizes)
    ac, qc, kc, vc = split_cols(hc @ w_in, sizes)
    q = axial_rope(rmsnorm(q.reshape(B, S, GQA_KV_HEADS, GQA_GROUP, HEAD_DIM), q_gain), row, col)
    k = axial_rope(rmsnorm(k.reshape(B, S, GQA_KV_HEADS, HEAD_DIM), k_gain), row, col)
    v = v.reshape(B, S, GQA_KV_HEADS, HEAD_DIM)
    kc = rmsnorm(kc.reshape(B, Tc, GQA_KV_HEADS, HEAD_DIM), k_gain)
    vc = vc.reshape(B, Tc, GQA_KV_HEADS, HEAD_DIM)
    scale = HEAD_DIM ** -0.5
    attn = block_attention(q, jnp.concatenate([kc, k], axis=1), jnp.concatenate([vc, v], axis=1),
                           scale).reshape(B, S, GQA_Q_DIM)
    out = jnp.concatenate([multiscale_pool(a, pool_w, pool_scale), attn], axis=-1) @ w_out
    if not need_ctx:
        return out, None
    qc = rmsnorm(qc.reshape(B, Tc, GQA_KV_HEADS, GQA_GROUP, HEAD_DIM), q_gain)
    attn_c = block_attention(qc, kc, vc, scale).reshape(B, Tc, GQA_Q_DIM)
    out_c = jnp.concatenate([multiscale_pool(ac, pool_w, pool_scale), attn_c], axis=-1) @ w_out
    return out, out_c


def mixer_mla_na(h, hc, row, col, need_ctx, w_in, q_a_gain, kv_a_gain, w_uq, w_ukv,
                 mla_q_gain, mla_k_gain, na_q_gain, na_k_gain, na_rpb, w_out):
    B, S, _ = h.shape
    Tc = hc.shape[1]
    sizes = (MLA_Q_LORA, MLA_KV_LORA, MLA_ROPE, NA_DIM, NA_DIM, NA_DIM)
    cq, ckv, kr, nq, nk, nv = split_cols(h @ w_in, sizes)
    cqc, ckvc, krc, nqc, nkc, nvc = split_cols(hc @ w_in, sizes)

    def mla_q(cq_, T):
        q_ = (rmsnorm(cq_, q_a_gain) @ w_uq).reshape(B, T, MLA_HEADS, MLA_QK)
        return rmsnorm(q_, mla_q_gain)

    def mla_kv(ckv_, kr_, T):
        kv = (rmsnorm(ckv_, kv_a_gain) @ w_ukv).reshape(B, T, MLA_HEADS, MLA_NOPE + MLA_V)
        k_rope = jnp.broadcast_to(kr_[:, :, None, :], (B, T, MLA_HEADS, MLA_ROPE))
        k_ = jnp.concatenate([kv[..., :MLA_NOPE], k_rope], axis=-1)
        return rmsnorm(k_, mla_k_gain), kv[..., MLA_NOPE:]

    def rope_tail(t):
        return jnp.concatenate([t[..., :MLA_NOPE], axial_rope(t[..., MLA_NOPE:], row, col)], axis=-1)

    mq = rope_tail(mla_q(cq, S))
    mk, mv = mla_kv(ckv, kr, S)
    mk = rope_tail(mk)
    mkc, mvc = mla_kv(ckvc, krc, Tc)
    mla_scale = MLA_QK ** -0.5
    o_mla = block_attention(mq[:, :, :, None], jnp.concatenate([mkc, mk], axis=1),
                            jnp.concatenate([mvc, mv], axis=1), mla_scale).reshape(B, S, MLA_HEADS * MLA_V)

    q_na = rmsnorm(nq.reshape(B, S, NA_HEADS, NA_HEAD_DIM), na_q_gain)
    k_na = rmsnorm(nk.reshape(B, S, NA_HEADS, NA_HEAD_DIM), na_k_gain)
    v_na = nv.reshape(B, S, NA_HEADS, NA_HEAD_DIM)
    k_nac = rmsnorm(nkc.reshape(B, Tc, NA_HEADS, NA_HEAD_DIM), na_k_gain)
    v_nac = nvc.reshape(B, Tc, NA_HEADS, NA_HEAD_DIM)
    na_scale = NA_HEAD_DIM ** -0.5
    o_na = neighbourhood_attention(q_na, k_na, v_na, k_nac, v_nac, na_rpb, na_scale).reshape(B, S, NA_DIM)

    out = jnp.concatenate([o_mla, o_na], axis=-1) @ w_out
    if not need_ctx:
        return out, None
    mqc = mla_q(cqc, Tc)
    o_mla_c = block_attention(mqc[:, :, :, None], mkc, mvc, mla_scale).reshape(B, Tc, MLA_HEADS * MLA_V)
    q_nac = rmsnorm(nqc.reshape(B, Tc, NA_HEADS, NA_HEAD_DIM), na_q_gain)
    o_na_c = block_attention(q_nac[:, :, :, None], k_nac, v_nac, na_scale).reshape(B, Tc, NA_DIM)
    out_c = jnp.concatenate([o_mla_c, o_na_c], axis=-1) @ w_out
    return out, out_c


def setup_inputs(seed: int = 0) -> dict:
    key = jax.random.key(seed)
    ks = iter(jax.random.split(key, 48))

    def nrm(shape, s):
        return jax.random.normal(next(ks), shape, jnp.float32) * s

    def gain(n):
        return 1.0 + nrm((n,), 0.1)

    D = D_MODEL
    inp = {}
    inp["x"] = nrm((BATCH, SEQ, D), 1.0)
    inp["c"] = nrm((BATCH, D), 1.0)
    inp["ctx"] = nrm((BATCH, CTX_LEN, D), 1.0)
    inp["c_ctx"] = nrm((D,), 1.0)
    inp["l0_ada_w"] = nrm((D, 6 * D), 0.5 * D ** -0.5)
    inp["l0_ada_b"] = nrm((6 * D,), 0.02)
    inp["l0_norm_mix"] = gain(D)
    inp["l0_norm_ffn"] = gain(D)
    inp["l0_w_in"] = nrm((D, L0_IN), D ** -0.5)
    inp["l0_pool_w"] = nrm((POOL_GROUPS, POOL_CH, POOL_CH), POOL_CH ** -0.5)
    inp["l0_pool_scale"] = gain(POOL_DIM)
    inp["l0_q_gain"] = gain(HEAD_DIM)
    inp["l0_k_gain"] = gain(HEAD_DIM)
    inp["l0_w_out"] = nrm((L0_MIX, D), L0_MIX ** -0.5)
    inp["l0_ffn_w_gate"] = nrm((D, FFN_DIM), D ** -0.5)
    inp["l0_ffn_w_up"] = nrm((D, FFN_DIM), D ** -0.5)
    inp["l0_ffn_w_down"] = nrm((FFN_DIM, D), FFN_DIM ** -0.5)
    inp["l1_ada_w"] = nrm((D, 6 * D), 0.5 * D ** -0.5)
    inp["l1_ada_b"] = nrm((6 * D,), 0.02)
    inp["l1_norm_mix"] = gain(D)
    inp["l1_norm_ffn"] = gain(D)
    inp["l1_w_in"] = nrm((D, L1_IN), D ** -0.5)
    inp["l1_mla_q_a_gain"] = gain(MLA_Q_LORA)
    inp["l1_mla_kv_a_gain"] = gain(MLA_KV_LORA)
    inp["l1_mla_w_uq"] = nrm((MLA_Q_LORA, MLA_HEADS * MLA_QK), MLA_Q_LORA ** -0.5)
    inp["l1_mla_w_ukv"] = nrm((MLA_KV_LORA, MLA_HEADS * (MLA_NOPE + MLA_V)), MLA_KV_LORA ** -0.5)
    inp["l1_mla_q_gain"] = gain(MLA_QK)
    inp["l1_mla_k_gain"] = gain(MLA_QK)
    inp["l1_na_q_gain"] = gain(NA_HEAD_DIM)
    inp["l1_na_k_gain"] = gain(NA_HEAD_DIM)
    inp["l1_na_rpb"] = nrm((NA_HEADS, 2 * NA_ROWS - 1, 2 * NA_COLS - 1), 0.1)
    inp["l1_w_out"] = nrm((L1_MIX, D), L1_MIX ** -0.5)
    inp["l1_ffn_w_gate"] = nrm((D, FFN_DIM), D ** -0.5)
    inp["l1_ffn_w_up"] = nrm((D, FFN_DIM), D ** -0.5)
    inp["l1_ffn_w_down"] = nrm((FFN_DIM, D), FFN_DIM ** -0.5)
    return inp


def reference(x, c, ctx, c_ctx,
              l0_ada_w, l0_ada_b, l0_norm_mix, l0_norm_ffn, l0_w_in, l0_pool_w, l0_pool_scale,
              l0_q_gain, l0_k_gain, l0_w_out, l0_ffn_w_gate, l0_ffn_w_up, l0_ffn_w_down,
              l1_ada_w, l1_ada_b, l1_norm_mix, l1_norm_ffn, l1_w_in, l1_mla_q_a_gain, l1_mla_kv_a_gain,
              l1_mla_w_uq, l1_mla_w_ukv, l1_mla_q_gain, l1_mla_k_gain, l1_na_q_gain, l1_na_k_gain,
              l1_na_rpb, l1_w_out, l1_ffn_w_gate, l1_ffn_w_up, l1_ffn_w_down):
    S = x.shape[1]
    tok = jnp.arange(S, dtype=jnp.int32)
    row = tok // GRID_W
    col = tok % GRID_W
    layers = [
        (l0_ada_w, l0_ada_b, l0_norm_mix, l0_norm_ffn, l0_ffn_w_gate, l0_ffn_w_up, l0_ffn_w_down,
         mixer_pool_gqa, (l0_w_in, l0_pool_w, l0_pool_scale, l0_q_gain, l0_k_gain, l0_w_out)),
        (l1_ada_w, l1_ada_b, l1_norm_mix, l1_norm_ffn, l1_ffn_w_gate, l1_ffn_w_up, l1_ffn_w_down,
         mixer_mla_na, (l1_w_in, l1_mla_q_a_gain, l1_mla_kv_a_gain, l1_mla_w_uq, l1_mla_w_ukv,
                        l1_mla_q_gain, l1_mla_k_gain, l1_na_q_gain, l1_na_k_gain, l1_na_rpb, l1_w_out)),
    ]
    xc = ctx
    for i in range(DEPTH):
        ada_w, ada_b, n_mix, n_ffn, wg, wu, wd, mixer, mparams = layers[i]
        need_ctx = i < DEPTH - 1
        sh1, sc1, g1, sh2, sc2, g2 = adaln(c, ada_w, ada_b)
        csh1, csc1, cg1, csh2, csc2, cg2 = adaln(c_ctx[None, :], ada_w, ada_b)
        h = modulate(x, n_mix, sh1, sc1)
        hc = modulate(xc, n_mix, csh1, csc1)
        o, oc = mixer(h, hc, row, col, need_ctx, *mparams)
        x = x + g1 * o
        x = x + g2 * swiglu(modulate(x, n_ffn, sh2, sc2), wg, wu, wd)
        if need_ctx:
            xc = xc + cg1 * oc
            xc = xc + cg2 * swiglu(modulate(xc, n_ffn, csh2, csc2), wg, wu, wd)
    return x
```

```python
import functools

import numpy as np
import jax
import jax.numpy as jnp
from jax import lax
from jax.experimental import pallas as pl
from jax.experimental.pallas import tpu as pltpu

F32 = jnp.float32
BF16 = jnp.bfloat16

D_MODEL = 1024
GRID_W = 64
ROPE_THETA = 10000.0
EPS = 1e-6
LANES = 128

POOL_GROUPS = 4
POOL_CH = 64
POOL_DIM = POOL_GROUPS * POOL_CH
POOL_HALO = 16

HEAD_DIM = 64
GQA_HEADS = 12
GQA_KV_HEADS = 4
GQA_GROUP = GQA_HEADS // GQA_KV_HEADS
GQA_Q_DIM = GQA_HEADS * HEAD_DIM
GQA_KV_DIM = GQA_KV_HEADS * HEAD_DIM

MLA_HEADS = 8
MLA_NOPE = 64
MLA_ROPE = 32
MLA_QK = MLA_NOPE + MLA_ROPE
MLA_V = 64
MLA_Q_LORA = 384
MLA_KV_LORA = 256
MLA_PAIR_W = 2 * LANES

NA_HEADS = 8
NA_HEAD_DIM = 64
NA_DIM = NA_HEADS * NA_HEAD_DIM
NA_ROWS = 8
NA_COLS = 16

FFN_DIM = -(-8 * D_MODEL // (3 * 256)) * 256
FFN_CHUNK = 256

NEG_BIG = -1e30

VMEM_LIMIT = 56 * 1024 * 1024

_NT = (((1,), (1,)), ((), ()))


def _params(n_axes):
    return pltpu.CompilerParams(dimension_semantics=("arbitrary",) * n_axes,
                                vmem_limit_bytes=VMEM_LIMIT)


def _modulate(x, gain, shift, scale):
    ms = jnp.mean(x * x, axis=-1, keepdims=True)
    return x * lax.rsqrt(ms + EPS) * gain * (1.0 + scale) + shift


def _row_rmsnorm(x, gain):
    ms = jnp.mean(x * x, axis=-1, keepdims=True)
    return x * lax.rsqrt(ms + EPS) * gain


def _lane(shape):
    return lax.broadcasted_iota(jnp.int32, shape, 1)


def _pair_rmsnorm(blk, gain2):
    first = _lane(blk.shape) < HEAD_DIM
    y = blk * blk
    sa = jnp.sum(jnp.where(first, y, 0.0), axis=-1, keepdims=True)
    sb = jnp.sum(jnp.where(first, 0.0, y), axis=-1, keepdims=True)
    inv = jnp.where(first, lax.rsqrt(sa * (1.0 / HEAD_DIM) + EPS),
                    lax.rsqrt(sb * (1.0 / HEAD_DIM) + EPS))
    return blk * inv * gain2


def _mla_pair_rmsnorm(nblk, rblk, gain_n, gain_r):
    lane = _lane(nblk.shape)
    na = lane < MLA_NOPE
    ra = lane < MLA_ROPE
    rb = jnp.logical_and(lane >= MLA_ROPE, lane < 2 * MLA_ROPE)
    yn = nblk * nblk
    yr = rblk * rblk
    sa = (jnp.sum(jnp.where(na, yn, 0.0), axis=-1, keepdims=True)
          + jnp.sum(jnp.where(ra, yr, 0.0), axis=-1, keepdims=True))
    sb = (jnp.sum(jnp.where(na, 0.0, yn), axis=-1, keepdims=True)
          + jnp.sum(jnp.where(rb, yr, 0.0), axis=-1, keepdims=True))
    inva = lax.rsqrt(sa * (1.0 / MLA_QK) + EPS)
    invb = lax.rsqrt(sb * (1.0 / MLA_QK) + EPS)
    return (nblk * jnp.where(na, inva, invb) * gain_n,
            rblk * jnp.where(ra, inva, invb) * gain_r)


def _rope(blk, cos, sin_signed, half):
    first = (_lane(blk.shape) & (2 * half - 1)) < half
    rot = jnp.where(first, pltpu.roll(blk, LANES - half, 1), pltpu.roll(blk, half, 1))
    return blk * cos + rot * sin_signed


def _ada_kernel(c_ref, w_ref, b_ref, o_ref):
    cnd = c_ref[...]
    act = cnd * jax.nn.sigmoid(cnd)
    o_ref[...] = jnp.dot(act, w_ref[...], preferred_element_type=F32,
                         precision=lax.Precision.HIGHEST) + b_ref[...]


def _ada(cond, w, b):
    n_rows, d = cond.shape
    n = w.shape[1]
    tn = 1024
    return pl.pallas_call(
        _ada_kernel,
        grid=(n // tn,),
        in_specs=[pl.BlockSpec((n_rows, d), lambda j: (0, 0)),
                  pl.BlockSpec((d, tn), lambda j: (0, j)),
                  pl.BlockSpec((1, tn), lambda j: (0, j))],
        out_specs=pl.BlockSpec((n_rows, tn), lambda j: (0, j)),
        out_shape=jax.ShapeDtypeStruct((n_rows, n), F32),
        compiler_params=_params(1),
        name="ada",
    )(cond, w, b.reshape(1, n))


def _inproj0_kernel(x_ref, sh_ref, sc_ref, g_ref, w_ref, qg_ref, kg_ref, cos_ref, sin_ref,
                    a_ref, q_ref, k_ref, v_ref, *, rope):
    h = _modulate(x_ref[0], g_ref[...], sh_ref[0], sc_ref[0])
    p = jnp.dot(h.astype(BF16), w_ref[...], preferred_element_type=F32)
    a_ref[0] = p[:, :POOL_DIM]
    cos = cos_ref[...]
    sin = sin_ref[...]
    q0 = POOL_DIM
    for j in range(GQA_Q_DIM // LANES):
        blk = _pair_rmsnorm(p[:, q0 + j * LANES:q0 + (j + 1) * LANES], qg_ref[...])
        if rope:
            blk = _rope(blk, cos, sin, HEAD_DIM // 4)
        q_ref[0, :, j * LANES:(j + 1) * LANES] = (blk * (HEAD_DIM ** -0.5)).astype(BF16)
    k0 = q0 + GQA_Q_DIM
    for j in range(GQA_KV_DIM // LANES):
        blk = _pair_rmsnorm(p[:, k0 + j * LANES:k0 + (j + 1) * LANES], kg_ref[...])
        if rope:
            blk = _rope(blk, cos, sin, HEAD_DIM // 4)
        k_ref[0, :, j * LANES:(j + 1) * LANES] = blk.astype(BF16)
    v0 = k0 + GQA_KV_DIM
    v_ref[0] = p[:, v0:v0 + GQA_KV_DIM].astype(BF16)


def _inproj0(x, shift, scale, gain, w, q_gain2, k_gain2, cos, sin, *, rope, tm):
    b, t, d = x.shape
    n = w.shape[1]
    row = lambda bi, i: (bi, i, 0)
    vec = lambda bi, i: (bi, 0, 0)
    const = lambda bi, i: (0, 0)
    return pl.pallas_call(
        functools.partial(_inproj0_kernel, rope=rope),
        grid=(b, t // tm),
        in_specs=[pl.BlockSpec((1, tm, d), row),
                  pl.BlockSpec((1, 1, d), vec),
                  pl.BlockSpec((1, 1, d), vec),
                  pl.BlockSpec((1, d), const),
                  pl.BlockSpec((d, n), const),
                  pl.BlockSpec((1, LANES), const),
                  pl.BlockSpec((1, LANES), const),
                  pl.BlockSpec((tm, LANES), lambda bi, i: (i, 0)),
                  pl.BlockSpec((tm, LANES), lambda bi, i: (i, 0))],
        out_specs=[pl.BlockSpec((1, tm, POOL_DIM), row),
                   pl.BlockSpec((1, tm, GQA_Q_DIM), row),
                   pl.BlockSpec((1, tm, GQA_KV_DIM), row),
                   pl.BlockSpec((1, tm, GQA_KV_DIM), row)],
        out_shape=[jax.ShapeDtypeStruct((b, t, POOL_DIM), F32),
                   jax.ShapeDtypeStruct((b, t, GQA_Q_DIM), BF16),
                   jax.ShapeDtypeStruct((b, t, GQA_KV_DIM), BF16),
                   jax.ShapeDtypeStruct((b, t, GQA_KV_DIM), BF16)],
        compiler_params=_params(2),
        name="inproj0_rope" if rope else "inproj0_ctx",
    )(x, shift, scale, gain, w, q_gain2, k_gain2, cos, sin)


def _inproj1_kernel(x_ref, sh_ref, sc_ref, g_ref, w_ref, qag_ref, kvag_ref, wuq_ref, wukv_ref,
                    mqn_ref, mqr_ref, mkn_ref, mkr_ref, nqg_ref, nkg_ref, cos_ref, sin_ref,
                    *out_refs, rope, want_q):
    if want_q:
        mq_ref, mk_ref, mv_ref, nq_ref, nk_ref, nv_ref = out_refs
    else:
        mk_ref, mv_ref, nk_ref, nv_ref = out_refs
    h = _modulate(x_ref[0], g_ref[...], sh_ref[0], sc_ref[0])
    p = jnp.dot(h.astype(BF16), w_ref[...], preferred_element_type=F32)
    cos = cos_ref[...]
    sin = sin_ref[...]
    o_ckv = MLA_Q_LORA
    o_kr = o_ckv + MLA_KV_LORA
    o_nq = o_kr + LANES
    o_nk = o_nq + NA_DIM
    o_nv = o_nk + NA_DIM
    n_pairs = MLA_HEADS // 2

    if want_q:
        cq = _row_rmsnorm(p[:, :MLA_Q_LORA], qag_ref[...])
        qq = jnp.dot(cq.astype(BF16), wuq_ref[...], preferred_element_type=F32)
        for m in range(n_pairs):
            c0 = m * MLA_PAIR_W
            nblk, rblk = _mla_pair_rmsnorm(qq[:, c0:c0 + LANES], qq[:, c0 + LANES:c0 + 2 * LANES],
                                           mqn_ref[...], mqr_ref[...])
            if rope:
                rblk = _rope(rblk, cos, sin, MLA_ROPE // 4)
            mq_ref[0, :, c0:c0 + LANES] = (nblk * (MLA_QK ** -0.5)).astype(BF16)
            mq_ref[0, :, c0 + LANES:c0 + 2 * LANES] = (rblk * (MLA_QK ** -0.5)).astype(BF16)
        for m in range(NA_DIM // LANES):
            blk = _pair_rmsnorm(p[:, o_nq + m * LANES:o_nq + (m + 1) * LANES], nqg_ref[...])
            nq_ref[0, :, m * LANES:(m + 1) * LANES] = (blk * (NA_HEAD_DIM ** -0.5)).astype(BF16)

    ckv = _row_rmsnorm(p[:, o_ckv:o_ckv + MLA_KV_LORA], kvag_ref[...])
    kv = jnp.dot(ckv.astype(BF16), wukv_ref[...], preferred_element_type=F32)
    kr_blk = p[:, o_kr:o_kr + LANES]
    for m in range(n_pairs):
        c0 = m * MLA_PAIR_W
        nblk, rblk = _mla_pair_rmsnorm(kv[:, m * LANES:(m + 1) * LANES], kr_blk,
                                       mkn_ref[...], mkr_ref[...])
        if rope:
            rblk = _rope(rblk, cos, sin, MLA_ROPE // 4)
        mk_ref[0, :, c0:c0 + LANES] = nblk.astype(BF16)
        mk_ref[0, :, c0 + LANES:c0 + 2 * LANES] = rblk.astype(BF16)
    v0 = n_pairs * LANES
    mv_ref[0] = kv[:, v0:v0 + MLA_HEADS * MLA_V].astype(BF16)
    for m in range(NA_DIM // LANES):
        blk = _pair_rmsnorm(p[:, o_nk + m * LANES:o_nk + (m + 1) * LANES], nkg_ref[...])
        nk_ref[0, :, m * LANES:(m + 1) * LANES] = blk.astype(BF16)
    nv_ref[0] = p[:, o_nv:o_nv + NA_DIM].astype(BF16)


def _inproj1(x, shift, scale, gain, w, qag, kvag, wuq, wukv, mqn, mqr, mkn, mkr, nqg, nkg,
             cos, sin, *, rope, want_q, tm):
    b, t, d = x.shape
    row = lambda bi, i: (bi, i, 0)
    vec = lambda bi, i: (bi, 0, 0)
    const = lambda bi, i: (0, 0)
    full = lambda a: pl.BlockSpec(a.shape, const)
    mla_w = (MLA_HEADS // 2) * MLA_PAIR_W
    outs = [(mla_w, "mk"), (MLA_HEADS * MLA_V, "mv"), (NA_DIM, "nk"), (NA_DIM, "nv")]
    if want_q:
        outs = [(mla_w, "mq")] + outs[:2] + [(NA_DIM, "nq")] + outs[2:]
    return pl.pallas_call(
        functools.partial(_inproj1_kernel, rope=rope, want_q=want_q),
        grid=(b, t // tm),
        in_specs=[pl.BlockSpec((1, tm, d), row),
                  pl.BlockSpec((1, 1, d), vec),
                  pl.BlockSpec((1, 1, d), vec),
                  full(gain), full(w), full(qag), full(kvag), full(wuq), full(wukv),
                  full(mqn), full(mqr), full(mkn), full(mkr), full(nqg), full(nkg),
                  pl.BlockSpec((tm, LANES), lambda bi, i: (i, 0)),
                  pl.BlockSpec((tm, LANES), lambda bi, i: (i, 0))],
        out_specs=[pl.BlockSpec((1, tm, n), row) for n, _ in outs],
        out_shape=[jax.ShapeDtypeStruct((b, t, n), BF16) for n, _ in outs],
        compiler_params=_params(2),
        name="inproj1_lat" if want_q else "inproj1_ctx",
    )(x, shift, scale, gain, w, qag, kvag, wuq, wukv, mqn, mqr, mkn, mkr, nqg, nkg, cos, sin)


def _pair_attn_kernel(*refs, use_lat):
    if use_lat:
        q_ref, ma_ref, mb_ref, kc_ref, vc_ref, kl_ref, vl_ref, o_ref = refs
        segs = ((kc_ref, vc_ref), (kl_ref, vl_ref))
    else:
        q_ref, ma_ref, mb_ref, kc_ref, vc_ref, o_ref = refs
        segs = ((kc_ref, vc_ref),)
    q = q_ref[0]
    outs = []
    for mask_ref in (ma_ref, mb_ref):
        mask = mask_ref[...]
        scores = [lax.dot_general(q, k_ref[0] * mask, _NT, preferred_element_type=F32)
                  for k_ref, _ in segs]
        mx = scores[0].max(axis=-1, keepdims=True)
        for s in scores[1:]:
            mx = jnp.maximum(mx, s.max(axis=-1, keepdims=True))
        den = None
        acc = None
        for s, (_, v_ref) in zip(scores, segs):
            e = jnp.exp(s - mx)
            part = jnp.sum(e, axis=-1, keepdims=True)
            pv = jnp.dot(e.astype(BF16), v_ref[0], preferred_element_type=F32)
            den = part if den is None else den + part
            acc = pv if acc is None else acc + pv
        outs.append(acc / den)
    first = _lane(outs[0].shape) < HEAD_DIM
    o_ref[0] = jnp.where(first, outs[0], outs[1]).astype(o_ref.dtype)


def _pair_attn(q, mask_a, mask_b, kc, vc, kl, vl, *, dk, n_pairs, q_per_pair, tq, name):
    b, t, _ = q.shape
    n_qblk = n_pairs * q_per_pair
    use_lat = kl is not None
    qmap = lambda bi, c, i: (bi, i, c)
    kvmap = lambda bi, c, i: (bi, 0, c // q_per_pair)
    const = lambda bi, c, i: (0, 0)
    in_specs = [pl.BlockSpec((1, tq, dk), qmap),
                pl.BlockSpec((1, dk), const),
                pl.BlockSpec((1, dk), const),
                pl.BlockSpec((1, kc.shape[1], dk), kvmap),
                pl.BlockSpec((1, vc.shape[1], LANES), kvmap)]
    args = [q, mask_a, mask_b, kc, vc]
    if use_lat:
        in_specs += [pl.BlockSpec((1, kl.shape[1], dk), kvmap),
                     pl.BlockSpec((1, vl.shape[1], LANES), kvmap)]
        args += [kl, vl]
    return pl.pallas_call(
        functools.partial(_pair_attn_kernel, use_lat=use_lat),
        grid=(b, n_qblk, t // tq),
        in_specs=in_specs,
        out_specs=pl.BlockSpec((1, tq, LANES), qmap),
        out_shape=jax.ShapeDtypeStruct((b, t, n_qblk * LANES), BF16),
        compiler_params=_params(3),
        name=name,
    )(*args)


def _na_kernel(q_ref, kc_ref, vc_ref, k_ref, v_ref, bias_ref, o_ref, *, rows_per_step, n_rows):
    rb = pl.program_id(2)
    kc = kc_ref[0]
    vc = vc_ref[0]
    lane = _lane((1, LANES))
    masks = [(lane < NA_HEAD_DIM).astype(BF16), (lane >= NA_HEAD_DIM).astype(BF16)]
    kcm = [kc * m for m in masks]
    n_chunks = NA_ROWS // 2

    def one_row(i, carry):
        r = rb * rows_per_step + i
        r0 = jnp.clip(r - NA_ROWS // 2, 0, n_rows - NA_ROWS)
        kstart = pl.multiple_of(r0 * GRID_W, GRID_W)
        q = q_ref[0, pl.ds(pl.multiple_of(i * GRID_W, GRID_W), GRID_W), :]
        kw = k_ref[0, pl.ds(kstart, NA_ROWS * GRID_W), :]
        vw = v_ref[0, pl.ds(kstart, NA_ROWS * GRID_W), :]
        j0 = r0 - r + (NA_ROWS - 1)
        outs = []
        for hh in range(2):
            kwm = kw * masks[hh]
            s_ctx = lax.dot_general(q, kcm[hh], _NT, preferred_element_type=F32)
            s_win = []
            for c in range(n_chunks):
                s = lax.dot_general(q, kwm[c * LANES:(c + 1) * LANES], _NT,
                                    preferred_element_type=F32)
                s_win.append(s + bias_ref[hh, j0 + 2 * c])
            mx = s_ctx.max(axis=-1, keepdims=True)
            for s in s_win:
                mx = jnp.maximum(mx, s.max(axis=-1, keepdims=True))
            e = jnp.exp(s_ctx - mx)
            den = jnp.sum(e, axis=-1, keepdims=True)
            acc = jnp.dot(e.astype(BF16), vc, preferred_element_type=F32)
            for c, s in enumerate(s_win):
                e = jnp.exp(s - mx)
                den = den + jnp.sum(e, axis=-1, keepdims=True)
                acc = acc + jnp.dot(e.astype(BF16), vw[c * LANES:(c + 1) * LANES],
                                    preferred_element_type=F32)
            outs.append(acc / den)
        first = _lane(outs[0].shape) < NA_HEAD_DIM
        o_ref[0, pl.ds(pl.multiple_of(i * GRID_W, GRID_W), GRID_W), :] = (
            jnp.where(first, outs[0], outs[1]).astype(o_ref.dtype))
        return carry

    lax.fori_loop(0, rows_per_step, one_row, 0)


def _na_attn(q, kc, vc, k, v, bias, *, rows_per_step):
    b, s, _ = q.shape
    n_rows = s // GRID_W
    n_pairs = NA_HEADS // 2
    tq = rows_per_step * GRID_W
    return pl.pallas_call(
        functools.partial(_na_kernel, rows_per_step=rows_per_step, n_rows=n_rows),
        grid=(b, n_pairs, n_rows // rows_per_step),
        in_specs=[pl.BlockSpec((1, tq, LANES), lambda bi, m, i: (bi, i, m)),
                  pl.BlockSpec((1, kc.shape[1], LANES), lambda bi, m, i: (bi, 0, m)),
                  pl.BlockSpec((1, vc.shape[1], LANES), lambda bi, m, i: (bi, 0, m)),
                  pl.BlockSpec((1, s, LANES), lambda bi, m, i: (bi, 0, m)),
                  pl.BlockSpec((1, s, LANES), lambda bi, m, i: (bi, 0, m)),
                  pl.BlockSpec((2,) + bias.shape[1:], lambda bi, m, i: (m, 0, 0, 0))],
        out_specs=pl.BlockSpec((1, tq, LANES), lambda bi, m, i: (bi, i, m)),
        out_shape=jax.ShapeDtypeStruct((b, s, NA_DIM), BF16),
        compiler_params=_params(3),
        name="na_attn",
    )(q, kc, vc, k, v, bias)


def _pool_kernel(a_ref, pw_ref, ps_ref, o_ref, pad_ref, *, t_len, chunk):
    pad_ref[0:POOL_HALO, :] = jnp.zeros((POOL_HALO, POOL_DIM), F32)
    pad_ref[POOL_HALO + t_len:2 * POOL_HALO + t_len, :] = jnp.zeros((POOL_HALO, POOL_DIM), F32)
    pad_ref[POOL_HALO:POOL_HALO + t_len, :] = a_ref[0]
    lane = _lane((chunk, POOL_DIM))
    g0 = lane < POOL_CH
    g1 = lane < 2 * POOL_CH
    g2 = lane < 3 * POOL_CH
    half_w = jnp.where(g0, 1, jnp.where(g1, 2, jnp.where(g2, 4, 8)))
    for c in range(t_len // chunk):
        base = c * chunk

        def ld(off, base=base):
            return pad_ref[POOL_HALO + base + off:POOL_HALO + base + off + chunk, :]

        a0 = ld(0)
        w2 = ld(-1) + a0
        w4 = w2 + ld(-2) + ld(1)
        w8 = w4 + ld(-4) + ld(-3) + ld(2) + ld(3)
        w16 = w8
        for off in (-8, -7, -6, -5, 4, 5, 6, 7):
            w16 = w16 + ld(off)
        tok = base + lax.broadcasted_iota(jnp.int32, (chunk, POOL_DIM), 0)
        cnt = (jnp.minimum(tok + half_w, t_len) - jnp.maximum(tok - half_w, 0)).astype(F32)
        wsum = jnp.where(g0, w2, jnp.where(g1, w4, jnp.where(g2, w8, w16)))
        dlt = wsum / cnt - a0
        y = jnp.dot(dlt.astype(BF16), pw_ref[...], preferred_element_type=F32) * ps_ref[...]
        o_ref[0, base:base + chunk, :] = y.astype(o_ref.dtype)


def _pool(a, pw_bd, pscale):
    b, t, _ = a.shape
    chunk = min(t, 256)
    return pl.pallas_call(
        functools.partial(_pool_kernel, t_len=t, chunk=chunk),
        grid=(b,),
        in_specs=[pl.BlockSpec((1, t, POOL_DIM), lambda bi: (bi, 0, 0)),
                  pl.BlockSpec((POOL_DIM, POOL_DIM), lambda bi: (0, 0)),
                  pl.BlockSpec((1, POOL_DIM), lambda bi: (0, 0))],
        out_specs=pl.BlockSpec((1, t, POOL_DIM), lambda bi: (bi, 0, 0)),
        out_shape=jax.ShapeDtypeStruct((b, t, POOL_DIM), BF16),
        scratch_shapes=[pltpu.VMEM((t + 2 * POOL_HALO, POOL_DIM), F32)],
        compiler_params=_params(1),
        name="pool",
    )(a, pw_bd, pscale)


def _outproj_kernel(x_ref, gate_ref, m0_ref, m1_ref, w0_ref, w1_ref, o_ref):
    acc = jnp.dot(m0_ref[0], w0_ref[...], preferred_element_type=F32)
    acc = acc + jnp.dot(m1_ref[0], w1_ref[...], preferred_element_type=F32)
    o_ref[0] = x_ref[0] + gate_ref[0] * acc


def _outproj(x, gate, m0, m1, w0, w1, *, tm):
    b, t, d = x.shape
    row = lambda bi, i: (bi, i, 0)
    vec = lambda bi, i: (bi, 0, 0)
    const = lambda bi, i: (0, 0)
    return pl.pallas_call(
        _outproj_kernel,
        grid=(b, t // tm),
        in_specs=[pl.BlockSpec((1, tm, d), row),
                  pl.BlockSpec((1, 1, d), vec),
                  pl.BlockSpec((1, tm, m0.shape[2]), row),
                  pl.BlockSpec((1, tm, m1.shape[2]), row),
                  pl.BlockSpec(w0.shape, const),
                  pl.BlockSpec(w1.shape, const)],
        out_specs=pl.BlockSpec((1, tm, d), row),
        out_shape=jax.ShapeDtypeStruct((b, t, d), F32),
        compiler_params=_params(2),
        name="outproj",
    )(x, gate, m0, m1, w0, w1)


def _ffn_kernel(x_ref, sh_ref, sc_ref, gate_ref, g_ref, wg_ref, wu_ref, wd_ref, o_ref, acc_ref):
    x = x_ref[0]
    h = _modulate(x, g_ref[...], sh_ref[0], sc_ref[0]).astype(BF16)
    acc_ref[...] = jnp.zeros_like(acc_ref)

    def chunk(c, carry):
        gt = jnp.dot(h, wg_ref[c], preferred_element_type=F32)
        up = jnp.dot(h, wu_ref[c], preferred_element_type=F32)
        act = (gt * jax.nn.sigmoid(gt) * up).astype(BF16)
        acc_ref[...] += jnp.dot(act, wd_ref[c], preferred_element_type=F32)
        return carry

    lax.fori_loop(0, wg_ref.shape[0], chunk, 0)
    o_ref[0] = x + gate_ref[0] * acc_ref[...]


def _ffn(x, shift, scale, gate, gain, wg, wu, wd, *, tm):
    b, t, d = x.shape
    row = lambda bi, i: (bi, i, 0)
    vec = lambda bi, i: (bi, 0, 0)
    return pl.pallas_call(
        _ffn_kernel,
        grid=(b, t // tm),
        in_specs=[pl.BlockSpec((1, tm, d), row),
                  pl.BlockSpec((1, 1, d), vec),
                  pl.BlockSpec((1, 1, d), vec),
                  pl.BlockSpec((1, 1, d), vec),
                  pl.BlockSpec((1, d), lambda bi, i: (0, 0)),
                  pl.BlockSpec(wg.shape, lambda bi, i: (0, 0, 0)),
                  pl.BlockSpec(wu.shape, lambda bi, i: (0, 0, 0)),
                  pl.BlockSpec(wd.shape, lambda bi, i: (0, 0, 0))],
        out_specs=pl.BlockSpec((1, tm, d), row),
        out_shape=jax.ShapeDtypeStruct((b, t, d), F32),
        scratch_shapes=[pltpu.VMEM((tm, d), F32)],
        compiler_params=_params(2),
        name="ffn",
    )(x, shift, scale, gate, gain, wg, wu, wd)


def _rope_tables(seq, head_half, n_rep):
    tok = np.arange(seq)
    inv = (ROPE_THETA ** (-np.arange(head_half, dtype=np.float32) * 2.0 / (2 * head_half))).astype(np.float32)
    cos = np.ones((seq, LANES), np.float64)
    sin = np.zeros((seq, LANES), np.float64)
    for rep in range(n_rep):
        for part, pos in enumerate((tok // GRID_W, tok % GRID_W)):
            ang = pos.astype(np.float32).astype(np.float64)[:, None] * inv.astype(np.float64)[None, :]
            o = rep * 4 * head_half + part * 2 * head_half
            cos[:, o:o + head_half] = np.cos(ang)
            cos[:, o + head_half:o + 2 * head_half] = np.cos(ang)
            sin[:, o:o + head_half] = -np.sin(ang)
            sin[:, o + head_half:o + 2 * head_half] = np.sin(ang)
    return jnp.asarray(cos, F32), jnp.asarray(sin, F32)


def _gqa_pair_columns():
    cols = []
    for m in range(GQA_KV_HEADS // 2):
        for g in range(GQA_GROUP):
            for kvh in (2 * m, 2 * m + 1):
                h = kvh * GQA_GROUP + g
                cols.extend(range(h * HEAD_DIM, (h + 1) * HEAD_DIM))
    return np.asarray(cols, np.int32)


def _na_bias_table(rpb):
    cols = np.arange(GRID_W)
    c0 = np.clip(cols - NA_COLS // 2, 0, GRID_W - NA_COLS)
    kc = np.arange(GRID_W)
    inside = (kc[None, :] >= c0[:, None]) & (kc[None, :] < c0[:, None] + NA_COLS)
    dc = np.clip(kc[None, :] - cols[:, None] + (NA_COLS - 1), 0, 2 * NA_COLS - 2)
    t = jnp.where(jnp.asarray(inside)[None, None], rpb[:, :, dc], NEG_BIG)
    return jnp.concatenate([t[:, :-1], t[:, 1:]], axis=-1).astype(F32)


def _bcast_rows(vec, b):
    return jnp.broadcast_to(vec.reshape(1, 1, -1), (b, 1, vec.shape[-1]))


def kernel(x, c, ctx, c_ctx, l0_ada_w, l0_ada_b, l0_norm_mix, l0_norm_ffn, l0_w_in, l0_pool_w, l0_pool_scale, l0_q_gain, l0_k_gain, l0_w_out, l0_ffn_w_gate, l0_ffn_w_up, l0_ffn_w_down, l1_ada_w, l1_ada_b, l1_norm_mix, l1_norm_ffn, l1_w_in, l1_mla_q_a_gain, l1_mla_kv_a_gain, l1_mla_w_uq, l1_mla_w_ukv, l1_mla_q_gain, l1_mla_k_gain, l1_na_q_gain, l1_na_k_gain, l1_na_rpb, l1_w_out, l1_ffn_w_gate, l1_ffn_w_up, l1_ffn_w_down):
    b, s, d = x.shape
    tc = ctx.shape[1]
    tm_lat = 512
    tq_lat = 256

    cond = jnp.concatenate([c, c_ctx[None, :], jnp.zeros((7, d), F32)], axis=0)

    def mods(ada_w, ada_b):
        m = _ada(cond, ada_w, ada_b)
        lat = [m[:b, None, i * d:(i + 1) * d] for i in range(6)]
        cx = [_bcast_rows(m[b, i * d:(i + 1) * d], b) for i in range(6)]
        return lat, cx

    def ffn_weights(wg, wu, wd):
        nc = FFN_DIM // FFN_CHUNK
        wg3 = wg.astype(BF16).reshape(d, nc, FFN_CHUNK).transpose(1, 0, 2)
        wu3 = wu.astype(BF16).reshape(d, nc, FFN_CHUNK).transpose(1, 0, 2)
        wd3 = wd.astype(BF16).reshape(nc, FFN_CHUNK, d)
        return wg3, wu3, wd3

    row1 = lambda v: v.reshape(1, -1)
    tile2 = lambda v: jnp.concatenate([v, v]).reshape(1, -1)

    (sh1, sc1, g1, sh2, sc2, g2), (csh1, csc1, cg1, csh2, csc2, cg2) = mods(l0_ada_w, l0_ada_b)
    qperm = _gqa_pair_columns()
    w_in0 = jnp.concatenate([l0_w_in[:, :POOL_DIM],
                             l0_w_in[:, POOL_DIM:POOL_DIM + GQA_Q_DIM][:, qperm],
                             l0_w_in[:, POOL_DIM + GQA_Q_DIM:]], axis=1).astype(BF16)
    cos0, sin0 = _rope_tables(s, HEAD_DIM // 4, 2)
    ones_c = jnp.ones((tc, LANES), F32)
    zeros_c = jnp.zeros((tc, LANES), F32)
    qg2, kg2 = tile2(l0_q_gain), tile2(l0_k_gain)
    gmask_a = jnp.asarray(np.concatenate([np.ones(HEAD_DIM), np.zeros(HEAD_DIM)])[None], BF16)
    gmask_b = 1 - gmask_a

    a_l, q_l, k_l, v_l = _inproj0(x, sh1, sc1, row1(l0_norm_mix), w_in0, qg2, kg2, cos0, sin0,
                                  rope=True, tm=tm_lat)
    a_c, q_c, k_c, v_c = _inproj0(ctx, csh1, csc1, row1(l0_norm_mix), w_in0, qg2, kg2, ones_c, zeros_c,
                                  rope=False, tm=tc)
    n_kv_pairs = GQA_KV_HEADS // 2
    attn_l = _pair_attn(q_l, gmask_a, gmask_b, k_c, v_c, k_l, v_l, dk=LANES, n_pairs=n_kv_pairs,
                        q_per_pair=GQA_GROUP, tq=tq_lat, name="gqa_lat")
    attn_c = _pair_attn(q_c, gmask_a, gmask_b, k_c, v_c, None, None, dk=LANES, n_pairs=n_kv_pairs,
                        q_per_pair=GQA_GROUP, tq=tc, name="gqa_ctx")

    eye = jnp.eye(POOL_GROUPS, dtype=F32)
    pw_bd = (eye[:, None, :, None] * l0_pool_w[:, :, None, :]).reshape(POOL_DIM, POOL_DIM).astype(BF16)
    pool_l = _pool(a_l, pw_bd, row1(l0_pool_scale))
    pool_c = _pool(a_c, pw_bd, row1(l0_pool_scale))

    w_out0_pool = l0_w_out[:POOL_DIM].astype(BF16)
    w_out0_attn = l0_w_out[POOL_DIM:][qperm].astype(BF16)
    ffn0 = ffn_weights(l0_ffn_w_gate, l0_ffn_w_up, l0_ffn_w_down)

    x1 = _outproj(x, g1, pool_l, attn_l, w_out0_pool, w_out0_attn, tm=tm_lat)
    x1 = _ffn(x1, sh2, sc2, g2, row1(l0_norm_ffn), *ffn0, tm=tm_lat)
    xc = _outproj(ctx, cg1, pool_c, attn_c, w_out0_pool, w_out0_attn, tm=tc)
    xc = _ffn(xc, csh2, csc2, cg2, row1(l0_norm_ffn), *ffn0, tm=tc)

    (sh1, sc1, g1, sh2, sc2, g2), (csh1, csc1, _, _, _, _) = mods(l1_ada_w, l1_ada_b)
    o_kr = MLA_Q_LORA + MLA_KV_LORA
    kr_cols = l1_w_in[:, o_kr:o_kr + MLA_ROPE]
    w_in1 = jnp.concatenate([l1_w_in[:, :o_kr], kr_cols, kr_cols,
                             jnp.zeros((d, LANES - 2 * MLA_ROPE), F32),
                             l1_w_in[:, o_kr + MLA_ROPE:]], axis=1).astype(BF16)
    n_pairs = MLA_HEADS // 2
    uq = l1_mla_w_uq.reshape(MLA_Q_LORA, MLA_HEADS, MLA_QK)
    ukv = l1_mla_w_ukv.reshape(MLA_KV_LORA, MLA_HEADS, MLA_NOPE + MLA_V)
    zpad = jnp.zeros((MLA_Q_LORA, LANES - 2 * MLA_ROPE), F32)
    wuq = jnp.concatenate(
        [blk for m in range(n_pairs) for blk in
         (uq[:, 2 * m, :MLA_NOPE], uq[:, 2 * m + 1, :MLA_NOPE],
          uq[:, 2 * m, MLA_NOPE:], uq[:, 2 * m + 1, MLA_NOPE:], zpad)], axis=1).astype(BF16)
    wukv = jnp.concatenate(
        [ukv[:, :, :MLA_NOPE].reshape(MLA_KV_LORA, -1), ukv[:, :, MLA_NOPE:].reshape(MLA_KV_LORA, -1)],
        axis=1).astype(BF16)

    def mla_gains(g):
        zr = jnp.zeros((LANES - 2 * MLA_ROPE,), F32)
        return (jnp.concatenate([g[:MLA_NOPE], g[:MLA_NOPE]]).reshape(1, -1),
                jnp.concatenate([g[MLA_NOPE:], g[MLA_NOPE:], zr]).reshape(1, -1))

    mqn, mqr = mla_gains(l1_mla_q_gain)
    mkn, mkr = mla_gains(l1_mla_k_gain)
    nqg, nkg = tile2(l1_na_q_gain), tile2(l1_na_k_gain)
    cos1, sin1 = _rope_tables(s, MLA_ROPE // 4, 2)
    common = (row1(l1_norm_mix), w_in1, row1(l1_mla_q_a_gain), row1(l1_mla_kv_a_gain), wuq, wukv,
              mqn, mqr, mkn, mkr, nqg, nkg)
    mq, mk, mv, nq, nk, nv = _inproj1(x1, sh1, sc1, *common, cos1, sin1, rope=True, want_q=True,
                                      tm=tm_lat)
    mkc, mvc, nkc, nvc = _inproj1(xc, csh1, csc1, *common, ones_c, zeros_c, rope=False, want_q=False,
                                  tm=tc)

    mm_a = np.zeros((1, MLA_PAIR_W), np.float32)
    mm_b = np.zeros((1, MLA_PAIR_W), np.float32)
    mm_a[0, :MLA_NOPE] = 1
    mm_a[0, LANES:LANES + MLA_ROPE] = 1
    mm_b[0, MLA_NOPE:LANES] = 1
    mm_b[0, LANES + MLA_ROPE:LANES + 2 * MLA_ROPE] = 1
    o_mla = _pair_attn(mq, jnp.asarray(mm_a, BF16), jnp.asarray(mm_b, BF16), mkc, mvc, mk, mv,
                       dk=MLA_PAIR_W, n_pairs=n_pairs, q_per_pair=1, tq=tq_lat, name="mla")
    o_na = _na_attn(nq, nkc, nvc, nk, nv, _na_bias_table(l1_na_rpb), rows_per_step=4)

    w_out1 = l1_w_out.astype(BF16)
    x2 = _outproj(x1, g1, o_mla, o_na, w_out1[:MLA_HEADS * MLA_V], w_out1[MLA_HEADS * MLA_V:], tm=tm_lat)
    ffn1 = ffn_weights(l1_ffn_w_gate, l1_ffn_w_up, l1_ffn_w_down)
    return _ffn(x2, sh2, sc2, g2, row1(l1_norm_ffn), *ffn1, tm=tm_lat)
```

```python
import functools

import numpy as np
import jax
import jax.numpy as jnp
from jax import lax
from jax.experimental import pallas as pl
from jax.experimental.pallas import tpu as pltpu

F32 = jnp.float32
BF16 = jnp.bfloat16

D_MODEL = 1024
GRID_W = 64
ROPE_THETA = 10000.0
EPS = 1e-6
LANES = 128

POOL_GROUPS = 4
POOL_CH = 64
POOL_DIM = POOL_GROUPS * POOL_CH
POOL_HALO = 16

HEAD_DIM = 64
GQA_HEADS = 12
GQA_KV_HEADS = 4
GQA_GROUP = GQA_HEADS // GQA_KV_HEADS
GQA_Q_DIM = GQA_HEADS * HEAD_DIM
GQA_KV_DIM = GQA_KV_HEADS * HEAD_DIM

MLA_HEADS = 8
MLA_NOPE = 64
MLA_ROPE = 32
MLA_QK = MLA_NOPE + MLA_ROPE
MLA_V = 64
MLA_Q_LORA = 384
MLA_KV_LORA = 256
MLA_PAIR_W = 2 * LANES

NA_HEADS = 8
NA_HEAD_DIM = 64
NA_DIM = NA_HEADS * NA_HEAD_DIM
NA_ROWS = 8
NA_COLS = 16

FFN_DIM = -(-8 * D_MODEL // (3 * 256)) * 256
FFN_CHUNK = 256

NEG_BIG = -1e30
LOG2E = 1.4426950408889634
GQA_QSCALE = HEAD_DIM ** -0.5 * LOG2E
MLA_QSCALE = MLA_QK ** -0.5 * LOG2E
NA_QSCALE = NA_HEAD_DIM ** -0.5 * LOG2E
ATTN_SUB = 256

VMEM_LIMIT = 56 * 1024 * 1024

_NT = (((1,), (1,)), ((), ()))


def _params(n_axes):
    return pltpu.CompilerParams(dimension_semantics=("arbitrary",) * n_axes,
                                vmem_limit_bytes=VMEM_LIMIT)


def _modulate(x, gain, shift, scale):
    ms = jnp.mean(x * x, axis=-1, keepdims=True)
    return x * lax.rsqrt(ms + EPS) * gain * (1.0 + scale) + shift


def _row_rmsnorm(x, gain):
    ms = jnp.mean(x * x, axis=-1, keepdims=True)
    return x * lax.rsqrt(ms + EPS) * gain


def _lane(shape):
    return lax.broadcasted_iota(jnp.int32, shape, 1)


def _pair_rmsnorm(blk, gain2):
    first = _lane(blk.shape) < HEAD_DIM
    y = blk * blk
    sa = jnp.sum(jnp.where(first, y, 0.0), axis=-1, keepdims=True)
    sb = jnp.sum(jnp.where(first, 0.0, y), axis=-1, keepdims=True)
    inv = jnp.where(first, lax.rsqrt(sa * (1.0 / HEAD_DIM) + EPS),
                    lax.rsqrt(sb * (1.0 / HEAD_DIM) + EPS))
    return blk * inv * gain2


def _mla_pair_rmsnorm(nblk, rblk, gain_n, gain_r):
    lane = _lane(nblk.shape)
    na = lane < MLA_NOPE
    ra = lane < MLA_ROPE
    rb = jnp.logical_and(lane >= MLA_ROPE, lane < 2 * MLA_ROPE)
    yn = nblk * nblk
    yr = rblk * rblk
    sa = (jnp.sum(jnp.where(na, yn, 0.0), axis=-1, keepdims=True)
          + jnp.sum(jnp.where(ra, yr, 0.0), axis=-1, keepdims=True))
    sb = (jnp.sum(jnp.where(na, 0.0, yn), axis=-1, keepdims=True)
          + jnp.sum(jnp.where(rb, yr, 0.0), axis=-1, keepdims=True))
    inva = lax.rsqrt(sa * (1.0 / MLA_QK) + EPS)
    invb = lax.rsqrt(sb * (1.0 / MLA_QK) + EPS)
    return (nblk * jnp.where(na, inva, invb) * gain_n,
            rblk * jnp.where(ra, inva, invb) * gain_r)


def _rope(blk, cos, sin_signed, half):
    first = (_lane(blk.shape) & (2 * half - 1)) < half
    rot = jnp.where(first, pltpu.roll(blk, LANES - half, 1), pltpu.roll(blk, half, 1))
    return blk * cos + rot * sin_signed


def _ada_kernel(c_ref, w_ref, b_ref, o_ref):
    cnd = c_ref[...]
    act = cnd * jax.nn.sigmoid(cnd)
    o_ref[...] = jnp.dot(act, w_ref[...], preferred_element_type=F32,
                         precision=lax.Precision.HIGHEST) + b_ref[...]


def _ada(cond, w, b):
    n_rows, d = cond.shape
    n = w.shape[1]
    tn = 1024
    return pl.pallas_call(
        _ada_kernel,
        grid=(n // tn,),
        in_specs=[pl.BlockSpec((n_rows, d), lambda j: (0, 0)),
                  pl.BlockSpec((d, tn), lambda j: (0, j)),
                  pl.BlockSpec((1, tn), lambda j: (0, j))],
        out_specs=pl.BlockSpec((n_rows, tn), lambda j: (0, j)),
        out_shape=jax.ShapeDtypeStruct((n_rows, n), F32),
        compiler_params=_params(1),
        name="ada",
    )(cond, w, b.reshape(1, n))


def _inproj0_kernel(x_ref, sh_ref, sc_ref, g_ref, w_ref, qg_ref, kg_ref, cos_ref, sin_ref,
                    a_ref, q_ref, k_ref, v_ref, *, rope):
    h = _modulate(x_ref[0], g_ref[...], sh_ref[0], sc_ref[0])
    p = jnp.dot(h.astype(BF16), w_ref[...], preferred_element_type=F32)
    a_ref[0] = p[:, :POOL_DIM]
    cos = cos_ref[...]
    sin = sin_ref[...]
    q0 = POOL_DIM
    for j in range(GQA_Q_DIM // LANES):
        blk = _pair_rmsnorm(p[:, q0 + j * LANES:q0 + (j + 1) * LANES], qg_ref[...])
        if rope:
            blk = _rope(blk, cos, sin, HEAD_DIM // 4)
        q_ref[0, :, j * LANES:(j + 1) * LANES] = (blk * GQA_QSCALE).astype(BF16)
    k0 = q0 + GQA_Q_DIM
    for j in range(GQA_KV_DIM // LANES):
        blk = _pair_rmsnorm(p[:, k0 + j * LANES:k0 + (j + 1) * LANES], kg_ref[...])
        if rope:
            blk = _rope(blk, cos, sin, HEAD_DIM // 4)
        k_ref[0, :, j * LANES:(j + 1) * LANES] = blk.astype(BF16)
    v0 = k0 + GQA_KV_DIM
    v_ref[0] = p[:, v0:v0 + GQA_KV_DIM].astype(BF16)


def _inproj0(x, shift, scale, gain, w, q_gain2, k_gain2, cos, sin, *, rope, tm):
    b, t, d = x.shape
    n = w.shape[1]
    row = lambda bi, i: (bi, i, 0)
    vec = lambda bi, i: (bi, 0, 0)
    const = lambda bi, i: (0, 0)
    return pl.pallas_call(
        functools.partial(_inproj0_kernel, rope=rope),
        grid=(b, t // tm),
        in_specs=[pl.BlockSpec((1, tm, d), row),
                  pl.BlockSpec((1, 1, d), vec),
                  pl.BlockSpec((1, 1, d), vec),
                  pl.BlockSpec((1, d), const),
                  pl.BlockSpec((d, n), const),
                  pl.BlockSpec((1, LANES), const),
                  pl.BlockSpec((1, LANES), const),
                  pl.BlockSpec((tm, LANES), lambda bi, i: (i, 0)),
                  pl.BlockSpec((tm, LANES), lambda bi, i: (i, 0))],
        out_specs=[pl.BlockSpec((1, tm, POOL_DIM), row),
                   pl.BlockSpec((1, tm, GQA_Q_DIM), row),
                   pl.BlockSpec((1, tm, GQA_KV_DIM), row),
                   pl.BlockSpec((1, tm, GQA_KV_DIM), row)],
        out_shape=[jax.ShapeDtypeStruct((b, t, POOL_DIM), F32),
                   jax.ShapeDtypeStruct((b, t, GQA_Q_DIM), BF16),
                   jax.ShapeDtypeStruct((b, t, GQA_KV_DIM), BF16),
                   jax.ShapeDtypeStruct((b, t, GQA_KV_DIM), BF16)],
        compiler_params=_params(2),
        name="inproj0_rope" if rope else "inproj0_ctx",
    )(x, shift, scale, gain, w, q_gain2, k_gain2, cos, sin)


def _inproj1_kernel(x_ref, sh_ref, sc_ref, g_ref, w_ref, qag_ref, kvag_ref, wuq_ref, wukv_ref,
                    mqn_ref, mqr_ref, mkn_ref, mkr_ref, nqg_ref, nkg_ref, cos_ref, sin_ref,
                    *out_refs, rope, want_q):
    if want_q:
        mq_ref, mk_ref, mv_ref, nq_ref, nk_ref, nv_ref = out_refs
    else:
        mk_ref, mv_ref, nk_ref, nv_ref = out_refs
    h = _modulate(x_ref[0], g_ref[...], sh_ref[0], sc_ref[0])
    p = jnp.dot(h.astype(BF16), w_ref[...], preferred_element_type=F32)
    cos = cos_ref[...]
    sin = sin_ref[...]
    o_ckv = MLA_Q_LORA
    o_kr = o_ckv + MLA_KV_LORA
    o_nq = o_kr + LANES
    o_nk = o_nq + NA_DIM
    o_nv = o_nk + NA_DIM
    n_pairs = MLA_HEADS // 2

    if want_q:
        cq = _row_rmsnorm(p[:, :MLA_Q_LORA], qag_ref[...])
        qq = jnp.dot(cq.astype(BF16), wuq_ref[...], preferred_element_type=F32)
        for m in range(n_pairs):
            c0 = m * MLA_PAIR_W
            nblk, rblk = _mla_pair_rmsnorm(qq[:, c0:c0 + LANES], qq[:, c0 + LANES:c0 + 2 * LANES],
                                           mqn_ref[...], mqr_ref[...])
            if rope:
                rblk = _rope(rblk, cos, sin, MLA_ROPE // 4)
            mq_ref[0, :, c0:c0 + LANES] = (nblk * MLA_QSCALE).astype(BF16)
            mq_ref[0, :, c0 + LANES:c0 + 2 * LANES] = (rblk * MLA_QSCALE).astype(BF16)
        for m in range(NA_DIM // LANES):
            blk = _pair_rmsnorm(p[:, o_nq + m * LANES:o_nq + (m + 1) * LANES], nqg_ref[...])
            nq_ref[0, :, m * LANES:(m + 1) * LANES] = (blk * NA_QSCALE).astype(BF16)

    ckv = _row_rmsnorm(p[:, o_ckv:o_ckv + MLA_KV_LORA], kvag_ref[...])
    kv = jnp.dot(ckv.astype(BF16), wukv_ref[...], preferred_element_type=F32)
    kr_blk = p[:, o_kr:o_kr + LANES]
    for m in range(n_pairs):
        c0 = m * MLA_PAIR_W
        nblk, rblk = _mla_pair_rmsnorm(kv[:, m * LANES:(m + 1) * LANES], kr_blk,
                                       mkn_ref[...], mkr_ref[...])
        if rope:
            rblk = _rope(rblk, cos, sin, MLA_ROPE // 4)
        mk_ref[0, :, c0:c0 + LANES] = nblk.astype(BF16)
        mk_ref[0, :, c0 + LANES:c0 + 2 * LANES] = rblk.astype(BF16)
    v0 = n_pairs * LANES
    mv_ref[0] = kv[:, v0:v0 + MLA_HEADS * MLA_V].astype(BF16)
    for m in range(NA_DIM // LANES):
        blk = _pair_rmsnorm(p[:, o_nk + m * LANES:o_nk + (m + 1) * LANES], nkg_ref[...])
        nk_ref[0, :, m * LANES:(m + 1) * LANES] = blk.astype(BF16)
    nv_ref[0] = p[:, o_nv:o_nv + NA_DIM].astype(BF16)


def _inproj1(x, shift, scale, gain, w, qag, kvag, wuq, wukv, mqn, mqr, mkn, mkr, nqg, nkg,
             cos, sin, *, rope, want_q, tm):
    b, t, d = x.shape
    row = lambda bi, i: (bi, i, 0)
    vec = lambda bi, i: (bi, 0, 0)
    const = lambda bi, i: (0, 0)
    full = lambda a: pl.BlockSpec(a.shape, const)
    mla_w = (MLA_HEADS // 2) * MLA_PAIR_W
    outs = [(mla_w, "mk"), (MLA_HEADS * MLA_V, "mv"), (NA_DIM, "nk"), (NA_DIM, "nv")]
    if want_q:
        outs = [(mla_w, "mq")] + outs[:2] + [(NA_DIM, "nq")] + outs[2:]
    return pl.pallas_call(
        functools.partial(_inproj1_kernel, rope=rope, want_q=want_q),
        grid=(b, t // tm),
        in_specs=[pl.BlockSpec((1, tm, d), row),
                  pl.BlockSpec((1, 1, d), vec),
                  pl.BlockSpec((1, 1, d), vec),
                  full(gain), full(w), full(qag), full(kvag), full(wuq), full(wukv),
                  full(mqn), full(mqr), full(mkn), full(mkr), full(nqg), full(nkg),
                  pl.BlockSpec((tm, LANES), lambda bi, i: (i, 0)),
                  pl.BlockSpec((tm, LANES), lambda bi, i: (i, 0))],
        out_specs=[pl.BlockSpec((1, tm, n), row) for n, _ in outs],
        out_shape=[jax.ShapeDtypeStruct((b, t, n), BF16) for n, _ in outs],
        compiler_params=_params(2),
        name="inproj1_lat" if want_q else "inproj1_ctx",
    )(x, shift, scale, gain, w, qag, kvag, wuq, wukv, mqn, mqr, mkn, mkr, nqg, nkg, cos, sin)


def _fill_pair_kv(k, v, mask_a, mask_b, ka_ref, kb_ref, va_ref, vb_ref, r0):
    n = k.shape[0]
    first = _lane(v.shape) < HEAD_DIM
    one = jnp.ones_like(v)
    ka_ref[r0:r0 + n, :] = k * mask_a
    kb_ref[r0:r0 + n, :] = k * mask_b
    va_ref[r0:r0 + n, :] = jnp.where(first, v, one)
    vb_ref[r0:r0 + n, :] = jnp.where(first, one, v)


def _pair_softmax_pv(q_tiles, ka, kb, va, vb):
    scores = [[lax.dot_general(q, k, _NT, preferred_element_type=F32) for k in (ka, kb)]
              for q in q_tiles]
    probs = [[jnp.exp2((s - s.max(axis=-1, keepdims=True)).astype(BF16)) for s in pair]
             for pair in scores]
    outs = []
    for pa, pb in probs:
        na = jnp.dot(pa, va, preferred_element_type=F32)
        nb = jnp.dot(pb, vb, preferred_element_type=F32)
        first = _lane(na.shape) < HEAD_DIM
        num = jnp.where(first, na, nb)
        den = pltpu.roll(jnp.where(first, nb, na), HEAD_DIM, 1)
        outs.append(num / den)
    return outs


def _pair_attn_kernel(*refs, use_lat):
    if use_lat:
        q_ref, ma_ref, mb_ref, kc_ref, vc_ref, kl_ref, vl_ref, o_ref, ka, kb, va, vb = refs
        segs = ((kc_ref, vc_ref), (kl_ref, vl_ref))
    else:
        q_ref, ma_ref, mb_ref, kc_ref, vc_ref, o_ref, ka, kb, va, vb = refs
        segs = ((kc_ref, vc_ref),)

    @pl.when(pl.program_id(2) == 0)
    def _():
        r0 = 0
        for k_ref, v_ref in segs:
            _fill_pair_kv(k_ref[0], v_ref[0], ma_ref[...], mb_ref[...], ka, kb, va, vb, r0)
            r0 += k_ref.shape[1]

    sub = min(q_ref.shape[1], ATTN_SUB)
    starts = range(0, q_ref.shape[1], sub)
    outs = _pair_softmax_pv([q_ref[0, r0:r0 + sub, :] for r0 in starts],
                            ka[...], kb[...], va[...], vb[...])
    for r0, o in zip(starts, outs):
        o_ref[0, r0:r0 + sub, :] = o.astype(o_ref.dtype)


def _pair_attn(q, mask_a, mask_b, kc, vc, kl, vl, *, dk, n_pairs, q_per_pair, tq, name):
    b, t, _ = q.shape
    n_qblk = n_pairs * q_per_pair
    use_lat = kl is not None
    qmap = lambda bi, c, i: (bi, i, c)
    kvmap = lambda bi, c, i: (bi, 0, c // q_per_pair)
    const = lambda bi, c, i: (0, 0)
    in_specs = [pl.BlockSpec((1, tq, dk), qmap),
                pl.BlockSpec((1, dk), const),
                pl.BlockSpec((1, dk), const),
                pl.BlockSpec((1, kc.shape[1], dk), kvmap),
                pl.BlockSpec((1, vc.shape[1], LANES), kvmap)]
    args = [q, mask_a, mask_b, kc, vc]
    if use_lat:
        in_specs += [pl.BlockSpec((1, kl.shape[1], dk), kvmap),
                     pl.BlockSpec((1, vl.shape[1], LANES), kvmap)]
        args += [kl, vl]
    n_keys = kc.shape[1] + (kl.shape[1] if use_lat else 0)
    return pl.pallas_call(
        functools.partial(_pair_attn_kernel, use_lat=use_lat),
        grid=(b, n_qblk, t // tq),
        in_specs=in_specs,
        out_specs=pl.BlockSpec((1, tq, LANES), qmap),
        out_shape=jax.ShapeDtypeStruct((b, t, n_qblk * LANES), BF16),
        scratch_shapes=[pltpu.VMEM((n_keys, dk), BF16), pltpu.VMEM((n_keys, dk), BF16),
                        pltpu.VMEM((n_keys, LANES), BF16), pltpu.VMEM((n_keys, LANES), BF16)],
        compiler_params=_params(3),
        name=name,
    )(*args)


def _na_kernel(q_ref, kc_ref, vc_ref, k_ref, v_ref, bias_ref, o_ref, ka, kb, va, vb,
               *, rows_per_step, n_rows):
    rb = pl.program_id(2)
    tc = kc_ref.shape[1]
    n_win = NA_ROWS * GRID_W

    @pl.when(rb == 0)
    def _():
        lane = _lane((1, LANES))
        mask_a = jnp.where(lane < NA_HEAD_DIM, 1.0, 0.0).astype(BF16)
        mask_b = jnp.where(lane < NA_HEAD_DIM, 0.0, 1.0).astype(BF16)
        _fill_pair_kv(kc_ref[0], vc_ref[0], mask_a, mask_b, ka, kb, va, vb, 0)
        _fill_pair_kv(k_ref[0], v_ref[0], mask_a, mask_b, ka, kb, va, vb, tc)

    heads = ((ka, va), (kb, vb))
    q_all = q_ref[0]
    rows = []
    for i in range(rows_per_step):
        r = rb * rows_per_step + i
        r0 = jnp.clip(r - NA_ROWS // 2, 0, n_rows - NA_ROWS)
        rows.append((pl.multiple_of(tc + r0 * GRID_W, GRID_W), r0 - r + (NA_ROWS - 1)))

    s_ctx = [lax.dot_general(q_all, k_s[0:tc, :], _NT, preferred_element_type=F32)
             for k_s, _ in heads]
    s_win = []
    for i, (kstart, j0) in enumerate(rows):
        q = q_all[i * GRID_W:(i + 1) * GRID_W]
        per_head = []
        for hh, (k_s, _) in enumerate(heads):
            bias = jnp.concatenate([bias_ref[hh, j0 + 2 * c] for c in range(NA_ROWS // 2)], axis=1)
            per_head.append(lax.dot_general(q, k_s[pl.ds(kstart, n_win), :], _NT,
                                            preferred_element_type=F32) + bias)
        s_win.append(per_head)

    p_ctx = [[], []]
    p_win = []
    for i in range(rows_per_step):
        per_head = []
        for hh in range(2):
            sc = s_ctx[hh][i * GRID_W:(i + 1) * GRID_W]
            sw = s_win[i][hh]
            mx = jnp.maximum(sc.max(axis=-1, keepdims=True), sw.max(axis=-1, keepdims=True))
            p_ctx[hh].append(jnp.exp2((sc - mx).astype(BF16)))
            per_head.append(jnp.exp2((sw - mx).astype(BF16)))
        p_win.append(per_head)

    n_ctx = [jnp.dot(jnp.concatenate(p_ctx[hh], axis=0), v_s[0:tc, :], preferred_element_type=F32)
             for hh, (_, v_s) in enumerate(heads)]
    outs = []
    for i, (kstart, _) in enumerate(rows):
        nums = [n_ctx[hh][i * GRID_W:(i + 1) * GRID_W]
                + jnp.dot(p_win[i][hh], v_s[pl.ds(kstart, n_win), :], preferred_element_type=F32)
                for hh, (_, v_s) in enumerate(heads)]
        first = _lane(nums[0].shape) < NA_HEAD_DIM
        num = jnp.where(first, nums[0], nums[1])
        den = pltpu.roll(jnp.where(first, nums[1], nums[0]), NA_HEAD_DIM, 1)
        outs.append((num / den).astype(o_ref.dtype))
    o_ref[0] = jnp.concatenate(outs, axis=0)


def _na_attn(q, kc, vc, k, v, bias, *, rows_per_step):
    b, s, _ = q.shape
    n_rows = s // GRID_W
    n_pairs = NA_HEADS // 2
    tq = rows_per_step * GRID_W
    n_keys = kc.shape[1] + s
    return pl.pallas_call(
        functools.partial(_na_kernel, rows_per_step=rows_per_step, n_rows=n_rows),
        grid=(b, n_pairs, n_rows // rows_per_step),
        in_specs=[pl.BlockSpec((1, tq, LANES), lambda bi, m, i: (bi, i, m)),
                  pl.BlockSpec((1, kc.shape[1], LANES), lambda bi, m, i: (bi, 0, m)),
                  pl.BlockSpec((1, vc.shape[1], LANES), lambda bi, m, i: (bi, 0, m)),
                  pl.BlockSpec((1, s, LANES), lambda bi, m, i: (bi, 0, m)),
                  pl.BlockSpec((1, s, LANES), lambda bi, m, i: (bi, 0, m)),
                  pl.BlockSpec((2,) + bias.shape[1:], lambda bi, m, i: (m, 0, 0, 0))],
        out_specs=pl.BlockSpec((1, tq, LANES), lambda bi, m, i: (bi, i, m)),
        out_shape=jax.ShapeDtypeStruct((b, s, NA_DIM), BF16),
        scratch_shapes=[pltpu.VMEM((n_keys, LANES), BF16) for _ in range(4)],
        compiler_params=_params(3),
        name="na_attn",
    )(q, kc, vc, k, v, bias)


def _pool_kernel(a_ref, pw_ref, ps_ref, o_ref, pad_ref, *, t_len, chunk):
    pad_ref[0:POOL_HALO, :] = jnp.zeros((POOL_HALO, POOL_DIM), F32)
    pad_ref[POOL_HALO + t_len:2 * POOL_HALO + t_len, :] = jnp.zeros((POOL_HALO, POOL_DIM), F32)
    pad_ref[POOL_HALO:POOL_HALO + t_len, :] = a_ref[0]
    lane = _lane((chunk, POOL_DIM))
    g0 = lane < POOL_CH
    g1 = lane < 2 * POOL_CH
    g2 = lane < 3 * POOL_CH
    half_w = jnp.where(g0, 1, jnp.where(g1, 2, jnp.where(g2, 4, 8)))
    for c in range(t_len // chunk):
        base = c * chunk

        def ld(off, base=base):
            return pad_ref[POOL_HALO + base + off:POOL_HALO + base + off + chunk, :]

        a0 = ld(0)
        w2 = ld(-1) + a0
        w4 = w2 + ld(-2) + ld(1)
        w8 = w4 + ld(-4) + ld(-3) + ld(2) + ld(3)
        w16 = w8
        for off in (-8, -7, -6, -5, 4, 5, 6, 7):
            w16 = w16 + ld(off)
        tok = base + lax.broadcasted_iota(jnp.int32, (chunk, POOL_DIM), 0)
        cnt = (jnp.minimum(tok + half_w, t_len) - jnp.maximum(tok - half_w, 0)).astype(F32)
        wsum = jnp.where(g0, w2, jnp.where(g1, w4, jnp.where(g2, w8, w16)))
        dlt = wsum / cnt - a0
        y = jnp.dot(dlt.astype(BF16), pw_ref[...], preferred_element_type=F32) * ps_ref[...]
        o_ref[0, base:base + chunk, :] = y.astype(o_ref.dtype)


def _pool(a, pw_bd, pscale):
    b, t, _ = a.shape
    chunk = min(t, 256)
    return pl.pallas_call(
        functools.partial(_pool_kernel, t_len=t, chunk=chunk),
        grid=(b,),
        in_specs=[pl.BlockSpec((1, t, POOL_DIM), lambda bi: (bi, 0, 0)),
                  pl.BlockSpec((POOL_DIM, POOL_DIM), lambda bi: (0, 0)),
                  pl.BlockSpec((1, POOL_DIM), lambda bi: (0, 0))],
        out_specs=pl.BlockSpec((1, t, POOL_DIM), lambda bi: (bi, 0, 0)),
        out_shape=jax.ShapeDtypeStruct((b, t, POOL_DIM), BF16),
        scratch_shapes=[pltpu.VMEM((t + 2 * POOL_HALO, POOL_DIM), F32)],
        compiler_params=_params(1),
        name="pool",
    )(a, pw_bd, pscale)


def _outproj_kernel(x_ref, gate_ref, m0_ref, m1_ref, w0_ref, w1_ref, o_ref):
    acc = jnp.dot(m0_ref[0], w0_ref[...], preferred_element_type=F32)
    acc = acc + jnp.dot(m1_ref[0], w1_ref[...], preferred_element_type=F32)
    o_ref[0] = x_ref[0] + gate_ref[0] * acc


def _outproj(x, gate, m0, m1, w0, w1, *, tm):
    b, t, d = x.shape
    row = lambda bi, i: (bi, i, 0)
    vec = lambda bi, i: (bi, 0, 0)
    const = lambda bi, i: (0, 0)
    return pl.pallas_call(
        _outproj_kernel,
        grid=(b, t // tm),
        in_specs=[pl.BlockSpec((1, tm, d), row),
                  pl.BlockSpec((1, 1, d), vec),
                  pl.BlockSpec((1, tm, m0.shape[2]), row),
                  pl.BlockSpec((1, tm, m1.shape[2]), row),
                  pl.BlockSpec(w0.shape, const),
                  pl.BlockSpec(w1.shape, const)],
        out_specs=pl.BlockSpec((1, tm, d), row),
        out_shape=jax.ShapeDtypeStruct((b, t, d), F32),
        compiler_params=_params(2),
        name="outproj",
    )(x, gate, m0, m1, w0, w1)


def _ffn_kernel(x_ref, sh_ref, sc_ref, gate_ref, g_ref, wg_ref, wu_ref, wd_ref, o_ref, acc_ref):
    x = x_ref[0]
    h = _modulate(x, g_ref[...], sh_ref[0], sc_ref[0]).astype(BF16)
    acc_ref[...] = jnp.zeros_like(acc_ref)

    def chunk(c, carry):
        gt = jnp.dot(h, wg_ref[c], preferred_element_type=F32)
        up = jnp.dot(h, wu_ref[c], preferred_element_type=F32)
        act = (gt * jax.nn.sigmoid(gt) * up).astype(BF16)
        acc_ref[...] += jnp.dot(act, wd_ref[c], preferred_element_type=F32)
        return carry

    lax.fori_loop(0, wg_ref.shape[0], chunk, 0)
    o_ref[0] = x + gate_ref[0] * acc_ref[...]


def _ffn(x, shift, scale, gate, gain, wg, wu, wd, *, tm):
    b, t, d = x.shape
    row = lambda bi, i: (bi, i, 0)
    vec = lambda bi, i: (bi, 0, 0)
    return pl.pallas_call(
        _ffn_kernel,
        grid=(b, t // tm),
        in_specs=[pl.BlockSpec((1, tm, d), row),
                  pl.BlockSpec((1, 1, d), vec),
                  pl.BlockSpec((1, 1, d), vec),
                  pl.BlockSpec((1, 1, d), vec),
                  pl.BlockSpec((1, d), lambda bi, i: (0, 0)),
                  pl.BlockSpec(wg.shape, lambda bi, i: (0, 0, 0)),
                  pl.BlockSpec(wu.shape, lambda bi, i: (0, 0, 0)),
                  pl.BlockSpec(wd.shape, lambda bi, i: (0, 0, 0))],
        out_specs=pl.BlockSpec((1, tm, d), row),
        out_shape=jax.ShapeDtypeStruct((b, t, d), F32),
        scratch_shapes=[pltpu.VMEM((tm, d), F32)],
        compiler_params=_params(2),
        name="ffn",
    )(x, shift, scale, gate, gain, wg, wu, wd)


def _rope_tables(seq, head_half, n_rep):
    tok = np.arange(seq)
    inv = (ROPE_THETA ** (-np.arange(head_half, dtype=np.float32) * 2.0 / (2 * head_half))).astype(np.float32)
    cos = np.ones((seq, LANES), np.float64)
    sin = np.zeros((seq, LANES), np.float64)
    for rep in range(n_rep):
        for part, pos in enumerate((tok // GRID_W, tok % GRID_W)):
            ang = pos.astype(np.float32).astype(np.float64)[:, None] * inv.astype(np.float64)[None, :]
            o = rep * 4 * head_half + part * 2 * head_half
            cos[:, o:o + head_half] = np.cos(ang)
            cos[:, o + head_half:o + 2 * head_half] = np.cos(ang)
            sin[:, o:o + head_half] = -np.sin(ang)
            sin[:, o + head_half:o + 2 * head_half] = np.sin(ang)
    return jnp.asarray(cos, F32), jnp.asarray(sin, F32)


def _gqa_pair_columns():
    cols = []
    for m in range(GQA_KV_HEADS // 2):
        for g in range(GQA_GROUP):
            for kvh in (2 * m, 2 * m + 1):
                h = kvh * GQA_GROUP + g
                cols.extend(range(h * HEAD_DIM, (h + 1) * HEAD_DIM))
    return np.asarray(cols, np.int32)


def _na_bias_table(rpb):
    cols = np.arange(GRID_W)
    c0 = np.clip(cols - NA_COLS // 2, 0, GRID_W - NA_COLS)
    kc = np.arange(GRID_W)
    inside = (kc[None, :] >= c0[:, None]) & (kc[None, :] < c0[:, None] + NA_COLS)
    dc = np.clip(kc[None, :] - cols[:, None] + (NA_COLS - 1), 0, 2 * NA_COLS - 2)
    t = jnp.where(jnp.asarray(inside)[None, None], rpb[:, :, dc] * LOG2E, NEG_BIG)
    return jnp.concatenate([t[:, :-1], t[:, 1:]], axis=-1).astype(F32)


def _bcast_rows(vec, b):
    return jnp.broadcast_to(vec.reshape(1, 1, -1), (b, 1, vec.shape[-1]))


def kernel(x, c, ctx, c_ctx, l0_ada_w, l0_ada_b, l0_norm_mix, l0_norm_ffn, l0_w_in, l0_pool_w, l0_pool_scale, l0_q_gain, l0_k_gain, l0_w_out, l0_ffn_w_gate, l0_ffn_w_up, l0_ffn_w_down, l1_ada_w, l1_ada_b, l1_norm_mix, l1_norm_ffn, l1_w_in, l1_mla_q_a_gain, l1_mla_kv_a_gain, l1_mla_w_uq, l1_mla_w_ukv, l1_mla_q_gain, l1_mla_k_gain, l1_na_q_gain, l1_na_k_gain, l1_na_rpb, l1_w_out, l1_ffn_w_gate, l1_ffn_w_up, l1_ffn_w_down):
    b, s, d = x.shape
    tc = ctx.shape[1]
    tm_lat = 512
    tq_lat = 512

    cond = jnp.concatenate([c, c_ctx[None, :], jnp.zeros((7, d), F32)], axis=0)

    def mods(ada_w, ada_b):
        m = _ada(cond, ada_w, ada_b)
        lat = [m[:b, None, i * d:(i + 1) * d] for i in range(6)]
        cx = [_bcast_rows(m[b, i * d:(i + 1) * d], b) for i in range(6)]
        return lat, cx

    def ffn_weights(wg, wu, wd):
        nc = FFN_DIM // FFN_CHUNK
        wg3 = wg.astype(BF16).reshape(d, nc, FFN_CHUNK).transpose(1, 0, 2)
        wu3 = wu.astype(BF16).reshape(d, nc, FFN_CHUNK).transpose(1, 0, 2)
        wd3 = wd.astype(BF16).reshape(nc, FFN_CHUNK, d)
        return wg3, wu3, wd3

    row1 = lambda v: v.reshape(1, -1)
    tile2 = lambda v: jnp.concatenate([v, v]).reshape(1, -1)

    (sh1, sc1, g1, sh2, sc2, g2), (csh1, csc1, cg1, csh2, csc2, cg2) = mods(l0_ada_w, l0_ada_b)
    qperm = _gqa_pair_columns()
    w_in0 = jnp.concatenate([l0_w_in[:, :POOL_DIM],
                             l0_w_in[:, POOL_DIM:POOL_DIM + GQA_Q_DIM][:, qperm],
                             l0_w_in[:, POOL_DIM + GQA_Q_DIM:]], axis=1).astype(BF16)
    cos0, sin0 = _rope_tables(s, HEAD_DIM // 4, 2)
    ones_c = jnp.ones((tc, LANES), F32)
    zeros_c = jnp.zeros((tc, LANES), F32)
    qg2, kg2 = tile2(l0_q_gain), tile2(l0_k_gain)
    gmask_a = jnp.asarray(np.concatenate([np.ones(HEAD_DIM), np.zeros(HEAD_DIM)])[None], BF16)
    gmask_b = 1 - gmask_a

    a_l, q_l, k_l, v_l = _inproj0(x, sh1, sc1, row1(l0_norm_mix), w_in0, qg2, kg2, cos0, sin0,
                                  rope=True, tm=tm_lat)
    a_c, q_c, k_c, v_c = _inproj0(ctx, csh1, csc1, row1(l0_norm_mix), w_in0, qg2, kg2, ones_c, zeros_c,
                                  rope=False, tm=tc)
    n_kv_pairs = GQA_KV_HEADS // 2
    attn_l = _pair_attn(q_l, gmask_a, gmask_b, k_c, v_c, k_l, v_l, dk=LANES, n_pairs=n_kv_pairs,
                        q_per_pair=GQA_GROUP, tq=tq_lat, name="gqa_lat")
    attn_c = _pair_attn(q_c, gmask_a, gmask_b, k_c, v_c, None, None, dk=LANES, n_pairs=n_kv_pairs,
                        q_per_pair=GQA_GROUP, tq=tc, name="gqa_ctx")

    eye = jnp.eye(POOL_GROUPS, dtype=F32)
    pw_bd = (eye[:, None, :, None] * l0_pool_w[:, :, None, :]).reshape(POOL_DIM, POOL_DIM).astype(BF16)
    pool_l = _pool(a_l, pw_bd, row1(l0_pool_scale))
    pool_c = _pool(a_c, pw_bd, row1(l0_pool_scale))

    w_out0_pool = l0_w_out[:POOL_DIM].astype(BF16)
    w_out0_attn = l0_w_out[POOL_DIM:][qperm].astype(BF16)
    ffn0 = ffn_weights(l0_ffn_w_gate, l0_ffn_w_up, l0_ffn_w_down)

    x1 = _outproj(x, g1, pool_l, attn_l, w_out0_pool, w_out0_attn, tm=tm_lat)
    x1 = _ffn(x1, sh2, sc2, g2, row1(l0_norm_ffn), *ffn0, tm=tm_lat)
    xc = _outproj(ctx, cg1, pool_c, attn_c, w_out0_pool, w_out0_attn, tm=tc)
    xc = _ffn(xc, csh2, csc2, cg2, row1(l0_norm_ffn), *ffn0, tm=tc)

    (sh1, sc1, g1, sh2, sc2, g2), (csh1, csc1, _, _, _, _) = mods(l1_ada_w, l1_ada_b)
    o_kr = MLA_Q_LORA + MLA_KV_LORA
    kr_cols = l1_w_in[:, o_kr:o_kr + MLA_ROPE]
    w_in1 = jnp.concatenate([l1_w_in[:, :o_kr], kr_cols, kr_cols,
                             jnp.zeros((d, LANES - 2 * MLA_ROPE), F32),
                             l1_w_in[:, o_kr + MLA_ROPE:]], axis=1).astype(BF16)
    n_pairs = MLA_HEADS // 2
    uq = l1_mla_w_uq.reshape(MLA_Q_LORA, MLA_HEADS, MLA_QK)
    ukv = l1_mla_w_ukv.reshape(MLA_KV_LORA, MLA_HEADS, MLA_NOPE + MLA_V)
    zpad = jnp.zeros((MLA_Q_LORA, LANES - 2 * MLA_ROPE), F32)
    wuq = jnp.concatenate(
        [blk for m in range(n_pairs) for blk in
         (uq[:, 2 * m, :MLA_NOPE], uq[:, 2 * m + 1, :MLA_NOPE],
          uq[:, 2 * m, MLA_NOPE:], uq[:, 2 * m + 1, MLA_NOPE:], zpad)], axis=1).astype(BF16)
    wukv = jnp.concatenate(
        [ukv[:, :, :MLA_NOPE].reshape(MLA_KV_LORA, -1), ukv[:, :, MLA_NOPE:].reshape(MLA_KV_LORA, -1)],
        axis=1).astype(BF16)

    def mla_gains(g):
        zr = jnp.zeros((LANES - 2 * MLA_ROPE,), F32)
        return (jnp.concatenate([g[:MLA_NOPE], g[:MLA_NOPE]]).reshape(1, -1),
                jnp.concatenate([g[MLA_NOPE:], g[MLA_NOPE:], zr]).reshape(1, -1))

    mqn, mqr = mla_gains(l1_mla_q_gain)
    mkn, mkr = mla_gains(l1_mla_k_gain)
    nqg, nkg = tile2(l1_na_q_gain), tile2(l1_na_k_gain)
    cos1, sin1 = _rope_tables(s, MLA_ROPE // 4, 2)
    common = (row1(l1_norm_mix), w_in1, row1(l1_mla_q_a_gain), row1(l1_mla_kv_a_gain), wuq, wukv,
              mqn, mqr, mkn, mkr, nqg, nkg)
    mq, mk, mv, nq, nk, nv = _inproj1(x1, sh1, sc1, *common, cos1, sin1, rope=True, want_q=True,
                                      tm=tm_lat)
    mkc, mvc, nkc, nvc = _inproj1(xc, csh1, csc1, *common, ones_c, zeros_c, rope=False, want_q=False,
                                  tm=tc)

    mm_a = np.zeros((1, MLA_PAIR_W), np.float32)
    mm_b = np.zeros((1, MLA_PAIR_W), np.float32)
    mm_a[0, :MLA_NOPE] = 1
    mm_a[0, LANES:LANES + MLA_ROPE] = 1
    mm_b[0, MLA_NOPE:LANES] = 1
    mm_b[0, LANES + MLA_ROPE:LANES + 2 * MLA_ROPE] = 1
    o_mla = _pair_attn(mq, jnp.asarray(mm_a, BF16), jnp.asarray(mm_b, BF16), mkc, mvc, mk, mv,
                       dk=MLA_PAIR_W, n_pairs=n_pairs, q_per_pair=1, tq=tq_lat, name="mla")
    o_na = _na_attn(nq, nkc, nvc, nk, nv, _na_bias_table(l1_na_rpb), rows_per_step=8)

    w_out1 = l1_w_out.astype(BF16)
    x2 = _outproj(x1, g1, o_mla, o_na, w_out1[:MLA_HEADS * MLA_V], w_out1[MLA_HEADS * MLA_V:], tm=tm_lat)
    ffn1 = ffn_weights(l1_ffn_w_gate, l1_ffn_w_up, l1_ffn_w_down)
    return _ffn(x2, sh2, sc2, g2, row1(l1_norm_ffn), *ffn1, tm=tm_lat)
```

```python
import functools

import numpy as np
import jax
import jax.numpy as jnp
from jax import lax
from jax.experimental import pallas as pl
from jax.experimental.pallas import tpu as pltpu

F32 = jnp.float32
BF16 = jnp.bfloat16

D_MODEL = 1024
GRID_W = 64
ROPE_THETA = 10000.0
EPS = 1e-6
LANES = 128
SLAB = 256

POOL_GROUPS = 4
POOL_CH = 64
POOL_DIM = POOL_GROUPS * POOL_CH
POOL_HALO = 16

HEAD_DIM = 64
GQA_HEADS = 12
GQA_KV_HEADS = 4
GQA_GROUP = GQA_HEADS // GQA_KV_HEADS
GQA_Q_DIM = GQA_HEADS * HEAD_DIM
GQA_KV_DIM = GQA_KV_HEADS * HEAD_DIM

MLA_HEADS = 8
MLA_NOPE = 64
MLA_ROPE = 32
MLA_QK = MLA_NOPE + MLA_ROPE
MLA_V = 64
MLA_Q_LORA = 384
MLA_KV_LORA = 256
MLA_PAIR_W = 2 * LANES

NA_HEADS = 8
NA_HEAD_DIM = 64
NA_DIM = NA_HEADS * NA_HEAD_DIM
NA_ROWS = 8
NA_COLS = 16

FFN_DIM = -(-8 * D_MODEL // (3 * 256)) * 256
FFN_CHUNK = 256

NEG_BIG = -1e30
LOG2E = 1.4426950408889634
GQA_QSCALE = HEAD_DIM ** -0.5 * LOG2E
MLA_QSCALE = MLA_QK ** -0.5 * LOG2E
NA_QSCALE = NA_HEAD_DIM ** -0.5 * LOG2E
ATTN_SUB = 256

VMEM_LIMIT = 56 * 1024 * 1024

_NT = (((1,), (1,)), ((), ()))


def _params(n_axes):
    return pltpu.CompilerParams(dimension_semantics=("arbitrary",) * n_axes,
                                vmem_limit_bytes=VMEM_LIMIT)


def _modulate(x, gain, shift, scale):
    ms = jnp.mean(x * x, axis=-1, keepdims=True)
    return x * lax.rsqrt(ms + EPS) * gain * (1.0 + scale) + shift


def _row_rmsnorm(x, gain):
    ms = jnp.mean(x * x, axis=-1, keepdims=True)
    return x * lax.rsqrt(ms + EPS) * gain


def _lane(shape):
    return lax.broadcasted_iota(jnp.int32, shape, 1)


def _head_rmsnorm(slab, same_head, gain, n):
    ssq = jnp.dot((slab * slab).astype(BF16), same_head, preferred_element_type=F32)
    return slab * lax.rsqrt(ssq + n * EPS) * gain


def _rope(blk, cos, sin_signed, half):
    first = (_lane(blk.shape) & (2 * half - 1)) < half
    rot = jnp.where(first, pltpu.roll(blk, LANES - half, 1), pltpu.roll(blk, half, 1))
    return blk * cos + rot * sin_signed


def _ada_kernel(c_ref, w_ref, b_ref, o_ref):
    cnd = c_ref[...]
    act = cnd * jax.nn.sigmoid(cnd)
    o_ref[...] = jnp.dot(act, w_ref[...], preferred_element_type=F32,
                         precision=lax.Precision.HIGHEST) + b_ref[...]


def _ada(cond, w, b):
    n_rows, d = cond.shape
    n = w.shape[1]
    tn = 1024
    return pl.pallas_call(
        _ada_kernel,
        grid=(n // tn,),
        in_specs=[pl.BlockSpec((n_rows, d), lambda j: (0, 0)),
                  pl.BlockSpec((d, tn), lambda j: (0, j)),
                  pl.BlockSpec((1, tn), lambda j: (0, j))],
        out_specs=pl.BlockSpec((n_rows, tn), lambda j: (0, j)),
        out_shape=jax.ShapeDtypeStruct((n_rows, n), F32),
        compiler_params=_params(1),
        name="ada",
    )(cond, w, b.reshape(1, n))


def _slabs(src, c0, width):
    return [src[:, c0 + j * SLAB:c0 + (j + 1) * SLAB] for j in range(width // SLAB)]


def _norm_rope_store(slabs, same_head, gain, n, dst_ref, cos, sin, half, rope_blocks):
    for j, slab in enumerate(slabs):
        slab = _head_rmsnorm(slab, same_head, gain, n)
        for blk in range(SLAB // LANES):
            part = slab[:, blk * LANES:(blk + 1) * LANES]
            if blk in rope_blocks:
                part = _rope(part, cos, sin, half)
            o = j * SLAB + blk * LANES
            dst_ref[0, :, o:o + LANES] = part.astype(BF16)


def _inproj0_kernel(x_ref, sh_ref, sc_ref, g_ref, w_ref, sh64_ref, qg_ref, kg_ref, cos_ref, sin_ref,
                    a_ref, q_ref, k_ref, v_ref, *, rope):
    h = _modulate(x_ref[0], g_ref[...], sh_ref[0], sc_ref[0])
    p = jnp.dot(h.astype(BF16), w_ref[...], preferred_element_type=F32)
    a_ref[0] = p[:, :POOL_DIM]
    rope_blocks = (0, 1) if rope else ()
    q0 = POOL_DIM
    k0 = q0 + GQA_Q_DIM
    v0 = k0 + GQA_KV_DIM
    for c0, width, gain_ref, dst in ((q0, GQA_Q_DIM, qg_ref, q_ref), (k0, GQA_KV_DIM, kg_ref, k_ref)):
        _norm_rope_store(_slabs(p, c0, width), sh64_ref[...], gain_ref[...], HEAD_DIM, dst,
                         cos_ref[...], sin_ref[...], HEAD_DIM // 4, rope_blocks)
    v_ref[0] = p[:, v0:v0 + GQA_KV_DIM].astype(BF16)


def _inproj0(x, shift, scale, gain, w, same_head, q_gain, k_gain, cos, sin, *, rope, tm):
    b, t, d = x.shape
    n = w.shape[1]
    row = lambda bi, i: (bi, i, 0)
    vec = lambda bi, i: (bi, 0, 0)
    const = lambda bi, i: (0, 0)
    return pl.pallas_call(
        functools.partial(_inproj0_kernel, rope=rope),
        grid=(b, t // tm),
        in_specs=[pl.BlockSpec((1, tm, d), row),
                  pl.BlockSpec((1, 1, d), vec),
                  pl.BlockSpec((1, 1, d), vec),
                  pl.BlockSpec((1, d), const),
                  pl.BlockSpec((d, n), const),
                  pl.BlockSpec((SLAB, SLAB), const),
                  pl.BlockSpec((1, SLAB), const),
                  pl.BlockSpec((1, SLAB), const),
                  pl.BlockSpec((tm, LANES), lambda bi, i: (i, 0)),
                  pl.BlockSpec((tm, LANES), lambda bi, i: (i, 0))],
        out_specs=[pl.BlockSpec((1, tm, POOL_DIM), row),
                   pl.BlockSpec((1, tm, GQA_Q_DIM), row),
                   pl.BlockSpec((1, tm, GQA_KV_DIM), row),
                   pl.BlockSpec((1, tm, GQA_KV_DIM), row)],
        out_shape=[jax.ShapeDtypeStruct((b, t, POOL_DIM), F32),
                   jax.ShapeDtypeStruct((b, t, GQA_Q_DIM), BF16),
                   jax.ShapeDtypeStruct((b, t, GQA_KV_DIM), BF16),
                   jax.ShapeDtypeStruct((b, t, GQA_KV_DIM), BF16)],
        compiler_params=_params(2),
        name="inproj0_rope" if rope else "inproj0_ctx",
    )(x, shift, scale, gain, w, same_head, q_gain, k_gain, cos, sin)


def _inproj1_kernel(x_ref, sh_ref, sc_ref, g_ref, w_ref, qag_ref, kvag_ref, wuq_ref, wukv_ref,
                    sh64_ref, shmla_ref, mqg_ref, mkg_ref, nqg_ref, nkg_ref, cos_ref, sin_ref,
                    *out_refs, rope, want_q):
    if want_q:
        mq_ref, mk_ref, mv_ref, nq_ref, nk_ref, nv_ref = out_refs
    else:
        mk_ref, mv_ref, nk_ref, nv_ref = out_refs
    h = _modulate(x_ref[0], g_ref[...], sh_ref[0], sc_ref[0])
    p = jnp.dot(h.astype(BF16), w_ref[...], preferred_element_type=F32)
    cos = cos_ref[...]
    sin = sin_ref[...]
    o_ckv = MLA_Q_LORA
    o_kr = o_ckv + MLA_KV_LORA
    o_nq = o_kr + LANES
    o_nk = o_nq + NA_DIM
    o_nv = o_nk + NA_DIM
    n_pairs = MLA_HEADS // 2
    mla_rope = (1,) if rope else ()

    if want_q:
        cq = _row_rmsnorm(p[:, :MLA_Q_LORA], qag_ref[...])
        qq = jnp.dot(cq.astype(BF16), wuq_ref[...], preferred_element_type=F32)
        _norm_rope_store(_slabs(qq, 0, n_pairs * MLA_PAIR_W), shmla_ref[...], mqg_ref[...], MLA_QK,
                         mq_ref, cos, sin, MLA_ROPE // 4, mla_rope)
        _norm_rope_store(_slabs(p, o_nq, NA_DIM), sh64_ref[...], nqg_ref[...], NA_HEAD_DIM,
                         nq_ref, cos, sin, 0, ())

    ckv = _row_rmsnorm(p[:, o_ckv:o_ckv + MLA_KV_LORA], kvag_ref[...])
    kv = jnp.dot(ckv.astype(BF16), wukv_ref[...], preferred_element_type=F32)
    kr_blk = p[:, o_kr:o_kr + LANES]
    k_slabs = [jnp.concatenate([kv[:, m * LANES:(m + 1) * LANES], kr_blk], axis=1)
               for m in range(n_pairs)]
    _norm_rope_store(k_slabs, shmla_ref[...], mkg_ref[...], MLA_QK, mk_ref, cos, sin,
                     MLA_ROPE // 4, mla_rope)
    v0 = n_pairs * LANES
    mv_ref[0] = kv[:, v0:v0 + MLA_HEADS * MLA_V].astype(BF16)
    _norm_rope_store(_slabs(p, o_nk, NA_DIM), sh64_ref[...], nkg_ref[...], NA_HEAD_DIM,
                     nk_ref, cos, sin, 0, ())
    nv_ref[0] = p[:, o_nv:o_nv + NA_DIM].astype(BF16)


def _inproj1(x, shift, scale, gain, w, qag, kvag, wuq, wukv, sh64, shmla, mqg, mkg, nqg, nkg,
             cos, sin, *, rope, want_q, tm):
    b, t, d = x.shape
    row = lambda bi, i: (bi, i, 0)
    vec = lambda bi, i: (bi, 0, 0)
    const = lambda bi, i: (0, 0)
    full = lambda a: pl.BlockSpec(a.shape, const)
    mla_w = (MLA_HEADS // 2) * MLA_PAIR_W
    outs = [(mla_w, "mk"), (MLA_HEADS * MLA_V, "mv"), (NA_DIM, "nk"), (NA_DIM, "nv")]
    if want_q:
        outs = [(mla_w, "mq")] + outs[:2] + [(NA_DIM, "nq")] + outs[2:]
    return pl.pallas_call(
        functools.partial(_inproj1_kernel, rope=rope, want_q=want_q),
        grid=(b, t // tm),
        in_specs=[pl.BlockSpec((1, tm, d), row),
                  pl.BlockSpec((1, 1, d), vec),
                  pl.BlockSpec((1, 1, d), vec),
                  full(gain), full(w), full(qag), full(kvag), full(wuq), full(wukv),
                  full(sh64), full(shmla), full(mqg), full(mkg), full(nqg), full(nkg),
                  pl.BlockSpec((tm, LANES), lambda bi, i: (i, 0)),
                  pl.BlockSpec((tm, LANES), lambda bi, i: (i, 0))],
        out_specs=[pl.BlockSpec((1, tm, n), row) for n, _ in outs],
        out_shape=[jax.ShapeDtypeStruct((b, t, n), BF16) for n, _ in outs],
        compiler_params=_params(2),
        name="inproj1_lat" if want_q else "inproj1_ctx",
    )(x, shift, scale, gain, w, qag, kvag, wuq, wukv, sh64, shmla, mqg, mkg, nqg, nkg, cos, sin)


def _fill_pair_kv(k, v, mask_a, mask_b, ka_ref, kb_ref, va_ref, vb_ref, r0):
    n = k.shape[0]
    first = _lane(v.shape) < HEAD_DIM
    one = jnp.ones_like(v)
    ka_ref[r0:r0 + n, :] = k * mask_a
    kb_ref[r0:r0 + n, :] = k * mask_b
    va_ref[r0:r0 + n, :] = jnp.where(first, v, one)
    vb_ref[r0:r0 + n, :] = jnp.where(first, one, v)


def _pair_softmax_pv(q_tiles, ka, kb, va, vb):
    scores = [[lax.dot_general(q, k, _NT, preferred_element_type=F32) for k in (ka, kb)]
              for q in q_tiles]
    probs = [[jnp.exp2((s - s.max(axis=-1, keepdims=True)).astype(BF16)) for s in pair]
             for pair in scores]
    outs = []
    for pa, pb in probs:
        na = jnp.dot(pa, va, preferred_element_type=F32)
        nb = jnp.dot(pb, vb, preferred_element_type=F32)
        first = _lane(na.shape) < HEAD_DIM
        num = jnp.where(first, na, nb)
        den = pltpu.roll(jnp.where(first, nb, na), HEAD_DIM, 1)
        outs.append(num / den)
    return outs


def _pair_attn_kernel(*refs, use_lat):
    if use_lat:
        q_ref, ma_ref, mb_ref, kc_ref, vc_ref, kl_ref, vl_ref, o_ref, ka, kb, va, vb = refs
        segs = ((kc_ref, vc_ref), (kl_ref, vl_ref))
    else:
        q_ref, ma_ref, mb_ref, kc_ref, vc_ref, o_ref, ka, kb, va, vb = refs
        segs = ((kc_ref, vc_ref),)

    @pl.when(pl.program_id(2) == 0)
    def _():
        r0 = 0
        for k_ref, v_ref in segs:
            _fill_pair_kv(k_ref[0], v_ref[0], ma_ref[...], mb_ref[...], ka, kb, va, vb, r0)
            r0 += k_ref.shape[1]

    sub = min(q_ref.shape[1], ATTN_SUB)
    starts = range(0, q_ref.shape[1], sub)
    outs = _pair_softmax_pv([q_ref[0, r0:r0 + sub, :] for r0 in starts],
                            ka[...], kb[...], va[...], vb[...])
    for r0, o in zip(starts, outs):
        o_ref[0, r0:r0 + sub, :] = o.astype(o_ref.dtype)


def _pair_attn(q, mask_a, mask_b, kc, vc, kl, vl, *, dk, n_pairs, q_per_pair, tq, name):
    b, t, _ = q.shape
    n_qblk = n_pairs * q_per_pair
    use_lat = kl is not None
    qmap = lambda bi, c, i: (bi, i, c)
    kvmap = lambda bi, c, i: (bi, 0, c // q_per_pair)
    const = lambda bi, c, i: (0, 0)
    in_specs = [pl.BlockSpec((1, tq, dk), qmap),
                pl.BlockSpec((1, dk), const),
                pl.BlockSpec((1, dk), const),
                pl.BlockSpec((1, kc.shape[1], dk), kvmap),
                pl.BlockSpec((1, vc.shape[1], LANES), kvmap)]
    args = [q, mask_a, mask_b, kc, vc]
    if use_lat:
        in_specs += [pl.BlockSpec((1, kl.shape[1], dk), kvmap),
                     pl.BlockSpec((1, vl.shape[1], LANES), kvmap)]
        args += [kl, vl]
    n_keys = kc.shape[1] + (kl.shape[1] if use_lat else 0)
    return pl.pallas_call(
        functools.partial(_pair_attn_kernel, use_lat=use_lat),
        grid=(b, n_qblk, t // tq),
        in_specs=in_specs,
        out_specs=pl.BlockSpec((1, tq, LANES), qmap),
        out_shape=jax.ShapeDtypeStruct((b, t, n_qblk * LANES), BF16),
        scratch_shapes=[pltpu.VMEM((n_keys, dk), BF16), pltpu.VMEM((n_keys, dk), BF16),
                        pltpu.VMEM((n_keys, LANES), BF16), pltpu.VMEM((n_keys, LANES), BF16)],
        compiler_params=_params(3),
        name=name,
    )(*args)


def _na_kernel(q_ref, kc_ref, vc_ref, k_ref, v_ref, bias_ref, o_ref, ka, kb, va, vb,
               *, rows_per_step, n_rows):
    rb = pl.program_id(2)
    tc = kc_ref.shape[1]
    n_win = NA_ROWS * GRID_W

    @pl.when(rb == 0)
    def _():
        lane = _lane((1, LANES))
        mask_a = jnp.where(lane < NA_HEAD_DIM, 1.0, 0.0).astype(BF16)
        mask_b = jnp.where(lane < NA_HEAD_DIM, 0.0, 1.0).astype(BF16)
        _fill_pair_kv(kc_ref[0], vc_ref[0], mask_a, mask_b, ka, kb, va, vb, 0)
        _fill_pair_kv(k_ref[0], v_ref[0], mask_a, mask_b, ka, kb, va, vb, tc)

    heads = ((ka, va), (kb, vb))
    q_all = q_ref[0]
    rows = []
    for i in range(rows_per_step):
        r = rb * rows_per_step + i
        r0 = jnp.clip(r - NA_ROWS // 2, 0, n_rows - NA_ROWS)
        rows.append((pl.multiple_of(tc + r0 * GRID_W, GRID_W), r0 - r + (NA_ROWS - 1)))

    s_ctx = [lax.dot_general(q_all, k_s[0:tc, :], _NT, preferred_element_type=F32)
             for k_s, _ in heads]
    s_win = []
    for i, (kstart, j0) in enumerate(rows):
        q = q_all[i * GRID_W:(i + 1) * GRID_W]
        per_head = []
        for hh, (k_s, _) in enumerate(heads):
            bias = jnp.concatenate([bias_ref[hh, j0 + 2 * c] for c in range(NA_ROWS // 2)], axis=1)
            per_head.append(lax.dot_general(q, k_s[pl.ds(kstart, n_win), :], _NT,
                                            preferred_element_type=F32) + bias)
        s_win.append(per_head)

    p_ctx = [[], []]
    p_win = []
    for i in range(rows_per_step):
        per_head = []
        for hh in range(2):
            sc = s_ctx[hh][i * GRID_W:(i + 1) * GRID_W]
            sw = s_win[i][hh]
            mx = jnp.maximum(sc.max(axis=-1, keepdims=True), sw.max(axis=-1, keepdims=True))
            p_ctx[hh].append(jnp.exp2((sc - mx).astype(BF16)))
            per_head.append(jnp.exp2((sw - mx).astype(BF16)))
        p_win.append(per_head)

    n_ctx = [jnp.dot(jnp.concatenate(p_ctx[hh], axis=0), v_s[0:tc, :], preferred_element_type=F32)
             for hh, (_, v_s) in enumerate(heads)]
    outs = []
    for i, (kstart, _) in enumerate(rows):
        nums = [n_ctx[hh][i * GRID_W:(i + 1) * GRID_W]
                + jnp.dot(p_win[i][hh], v_s[pl.ds(kstart, n_win), :], preferred_element_type=F32)
                for hh, (_, v_s) in enumerate(heads)]
        first = _lane(nums[0].shape) < NA_HEAD_DIM
        num = jnp.where(first, nums[0], nums[1])
        den = pltpu.roll(jnp.where(first, nums[1], nums[0]), NA_HEAD_DIM, 1)
        outs.append((num / den).astype(o_ref.dtype))
    o_ref[0] = jnp.concatenate(outs, axis=0)


def _na_attn(q, kc, vc, k, v, bias, *, rows_per_step):
    b, s, _ = q.shape
    n_rows = s // GRID_W
    n_pairs = NA_HEADS // 2
    tq = rows_per_step * GRID_W
    n_keys = kc.shape[1] + s
    return pl.pallas_call(
        functools.partial(_na_kernel, rows_per_step=rows_per_step, n_rows=n_rows),
        grid=(b, n_pairs, n_rows // rows_per_step),
        in_specs=[pl.BlockSpec((1, tq, LANES), lambda bi, m, i: (bi, i, m)),
                  pl.BlockSpec((1, kc.shape[1], LANES), lambda bi, m, i: (bi, 0, m)),
                  pl.BlockSpec((1, vc.shape[1], LANES), lambda bi, m, i: (bi, 0, m)),
                  pl.BlockSpec((1, s, LANES), lambda bi, m, i: (bi, 0, m)),
                  pl.BlockSpec((1, s, LANES), lambda bi, m, i: (bi, 0, m)),
                  pl.BlockSpec((2,) + bias.shape[1:], lambda bi, m, i: (m, 0, 0, 0))],
        out_specs=pl.BlockSpec((1, tq, LANES), lambda bi, m, i: (bi, i, m)),
        out_shape=jax.ShapeDtypeStruct((b, s, NA_DIM), BF16),
        scratch_shapes=[pltpu.VMEM((n_keys, LANES), BF16) for _ in range(4)],
        compiler_params=_params(3),
        name="na_attn",
    )(q, kc, vc, k, v, bias)


def _pool_kernel(a_ref, pw_ref, ps_ref, o_ref, pad_ref, *, t_len, chunk):
    pad_ref[0:POOL_HALO, :] = jnp.zeros((POOL_HALO, POOL_DIM), F32)
    pad_ref[POOL_HALO + t_len:2 * POOL_HALO + t_len, :] = jnp.zeros((POOL_HALO, POOL_DIM), F32)
    pad_ref[POOL_HALO:POOL_HALO + t_len, :] = a_ref[0]
    lane = _lane((chunk, POOL_DIM))
    g0 = lane < POOL_CH
    g1 = lane < 2 * POOL_CH
    g2 = lane < 3 * POOL_CH
    half_w = jnp.where(g0, 1, jnp.where(g1, 2, jnp.where(g2, 4, 8)))
    for c in range(t_len // chunk):
        base = c * chunk

        def ld(off, base=base):
            return pad_ref[POOL_HALO + base + off:POOL_HALO + base + off + chunk, :]

        a0 = ld(0)
        w2 = ld(-1) + a0
        w4 = w2 + ld(-2) + ld(1)
        w8 = w4 + ld(-4) + ld(-3) + ld(2) + ld(3)
        w16 = w8
        for off in (-8, -7, -6, -5, 4, 5, 6, 7):
            w16 = w16 + ld(off)
        tok = base + lax.broadcasted_iota(jnp.int32, (chunk, POOL_DIM), 0)
        cnt = (jnp.minimum(tok + half_w, t_len) - jnp.maximum(tok - half_w, 0)).astype(F32)
        wsum = jnp.where(g0, w2, jnp.where(g1, w4, jnp.where(g2, w8, w16)))
        dlt = wsum / cnt - a0
        y = jnp.dot(dlt.astype(BF16), pw_ref[...], preferred_element_type=F32) * ps_ref[...]
        o_ref[0, base:base + chunk, :] = y.astype(o_ref.dtype)


def _pool(a, pw_bd, pscale):
    b, t, _ = a.shape
    chunk = min(t, 256)
    return pl.pallas_call(
        functools.partial(_pool_kernel, t_len=t, chunk=chunk),
        grid=(b,),
        in_specs=[pl.BlockSpec((1, t, POOL_DIM), lambda bi: (bi, 0, 0)),
                  pl.BlockSpec((POOL_DIM, POOL_DIM), lambda bi: (0, 0)),
                  pl.BlockSpec((1, POOL_DIM), lambda bi: (0, 0))],
        out_specs=pl.BlockSpec((1, t, POOL_DIM), lambda bi: (bi, 0, 0)),
        out_shape=jax.ShapeDtypeStruct((b, t, POOL_DIM), BF16),
        scratch_shapes=[pltpu.VMEM((t + 2 * POOL_HALO, POOL_DIM), F32)],
        compiler_params=_params(1),
        name="pool",
    )(a, pw_bd, pscale)


def _outproj_kernel(x_ref, gate_ref, m0_ref, m1_ref, w0_ref, w1_ref, o_ref):
    acc = jnp.dot(m0_ref[0], w0_ref[...], preferred_element_type=F32)
    acc = acc + jnp.dot(m1_ref[0], w1_ref[...], preferred_element_type=F32)
    o_ref[0] = x_ref[0] + gate_ref[0] * acc


def _outproj(x, gate, m0, m1, w0, w1, *, tm):
    b, t, d = x.shape
    row = lambda bi, i: (bi, i, 0)
    vec = lambda bi, i: (bi, 0, 0)
    const = lambda bi, i: (0, 0)
    return pl.pallas_call(
        _outproj_kernel,
        grid=(b, t // tm),
        in_specs=[pl.BlockSpec((1, tm, d), row),
                  pl.BlockSpec((1, 1, d), vec),
                  pl.BlockSpec((1, tm, m0.shape[2]), row),
                  pl.BlockSpec((1, tm, m1.shape[2]), row),
                  pl.BlockSpec(w0.shape, const),
                  pl.BlockSpec(w1.shape, const)],
        out_specs=pl.BlockSpec((1, tm, d), row),
        out_shape=jax.ShapeDtypeStruct((b, t, d), F32),
        compiler_params=_params(2),
        name="outproj",
    )(x, gate, m0, m1, w0, w1)


def _ffn_kernel(x_ref, sh_ref, sc_ref, gate_ref, g_ref, wg_ref, wu_ref, wd_ref, o_ref):
    x = x_ref[0]
    h = _modulate(x, g_ref[...], sh_ref[0], sc_ref[0]).astype(BF16)
    n_chunks = wg_ref.shape[1] // FFN_CHUNK

    def gate_up(c):
        cols = slice(c * FFN_CHUNK, (c + 1) * FFN_CHUNK)
        return (jnp.dot(h, wg_ref[:, cols], preferred_element_type=F32),
                jnp.dot(h, wu_ref[:, cols], preferred_element_type=F32))

    acc = None
    gt, up = gate_up(0)
    for c in range(n_chunks):
        nxt = gate_up(c + 1) if c + 1 < n_chunks else None
        act = (gt * jax.nn.sigmoid(gt) * up).astype(BF16)
        part = jnp.dot(act, wd_ref[c * FFN_CHUNK:(c + 1) * FFN_CHUNK, :], preferred_element_type=F32)
        acc = part if acc is None else acc + part
        if nxt is not None:
            gt, up = nxt
    o_ref[0] = x + gate_ref[0] * acc


def _ffn(x, shift, scale, gate, gain, wg, wu, wd, *, tm):
    b, t, d = x.shape
    row = lambda bi, i: (bi, i, 0)
    vec = lambda bi, i: (bi, 0, 0)
    const = lambda bi, i: (0, 0)
    return pl.pallas_call(
        _ffn_kernel,
        grid=(b, t // tm),
        in_specs=[pl.BlockSpec((1, tm, d), row),
                  pl.BlockSpec((1, 1, d), vec),
                  pl.BlockSpec((1, 1, d), vec),
                  pl.BlockSpec((1, 1, d), vec),
                  pl.BlockSpec((1, d), const),
                  pl.BlockSpec(wg.shape, const),
                  pl.BlockSpec(wu.shape, const),
                  pl.BlockSpec(wd.shape, const)],
        out_specs=pl.BlockSpec((1, tm, d), row),
        out_shape=jax.ShapeDtypeStruct((b, t, d), F32),
        compiler_params=_params(2),
        name="ffn",
    )(x, shift, scale, gate, gain, wg, wu, wd)


def _rope_tables(seq, head_half, n_rep):
    tok = np.arange(seq)
    inv = (ROPE_THETA ** (-np.arange(head_half, dtype=np.float32) * 2.0 / (2 * head_half))).astype(np.float32)
    cos = np.ones((seq, LANES), np.float64)
    sin = np.zeros((seq, LANES), np.float64)
    for rep in range(n_rep):
        for part, pos in enumerate((tok // GRID_W, tok % GRID_W)):
            ang = pos.astype(np.float32).astype(np.float64)[:, None] * inv.astype(np.float64)[None, :]
            o = rep * 4 * head_half + part * 2 * head_half
            cos[:, o:o + head_half] = np.cos(ang)
            cos[:, o + head_half:o + 2 * head_half] = np.cos(ang)
            sin[:, o:o + head_half] = -np.sin(ang)
            sin[:, o + head_half:o + 2 * head_half] = np.sin(ang)
    return jnp.asarray(cos, F32), jnp.asarray(sin, F32)


def _na_bias_table(rpb):
    cols = np.arange(GRID_W)
    c0 = np.clip(cols - NA_COLS // 2, 0, GRID_W - NA_COLS)
    kc = np.arange(GRID_W)
    inside = (kc[None, :] >= c0[:, None]) & (kc[None, :] < c0[:, None] + NA_COLS)
    dc = np.clip(kc[None, :] - cols[:, None] + (NA_COLS - 1), 0, 2 * NA_COLS - 2)
    t = jnp.where(jnp.asarray(inside)[None, None], rpb[:, :, dc] * LOG2E, NEG_BIG)
    return jnp.concatenate([t[:, :-1], t[:, 1:]], axis=-1).astype(F32)


def _bcast_rows(vec, b):
    return jnp.broadcast_to(vec.reshape(1, 1, -1), (b, 1, vec.shape[-1]))


def kernel(x, c, ctx, c_ctx, l0_ada_w, l0_ada_b, l0_norm_mix, l0_norm_ffn, l0_w_in, l0_pool_w, l0_pool_scale, l0_q_gain, l0_k_gain, l0_w_out, l0_ffn_w_gate, l0_ffn_w_up, l0_ffn_w_down, l1_ada_w, l1_ada_b, l1_norm_mix, l1_norm_ffn, l1_w_in, l1_mla_q_a_gain, l1_mla_kv_a_gain, l1_mla_w_uq, l1_mla_w_ukv, l1_mla_q_gain, l1_mla_k_gain, l1_na_q_gain, l1_na_k_gain, l1_na_rpb, l1_w_out, l1_ffn_w_gate, l1_ffn_w_up, l1_ffn_w_down):
    b, s, d = x.shape
    tc = ctx.shape[1]
    tm_lat = 512
    tq_lat = 512

    cond = jnp.concatenate([c, c_ctx[None, :], jnp.zeros((7, d), F32)], axis=0)

    def mods(ada_w, ada_b):
        m = _ada(cond, ada_w, ada_b)
        lat = [m[:b, None, i * d:(i + 1) * d] for i in range(6)]
        cx = [_bcast_rows(m[b, i * d:(i + 1) * d], b) for i in range(6)]
        return lat, cx

    def ffn_weights(wg, wu, wd):
        return wg.astype(BF16), wu.astype(BF16), wd.astype(BF16)

    row1 = lambda v: v.reshape(1, -1)
    tile4 = lambda v, mul: (jnp.tile(v, SLAB // v.shape[0]) * mul).reshape(1, -1)
    sh64 = jnp.asarray(np.kron(np.eye(SLAB // HEAD_DIM), np.ones((HEAD_DIM, HEAD_DIM))), BF16)

    def pair_order(w_q, axis):
        shp = w_q.shape
        split = shp[:axis] + (GQA_KV_HEADS // 2, 2, GQA_GROUP, HEAD_DIM) + shp[axis + 1:]
        perm = list(range(len(split)))
        perm[axis + 1], perm[axis + 2] = perm[axis + 2], perm[axis + 1]
        return w_q.reshape(split).transpose(perm).reshape(shp)

    (sh1, sc1, g1, sh2, sc2, g2), (csh1, csc1, cg1, csh2, csc2, cg2) = mods(l0_ada_w, l0_ada_b)
    w_in0 = jnp.concatenate([l0_w_in[:, :POOL_DIM],
                             pair_order(l0_w_in[:, POOL_DIM:POOL_DIM + GQA_Q_DIM], 1),
                             l0_w_in[:, POOL_DIM + GQA_Q_DIM:]], axis=1).astype(BF16)
    cos0, sin0 = _rope_tables(s, HEAD_DIM // 4, 2)
    ones_c = jnp.ones((tc, LANES), F32)
    zeros_c = jnp.zeros((tc, LANES), F32)
    qg4 = tile4(l0_q_gain, HEAD_DIM ** 0.5 * GQA_QSCALE)
    kg4 = tile4(l0_k_gain, HEAD_DIM ** 0.5)
    gmask_a = jnp.asarray(np.concatenate([np.ones(HEAD_DIM), np.zeros(HEAD_DIM)])[None], BF16)
    gmask_b = 1 - gmask_a

    a_l, q_l, k_l, v_l = _inproj0(x, sh1, sc1, row1(l0_norm_mix), w_in0, sh64, qg4, kg4, cos0, sin0,
                                  rope=True, tm=tm_lat)
    a_c, q_c, k_c, v_c = _inproj0(ctx, csh1, csc1, row1(l0_norm_mix), w_in0, sh64, qg4, kg4,
                                  ones_c, zeros_c, rope=False, tm=tc)
    n_kv_pairs = GQA_KV_HEADS // 2
    attn_l = _pair_attn(q_l, gmask_a, gmask_b, k_c, v_c, k_l, v_l, dk=LANES, n_pairs=n_kv_pairs,
                        q_per_pair=GQA_GROUP, tq=tq_lat, name="gqa_lat")
    attn_c = _pair_attn(q_c, gmask_a, gmask_b, k_c, v_c, None, None, dk=LANES, n_pairs=n_kv_pairs,
                        q_per_pair=GQA_GROUP, tq=tc, name="gqa_ctx")

    eye = jnp.eye(POOL_GROUPS, dtype=F32)
    pw_bd = (eye[:, None, :, None] * l0_pool_w[:, :, None, :]).reshape(POOL_DIM, POOL_DIM).astype(BF16)
    pool_l = _pool(a_l, pw_bd, row1(l0_pool_scale))
    pool_c = _pool(a_c, pw_bd, row1(l0_pool_scale))

    w_out0_pool = l0_w_out[:POOL_DIM].astype(BF16)
    w_out0_attn = pair_order(l0_w_out[POOL_DIM:], 0).astype(BF16)
    ffn0 = ffn_weights(l0_ffn_w_gate, l0_ffn_w_up, l0_ffn_w_down)

    x1 = _outproj(x, g1, pool_l, attn_l, w_out0_pool, w_out0_attn, tm=tm_lat)
    x1 = _ffn(x1, sh2, sc2, g2, row1(l0_norm_ffn), *ffn0, tm=tm_lat)
    xc = _outproj(ctx, cg1, pool_c, attn_c, w_out0_pool, w_out0_attn, tm=tc)
    xc = _ffn(xc, csh2, csc2, cg2, row1(l0_norm_ffn), *ffn0, tm=tc)

    (sh1, sc1, g1, sh2, sc2, g2), (csh1, csc1, _, _, _, _) = mods(l1_ada_w, l1_ada_b)
    o_kr = MLA_Q_LORA + MLA_KV_LORA
    kr_cols = l1_w_in[:, o_kr:o_kr + MLA_ROPE]
    w_in1 = jnp.concatenate([l1_w_in[:, :o_kr], kr_cols, kr_cols,
                             jnp.zeros((d, LANES - 2 * MLA_ROPE), F32),
                             l1_w_in[:, o_kr + MLA_ROPE:]], axis=1).astype(BF16)
    n_pairs = MLA_HEADS // 2
    uq = l1_mla_w_uq.reshape(MLA_Q_LORA, MLA_HEADS, MLA_QK)
    ukv = l1_mla_w_ukv.reshape(MLA_KV_LORA, MLA_HEADS, MLA_NOPE + MLA_V)
    zpad = jnp.zeros((MLA_Q_LORA, LANES - 2 * MLA_ROPE), F32)
    wuq = jnp.concatenate(
        [blk for m in range(n_pairs) for blk in
         (uq[:, 2 * m, :MLA_NOPE], uq[:, 2 * m + 1, :MLA_NOPE],
          uq[:, 2 * m, MLA_NOPE:], uq[:, 2 * m + 1, MLA_NOPE:], zpad)], axis=1).astype(BF16)
    wukv = jnp.concatenate(
        [ukv[:, :, :MLA_NOPE].reshape(MLA_KV_LORA, -1), ukv[:, :, MLA_NOPE:].reshape(MLA_KV_LORA, -1)],
        axis=1).astype(BF16)

    def mla_gain(g, mul):
        zr = jnp.zeros((LANES - 2 * MLA_ROPE,), F32)
        return (jnp.concatenate([g[:MLA_NOPE], g[:MLA_NOPE], g[MLA_NOPE:], g[MLA_NOPE:], zr])
                * mul).reshape(1, -1)

    mla_head = np.concatenate([np.zeros(MLA_NOPE), np.ones(MLA_NOPE), np.zeros(MLA_ROPE),
                               np.ones(MLA_ROPE), -np.ones(LANES - 2 * MLA_ROPE)])
    shmla = jnp.asarray((mla_head[:, None] == mla_head[None, :]) & (mla_head[:, None] >= 0), BF16)
    mqg = mla_gain(l1_mla_q_gain, MLA_QK ** 0.5 * MLA_QSCALE)
    mkg = mla_gain(l1_mla_k_gain, MLA_QK ** 0.5)
    nqg = tile4(l1_na_q_gain, NA_HEAD_DIM ** 0.5 * NA_QSCALE)
    nkg = tile4(l1_na_k_gain, NA_HEAD_DIM ** 0.5)
    cos1, sin1 = _rope_tables(s, MLA_ROPE // 4, 2)
    common = (row1(l1_norm_mix), w_in1, row1(l1_mla_q_a_gain), row1(l1_mla_kv_a_gain), wuq, wukv,
              sh64, shmla, mqg, mkg, nqg, nkg)
    mq, mk, mv, nq, nk, nv = _inproj1(x1, sh1, sc1, *common, cos1, sin1, rope=True, want_q=True,
                                      tm=tm_lat)
    mkc, mvc, nkc, nvc = _inproj1(xc, csh1, csc1, *common, ones_c, zeros_c, rope=False, want_q=False,
                                  tm=tc)

    mm_a = np.zeros((1, MLA_PAIR_W), np.float32)
    mm_b = np.zeros((1, MLA_PAIR_W), np.float32)
    mm_a[0, :MLA_NOPE] = 1
    mm_a[0, LANES:LANES + MLA_ROPE] = 1
    mm_b[0, MLA_NOPE:LANES] = 1
    mm_b[0, LANES + MLA_ROPE:LANES + 2 * MLA_ROPE] = 1
    o_mla = _pair_attn(mq, jnp.asarray(mm_a, BF16), jnp.asarray(mm_b, BF16), mkc, mvc, mk, mv,
                       dk=MLA_PAIR_W, n_pairs=n_pairs, q_per_pair=1, tq=tq_lat, name="mla")
    o_na = _na_attn(nq, nkc, nvc, nk, nv, _na_bias_table(l1_na_rpb), rows_per_step=8)

    w_out1 = l1_w_out.astype(BF16)
    x2 = _outproj(x1, g1, o_mla, o_na, w_out1[:MLA_HEADS * MLA_V], w_out1[MLA_HEADS * MLA_V:], tm=tm_lat)
    ffn1 = ffn_weights(l1_ffn_w_gate, l1_ffn_w_up, l1_ffn_w_down)
    return _ffn(x2, sh2, sc2, g2, row1(l1_norm_ffn), *ffn1, tm=tm_lat)
```

```python
import functools

import numpy as np
import jax
import jax.numpy as jnp
from jax import lax
from jax.experimental import pallas as pl
from jax.experimental.pallas import tpu as pltpu

F32 = jnp.float32
BF16 = jnp.bfloat16

D_MODEL = 1024
GRID_W = 64
ROPE_THETA = 10000.0
EPS = 1e-6
LANES = 128
SLAB = 256

POOL_GROUPS = 4
POOL_CH = 64
POOL_DIM = POOL_GROUPS * POOL_CH
POOL_HALO = 16

HEAD_DIM = 64
GQA_HEADS = 12
GQA_KV_HEADS = 4
GQA_GROUP = GQA_HEADS // GQA_KV_HEADS
GQA_Q_DIM = GQA_HEADS * HEAD_DIM
GQA_KV_DIM = GQA_KV_HEADS * HEAD_DIM

MLA_HEADS = 8
MLA_NOPE = 64
MLA_ROPE = 32
MLA_QK = MLA_NOPE + MLA_ROPE
MLA_V = 64
MLA_Q_LORA = 384
MLA_KV_LORA = 256
MLA_PAIR_W = 2 * LANES

NA_HEADS = 8
NA_HEAD_DIM = 64
NA_DIM = NA_HEADS * NA_HEAD_DIM
NA_ROWS = 8
NA_COLS = 16

FFN_DIM = -(-8 * D_MODEL // (3 * 256)) * 256
FFN_CHUNK = 256

NEG_BIG = -1e30
LOG2E = 1.4426950408889634
GQA_QSCALE = HEAD_DIM ** -0.5 * LOG2E
MLA_QSCALE = MLA_QK ** -0.5 * LOG2E
NA_QSCALE = NA_HEAD_DIM ** -0.5 * LOG2E
ATTN_SUB = 256

VMEM_LIMIT = 56 * 1024 * 1024

_NT = (((1,), (1,)), ((), ()))


def _params(n_axes):
    return pltpu.CompilerParams(dimension_semantics=("arbitrary",) * n_axes,
                                vmem_limit_bytes=VMEM_LIMIT)


def _modulate(x, gain, shift, scale):
    ms = jnp.mean(x * x, axis=-1, keepdims=True)
    return x * lax.rsqrt(ms + EPS) * gain * (1.0 + scale) + shift


def _row_rmsnorm(x, gain):
    ms = jnp.mean(x * x, axis=-1, keepdims=True)
    return x * lax.rsqrt(ms + EPS) * gain


def _lane(shape):
    return lax.broadcasted_iota(jnp.int32, shape, 1)


def _head_rmsnorm(slab, same_head, gain, n):
    ssq = jnp.dot((slab * slab).astype(BF16), same_head, preferred_element_type=F32)
    return slab * lax.rsqrt(ssq + n * EPS) * gain


def _rope(blk, cos, sin_signed, half):
    first = (_lane(blk.shape) & (2 * half - 1)) < half
    rot = jnp.where(first, pltpu.roll(blk, LANES - half, 1), pltpu.roll(blk, half, 1))
    return blk * cos + rot * sin_signed


def _ada_kernel(c_ref, w_ref, b_ref, o_ref):
    cnd = c_ref[...]
    act = cnd * jax.nn.sigmoid(cnd)
    o_ref[...] = jnp.dot(act, w_ref[...], preferred_element_type=F32,
                         precision=lax.Precision.HIGHEST) + b_ref[...]


def _ada(cond, w, b):
    n_rows, d = cond.shape
    n = w.shape[1]
    tn = 1024
    return pl.pallas_call(
        _ada_kernel,
        grid=(n // tn,),
        in_specs=[pl.BlockSpec((n_rows, d), lambda j: (0, 0)),
                  pl.BlockSpec((d, tn), lambda j: (0, j)),
                  pl.BlockSpec((1, tn), lambda j: (0, j))],
        out_specs=pl.BlockSpec((n_rows, tn), lambda j: (0, j)),
        out_shape=jax.ShapeDtypeStruct((n_rows, n), F32),
        compiler_params=_params(1),
        name="ada",
    )(cond, w, b.reshape(1, n))


def _slabs(src, c0, width):
    return [src[:, c0 + j * SLAB:c0 + (j + 1) * SLAB] for j in range(width // SLAB)]


def _norm_rope_store(slabs, same_head, gain, n, dst_ref, cos, sin, half, rope_blocks):
    for j, slab in enumerate(slabs):
        slab = _head_rmsnorm(slab, same_head, gain, n)
        for blk in range(SLAB // LANES):
            part = slab[:, blk * LANES:(blk + 1) * LANES]
            if blk in rope_blocks:
                part = _rope(part, cos, sin, half)
            o = j * SLAB + blk * LANES
            dst_ref[0, :, o:o + LANES] = part.astype(BF16)


def _inproj0_kernel(x_ref, sh_ref, sc_ref, g_ref, w_ref, sh64_ref, qg_ref, kg_ref, cos_ref, sin_ref,
                    a_ref, q_ref, k_ref, v_ref, *, rope):
    h = _modulate(x_ref[0], g_ref[...], sh_ref[0, 0], sc_ref[0, 0])
    p = jnp.dot(h.astype(BF16), w_ref[...], preferred_element_type=F32)
    a_ref[0] = p[:, :POOL_DIM]
    rope_blocks = (0, 1) if rope else ()
    q0 = POOL_DIM
    k0 = q0 + GQA_Q_DIM
    v0 = k0 + GQA_KV_DIM
    for c0, width, gain_ref, dst in ((q0, GQA_Q_DIM, qg_ref, q_ref), (k0, GQA_KV_DIM, kg_ref, k_ref)):
        _norm_rope_store(_slabs(p, c0, width), sh64_ref[...], gain_ref[...], HEAD_DIM, dst,
                         cos_ref[...], sin_ref[...], HEAD_DIM // 4, rope_blocks)
    v_ref[0] = p[:, v0:v0 + GQA_KV_DIM].astype(BF16)


def _inproj0(x, mods, ctx_row, gain, w, same_head, q_gain, k_gain, cos, sin, *, rope, tm):
    b, t, d = x.shape
    row = lambda bi, i: (bi, i, 0)
    return pl.pallas_call(
        functools.partial(_inproj0_kernel, rope=rope),
        grid=(b, t // tm),
        in_specs=[pl.BlockSpec((1, tm, d), row),
                  _mod_spec(mods, 0, ctx_row), _mod_spec(mods, 1, ctx_row),
                  _resident(gain), _resident(w), _resident(same_head),
                  _resident(q_gain), _resident(k_gain),
                  pl.BlockSpec((tm, LANES), lambda bi, i: (i, 0)),
                  pl.BlockSpec((tm, LANES), lambda bi, i: (i, 0))],
        out_specs=[pl.BlockSpec((1, tm, POOL_DIM), row),
                   pl.BlockSpec((1, tm, GQA_Q_DIM), row),
                   pl.BlockSpec((1, tm, GQA_KV_DIM), row),
                   pl.BlockSpec((1, tm, GQA_KV_DIM), row)],
        out_shape=[jax.ShapeDtypeStruct((b, t, POOL_DIM), F32),
                   jax.ShapeDtypeStruct((b, t, GQA_Q_DIM), BF16),
                   jax.ShapeDtypeStruct((b, t, GQA_KV_DIM), BF16),
                   jax.ShapeDtypeStruct((b, t, GQA_KV_DIM), BF16)],
        compiler_params=_params(2),
        name="inproj0_rope" if rope else "inproj0_ctx",
    )(x, mods, mods, gain, w, same_head, q_gain, k_gain, cos, sin)


def _inproj1_kernel(x_ref, sh_ref, sc_ref, g_ref, w_ref, qag_ref, kvag_ref, wuq_ref, wukv_ref,
                    sh64_ref, shmla_ref, mqg_ref, mkg_ref, nqg_ref, nkg_ref, cos_ref, sin_ref,
                    *out_refs, rope, want_q):
    if want_q:
        mq_ref, mk_ref, mv_ref, nq_ref, nk_ref, nv_ref = out_refs
    else:
        mk_ref, mv_ref, nk_ref, nv_ref = out_refs
    h = _modulate(x_ref[0], g_ref[...], sh_ref[0, 0], sc_ref[0, 0])
    p = jnp.dot(h.astype(BF16), w_ref[...], preferred_element_type=F32)
    cos = cos_ref[...]
    sin = sin_ref[...]
    o_ckv = MLA_Q_LORA
    o_kr = o_ckv + MLA_KV_LORA
    o_nq = o_kr + LANES
    o_nk = o_nq + NA_DIM
    o_nv = o_nk + NA_DIM
    n_pairs = MLA_HEADS // 2
    mla_rope = (1,) if rope else ()

    if want_q:
        cq = _row_rmsnorm(p[:, :MLA_Q_LORA], qag_ref[...])
        qq = jnp.dot(cq.astype(BF16), wuq_ref[...], preferred_element_type=F32)
        _norm_rope_store(_slabs(qq, 0, n_pairs * MLA_PAIR_W), shmla_ref[...], mqg_ref[...], MLA_QK,
                         mq_ref, cos, sin, MLA_ROPE // 4, mla_rope)
        _norm_rope_store(_slabs(p, o_nq, NA_DIM), sh64_ref[...], nqg_ref[...], NA_HEAD_DIM,
                         nq_ref, cos, sin, 0, ())

    ckv = _row_rmsnorm(p[:, o_ckv:o_ckv + MLA_KV_LORA], kvag_ref[...])
    kv = jnp.dot(ckv.astype(BF16), wukv_ref[...], preferred_element_type=F32)
    kr_blk = p[:, o_kr:o_kr + LANES]
    k_slabs = [jnp.concatenate([kv[:, m * LANES:(m + 1) * LANES], kr_blk], axis=1)
               for m in range(n_pairs)]
    _norm_rope_store(k_slabs, shmla_ref[...], mkg_ref[...], MLA_QK, mk_ref, cos, sin,
                     MLA_ROPE // 4, mla_rope)
    v0 = n_pairs * LANES
    mv_ref[0] = kv[:, v0:v0 + MLA_HEADS * MLA_V].astype(BF16)
    _norm_rope_store(_slabs(p, o_nk, NA_DIM), sh64_ref[...], nkg_ref[...], NA_HEAD_DIM,
                     nk_ref, cos, sin, 0, ())
    nv_ref[0] = p[:, o_nv:o_nv + NA_DIM].astype(BF16)


def _inproj1(x, mods, ctx_row, gain, w, qag, kvag, wuq, wukv, sh64, shmla, mqg, mkg, nqg, nkg,
             cos, sin, *, rope, want_q, tm):
    b, t, d = x.shape
    row = lambda bi, i: (bi, i, 0)
    full = _resident
    mla_w = (MLA_HEADS // 2) * MLA_PAIR_W
    outs = [(mla_w, "mk"), (MLA_HEADS * MLA_V, "mv"), (NA_DIM, "nk"), (NA_DIM, "nv")]
    if want_q:
        outs = [(mla_w, "mq")] + outs[:2] + [(NA_DIM, "nq")] + outs[2:]
    return pl.pallas_call(
        functools.partial(_inproj1_kernel, rope=rope, want_q=want_q),
        grid=(b, t // tm),
        in_specs=[pl.BlockSpec((1, tm, d), row),
                  _mod_spec(mods, 0, ctx_row), _mod_spec(mods, 1, ctx_row),
                  full(gain), full(w), full(qag), full(kvag), full(wuq), full(wukv),
                  full(sh64), full(shmla), full(mqg), full(mkg), full(nqg), full(nkg),
                  pl.BlockSpec((tm, LANES), lambda bi, i: (i, 0)),
                  pl.BlockSpec((tm, LANES), lambda bi, i: (i, 0))],
        out_specs=[pl.BlockSpec((1, tm, n), row) for n, _ in outs],
        out_shape=[jax.ShapeDtypeStruct((b, t, n), BF16) for n, _ in outs],
        compiler_params=_params(2),
        name="inproj1_lat" if want_q else "inproj1_ctx",
    )(x, mods, mods, gain, w, qag, kvag, wuq, wukv, sh64, shmla, mqg, mkg, nqg, nkg, cos, sin)


def _fill_pair_kv(k, v, mask_a, mask_b, ka_ref, kb_ref, va_ref, vb_ref, r0):
    n = k.shape[0]
    first = _lane(v.shape) < HEAD_DIM
    one = jnp.ones_like(v)
    ka_ref[r0:r0 + n, :] = k * mask_a
    kb_ref[r0:r0 + n, :] = k * mask_b
    va_ref[r0:r0 + n, :] = jnp.where(first, v, one)
    vb_ref[r0:r0 + n, :] = jnp.where(first, one, v)


def _pair_softmax_pv(q_tiles, ka, kb, va, vb):
    scores = [[lax.dot_general(q, k, _NT, preferred_element_type=F32) for k in (ka, kb)]
              for q in q_tiles]
    probs = [[jnp.exp2((s - s.max(axis=-1, keepdims=True)).astype(BF16)) for s in pair]
             for pair in scores]
    outs = []
    for pa, pb in probs:
        na = jnp.dot(pa, va, preferred_element_type=F32)
        nb = jnp.dot(pb, vb, preferred_element_type=F32)
        first = _lane(na.shape) < HEAD_DIM
        num = jnp.where(first, na, nb)
        den = pltpu.roll(jnp.where(first, nb, na), HEAD_DIM, 1)
        outs.append(num / den)
    return outs


def _pair_attn_kernel(*refs, use_lat):
    if use_lat:
        q_ref, ma_ref, mb_ref, kc_ref, vc_ref, kl_ref, vl_ref, o_ref, ka, kb, va, vb = refs
        segs = ((kc_ref, vc_ref), (kl_ref, vl_ref))
    else:
        q_ref, ma_ref, mb_ref, kc_ref, vc_ref, o_ref, ka, kb, va, vb = refs
        segs = ((kc_ref, vc_ref),)

    @pl.when(pl.program_id(2) == 0)
    def _():
        r0 = 0
        for k_ref, v_ref in segs:
            _fill_pair_kv(k_ref[0], v_ref[0], ma_ref[...], mb_ref[...], ka, kb, va, vb, r0)
            r0 += k_ref.shape[1]

    sub = min(q_ref.shape[1], ATTN_SUB)
    starts = range(0, q_ref.shape[1], sub)
    outs = _pair_softmax_pv([q_ref[0, r0:r0 + sub, :] for r0 in starts],
                            ka[...], kb[...], va[...], vb[...])
    for r0, o in zip(starts, outs):
        o_ref[0, r0:r0 + sub, :] = o.astype(o_ref.dtype)


def _pair_attn(q, mask_a, mask_b, kc, vc, kl, vl, *, dk, n_pairs, q_per_pair, tq, name):
    b, t, _ = q.shape
    n_qblk = n_pairs * q_per_pair
    use_lat = kl is not None
    qmap = lambda bi, c, i: (bi, i, c)
    kvmap = lambda bi, c, i: (bi, 0, c // q_per_pair)
    const = lambda bi, c, i: (0, 0)
    in_specs = [pl.BlockSpec((1, tq, dk), qmap),
                pl.BlockSpec((1, dk), const),
                pl.BlockSpec((1, dk), const),
                pl.BlockSpec((1, kc.shape[1], dk), kvmap),
                pl.BlockSpec((1, vc.shape[1], LANES), kvmap)]
    args = [q, mask_a, mask_b, kc, vc]
    if use_lat:
        in_specs += [pl.BlockSpec((1, kl.shape[1], dk), kvmap),
                     pl.BlockSpec((1, vl.shape[1], LANES), kvmap)]
        args += [kl, vl]
    n_keys = kc.shape[1] + (kl.shape[1] if use_lat else 0)
    return pl.pallas_call(
        functools.partial(_pair_attn_kernel, use_lat=use_lat),
        grid=(b, n_qblk, t // tq),
        in_specs=in_specs,
        out_specs=pl.BlockSpec((1, tq, LANES), qmap),
        out_shape=jax.ShapeDtypeStruct((b, t, n_qblk * LANES), BF16),
        scratch_shapes=[pltpu.VMEM((n_keys, dk), BF16), pltpu.VMEM((n_keys, dk), BF16),
                        pltpu.VMEM((n_keys, LANES), BF16), pltpu.VMEM((n_keys, LANES), BF16)],
        compiler_params=_params(3),
        name=name,
    )(*args)


def _na_kernel(q_ref, kc_ref, vc_ref, k_ref, v_ref, bias_ref, o_ref, ka, kb, va, vb,
               *, rows_per_step, n_rows):
    rb = pl.program_id(2)
    tc = kc_ref.shape[1]
    n_win = NA_ROWS * GRID_W

    @pl.when(rb == 0)
    def _():
        lane = _lane((1, LANES))
        mask_a = jnp.where(lane < NA_HEAD_DIM, 1.0, 0.0).astype(BF16)
        mask_b = jnp.where(lane < NA_HEAD_DIM, 0.0, 1.0).astype(BF16)
        _fill_pair_kv(kc_ref[0], vc_ref[0], mask_a, mask_b, ka, kb, va, vb, 0)
        _fill_pair_kv(k_ref[0], v_ref[0], mask_a, mask_b, ka, kb, va, vb, tc)

    heads = ((ka, va), (kb, vb))
    q_all = q_ref[0]
    rows = []
    for i in range(rows_per_step):
        r = rb * rows_per_step + i
        r0 = jnp.clip(r - NA_ROWS // 2, 0, n_rows - NA_ROWS)
        rows.append((pl.multiple_of(tc + r0 * GRID_W, GRID_W), r0 - r + (NA_ROWS - 1)))

    s_ctx = [lax.dot_general(q_all, k_s[0:tc, :], _NT, preferred_element_type=F32)
             for k_s, _ in heads]
    s_win = []
    for i, (kstart, j0) in enumerate(rows):
        q = q_all[i * GRID_W:(i + 1) * GRID_W]
        per_head = []
        for hh, (k_s, _) in enumerate(heads):
            bias = jnp.concatenate([bias_ref[hh, j0 + 2 * c] for c in range(NA_ROWS // 2)], axis=1)
            per_head.append(lax.dot_general(q, k_s[pl.ds(kstart, n_win), :], _NT,
                                            preferred_element_type=F32) + bias)
        s_win.append(per_head)

    p_ctx = [[], []]
    p_win = []
    for i in range(rows_per_step):
        per_head = []
        for hh in range(2):
            sc = s_ctx[hh][i * GRID_W:(i + 1) * GRID_W]
            sw = s_win[i][hh]
            mx = jnp.maximum(sc.max(axis=-1, keepdims=True), sw.max(axis=-1, keepdims=True))
            p_ctx[hh].append(jnp.exp2((sc - mx).astype(BF16)))
            per_head.append(jnp.exp2((sw - mx).astype(BF16)))
        p_win.append(per_head)

    n_ctx = [jnp.dot(jnp.concatenate(p_ctx[hh], axis=0), v_s[0:tc, :], preferred_element_type=F32)
             for hh, (_, v_s) in enumerate(heads)]
    outs = []
    for i, (kstart, _) in enumerate(rows):
        nums = [n_ctx[hh][i * GRID_W:(i + 1) * GRID_W]
                + jnp.dot(p_win[i][hh], v_s[pl.ds(kstart, n_win), :], preferred_element_type=F32)
                for hh, (_, v_s) in enumerate(heads)]
        first = _lane(nums[0].shape) < NA_HEAD_DIM
        num = jnp.where(first, nums[0], nums[1])
        den = pltpu.roll(jnp.where(first, nums[1], nums[0]), NA_HEAD_DIM, 1)
        outs.append((num / den).astype(o_ref.dtype))
    o_ref[0] = jnp.concatenate(outs, axis=0)


def _na_attn(q, kc, vc, k, v, bias, *, rows_per_step):
    b, s, _ = q.shape
    n_rows = s // GRID_W
    n_pairs = NA_HEADS // 2
    tq = rows_per_step * GRID_W
    n_keys = kc.shape[1] + s
    return pl.pallas_call(
        functools.partial(_na_kernel, rows_per_step=rows_per_step, n_rows=n_rows),
        grid=(b, n_pairs, n_rows // rows_per_step),
        in_specs=[pl.BlockSpec((1, tq, LANES), lambda bi, m, i: (bi, i, m)),
                  pl.BlockSpec((1, kc.shape[1], LANES), lambda bi, m, i: (bi, 0, m)),
                  pl.BlockSpec((1, vc.shape[1], LANES), lambda bi, m, i: (bi, 0, m)),
                  pl.BlockSpec((1, s, LANES), lambda bi, m, i: (bi, 0, m)),
                  pl.BlockSpec((1, s, LANES), lambda bi, m, i: (bi, 0, m)),
                  pl.BlockSpec((2,) + bias.shape[1:], lambda bi, m, i: (m, 0, 0, 0))],
        out_specs=pl.BlockSpec((1, tq, LANES), lambda bi, m, i: (bi, i, m)),
        out_shape=jax.ShapeDtypeStruct((b, s, NA_DIM), BF16),
        scratch_shapes=[pltpu.VMEM((n_keys, LANES), BF16) for _ in range(4)],
        compiler_params=_params(3),
        name="na_attn",
    )(q, kc, vc, k, v, bias)


def _pool_kernel(a_ref, pw_ref, ps_ref, o_ref, pad_ref, *, t_len, chunk):
    pad_ref[0:POOL_HALO, :] = jnp.zeros((POOL_HALO, POOL_DIM), F32)
    pad_ref[POOL_HALO + t_len:2 * POOL_HALO + t_len, :] = jnp.zeros((POOL_HALO, POOL_DIM), F32)
    pad_ref[POOL_HALO:POOL_HALO + t_len, :] = a_ref[0]
    lane = _lane((chunk, POOL_DIM))
    g0 = lane < POOL_CH
    g1 = lane < 2 * POOL_CH
    g2 = lane < 3 * POOL_CH
    half_w = jnp.where(g0, 1, jnp.where(g1, 2, jnp.where(g2, 4, 8)))
    for c in range(t_len // chunk):
        base = c * chunk

        def ld(off, base=base):
            return pad_ref[POOL_HALO + base + off:POOL_HALO + base + off + chunk, :]

        a0 = ld(0)
        w2 = ld(-1) + a0
        w4 = w2 + ld(-2) + ld(1)
        w8 = w4 + ld(-4) + ld(-3) + ld(2) + ld(3)
        w16 = w8
        for off in (-8, -7, -6, -5, 4, 5, 6, 7):
            w16 = w16 + ld(off)
        tok = base + lax.broadcasted_iota(jnp.int32, (chunk, POOL_DIM), 0)
        cnt = (jnp.minimum(tok + half_w, t_len) - jnp.maximum(tok - half_w, 0)).astype(F32)
        wsum = jnp.where(g0, w2, jnp.where(g1, w4, jnp.where(g2, w8, w16)))
        dlt = wsum / cnt - a0
        y = jnp.dot(dlt.astype(BF16), pw_ref[...], preferred_element_type=F32) * ps_ref[...]
        o_ref[0, base:base + chunk, :] = y.astype(o_ref.dtype)


def _pool(a, pw_bd, pscale):
    b, t, _ = a.shape
    chunk = min(t, 256)
    return pl.pallas_call(
        functools.partial(_pool_kernel, t_len=t, chunk=chunk),
        grid=(b,),
        in_specs=[pl.BlockSpec((1, t, POOL_DIM), lambda bi: (bi, 0, 0)),
                  pl.BlockSpec((POOL_DIM, POOL_DIM), lambda bi: (0, 0)),
                  pl.BlockSpec((1, POOL_DIM), lambda bi: (0, 0))],
        out_specs=pl.BlockSpec((1, t, POOL_DIM), lambda bi: (bi, 0, 0)),
        out_shape=jax.ShapeDtypeStruct((b, t, POOL_DIM), BF16),
        scratch_shapes=[pltpu.VMEM((t + 2 * POOL_HALO, POOL_DIM), F32)],
        compiler_params=_params(1),
        name="pool",
    )(a, pw_bd, pscale)


def _post_kernel(x_ref, g1_ref, sh_ref, sc_ref, g2_ref, m0_ref, m1_ref, w0_ref, w1_ref, g_ref,
                 wg_ref, wu_ref, wd_ref, o_ref):
    mix = (jnp.dot(m0_ref[0], w0_ref[...], preferred_element_type=F32)
           + jnp.dot(m1_ref[0], w1_ref[...], preferred_element_type=F32))
    x = x_ref[0] + g1_ref[0, 0] * mix
    h = _modulate(x, g_ref[...], sh_ref[0, 0], sc_ref[0, 0]).astype(BF16)
    n_chunks = wg_ref.shape[1] // FFN_CHUNK

    def gate_up(c):
        cols = slice(c * FFN_CHUNK, (c + 1) * FFN_CHUNK)
        return (jnp.dot(h, wg_ref[:, cols], preferred_element_type=F32),
                jnp.dot(h, wu_ref[:, cols], preferred_element_type=F32))

    acc = None
    gt, up = gate_up(0)
    for c in range(n_chunks):
        nxt = gate_up(c + 1) if c + 1 < n_chunks else None
        act = (gt * jax.nn.sigmoid(gt) * up).astype(BF16)
        part = jnp.dot(act, wd_ref[c * FFN_CHUNK:(c + 1) * FFN_CHUNK, :], preferred_element_type=F32)
        acc = part if acc is None else acc + part
        if nxt is not None:
            gt, up = nxt
    o_ref[0] = x + g2_ref[0, 0] * acc


def _mod_spec(mods, k, ctx_row):
    blk = (1, 1, 1, mods.shape[-1])
    if ctx_row is None:
        return pl.BlockSpec(blk, lambda bi, i: (bi, k, 0, 0))
    return pl.BlockSpec(blk, lambda bi, i: (ctx_row, k, 0, 0))


def _resident(a):
    nd = a.ndim
    return pl.BlockSpec(a.shape, lambda bi, i: (0,) * nd, pipeline_mode=pl.Buffered(1))


def _post(x, mods, ctx_row, m0, m1, w0, w1, gain, wg, wu, wd, *, tm, name):
    b, t, d = x.shape
    row = lambda bi, i: (bi, i, 0)
    return pl.pallas_call(
        _post_kernel,
        grid=(b, t // tm),
        in_specs=[pl.BlockSpec((1, tm, d), row),
                  _mod_spec(mods, 2, ctx_row), _mod_spec(mods, 3, ctx_row),
                  _mod_spec(mods, 4, ctx_row), _mod_spec(mods, 5, ctx_row),
                  pl.BlockSpec((1, tm, m0.shape[2]), row),
                  pl.BlockSpec((1, tm, m1.shape[2]), row),
                  _resident(w0), _resident(w1), _resident(gain),
                  _resident(wg), _resident(wu), _resident(wd)],
        out_specs=pl.BlockSpec((1, tm, d), row),
        out_shape=jax.ShapeDtypeStruct((b, t, d), F32),
        compiler_params=_params(2),
        name=name,
    )(x, mods, mods, mods, mods, m0, m1, w0, w1, gain, wg, wu, wd)


def _rope_tables(seq, head_half, n_rep):
    tok = np.arange(seq)
    inv = (ROPE_THETA ** (-np.arange(head_half, dtype=np.float32) * 2.0 / (2 * head_half))).astype(np.float32)
    cos = np.ones((seq, LANES), np.float64)
    sin = np.zeros((seq, LANES), np.float64)
    for rep in range(n_rep):
        for part, pos in enumerate((tok // GRID_W, tok % GRID_W)):
            ang = pos.astype(np.float32).astype(np.float64)[:, None] * inv.astype(np.float64)[None, :]
            o = rep * 4 * head_half + part * 2 * head_half
            cos[:, o:o + head_half] = np.cos(ang)
            cos[:, o + head_half:o + 2 * head_half] = np.cos(ang)
            sin[:, o:o + head_half] = -np.sin(ang)
            sin[:, o + head_half:o + 2 * head_half] = np.sin(ang)
    return jnp.asarray(cos, F32), jnp.asarray(sin, F32)


def _na_bias_table(rpb):
    cols = np.arange(GRID_W)
    c0 = np.clip(cols - NA_COLS // 2, 0, GRID_W - NA_COLS)
    kc = np.arange(GRID_W)
    inside = (kc[None, :] >= c0[:, None]) & (kc[None, :] < c0[:, None] + NA_COLS)
    dc = kc[None, :] - cols[:, None] + (NA_COLS - 1)
    onehot = (np.arange(2 * NA_COLS - 1)[:, None, None] == dc[None]) & inside[None]
    sel = jnp.einsum("hjd,dck->hjck", rpb * LOG2E, jnp.asarray(onehot, F32),
                     precision=lax.Precision.HIGHEST)
    t = jnp.where(jnp.asarray(inside)[None, None], sel, NEG_BIG)
    return jnp.concatenate([t[:, :-1], t[:, 1:]], axis=-1).astype(F32)


def kernel(x, c, ctx, c_ctx, l0_ada_w, l0_ada_b, l0_norm_mix, l0_norm_ffn, l0_w_in, l0_pool_w, l0_pool_scale, l0_q_gain, l0_k_gain, l0_w_out, l0_ffn_w_gate, l0_ffn_w_up, l0_ffn_w_down, l1_ada_w, l1_ada_b, l1_norm_mix, l1_norm_ffn, l1_w_in, l1_mla_q_a_gain, l1_mla_kv_a_gain, l1_mla_w_uq, l1_mla_w_ukv, l1_mla_q_gain, l1_mla_k_gain, l1_na_q_gain, l1_na_k_gain, l1_na_rpb, l1_w_out, l1_ffn_w_gate, l1_ffn_w_up, l1_ffn_w_down):
    b, s, d = x.shape
    tc = ctx.shape[1]
    tm_lat = 512
    tq_lat = 512

    cond = jnp.concatenate([c, c_ctx[None, :], jnp.zeros((7, d), F32)], axis=0)

    def mods(ada_w, ada_b):
        return _ada(cond, ada_w, ada_b).reshape(cond.shape[0], 6, 1, d)

    def ffn_weights(wg, wu, wd):
        return wg.astype(BF16), wu.astype(BF16), wd.astype(BF16)

    row1 = lambda v: v.reshape(1, -1)
    tile4 = lambda v, mul: (jnp.tile(v, SLAB // v.shape[0]) * mul).reshape(1, -1)
    sh64 = jnp.asarray(np.kron(np.eye(SLAB // HEAD_DIM), np.ones((HEAD_DIM, HEAD_DIM))), BF16)

    def pair_order(w_q, axis):
        shp = w_q.shape
        split = shp[:axis] + (GQA_KV_HEADS // 2, 2, GQA_GROUP, HEAD_DIM) + shp[axis + 1:]
        perm = list(range(len(split)))
        perm[axis + 1], perm[axis + 2] = perm[axis + 2], perm[axis + 1]
        return w_q.reshape(split).transpose(perm).reshape(shp)

    mods0 = mods(l0_ada_w, l0_ada_b)
    w_in0 = jnp.concatenate([l0_w_in[:, :POOL_DIM],
                             pair_order(l0_w_in[:, POOL_DIM:POOL_DIM + GQA_Q_DIM], 1),
                             l0_w_in[:, POOL_DIM + GQA_Q_DIM:]], axis=1).astype(BF16)
    cos0, sin0 = _rope_tables(s, HEAD_DIM // 4, 2)
    ones_c = jnp.ones((tc, LANES), F32)
    zeros_c = jnp.zeros((tc, LANES), F32)
    qg4 = tile4(l0_q_gain, HEAD_DIM ** 0.5 * GQA_QSCALE)
    kg4 = tile4(l0_k_gain, HEAD_DIM ** 0.5)
    gmask_a = jnp.asarray(np.concatenate([np.ones(HEAD_DIM), np.zeros(HEAD_DIM)])[None], BF16)
    gmask_b = 1 - gmask_a

    a_l, q_l, k_l, v_l = _inproj0(x, mods0, None, row1(l0_norm_mix), w_in0, sh64, qg4, kg4, cos0, sin0,
                                  rope=True, tm=tm_lat)
    a_c, q_c, k_c, v_c = _inproj0(ctx, mods0, b, row1(l0_norm_mix), w_in0, sh64, qg4, kg4,
                                  ones_c, zeros_c, rope=False, tm=tc)
    n_kv_pairs = GQA_KV_HEADS // 2
    attn_l = _pair_attn(q_l, gmask_a, gmask_b, k_c, v_c, k_l, v_l, dk=LANES, n_pairs=n_kv_pairs,
                        q_per_pair=GQA_GROUP, tq=tq_lat, name="gqa_lat")
    attn_c = _pair_attn(q_c, gmask_a, gmask_b, k_c, v_c, None, None, dk=LANES, n_pairs=n_kv_pairs,
                        q_per_pair=GQA_GROUP, tq=tc, name="gqa_ctx")

    eye = jnp.eye(POOL_GROUPS, dtype=F32)
    pw_bd = (eye[:, None, :, None] * l0_pool_w[:, :, None, :]).reshape(POOL_DIM, POOL_DIM).astype(BF16)
    pool_l = _pool(a_l, pw_bd, row1(l0_pool_scale))
    pool_c = _pool(a_c, pw_bd, row1(l0_pool_scale))

    w_out0_pool = l0_w_out[:POOL_DIM].astype(BF16)
    w_out0_attn = pair_order(l0_w_out[POOL_DIM:], 0).astype(BF16)
    ffn0 = ffn_weights(l0_ffn_w_gate, l0_ffn_w_up, l0_ffn_w_down)

    x1 = _post(x, mods0, None, pool_l, attn_l, w_out0_pool, w_out0_attn, row1(l0_norm_ffn), *ffn0,
               tm=tm_lat, name="post0_lat")
    xc = _post(ctx, mods0, b, pool_c, attn_c, w_out0_pool, w_out0_attn, row1(l0_norm_ffn), *ffn0,
               tm=tc, name="post0_ctx")

    mods1 = mods(l1_ada_w, l1_ada_b)
    o_kr = MLA_Q_LORA + MLA_KV_LORA
    kr_cols = l1_w_in[:, o_kr:o_kr + MLA_ROPE]
    w_in1 = jnp.concatenate([l1_w_in[:, :o_kr], kr_cols, kr_cols,
                             jnp.zeros((d, LANES - 2 * MLA_ROPE), F32),
                             l1_w_in[:, o_kr + MLA_ROPE:]], axis=1).astype(BF16)
    n_pairs = MLA_HEADS // 2
    uq = l1_mla_w_uq.reshape(MLA_Q_LORA, MLA_HEADS, MLA_QK)
    ukv = l1_mla_w_ukv.reshape(MLA_KV_LORA, MLA_HEADS, MLA_NOPE + MLA_V)
    zpad = jnp.zeros((MLA_Q_LORA, LANES - 2 * MLA_ROPE), F32)
    wuq = jnp.concatenate(
        [blk for m in range(n_pairs) for blk in
         (uq[:, 2 * m, :MLA_NOPE], uq[:, 2 * m + 1, :MLA_NOPE],
          uq[:, 2 * m, MLA_NOPE:], uq[:, 2 * m + 1, MLA_NOPE:], zpad)], axis=1).astype(BF16)
    wukv = jnp.concatenate(
        [ukv[:, :, :MLA_NOPE].reshape(MLA_KV_LORA, -1), ukv[:, :, MLA_NOPE:].reshape(MLA_KV_LORA, -1)],
        axis=1).astype(BF16)

    def mla_gain(g, mul):
        zr = jnp.zeros((LANES - 2 * MLA_ROPE,), F32)
        return (jnp.concatenate([g[:MLA_NOPE], g[:MLA_NOPE], g[MLA_NOPE:], g[MLA_NOPE:], zr])
                * mul).reshape(1, -1)

    mla_head = np.concatenate([np.zeros(MLA_NOPE), np.ones(MLA_NOPE), np.zeros(MLA_ROPE),
                               np.ones(MLA_ROPE), -np.ones(LANES - 2 * MLA_ROPE)])
    shmla = jnp.asarray((mla_head[:, None] == mla_head[None, :]) & (mla_head[:, None] >= 0), BF16)
    mqg = mla_gain(l1_mla_q_gain, MLA_QK ** 0.5 * MLA_QSCALE)
    mkg = mla_gain(l1_mla_k_gain, MLA_QK ** 0.5)
    nqg = tile4(l1_na_q_gain, NA_HEAD_DIM ** 0.5 * NA_QSCALE)
    nkg = tile4(l1_na_k_gain, NA_HEAD_DIM ** 0.5)
    cos1, sin1 = _rope_tables(s, MLA_ROPE // 4, 2)
    common = (row1(l1_norm_mix), w_in1, row1(l1_mla_q_a_gain), row1(l1_mla_kv_a_gain), wuq, wukv,
              sh64, shmla, mqg, mkg, nqg, nkg)
    mq, mk, mv, nq, nk, nv = _inproj1(x1, mods1, None, *common, cos1, sin1, rope=True, want_q=True,
                                      tm=tm_lat)
    mkc, mvc, nkc, nvc = _inproj1(xc, mods1, b, *common, ones_c, zeros_c, rope=False, want_q=False,
                                  tm=tc)

    mm_a = np.zeros((1, MLA_PAIR_W), np.float32)
    mm_b = np.zeros((1, MLA_PAIR_W), np.float32)
    mm_a[0, :MLA_NOPE] = 1
    mm_a[0, LANES:LANES + MLA_ROPE] = 1
    mm_b[0, MLA_NOPE:LANES] = 1
    mm_b[0, LANES + MLA_ROPE:LANES + 2 * MLA_ROPE] = 1
    o_mla = _pair_attn(mq, jnp.asarray(mm_a, BF16), jnp.asarray(mm_b, BF16), mkc, mvc, mk, mv,
                       dk=MLA_PAIR_W, n_pairs=n_pairs, q_per_pair=1, tq=tq_lat, name="mla")
    o_na = _na_attn(nq, nkc, nvc, nk, nv, _na_bias_table(l1_na_rpb), rows_per_step=8)

    w_out1 = l1_w_out.astype(BF16)
    ffn1 = ffn_weights(l1_ffn_w_gate, l1_ffn_w_up, l1_ffn_w_down)
    return _post(x1, mods1, None, o_mla, o_na, w_out1[:MLA_HEADS * MLA_V], w_out1[MLA_HEADS * MLA_V:],
                 row1(l1_norm_ffn), *ffn1, tm=tm_lat, name="post1_lat")
```

```python
import functools

import numpy as np
import jax
import jax.numpy as jnp
from jax import lax
from jax.experimental import pallas as pl
from jax.experimental.pallas import tpu as pltpu

F32 = jnp.float32
BF16 = jnp.bfloat16

D_MODEL = 1024
GRID_W = 64
ROPE_THETA = 10000.0
EPS = 1e-6
LANES = 128
SLAB = 256

POOL_GROUPS = 4
POOL_CH = 64
POOL_DIM = POOL_GROUPS * POOL_CH
POOL_HALO = 16

HEAD_DIM = 64
GQA_HEADS = 12
GQA_KV_HEADS = 4
GQA_GROUP = GQA_HEADS // GQA_KV_HEADS
GQA_Q_DIM = GQA_HEADS * HEAD_DIM
GQA_KV_DIM = GQA_KV_HEADS * HEAD_DIM

MLA_HEADS = 8
MLA_NOPE = 64
MLA_ROPE = 32
MLA_QK = MLA_NOPE + MLA_ROPE
MLA_V = 64
MLA_Q_LORA = 384
MLA_KV_LORA = 256
MLA_PAIR_W = 2 * LANES

NA_HEADS = 8
NA_HEAD_DIM = 64
NA_DIM = NA_HEADS * NA_HEAD_DIM
NA_ROWS = 8
NA_COLS = 16

FFN_DIM = -(-8 * D_MODEL // (3 * 256)) * 256
FFN_CHUNK = 256

NEG_BIG = -1e30
LOG2E = 1.4426950408889634
GQA_QSCALE = HEAD_DIM ** -0.5 * LOG2E
MLA_QSCALE = MLA_QK ** -0.5 * LOG2E
NA_QSCALE = NA_HEAD_DIM ** -0.5 * LOG2E
PROJ_SUB = 256
ATTN_SUB = 256
ATTN_KEY_CHUNK = 256

VMEM_LIMIT = 56 * 1024 * 1024

_NT = (((1,), (1,)), ((), ()))


def _params(n_axes):
    return pltpu.CompilerParams(dimension_semantics=("arbitrary",) * n_axes,
                                vmem_limit_bytes=VMEM_LIMIT)


def _modulate(x, gain, shift, scale):
    ms = jnp.mean(x * x, axis=-1, keepdims=True)
    return x * lax.rsqrt(ms + EPS) * gain * (1.0 + scale) + shift


def _row_rmsnorm(x, gain):
    ms = jnp.mean(x * x, axis=-1, keepdims=True)
    return x * lax.rsqrt(ms + EPS) * gain


def _lane(shape):
    return lax.broadcasted_iota(jnp.int32, shape, 1)


def _head_rmsnorm(slab, same_head, gain, n):
    ssq = jnp.dot((slab * slab).astype(BF16), same_head, preferred_element_type=F32)
    return slab * lax.rsqrt(ssq + n * EPS) * gain


def _rope(blk, cos, sin_signed, half):
    first = (_lane(blk.shape) & (2 * half - 1)) < half
    rot = jnp.where(first, pltpu.roll(blk, LANES - half, 1), pltpu.roll(blk, half, 1))
    return blk * cos + rot * sin_signed


def _ada_kernel(c_ref, w_ref, b_ref, o_ref):
    cnd = c_ref[...]
    act = cnd * jax.nn.sigmoid(cnd)
    o_ref[...] = jnp.dot(act, w_ref[...], preferred_element_type=F32,
                         precision=lax.Precision.HIGHEST) + b_ref[...]


def _ada(cond, w, b):
    n_rows, d = cond.shape
    n = w.shape[1]
    tn = 1024
    return pl.pallas_call(
        _ada_kernel,
        grid=(n // tn,),
        in_specs=[pl.BlockSpec((n_rows, d), lambda j: (0, 0)),
                  pl.BlockSpec((d, tn), lambda j: (0, j)),
                  pl.BlockSpec((1, tn), lambda j: (0, j))],
        out_specs=pl.BlockSpec((n_rows, tn), lambda j: (0, j)),
        out_shape=jax.ShapeDtypeStruct((n_rows, n), F32),
        compiler_params=_params(1),
        name="ada",
    )(cond, w, b.reshape(1, n))


def _slabs(src, c0, width):
    return [src[:, c0 + j * SLAB:c0 + (j + 1) * SLAB] for j in range(width // SLAB)]


def _norm_rope_store(slabs, same_head, gain, n, dst_ref, rows, cos, sin, half, rope_blocks):
    for j, slab in enumerate(slabs):
        slab = _head_rmsnorm(slab, same_head, gain, n)
        for blk in range(SLAB // LANES):
            part = slab[:, blk * LANES:(blk + 1) * LANES]
            if blk in rope_blocks:
                part = _rope(part, cos, sin, half)
            o = j * SLAB + blk * LANES
            dst_ref[0, rows, o:o + LANES] = part.astype(BF16)


def _software_pipeline(n_rows, stages):
    sub = min(n_rows, PROJ_SUB)
    tiles = [slice(r, r + sub) for r in range(0, n_rows, sub)]
    states = [{} for _ in tiles]
    for it in range(len(tiles) + len(stages) - 1):
        for s, stage in enumerate(stages):
            t = it - s
            if stage is not None and 0 <= t < len(tiles):
                stage(tiles[t], states[t])


def _inproj0_kernel(x_ref, sh_ref, sc_ref, g_ref, w_ref, sh64_ref, qg_ref, kg_ref, cos_ref, sin_ref,
                    a_ref, q_ref, k_ref, v_ref, *, rope):
    rope_blocks = (0, 1) if rope else ()
    q0 = POOL_DIM
    k0 = q0 + GQA_Q_DIM
    v0 = k0 + GQA_KV_DIM

    def project(rows, st):
        h = _modulate(x_ref[0, rows, :], g_ref[...], sh_ref[0, 0], sc_ref[0, 0]).astype(BF16)
        st["p"] = jnp.dot(h, w_ref[...], preferred_element_type=F32)

    def epilogue(rows, st):
        p = st["p"]
        a_ref[0, rows, :] = p[:, :POOL_DIM]
        for c0, width, gain_ref, dst in ((q0, GQA_Q_DIM, qg_ref, q_ref), (k0, GQA_KV_DIM, kg_ref, k_ref)):
            _norm_rope_store(_slabs(p, c0, width), sh64_ref[...], gain_ref[...], HEAD_DIM, dst, rows,
                             cos_ref[rows, :], sin_ref[rows, :], HEAD_DIM // 4, rope_blocks)
        v_ref[0, rows, :] = p[:, v0:v0 + GQA_KV_DIM].astype(BF16)

    _software_pipeline(x_ref.shape[1], [project, None, epilogue])


def _inproj0(x, mods, ctx_row, gain, w, same_head, q_gain, k_gain, cos, sin, *, rope, tm):
    b, t, d = x.shape
    row = lambda bi, i: (bi, i, 0)
    return pl.pallas_call(
        functools.partial(_inproj0_kernel, rope=rope),
        grid=(b, t // tm),
        in_specs=[pl.BlockSpec((1, tm, d), row),
                  _mod_spec(mods, 0, ctx_row), _mod_spec(mods, 1, ctx_row),
                  _resident(gain), _resident(w), _resident(same_head),
                  _resident(q_gain), _resident(k_gain),
                  pl.BlockSpec((tm, LANES), lambda bi, i: (i, 0)),
                  pl.BlockSpec((tm, LANES), lambda bi, i: (i, 0))],
        out_specs=[pl.BlockSpec((1, tm, POOL_DIM), row),
                   pl.BlockSpec((1, tm, GQA_Q_DIM), row),
                   pl.BlockSpec((1, tm, GQA_KV_DIM), row),
                   pl.BlockSpec((1, tm, GQA_KV_DIM), row)],
        out_shape=[jax.ShapeDtypeStruct((b, t, POOL_DIM), F32),
                   jax.ShapeDtypeStruct((b, t, GQA_Q_DIM), BF16),
                   jax.ShapeDtypeStruct((b, t, GQA_KV_DIM), BF16),
                   jax.ShapeDtypeStruct((b, t, GQA_KV_DIM), BF16)],
        compiler_params=_params(2),
        name="inproj0_rope" if rope else "inproj0_ctx",
    )(x, mods, mods, gain, w, same_head, q_gain, k_gain, cos, sin)


def _inproj1_kernel(x_ref, sh_ref, sc_ref, g_ref, w_ref, qag_ref, kvag_ref, wuq_ref, wukv_ref,
                    sh64_ref, shmla_ref, mqg_ref, mkg_ref, nqg_ref, nkg_ref, cos_ref, sin_ref,
                    *out_refs, rope, want_q):
    if want_q:
        mq_ref, mk_ref, mv_ref, nq_ref, nk_ref, nv_ref = out_refs
    else:
        mk_ref, mv_ref, nk_ref, nv_ref = out_refs
    o_ckv = MLA_Q_LORA
    o_kr = o_ckv + MLA_KV_LORA
    o_nq = o_kr + LANES
    o_nk = o_nq + NA_DIM
    o_nv = o_nk + NA_DIM
    n_pairs = MLA_HEADS // 2
    mla_rope = (1,) if rope else ()

    def project(rows, st):
        h = _modulate(x_ref[0, rows, :], g_ref[...], sh_ref[0, 0], sc_ref[0, 0]).astype(BF16)
        st["p"] = jnp.dot(h, w_ref[...], preferred_element_type=F32)

    def up_project(rows, st):
        p = st["p"]
        if want_q:
            st["qq"] = jnp.dot(_row_rmsnorm(p[:, :MLA_Q_LORA], qag_ref[...]).astype(BF16),
                               wuq_ref[...], preferred_element_type=F32)
        st["kv"] = jnp.dot(_row_rmsnorm(p[:, o_ckv:o_ckv + MLA_KV_LORA], kvag_ref[...]).astype(BF16),
                           wukv_ref[...], preferred_element_type=F32)

    def epilogue(rows, st):
        p, kv = st["p"], st["kv"]
        cos = cos_ref[rows, :]
        sin = sin_ref[rows, :]
        if want_q:
            _norm_rope_store(_slabs(st["qq"], 0, n_pairs * MLA_PAIR_W), shmla_ref[...], mqg_ref[...],
                             MLA_QK, mq_ref, rows, cos, sin, MLA_ROPE // 4, mla_rope)
            _norm_rope_store(_slabs(p, o_nq, NA_DIM), sh64_ref[...], nqg_ref[...], NA_HEAD_DIM,
                             nq_ref, rows, cos, sin, 0, ())
        kr_blk = p[:, o_kr:o_kr + LANES]
        k_slabs = [jnp.concatenate([kv[:, m * LANES:(m + 1) * LANES], kr_blk], axis=1)
                   for m in range(n_pairs)]
        _norm_rope_store(k_slabs, shmla_ref[...], mkg_ref[...], MLA_QK, mk_ref, rows, cos, sin,
                         MLA_ROPE // 4, mla_rope)
        v0 = n_pairs * LANES
        mv_ref[0, rows, :] = kv[:, v0:v0 + MLA_HEADS * MLA_V].astype(BF16)
        _norm_rope_store(_slabs(p, o_nk, NA_DIM), sh64_ref[...], nkg_ref[...], NA_HEAD_DIM,
                         nk_ref, rows, cos, sin, 0, ())
        nv_ref[0, rows, :] = p[:, o_nv:o_nv + NA_DIM].astype(BF16)

    _software_pipeline(x_ref.shape[1], [project, up_project, epilogue])


def _inproj1(x, mods, ctx_row, gain, w, qag, kvag, wuq, wukv, sh64, shmla, mqg, mkg, nqg, nkg,
             cos, sin, *, rope, want_q, tm):
    b, t, d = x.shape
    row = lambda bi, i: (bi, i, 0)
    full = _resident
    mla_w = (MLA_HEADS // 2) * MLA_PAIR_W
    outs = [(mla_w, "mk"), (MLA_HEADS * MLA_V, "mv"), (NA_DIM, "nk"), (NA_DIM, "nv")]
    if want_q:
        outs = [(mla_w, "mq")] + outs[:2] + [(NA_DIM, "nq")] + outs[2:]
    return pl.pallas_call(
        functools.partial(_inproj1_kernel, rope=rope, want_q=want_q),
        grid=(b, t // tm),
        in_specs=[pl.BlockSpec((1, tm, d), row),
                  _mod_spec(mods, 0, ctx_row), _mod_spec(mods, 1, ctx_row),
                  full(gain), full(w), full(qag), full(kvag), full(wuq), full(wukv),
                  full(sh64), full(shmla), full(mqg), full(mkg), full(nqg), full(nkg),
                  pl.BlockSpec((tm, LANES), lambda bi, i: (i, 0)),
                  pl.BlockSpec((tm, LANES), lambda bi, i: (i, 0))],
        out_specs=[pl.BlockSpec((1, tm, n), row) for n, _ in outs],
        out_shape=[jax.ShapeDtypeStruct((b, t, n), BF16) for n, _ in outs],
        compiler_params=_params(2),
        name="inproj1_lat" if want_q else "inproj1_ctx",
    )(x, mods, mods, gain, w, qag, kvag, wuq, wukv, sh64, shmla, mqg, mkg, nqg, nkg, cos, sin)


def _fill_pair_kv(k, v, mask_a, mask_b, ka_ref, kb_ref, va_ref, vb_ref, r0):
    n = k.shape[0]
    first = _lane(v.shape) < HEAD_DIM
    one = jnp.ones_like(v)
    ka_ref[r0:r0 + n, :] = k * mask_a
    kb_ref[r0:r0 + n, :] = k * mask_b
    va_ref[r0:r0 + n, :] = jnp.where(first, v, one)
    vb_ref[r0:r0 + n, :] = jnp.where(first, one, v)


def _fill_pair_kv_t(k, v, mask_a, mask_b, ka_ref, kb_ref, vat_ref, vbt_ref, r0):
    n = k.shape[0]
    ka_ref[r0:r0 + n, :] = k * mask_a
    kb_ref[r0:r0 + n, :] = k * mask_b
    vt = v.astype(F32).T
    first = lax.broadcasted_iota(jnp.int32, vt.shape, 0) < HEAD_DIM
    vat_ref[:, r0:r0 + n] = jnp.where(first, vt, 1.0).astype(BF16)
    vbt_ref[:, r0:r0 + n] = jnp.where(first, 1.0, vt).astype(BF16)


def _pair_softmax_pv(q_tiles, ka_ref, kb_ref, vat_ref, vbt_ref):
    n_keys = ka_ref.shape[0]
    chunk = min(n_keys, ATTN_KEY_CHUNK)
    n_chunks = n_keys // chunk
    chains = [(q, k_ref, vt_ref) for q in q_tiles
              for k_ref, vt_ref in ((ka_ref, vat_ref), (kb_ref, vbt_ref))]

    def scores(j):
        return [lax.dot_general(k_ref[j * chunk:(j + 1) * chunk, :], q, _NT,
                                preferred_element_type=F32) for q, k_ref, _ in chains]

    s = scores(0)
    run_max = [None] * len(chains)
    acc = [None] * len(chains)
    for j in range(n_chunks):
        s_next = scores(j + 1) if j + 1 < n_chunks else None
        for c, (_, _, vt_ref) in enumerate(chains):
            mx = s[c].max(axis=0, keepdims=True)
            if j > 0:
                mx = jnp.maximum(mx, run_max[c])
            p = jnp.exp2((s[c] - mx).astype(BF16))
            pv = jnp.dot(vt_ref[:, j * chunk:(j + 1) * chunk], p, preferred_element_type=F32)
            acc[c] = pv if j == 0 else acc[c] * jnp.exp2(run_max[c] - mx) + pv
            run_max[c] = mx
        s = s_next
    outs = []
    for t in range(len(q_tiles)):
        oa, ob = acc[2 * t], acc[2 * t + 1]
        top = oa[:HEAD_DIM] / oa[HEAD_DIM:HEAD_DIM + 1]
        bot = ob[HEAD_DIM:] / ob[0:1]
        outs.append(jnp.concatenate([top, bot], axis=0).T)
    return outs


def _pair_attn_kernel(*refs, use_lat, q_per_pair):
    if use_lat:
        q_ref, ma_ref, mb_ref, kc_ref, vc_ref, kl_ref, vl_ref, o_ref, ka, kb, vat, vbt = refs
        segs = ((kc_ref, vc_ref), (kl_ref, vl_ref))
    else:
        q_ref, ma_ref, mb_ref, kc_ref, vc_ref, o_ref, ka, kb, vat, vbt = refs
        segs = ((kc_ref, vc_ref),)

    @pl.when(jnp.logical_and(pl.program_id(2) == 0, pl.program_id(1) % q_per_pair == 0))
    def _():
        r0 = 0
        for k_ref, v_ref in segs:
            _fill_pair_kv_t(k_ref[0], v_ref[0], ma_ref[...], mb_ref[...], ka, kb, vat, vbt, r0)
            r0 += k_ref.shape[1]

    sub = min(q_ref.shape[1], ATTN_SUB)
    starts = range(0, q_ref.shape[1], sub)
    outs = _pair_softmax_pv([q_ref[0, r0:r0 + sub, :] for r0 in starts], ka, kb, vat, vbt)
    for r0, o in zip(starts, outs):
        o_ref[0, r0:r0 + sub, :] = o.astype(o_ref.dtype)


def _pair_attn(q, mask_a, mask_b, kc, vc, kl, vl, *, dk, n_pairs, q_per_pair, tq, name):
    b, t, _ = q.shape
    n_qblk = n_pairs * q_per_pair
    use_lat = kl is not None
    qmap = lambda bi, c, i: (bi, i, c)
    kvmap = lambda bi, c, i: (bi, 0, c // q_per_pair)
    const = lambda bi, c, i: (0, 0)
    in_specs = [pl.BlockSpec((1, tq, dk), qmap),
                pl.BlockSpec((1, dk), const),
                pl.BlockSpec((1, dk), const),
                pl.BlockSpec((1, kc.shape[1], dk), kvmap),
                pl.BlockSpec((1, vc.shape[1], LANES), kvmap)]
    args = [q, mask_a, mask_b, kc, vc]
    if use_lat:
        in_specs += [pl.BlockSpec((1, kl.shape[1], dk), kvmap),
                     pl.BlockSpec((1, vl.shape[1], LANES), kvmap)]
        args += [kl, vl]
    n_keys = kc.shape[1] + (kl.shape[1] if use_lat else 0)
    return pl.pallas_call(
        functools.partial(_pair_attn_kernel, use_lat=use_lat, q_per_pair=q_per_pair),
        grid=(b, n_qblk, t // tq),
        in_specs=in_specs,
        out_specs=pl.BlockSpec((1, tq, LANES), qmap),
        out_shape=jax.ShapeDtypeStruct((b, t, n_qblk * LANES), BF16),
        scratch_shapes=[pltpu.VMEM((n_keys, dk), BF16), pltpu.VMEM((n_keys, dk), BF16),
                        pltpu.VMEM((LANES, n_keys), BF16), pltpu.VMEM((LANES, n_keys), BF16)],
        compiler_params=_params(3),
        name=name,
    )(*args)


def _na_kernel(q_ref, kc_ref, vc_ref, k_ref, v_ref, bias_ref, o_ref, ka, kb, va, vb,
               *, rows_per_step, n_rows):
    rb = pl.program_id(2)
    tc = kc_ref.shape[1]
    n_win = NA_ROWS * GRID_W

    @pl.when(rb == 0)
    def _():
        lane = _lane((1, LANES))
        mask_a = jnp.where(lane < NA_HEAD_DIM, 1.0, 0.0).astype(BF16)
        mask_b = jnp.where(lane < NA_HEAD_DIM, 0.0, 1.0).astype(BF16)
        _fill_pair_kv(kc_ref[0], vc_ref[0], mask_a, mask_b, ka, kb, va, vb, 0)
        _fill_pair_kv(k_ref[0], v_ref[0], mask_a, mask_b, ka, kb, va, vb, tc)

    heads = ((ka, va), (kb, vb))
    q_all = q_ref[0]
    rows = []
    for i in range(rows_per_step):
        r = rb * rows_per_step + i
        r0 = jnp.clip(r - NA_ROWS // 2, 0, n_rows - NA_ROWS)
        rows.append((pl.multiple_of(tc + r0 * GRID_W, GRID_W), r0 - r + (NA_ROWS - 1)))

    s_ctx = [lax.dot_general(q_all, k_s[0:tc, :], _NT, preferred_element_type=F32)
             for k_s, _ in heads]
    s_win = []
    for i, (kstart, j0) in enumerate(rows):
        q = q_all[i * GRID_W:(i + 1) * GRID_W]
        per_head = []
        for hh, (k_s, _) in enumerate(heads):
            bias = jnp.concatenate([bias_ref[hh, j0 + 2 * c] for c in range(NA_ROWS // 2)], axis=1)
            per_head.append(lax.dot_general(q, k_s[pl.ds(kstart, n_win), :], _NT,
                                            preferred_element_type=F32) + bias)
        s_win.append(per_head)

    p_ctx = [[], []]
    p_win = []
    for i in range(rows_per_step):
        per_head = []
        for hh in range(2):
            sc = s_ctx[hh][i * GRID_W:(i + 1) * GRID_W]
            sw = s_win[i][hh]
            mx = jnp.maximum(sc.max(axis=-1, keepdims=True), sw.max(axis=-1, keepdims=True))
            p_ctx[hh].append(jnp.exp2((sc - mx).astype(BF16)))
            per_head.append(jnp.exp2((sw - mx).astype(BF16)))
        p_win.append(per_head)

    n_ctx = [jnp.dot(jnp.concatenate(p_ctx[hh], axis=0), v_s[0:tc, :], preferred_element_type=F32)
             for hh, (_, v_s) in enumerate(heads)]
    outs = []
    for i, (kstart, _) in enumerate(rows):
        nums = [n_ctx[hh][i * GRID_W:(i + 1) * GRID_W]
                + jnp.dot(p_win[i][hh], v_s[pl.ds(kstart, n_win), :], preferred_element_type=F32)
                for hh, (_, v_s) in enumerate(heads)]
        first = _lane(nums[0].shape) < NA_HEAD_DIM
        num = jnp.where(first, nums[0], nums[1])
        den = pltpu.roll(jnp.where(first, nums[1], nums[0]), NA_HEAD_DIM, 1)
        outs.append((num / den).astype(o_ref.dtype))
    o_ref[0] = jnp.concatenate(outs, axis=0)


def _na_attn(q, kc, vc, k, v, bias, *, rows_per_step):
    b, s, _ = q.shape
    n_rows = s // GRID_W
    n_pairs = NA_HEADS // 2
    tq = rows_per_step * GRID_W
    n_keys = kc.shape[1] + s
    return pl.pallas_call(
        functools.partial(_na_kernel, rows_per_step=rows_per_step, n_rows=n_rows),
        grid=(b, n_pairs, n_rows // rows_per_step),
        in_specs=[pl.BlockSpec((1, tq, LANES), lambda bi, m, i: (bi, i, m)),
                  pl.BlockSpec((1, kc.shape[1], LANES), lambda bi, m, i: (bi, 0, m)),
                  pl.BlockSpec((1, vc.shape[1], LANES), lambda bi, m, i: (bi, 0, m)),
                  pl.BlockSpec((1, s, LANES), lambda bi, m, i: (bi, 0, m)),
                  pl.BlockSpec((1, s, LANES), lambda bi, m, i: (bi, 0, m)),
                  pl.BlockSpec((2,) + bias.shape[1:], lambda bi, m, i: (m, 0, 0, 0))],
        out_specs=pl.BlockSpec((1, tq, LANES), lambda bi, m, i: (bi, i, m)),
        out_shape=jax.ShapeDtypeStruct((b, s, NA_DIM), BF16),
        scratch_shapes=[pltpu.VMEM((n_keys, LANES), BF16) for _ in range(4)],
        compiler_params=_params(3),
        name="na_attn",
    )(q, kc, vc, k, v, bias)


def _pool_kernel(a_ref, pw_ref, ps_ref, o_ref, pad_ref, *, t_len, chunk):
    pad_ref[0:POOL_HALO, :] = jnp.zeros((POOL_HALO, POOL_DIM), F32)
    pad_ref[POOL_HALO + t_len:2 * POOL_HALO + t_len, :] = jnp.zeros((POOL_HALO, POOL_DIM), F32)
    pad_ref[POOL_HALO:POOL_HALO + t_len, :] = a_ref[0]
    lane = _lane((chunk, POOL_DIM))
    g0 = lane < POOL_CH
    g1 = lane < 2 * POOL_CH
    g2 = lane < 3 * POOL_CH
    half_w = jnp.where(g0, 1, jnp.where(g1, 2, jnp.where(g2, 4, 8)))
    for c in range(t_len // chunk):
        base = c * chunk

        def ld(off, base=base):
            return pad_ref[POOL_HALO + base + off:POOL_HALO + base + off + chunk, :]

        a0 = ld(0)
        w2 = ld(-1) + a0
        w4 = w2 + ld(-2) + ld(1)
        w8 = w4 + ld(-4) + ld(-3) + ld(2) + ld(3)
        w16 = w8
        for off in (-8, -7, -6, -5, 4, 5, 6, 7):
            w16 = w16 + ld(off)
        tok = base + lax.broadcasted_iota(jnp.int32, (chunk, POOL_DIM), 0)
        cnt = (jnp.minimum(tok + half_w, t_len) - jnp.maximum(tok - half_w, 0)).astype(F32)
        wsum = jnp.where(g0, w2, jnp.where(g1, w4, jnp.where(g2, w8, w16)))
        dlt = wsum / cnt - a0
        y = jnp.dot(dlt.astype(BF16), pw_ref[...], preferred_element_type=F32) * ps_ref[...]
        o_ref[0, base:base + chunk, :] = y.astype(o_ref.dtype)


def _pool(a, pw_bd, pscale):
    b, t, _ = a.shape
    chunk = min(t, 256)
    return pl.pallas_call(
        functools.partial(_pool_kernel, t_len=t, chunk=chunk),
        grid=(b,),
        in_specs=[pl.BlockSpec((1, t, POOL_DIM), lambda bi: (bi, 0, 0)),
                  pl.BlockSpec((POOL_DIM, POOL_DIM), lambda bi: (0, 0)),
                  pl.BlockSpec((1, POOL_DIM), lambda bi: (0, 0))],
        out_specs=pl.BlockSpec((1, t, POOL_DIM), lambda bi: (bi, 0, 0)),
        out_shape=jax.ShapeDtypeStruct((b, t, POOL_DIM), BF16),
        scratch_shapes=[pltpu.VMEM((t + 2 * POOL_HALO, POOL_DIM), F32)],
        compiler_params=_params(1),
        name="pool",
    )(a, pw_bd, pscale)


def _post_kernel(x_ref, g1_ref, sh_ref, sc_ref, g2_ref, m0_ref, m1_ref, w0_ref, w1_ref, g_ref,
                 wg_ref, wu_ref, wd_ref, o_ref):
    mix = (jnp.dot(m0_ref[0], w0_ref[...], preferred_element_type=F32)
           + jnp.dot(m1_ref[0], w1_ref[...], preferred_element_type=F32))
    x = x_ref[0] + g1_ref[0, 0] * mix
    h = _modulate(x, g_ref[...], sh_ref[0, 0], sc_ref[0, 0]).astype(BF16)
    n_chunks = wg_ref.shape[1] // FFN_CHUNK

    def gate_up(c):
        cols = slice(c * FFN_CHUNK, (c + 1) * FFN_CHUNK)
        return (jnp.dot(h, wg_ref[:, cols], preferred_element_type=F32),
                jnp.dot(h, wu_ref[:, cols], preferred_element_type=F32))

    acc = None
    gt, up = gate_up(0)
    for c in range(n_chunks):
        nxt = gate_up(c + 1) if c + 1 < n_chunks else None
        act = (gt * jax.nn.sigmoid(gt) * up).astype(BF16)
        part = jnp.dot(act, wd_ref[c * FFN_CHUNK:(c + 1) * FFN_CHUNK, :], preferred_element_type=F32)
        acc = part if acc is None else acc + part
        if nxt is not None:
            gt, up = nxt
    o_ref[0] = x + g2_ref[0, 0] * acc


def _mod_spec(mods, k, ctx_row):
    blk = (1, 1, 1, mods.shape[-1])
    if ctx_row is None:
        return pl.BlockSpec(blk, lambda bi, i: (bi, k, 0, 0))
    return pl.BlockSpec(blk, lambda bi, i: (ctx_row, k, 0, 0))


def _resident(a):
    nd = a.ndim
    return pl.BlockSpec(a.shape, lambda bi, i: (0,) * nd, pipeline_mode=pl.Buffered(1))


def _post(x, mods, ctx_row, m0, m1, w0, w1, gain, wg, wu, wd, *, tm, name):
    b, t, d = x.shape
    row = lambda bi, i: (bi, i, 0)
    return pl.pallas_call(
        _post_kernel,
        grid=(b, t // tm),
        in_specs=[pl.BlockSpec((1, tm, d), row),
                  _mod_spec(mods, 2, ctx_row), _mod_spec(mods, 3, ctx_row),
                  _mod_spec(mods, 4, ctx_row), _mod_spec(mods, 5, ctx_row),
                  pl.BlockSpec((1, tm, m0.shape[2]), row),
                  pl.BlockSpec((1, tm, m1.shape[2]), row),
                  _resident(w0), _resident(w1), _resident(gain),
                  _resident(wg), _resident(wu), _resident(wd)],
        out_specs=pl.BlockSpec((1, tm, d), row),
        out_shape=jax.ShapeDtypeStruct((b, t, d), F32),
        compiler_params=_params(2),
        name=name,
    )(x, mods, mods, mods, mods, m0, m1, w0, w1, gain, wg, wu, wd)


def _rope_tables(seq, head_half, n_rep):
    tok = np.arange(seq)
    inv = (ROPE_THETA ** (-np.arange(head_half, dtype=np.float32) * 2.0 / (2 * head_half))).astype(np.float32)
    cos = np.ones((seq, LANES), np.float64)
    sin = np.zeros((seq, LANES), np.float64)
    for rep in range(n_rep):
        for part, pos in enumerate((tok // GRID_W, tok % GRID_W)):
            ang = pos.astype(np.float32).astype(np.float64)[:, None] * inv.astype(np.float64)[None, :]
            o = rep * 4 * head_half + part * 2 * head_half
            cos[:, o:o + head_half] = np.cos(ang)
            cos[:, o + head_half:o + 2 * head_half] = np.cos(ang)
            sin[:, o:o + head_half] = -np.sin(ang)
            sin[:, o + head_half:o + 2 * head_half] = np.sin(ang)
    return jnp.asarray(cos, F32), jnp.asarray(sin, F32)


def _na_bias_table(rpb):
    cols = np.arange(GRID_W)
    c0 = np.clip(cols - NA_COLS // 2, 0, GRID_W - NA_COLS)
    kc = np.arange(GRID_W)
    inside = (kc[None, :] >= c0[:, None]) & (kc[None, :] < c0[:, None] + NA_COLS)
    dc = kc[None, :] - cols[:, None] + (NA_COLS - 1)
    onehot = (np.arange(2 * NA_COLS - 1)[:, None, None] == dc[None]) & inside[None]
    sel = jnp.einsum("hjd,dck->hjck", rpb * LOG2E, jnp.asarray(onehot, F32),
                     precision=lax.Precision.HIGHEST)
    t = jnp.where(jnp.asarray(inside)[None, None], sel, NEG_BIG)
    return jnp.concatenate([t[:, :-1], t[:, 1:]], axis=-1).astype(F32)


def kernel(x, c, ctx, c_ctx, l0_ada_w, l0_ada_b, l0_norm_mix, l0_norm_ffn, l0_w_in, l0_pool_w, l0_pool_scale, l0_q_gain, l0_k_gain, l0_w_out, l0_ffn_w_gate, l0_ffn_w_up, l0_ffn_w_down, l1_ada_w, l1_ada_b, l1_norm_mix, l1_norm_ffn, l1_w_in, l1_mla_q_a_gain, l1_mla_kv_a_gain, l1_mla_w_uq, l1_mla_w_ukv, l1_mla_q_gain, l1_mla_k_gain, l1_na_q_gain, l1_na_k_gain, l1_na_rpb, l1_w_out, l1_ffn_w_gate, l1_ffn_w_up, l1_ffn_w_down):
    b, s, d = x.shape
    tc = ctx.shape[1]
    tm_lat = 512
    tp_lat = 1024
    tq_lat = 1024

    cond = jnp.concatenate([c, c_ctx[None, :], jnp.zeros((7, d), F32)], axis=0)

    def mods(ada_w, ada_b):
        return _ada(cond, ada_w, ada_b).reshape(cond.shape[0], 6, 1, d)

    def ffn_weights(wg, wu, wd):
        return wg.astype(BF16), wu.astype(BF16), wd.astype(BF16)

    row1 = lambda v: v.reshape(1, -1)
    tile4 = lambda v, mul: (jnp.tile(v, SLAB // v.shape[0]) * mul).reshape(1, -1)
    sh64 = jnp.asarray(np.kron(np.eye(SLAB // HEAD_DIM), np.ones((HEAD_DIM, HEAD_DIM))), BF16)

    def pair_order(w_q, axis):
        shp = w_q.shape
        split = shp[:axis] + (GQA_KV_HEADS // 2, 2, GQA_GROUP, HEAD_DIM) + shp[axis + 1:]
        perm = list(range(len(split)))
        perm[axis + 1], perm[axis + 2] = perm[axis + 2], perm[axis + 1]
        return w_q.reshape(split).transpose(perm).reshape(shp)

    mods0 = mods(l0_ada_w, l0_ada_b)
    w_in0 = jnp.concatenate([l0_w_in[:, :POOL_DIM],
                             pair_order(l0_w_in[:, POOL_DIM:POOL_DIM + GQA_Q_DIM], 1),
                             l0_w_in[:, POOL_DIM + GQA_Q_DIM:]], axis=1).astype(BF16)
    cos0, sin0 = _rope_tables(s, HEAD_DIM // 4, 2)
    ones_c = jnp.ones((tc, LANES), F32)
    zeros_c = jnp.zeros((tc, LANES), F32)
    qg4 = tile4(l0_q_gain, HEAD_DIM ** 0.5 * GQA_QSCALE)
    kg4 = tile4(l0_k_gain, HEAD_DIM ** 0.5)
    gmask_a = jnp.asarray(np.concatenate([np.ones(HEAD_DIM), np.zeros(HEAD_DIM)])[None], BF16)
    gmask_b = 1 - gmask_a

    a_l, q_l, k_l, v_l = _inproj0(x, mods0, None, row1(l0_norm_mix), w_in0, sh64, qg4, kg4, cos0, sin0,
                                  rope=True, tm=tp_lat)
    a_c, q_c, k_c, v_c = _inproj0(ctx, mods0, b, row1(l0_norm_mix), w_in0, sh64, qg4, kg4,
                                  ones_c, zeros_c, rope=False, tm=tc)
    n_kv_pairs = GQA_KV_HEADS // 2
    attn_l = _pair_attn(q_l, gmask_a, gmask_b, k_c, v_c, k_l, v_l, dk=LANES, n_pairs=n_kv_pairs,
                        q_per_pair=GQA_GROUP, tq=tq_lat, name="gqa_lat")
    attn_c = _pair_attn(q_c, gmask_a, gmask_b, k_c, v_c, None, None, dk=LANES, n_pairs=n_kv_pairs,
                        q_per_pair=GQA_GROUP, tq=tc, name="gqa_ctx")

    eye = jnp.eye(POOL_GROUPS, dtype=F32)
    pw_bd = (eye[:, None, :, None] * l0_pool_w[:, :, None, :]).reshape(POOL_DIM, POOL_DIM).astype(BF16)
    pool_l = _pool(a_l, pw_bd, row1(l0_pool_scale))
    pool_c = _pool(a_c, pw_bd, row1(l0_pool_scale))

    w_out0_pool = l0_w_out[:POOL_DIM].astype(BF16)
    w_out0_attn = pair_order(l0_w_out[POOL_DIM:], 0).astype(BF16)
    ffn0 = ffn_weights(l0_ffn_w_gate, l0_ffn_w_up, l0_ffn_w_down)

    x1 = _post(x, mods0, None, pool_l, attn_l, w_out0_pool, w_out0_attn, row1(l0_norm_ffn), *ffn0,
               tm=tm_lat, name="post0_lat")
    xc = _post(ctx, mods0, b, pool_c, attn_c, w_out0_pool, w_out0_attn, row1(l0_norm_ffn), *ffn0,
               tm=tc, name="post0_ctx")

    mods1 = mods(l1_ada_w, l1_ada_b)
    o_kr = MLA_Q_LORA + MLA_KV_LORA
    kr_cols = l1_w_in[:, o_kr:o_kr + MLA_ROPE]
    w_in1 = jnp.concatenate([l1_w_in[:, :o_kr], kr_cols, kr_cols,
                             jnp.zeros((d, LANES - 2 * MLA_ROPE), F32),
                             l1_w_in[:, o_kr + MLA_ROPE:]], axis=1).astype(BF16)
    n_pairs = MLA_HEADS // 2
    uq = l1_mla_w_uq.reshape(MLA_Q_LORA, MLA_HEADS, MLA_QK)
    ukv = l1_mla_w_ukv.reshape(MLA_KV_LORA, MLA_HEADS, MLA_NOPE + MLA_V)
    zpad = jnp.zeros((MLA_Q_LORA, LANES - 2 * MLA_ROPE), F32)
    wuq = jnp.concatenate(
        [blk for m in range(n_pairs) for blk in
         (uq[:, 2 * m, :MLA_NOPE], uq[:, 2 * m + 1, :MLA_NOPE],
          uq[:, 2 * m, MLA_NOPE:], uq[:, 2 * m + 1, MLA_NOPE:], zpad)], axis=1).astype(BF16)
    wukv = jnp.concatenate(
        [ukv[:, :, :MLA_NOPE].reshape(MLA_KV_LORA, -1), ukv[:, :, MLA_NOPE:].reshape(MLA_KV_LORA, -1)],
        axis=1).astype(BF16)

    def mla_gain(g, mul):
        zr = jnp.zeros((LANES - 2 * MLA_ROPE,), F32)
        return (jnp.concatenate([g[:MLA_NOPE], g[:MLA_NOPE], g[MLA_NOPE:], g[MLA_NOPE:], zr])
                * mul).reshape(1, -1)

    mla_head = np.concatenate([np.zeros(MLA_NOPE), np.ones(MLA_NOPE), np.zeros(MLA_ROPE),
                               np.ones(MLA_ROPE), -np.ones(LANES - 2 * MLA_ROPE)])
    shmla = jnp.asarray((mla_head[:, None] == mla_head[None, :]) & (mla_head[:, None] >= 0), BF16)
    mqg = mla_gain(l1_mla_q_gain, MLA_QK ** 0.5 * MLA_QSCALE)
    mkg = mla_gain(l1_mla_k_gain, MLA_QK ** 0.5)
    nqg = tile4(l1_na_q_gain, NA_HEAD_DIM ** 0.5 * NA_QSCALE)
    nkg = tile4(l1_na_k_gain, NA_HEAD_DIM ** 0.5)
    cos1, sin1 = _rope_tables(s, MLA_ROPE // 4, 2)
    common = (row1(l1_norm_mix), w_in1, row1(l1_mla_q_a_gain), row1(l1_mla_kv_a_gain), wuq, wukv,
              sh64, shmla, mqg, mkg, nqg, nkg)
    mq, mk, mv, nq, nk, nv = _inproj1(x1, mods1, None, *common, cos1, sin1, rope=True, want_q=True,
                                      tm=tp_lat)
    mkc, mvc, nkc, nvc = _inproj1(xc, mods1, b, *common, ones_c, zeros_c, rope=False, want_q=False,
                                  tm=tc)

    mm_a = np.zeros((1, MLA_PAIR_W), np.float32)
    mm_b = np.zeros((1, MLA_PAIR_W), np.float32)
    mm_a[0, :MLA_NOPE] = 1
    mm_a[0, LANES:LANES + MLA_ROPE] = 1
    mm_b[0, MLA_NOPE:LANES] = 1
    mm_b[0, LANES + MLA_ROPE:LANES + 2 * MLA_ROPE] = 1
    o_mla = _pair_attn(mq, jnp.asarray(mm_a, BF16), jnp.asarray(mm_b, BF16), mkc, mvc, mk, mv,
                       dk=MLA_PAIR_W, n_pairs=n_pairs, q_per_pair=1, tq=tq_lat, name="mla")
    o_na = _na_attn(nq, nkc, nvc, nk, nv, _na_bias_table(l1_na_rpb), rows_per_step=8)

    w_out1 = l1_w_out.astype(BF16)
    ffn1 = ffn_weights(l1_ffn_w_gate, l1_ffn_w_up, l1_ffn_w_down)
    return _post(x1, mods1, None, o_mla, o_na, w_out1[:MLA_HEADS * MLA_V], w_out1[MLA_HEADS * MLA_V:],
                 row1(l1_norm_ffn), *ffn1, tm=tm_lat, name="post1_lat")
```

```python
import functools

import numpy as np
import jax
import jax.numpy as jnp
from jax import lax
from jax.experimental import pallas as pl
from jax.experimental.pallas import tpu as pltpu

F32 = jnp.float32
BF16 = jnp.bfloat16

D_MODEL = 1024
GRID_W = 64
ROPE_THETA = 10000.0
EPS = 1e-6
LANES = 128
SLAB = 256

POOL_GROUPS = 4
POOL_CH = 64
POOL_DIM = POOL_GROUPS * POOL_CH
POOL_HALO = 16

HEAD_DIM = 64
GQA_HEADS = 12
GQA_KV_HEADS = 4
GQA_GROUP = GQA_HEADS // GQA_KV_HEADS
GQA_Q_DIM = GQA_HEADS * HEAD_DIM
GQA_KV_DIM = GQA_KV_HEADS * HEAD_DIM

MLA_HEADS = 8
MLA_NOPE = 64
MLA_ROPE = 32
MLA_QK = MLA_NOPE + MLA_ROPE
MLA_V = 64
MLA_Q_LORA = 384
MLA_KV_LORA = 256
MLA_PAIR_W = 2 * LANES

NA_HEADS = 8
NA_HEAD_DIM = 64
NA_DIM = NA_HEADS * NA_HEAD_DIM
NA_ROWS = 8
NA_COLS = 16

FFN_DIM = -(-8 * D_MODEL // (3 * 256)) * 256
FFN_CHUNK = 256

NEG_BIG = -1e30
LOG2E = 1.4426950408889634
GQA_QSCALE = HEAD_DIM ** -0.5 * LOG2E
MLA_QSCALE = MLA_QK ** -0.5 * LOG2E
NA_QSCALE = NA_HEAD_DIM ** -0.5 * LOG2E
PROJ_SUB = 256
ATTN_SUB = 256
ATTN_KEY_CHUNK = 256

VMEM_LIMIT = 56 * 1024 * 1024

_NT = (((1,), (1,)), ((), ()))


def _params(n_axes):
    return pltpu.CompilerParams(dimension_semantics=("arbitrary",) * n_axes,
                                vmem_limit_bytes=VMEM_LIMIT)


def _modulate(x, gain, shift, scale):
    ms = jnp.mean(x * x, axis=-1, keepdims=True)
    return x * lax.rsqrt(ms + EPS) * gain * (1.0 + scale) + shift


def _row_rmsnorm(x, gain):
    ms = jnp.mean(x * x, axis=-1, keepdims=True)
    return x * lax.rsqrt(ms + EPS) * gain


def _lane(shape):
    return lax.broadcasted_iota(jnp.int32, shape, 1)


def _head_rmsnorm(slab, same_head, gain, n):
    ssq = jnp.dot((slab * slab).astype(BF16), same_head, preferred_element_type=F32)
    return slab * lax.rsqrt(ssq + n * EPS) * gain


def _rope(blk, cos, sin_signed):
    return blk * cos + pltpu.roll(blk, LANES // 2, 1) * sin_signed


def _ada_kernel(c_ref, w_ref, b_ref, o_ref):
    cnd = c_ref[...]
    act = cnd * jax.nn.sigmoid(cnd)
    o_ref[...] = jnp.dot(act.astype(BF16), w_ref[...].astype(BF16),
                         preferred_element_type=F32) + b_ref[...]


def _ada(cond, w, b):
    n_rows, d = cond.shape
    n = w.shape[1]
    tn = 1024
    return pl.pallas_call(
        _ada_kernel,
        grid=(n // tn,),
        in_specs=[pl.BlockSpec((n_rows, d), lambda j: (0, 0)),
                  pl.BlockSpec((d, tn), lambda j: (0, j)),
                  pl.BlockSpec((1, tn), lambda j: (0, j))],
        out_specs=pl.BlockSpec((n_rows, tn), lambda j: (0, j)),
        out_shape=jax.ShapeDtypeStruct((n_rows, n), F32),
        compiler_params=_params(1),
        name="ada",
    )(cond, w, b.reshape(1, n))


def _slabs(src, c0, width):
    return [src[:, c0 + j * SLAB:c0 + (j + 1) * SLAB] for j in range(width // SLAB)]


def _norm_rope_store(slabs, same_head, gain, n, dst_ref, rows, rope_tables=None):
    for j, slab in enumerate(slabs):
        slab = _head_rmsnorm(slab, same_head, gain, n)
        for blk in range(SLAB // LANES):
            part = slab[:, blk * LANES:(blk + 1) * LANES]
            if rope_tables is not None:
                part = _rope(part, *rope_tables)
            o = j * SLAB + blk * LANES
            dst_ref[0, rows, o:o + LANES] = part.astype(BF16)


def _software_pipeline(n_rows, stages):
    sub = min(n_rows, PROJ_SUB)
    tiles = [slice(r, r + sub) for r in range(0, n_rows, sub)]
    states = [{} for _ in tiles]
    for it in range(len(tiles) + len(stages) - 1):
        for s, stage in enumerate(stages):
            t = it - s
            if stage is not None and 0 <= t < len(tiles):
                stage(tiles[t], states[t])


def _inproj0_kernel(x_ref, sh_ref, sc_ref, g_ref, w_ref, same_ref, qg_ref, kg_ref, cos_ref, sin_ref,
                    a_ref, q_ref, k_ref, v_ref, *, rope):
    q0 = POOL_DIM
    k0 = q0 + GQA_Q_DIM
    v0 = k0 + GQA_KV_DIM

    def project(rows, st):
        h = _modulate(x_ref[0, rows, :], g_ref[...], sh_ref[0, 0], sc_ref[0, 0]).astype(BF16)
        st["p"] = jnp.dot(h, w_ref[...], preferred_element_type=F32)

    def epilogue(rows, st):
        p = st["p"]
        a_ref[0, rows, :] = p[:, :POOL_DIM]
        tables = (cos_ref[rows, :], sin_ref[rows, :]) if rope else None
        for c0, width, gain_ref, dst in ((q0, GQA_Q_DIM, qg_ref, q_ref), (k0, GQA_KV_DIM, kg_ref, k_ref)):
            _norm_rope_store(_slabs(p, c0, width), same_ref[...], gain_ref[...], HEAD_DIM, dst, rows,
                             tables)
        v_ref[0, rows, :] = p[:, v0:v0 + GQA_KV_DIM].astype(BF16)

    _software_pipeline(x_ref.shape[1], [project, None, epilogue])


def _inproj0(x, mods, ctx_row, gain, w, same_head, q_gain, k_gain, cos, sin, *, rope, tm):
    b, t, d = x.shape
    row = lambda bi, i: (bi, i, 0)
    return pl.pallas_call(
        functools.partial(_inproj0_kernel, rope=rope),
        grid=(b, t // tm),
        in_specs=[pl.BlockSpec((1, tm, d), row),
                  _mod_spec(mods, 0, ctx_row), _mod_spec(mods, 1, ctx_row),
                  _resident(gain), _resident(w), _resident(same_head),
                  _resident(q_gain), _resident(k_gain),
                  pl.BlockSpec((tm, LANES), lambda bi, i: (i, 0)),
                  pl.BlockSpec((tm, LANES), lambda bi, i: (i, 0))],
        out_specs=[pl.BlockSpec((1, tm, POOL_DIM), row),
                   pl.BlockSpec((1, tm, GQA_Q_DIM), row),
                   pl.BlockSpec((1, tm, GQA_KV_DIM), row),
                   pl.BlockSpec((1, tm, GQA_KV_DIM), row)],
        out_shape=[jax.ShapeDtypeStruct((b, t, POOL_DIM), F32),
                   jax.ShapeDtypeStruct((b, t, GQA_Q_DIM), BF16),
                   jax.ShapeDtypeStruct((b, t, GQA_KV_DIM), BF16),
                   jax.ShapeDtypeStruct((b, t, GQA_KV_DIM), BF16)],
        compiler_params=_params(2),
        name="inproj0_rope" if rope else "inproj0_ctx",
    )(x, mods, mods, gain, w, same_head, q_gain, k_gain, cos, sin)


def _inproj1_kernel(x_ref, sh_ref, sc_ref, g_ref, w_ref, qag_ref, kvag_ref, wuq_ref, wukv_ref,
                    sh64_ref, shmla_ref, mqg_ref, mkg_ref, nqg_ref, nkg_ref, cos_ref, sin_ref,
                    *out_refs, rope, want_q):
    if want_q:
        mq_ref, mk_ref, mv_ref, nq_ref, nk_ref, nv_ref = out_refs
    else:
        mk_ref, mv_ref, nk_ref, nv_ref = out_refs
    o_ckv = MLA_Q_LORA
    o_kr = o_ckv + MLA_KV_LORA
    o_nq = o_kr + LANES
    o_nk = o_nq + NA_DIM
    o_nv = o_nk + NA_DIM
    mla_w = MLA_HEADS * LANES

    def project(rows, st):
        h = _modulate(x_ref[0, rows, :], g_ref[...], sh_ref[0, 0], sc_ref[0, 0]).astype(BF16)
        st["p"] = jnp.dot(h, w_ref[...], preferred_element_type=F32)

    def up_project(rows, st):
        p = st["p"]
        if want_q:
            st["qq"] = jnp.dot(_row_rmsnorm(p[:, :MLA_Q_LORA], qag_ref[...]).astype(BF16),
                               wuq_ref[...], preferred_element_type=F32)
        st["kv"] = jnp.dot(_row_rmsnorm(p[:, o_ckv:o_ckv + MLA_KV_LORA], kvag_ref[...]).astype(BF16),
                           wukv_ref[...], preferred_element_type=F32)

    def epilogue(rows, st):
        p, kv = st["p"], st["kv"]
        tables = (cos_ref[rows, :], sin_ref[rows, :]) if rope else None
        if want_q:
            _norm_rope_store(_slabs(st["qq"], 0, mla_w), shmla_ref[...], mqg_ref[...], MLA_QK,
                             mq_ref, rows, tables)
            _norm_rope_store(_slabs(p, o_nq, NA_DIM), sh64_ref[...], nqg_ref[...], NA_HEAD_DIM,
                             nq_ref, rows)
        kr_blk = p[:, o_kr:o_kr + LANES]
        kr_slab = jnp.concatenate([kr_blk, kr_blk], axis=1)
        _norm_rope_store([slab + kr_slab for slab in _slabs(kv, 0, mla_w)], shmla_ref[...],
                         mkg_ref[...], MLA_QK, mk_ref, rows, tables)
        mv_ref[0, rows, :] = kv[:, mla_w:mla_w + MLA_HEADS * MLA_V].astype(BF16)
        _norm_rope_store(_slabs(p, o_nk, NA_DIM), sh64_ref[...], nkg_ref[...], NA_HEAD_DIM,
                         nk_ref, rows)
        nv_ref[0, rows, :] = p[:, o_nv:o_nv + NA_DIM].astype(BF16)

    _software_pipeline(x_ref.shape[1], [project, up_project, epilogue])


def _inproj1(x, mods, ctx_row, gain, w, qag, kvag, wuq, wukv, sh64, shmla, mqg, mkg, nqg, nkg,
             cos, sin, *, rope, want_q, tm):
    b, t, d = x.shape
    row = lambda bi, i: (bi, i, 0)
    full = _resident
    mla_w = (MLA_HEADS // 2) * MLA_PAIR_W
    outs = [(mla_w, "mk"), (MLA_HEADS * MLA_V, "mv"), (NA_DIM, "nk"), (NA_DIM, "nv")]
    if want_q:
        outs = [(mla_w, "mq")] + outs[:2] + [(NA_DIM, "nq")] + outs[2:]
    return pl.pallas_call(
        functools.partial(_inproj1_kernel, rope=rope, want_q=want_q),
        grid=(b, t // tm),
        in_specs=[pl.BlockSpec((1, tm, d), row),
                  _mod_spec(mods, 0, ctx_row), _mod_spec(mods, 1, ctx_row),
                  full(gain), full(w), full(qag), full(kvag), full(wuq), full(wukv),
                  full(sh64), full(shmla), full(mqg), full(mkg), full(nqg), full(nkg),
                  pl.BlockSpec((tm, LANES), lambda bi, i: (i, 0)),
                  pl.BlockSpec((tm, LANES), lambda bi, i: (i, 0))],
        out_specs=[pl.BlockSpec((1, tm, n), row) for n, _ in outs],
        out_shape=[jax.ShapeDtypeStruct((b, t, n), BF16) for n, _ in outs],
        compiler_params=_params(2),
        name="inproj1_lat" if want_q else "inproj1_ctx",
    )(x, mods, mods, gain, w, qag, kvag, wuq, wukv, sh64, shmla, mqg, mkg, nqg, nkg, cos, sin)


def _fill_pair_kv(k, v, mask_a, mask_b, ka_ref, kb_ref, va_ref, vb_ref, r0):
    n = k.shape[0]
    first = _lane(v.shape) < HEAD_DIM
    one = jnp.ones_like(v)
    ka_ref[r0:r0 + n, :] = k * mask_a
    kb_ref[r0:r0 + n, :] = k * mask_b
    va_ref[r0:r0 + n, :] = jnp.where(first, v, one)
    vb_ref[r0:r0 + n, :] = jnp.where(first, one, v)


def _fill_pair_kv_t(k, v, mask_a, mask_b, ka_ref, kb_ref, vat_ref, vbt_ref, r0):
    n = k.shape[0]
    if mask_a is None:
        ka_ref[r0:r0 + n, :] = k[:, :LANES]
        kb_ref[r0:r0 + n, :] = k[:, LANES:]
    else:
        ka_ref[r0:r0 + n, :] = k * mask_a
        kb_ref[r0:r0 + n, :] = k * mask_b
    vt = v.astype(F32).T
    first = lax.broadcasted_iota(jnp.int32, vt.shape, 0) < HEAD_DIM
    vat_ref[:, r0:r0 + n] = jnp.where(first, vt, 1.0).astype(BF16)
    vbt_ref[:, r0:r0 + n] = jnp.where(first, 1.0, vt).astype(BF16)


def _pair_softmax_pv(q_tiles, ka_ref, kb_ref, vat_ref, vbt_ref):
    n_keys = ka_ref.shape[0]
    chunk = min(n_keys, ATTN_KEY_CHUNK)
    n_chunks = n_keys // chunk
    chains = [(q, k_ref, vt_ref) for qa, qb in q_tiles
              for q, k_ref, vt_ref in ((qa, ka_ref, vat_ref), (qb, kb_ref, vbt_ref))]

    def scores(j):
        return [lax.dot_general(k_ref[j * chunk:(j + 1) * chunk, :], q, _NT,
                                preferred_element_type=F32) for q, k_ref, _ in chains]

    s = scores(0)
    run_max = [None] * len(chains)
    acc = [None] * len(chains)
    for j in range(n_chunks):
        s_next = scores(j + 1) if j + 1 < n_chunks else None
        for c, (_, _, vt_ref) in enumerate(chains):
            mx = s[c].max(axis=0, keepdims=True)
            if j > 0:
                mx = jnp.maximum(mx, run_max[c])
            p = jnp.exp2((s[c] - mx).astype(BF16))
            pv = jnp.dot(vt_ref[:, j * chunk:(j + 1) * chunk], p, preferred_element_type=F32)
            acc[c] = pv if j == 0 else acc[c] * jnp.exp2(run_max[c] - mx) + pv
            run_max[c] = mx
        s = s_next
    outs = []
    for t in range(len(q_tiles)):
        oa, ob = acc[2 * t], acc[2 * t + 1]
        top = oa[:HEAD_DIM] / oa[HEAD_DIM:HEAD_DIM + 1]
        bot = ob[HEAD_DIM:] / ob[0:1]
        outs.append(jnp.concatenate([top, bot], axis=0).T)
    return outs


def _pair_attn_kernel(*refs, use_lat, steps_per_pair, dk, split):
    if use_lat:
        q_ref, ma_ref, mb_ref, kc_ref, vc_ref, kl_ref, vl_ref, o_ref, ka, kb, vat, vbt = refs
        segs = ((kc_ref, vc_ref), (kl_ref, vl_ref))
    else:
        q_ref, ma_ref, mb_ref, kc_ref, vc_ref, o_ref, ka, kb, vat, vbt = refs
        segs = ((kc_ref, vc_ref),)

    @pl.when(jnp.logical_and(pl.program_id(2) == 0, pl.program_id(1) % steps_per_pair == 0))
    def _():
        r0 = 0
        for k_ref, v_ref in segs:
            masks = (None, None) if split else (ma_ref[...], mb_ref[...])
            _fill_pair_kv_t(k_ref[0], v_ref[0], *masks, ka, kb, vat, vbt, r0)
            r0 += k_ref.shape[1]

    sub = min(q_ref.shape[1], ATTN_SUB)
    tiles = [(r0, blk) for blk in range(q_ref.shape[2] // dk) for r0 in range(0, q_ref.shape[1], sub)]

    def q_pair(r0, blk):
        q = q_ref[0, r0:r0 + sub, blk * dk:(blk + 1) * dk]
        return (q[:, :LANES], q[:, LANES:]) if split else (q, q)

    outs = _pair_softmax_pv([q_pair(r0, blk) for r0, blk in tiles], ka, kb, vat, vbt)
    for (r0, blk), o in zip(tiles, outs):
        o_ref[0, r0:r0 + sub, blk * LANES:(blk + 1) * LANES] = o.astype(o_ref.dtype)


def _pair_attn(q, mask_a, mask_b, kc, vc, kl, vl, *, dk, n_pairs, q_per_pair, blocks_per_step, tq,
               name):
    b, t, _ = q.shape
    split = dk == 2 * LANES
    dk_head = dk // 2 if split else dk
    assert q_per_pair % blocks_per_step == 0
    steps_per_pair = q_per_pair // blocks_per_step
    n_qblk = n_pairs * steps_per_pair
    use_lat = kl is not None
    qmap = lambda bi, c, i: (bi, i, c)
    kvmap = lambda bi, c, i: (bi, 0, c // steps_per_pair)
    const = lambda bi, c, i: (0, 0)
    in_specs = [pl.BlockSpec((1, tq, blocks_per_step * dk), qmap),
                pl.BlockSpec((1, dk), const),
                pl.BlockSpec((1, dk), const),
                pl.BlockSpec((1, kc.shape[1], dk), kvmap),
                pl.BlockSpec((1, vc.shape[1], LANES), kvmap)]
    args = [q, mask_a, mask_b, kc, vc]
    if use_lat:
        in_specs += [pl.BlockSpec((1, kl.shape[1], dk), kvmap),
                     pl.BlockSpec((1, vl.shape[1], LANES), kvmap)]
        args += [kl, vl]
    n_keys = kc.shape[1] + (kl.shape[1] if use_lat else 0)
    return pl.pallas_call(
        functools.partial(_pair_attn_kernel, use_lat=use_lat, steps_per_pair=steps_per_pair, dk=dk,
                          split=split),
        grid=(b, n_qblk, t // tq),
        in_specs=in_specs,
        out_specs=pl.BlockSpec((1, tq, blocks_per_step * LANES), qmap),
        out_shape=jax.ShapeDtypeStruct((b, t, n_pairs * q_per_pair * LANES), BF16),
        scratch_shapes=[pltpu.VMEM((n_keys, dk_head), BF16), pltpu.VMEM((n_keys, dk_head), BF16),
                        pltpu.VMEM((LANES, n_keys), BF16), pltpu.VMEM((LANES, n_keys), BF16)],
        compiler_params=_params(3),
        name=name,
    )(*args)


def _na_kernel(q_ref, kc_ref, vc_ref, k_ref, v_ref, bias_ref, o_ref, ka, kb, va, vb,
               *, rows_per_step, n_rows):
    rb = pl.program_id(2)
    tc = kc_ref.shape[1]
    n_win = NA_ROWS * GRID_W

    @pl.when(rb == 0)
    def _():
        lane = _lane((1, LANES))
        mask_a = jnp.where(lane < NA_HEAD_DIM, 1.0, 0.0).astype(BF16)
        mask_b = jnp.where(lane < NA_HEAD_DIM, 0.0, 1.0).astype(BF16)
        _fill_pair_kv(kc_ref[0], vc_ref[0], mask_a, mask_b, ka, kb, va, vb, 0)
        _fill_pair_kv(k_ref[0], v_ref[0], mask_a, mask_b, ka, kb, va, vb, tc)

    heads = ((ka, va), (kb, vb))
    q_all = q_ref[0]
    rows = []
    for i in range(rows_per_step):
        r = rb * rows_per_step + i
        r0 = jnp.clip(r - NA_ROWS // 2, 0, n_rows - NA_ROWS)
        rows.append((pl.multiple_of(tc + r0 * GRID_W, GRID_W), r0 - r + (NA_ROWS - 1)))

    s_ctx = [lax.dot_general(q_all, k_s[0:tc, :], _NT, preferred_element_type=F32)
             for k_s, _ in heads]
    s_win = []
    for i, (kstart, j0) in enumerate(rows):
        q = q_all[i * GRID_W:(i + 1) * GRID_W]
        per_head = []
        for hh, (k_s, _) in enumerate(heads):
            bias = jnp.concatenate([bias_ref[hh, j0 + 2 * c] for c in range(NA_ROWS // 2)], axis=1)
            per_head.append(lax.dot_general(q, k_s[pl.ds(kstart, n_win), :], _NT,
                                            preferred_element_type=F32) + bias)
        s_win.append(per_head)

    p_ctx = [[], []]
    p_win = []
    for i in range(rows_per_step):
        per_head = []
        for hh in range(2):
            sc = s_ctx[hh][i * GRID_W:(i + 1) * GRID_W]
            sw = s_win[i][hh]
            mx = jnp.maximum(sc.max(axis=-1, keepdims=True), sw.max(axis=-1, keepdims=True))
            p_ctx[hh].append(jnp.exp2((sc - mx).astype(BF16)))
            per_head.append(jnp.exp2((sw - mx).astype(BF16)))
        p_win.append(per_head)

    n_ctx = [jnp.dot(jnp.concatenate(p_ctx[hh], axis=0), v_s[0:tc, :], preferred_element_type=F32)
             for hh, (_, v_s) in enumerate(heads)]
    outs = []
    for i, (kstart, _) in enumerate(rows):
        nums = [n_ctx[hh][i * GRID_W:(i + 1) * GRID_W]
                + jnp.dot(p_win[i][hh], v_s[pl.ds(kstart, n_win), :], preferred_element_type=F32)
                for hh, (_, v_s) in enumerate(heads)]
        first = _lane(nums[0].shape) < NA_HEAD_DIM
        num = jnp.where(first, nums[0], nums[1])
        den = pltpu.roll(jnp.where(first, nums[1], nums[0]), NA_HEAD_DIM, 1)
        outs.append((num / den).astype(o_ref.dtype))
    o_ref[0] = jnp.concatenate(outs, axis=0)


def _na_attn(q, kc, vc, k, v, bias, *, rows_per_step):
    b, s, _ = q.shape
    n_rows = s // GRID_W
    n_pairs = NA_HEADS // 2
    tq = rows_per_step * GRID_W
    n_keys = kc.shape[1] + s
    return pl.pallas_call(
        functools.partial(_na_kernel, rows_per_step=rows_per_step, n_rows=n_rows),
        grid=(b, n_pairs, n_rows // rows_per_step),
        in_specs=[pl.BlockSpec((1, tq, LANES), lambda bi, m, i: (bi, i, m)),
                  pl.BlockSpec((1, kc.shape[1], LANES), lambda bi, m, i: (bi, 0, m)),
                  pl.BlockSpec((1, vc.shape[1], LANES), lambda bi, m, i: (bi, 0, m)),
                  pl.BlockSpec((1, s, LANES), lambda bi, m, i: (bi, 0, m)),
                  pl.BlockSpec((1, s, LANES), lambda bi, m, i: (bi, 0, m)),
                  pl.BlockSpec((2,) + bias.shape[1:], lambda bi, m, i: (m, 0, 0, 0))],
        out_specs=pl.BlockSpec((1, tq, LANES), lambda bi, m, i: (bi, i, m)),
        out_shape=jax.ShapeDtypeStruct((b, s, NA_DIM), BF16),
        scratch_shapes=[pltpu.VMEM((n_keys, LANES), BF16) for _ in range(4)],
        compiler_params=_params(3),
        name="na_attn",
    )(q, kc, vc, k, v, bias)


def _pool_kernel(a_ref, pw_ref, ps_ref, o_ref, pad_ref, *, t_len, chunk):
    pad_ref[0:POOL_HALO, :] = jnp.zeros((POOL_HALO, POOL_DIM), F32)
    pad_ref[POOL_HALO + t_len:2 * POOL_HALO + t_len, :] = jnp.zeros((POOL_HALO, POOL_DIM), F32)
    pad_ref[POOL_HALO:POOL_HALO + t_len, :] = a_ref[0]
    lane = _lane((chunk, POOL_DIM))
    g0 = lane < POOL_CH
    g1 = lane < 2 * POOL_CH
    g2 = lane < 3 * POOL_CH
    half_w = jnp.where(g0, 1, jnp.where(g1, 2, jnp.where(g2, 4, 8)))
    for c in range(t_len // chunk):
        base = c * chunk

        def ld(off, base=base):
            return pad_ref[POOL_HALO + base + off:POOL_HALO + base + off + chunk, :]

        a0 = ld(0)
        w2 = ld(-1) + a0
        w4 = w2 + ld(-2) + ld(1)
        w8 = w4 + ld(-4) + ld(-3) + ld(2) + ld(3)
        w16 = w8
        for off in (-8, -7, -6, -5, 4, 5, 6, 7):
            w16 = w16 + ld(off)
        tok = base + lax.broadcasted_iota(jnp.int32, (chunk, POOL_DIM), 0)
        cnt = (jnp.minimum(tok + half_w, t_len) - jnp.maximum(tok - half_w, 0)).astype(F32)
        wsum = jnp.where(g0, w2, jnp.where(g1, w4, jnp.where(g2, w8, w16)))
        dlt = wsum / cnt - a0
        y = jnp.dot(dlt.astype(BF16), pw_ref[...], preferred_element_type=F32) * ps_ref[...]
        o_ref[0, base:base + chunk, :] = y.astype(o_ref.dtype)


def _pool(a, pw_bd, pscale):
    b, t, _ = a.shape
    chunk = min(t, 256)
    return pl.pallas_call(
        functools.partial(_pool_kernel, t_len=t, chunk=chunk),
        grid=(b,),
        in_specs=[pl.BlockSpec((1, t, POOL_DIM), lambda bi: (bi, 0, 0)),
                  pl.BlockSpec((POOL_DIM, POOL_DIM), lambda bi: (0, 0)),
                  pl.BlockSpec((1, POOL_DIM), lambda bi: (0, 0))],
        out_specs=pl.BlockSpec((1, t, POOL_DIM), lambda bi: (bi, 0, 0)),
        out_shape=jax.ShapeDtypeStruct((b, t, POOL_DIM), BF16),
        scratch_shapes=[pltpu.VMEM((t + 2 * POOL_HALO, POOL_DIM), F32)],
        compiler_params=_params(1),
        name="pool",
    )(a, pw_bd, pscale)


def _post_kernel(x_ref, g1_ref, sh_ref, sc_ref, g2_ref, m0_ref, m1_ref, w0_ref, w1_ref, g_ref,
                 wg_ref, wu_ref, wd_ref, o_ref):
    mix = (jnp.dot(m0_ref[0], w0_ref[...], preferred_element_type=F32)
           + jnp.dot(m1_ref[0], w1_ref[...], preferred_element_type=F32))
    x = x_ref[0] + g1_ref[0, 0] * mix
    h = _modulate(x, g_ref[...], sh_ref[0, 0], sc_ref[0, 0]).astype(BF16)
    n_chunks = wg_ref.shape[1] // FFN_CHUNK

    def gate_up(c):
        cols = slice(c * FFN_CHUNK, (c + 1) * FFN_CHUNK)
        return (jnp.dot(h, wg_ref[:, cols], preferred_element_type=F32),
                jnp.dot(h, wu_ref[:, cols], preferred_element_type=F32))

    acc = None
    gt, up = gate_up(0)
    for c in range(n_chunks):
        nxt = gate_up(c + 1) if c + 1 < n_chunks else None
        act = (gt * jax.nn.sigmoid(gt) * up).astype(BF16)
        part = jnp.dot(act, wd_ref[c * FFN_CHUNK:(c + 1) * FFN_CHUNK, :], preferred_element_type=F32)
        acc = part if acc is None else acc + part
        if nxt is not None:
            gt, up = nxt
    o_ref[0] = x + g2_ref[0, 0] * acc


def _mod_spec(mods, k, ctx_row):
    blk = (1, 1, 1, mods.shape[-1])
    if ctx_row is None:
        return pl.BlockSpec(blk, lambda bi, i: (bi, k, 0, 0))
    return pl.BlockSpec(blk, lambda bi, i: (ctx_row, k, 0, 0))


def _resident(a):
    nd = a.ndim
    return pl.BlockSpec(a.shape, lambda bi, i: (0,) * nd, pipeline_mode=pl.Buffered(1))


def _post(x, mods, ctx_row, m0, m1, w0, w1, gain, wg, wu, wd, *, tm, name):
    b, t, d = x.shape
    row = lambda bi, i: (bi, i, 0)
    return pl.pallas_call(
        _post_kernel,
        grid=(b, t // tm),
        in_specs=[pl.BlockSpec((1, tm, d), row),
                  _mod_spec(mods, 2, ctx_row), _mod_spec(mods, 3, ctx_row),
                  _mod_spec(mods, 4, ctx_row), _mod_spec(mods, 5, ctx_row),
                  pl.BlockSpec((1, tm, m0.shape[2]), row),
                  pl.BlockSpec((1, tm, m1.shape[2]), row),
                  _resident(w0), _resident(w1), _resident(gain),
                  _resident(wg), _resident(wu), _resident(wd)],
        out_specs=pl.BlockSpec((1, tm, d), row),
        out_shape=jax.ShapeDtypeStruct((b, t, d), F32),
        compiler_params=_params(2),
        name=name,
    )(x, mods, mods, mods, mods, m0, m1, w0, w1, gain, wg, wu, wd)


class _LaneLayout:
    def __init__(self, dim, slot, axis, freq, x2, n_freq):
        self.dim, self.slot, self.axis, self.freq, self.x2, self.n_freq = dim, slot, axis, freq, x2, n_freq

    def rope_tables(self, seq):
        tok = np.arange(seq)
        inv = (ROPE_THETA ** (-np.arange(self.n_freq, dtype=np.float32) * 2.0
                              / (2 * self.n_freq))).astype(np.float32).astype(np.float64)
        pos = np.stack([tok // GRID_W, tok % GRID_W]).astype(np.float64)
        rot = self.axis >= 0
        ang = pos[np.maximum(self.axis, 0)].T * inv[self.freq][None, :]
        cos = np.where(rot[None], np.cos(ang), 1.0)
        sin = np.where(rot[None], np.sin(ang) * np.where(self.x2, 1.0, -1.0)[None], 0.0)
        return jnp.asarray(cos, F32), jnp.asarray(sin, F32)


def _gqa_layout():
    lane = np.arange(LANES)
    x2, slot, axis, freq = lane // 64, (lane % 64) // 32, (lane % 32) // 16, lane % 16
    return _LaneLayout((axis * 2 + x2) * 16 + freq, slot, axis, freq, x2.astype(bool), HEAD_DIM // 4)


def _mla_layout():
    lane = np.arange(LANES)
    is_rot = (lane % 64) < 16
    x2 = lane >= 64
    axis = np.where(is_rot, (lane % 64) // 8, -1)
    freq = np.where(is_rot, lane % 8, 0)
    nope = np.where(lane < 64, lane - 16, 48 + lane - 80)
    dim = np.where(is_rot, MLA_NOPE + (np.maximum(axis, 0) * 2 + x2) * 8 + freq, nope)
    dim = np.where(lane >= MLA_QK, -1, dim)
    return _LaneLayout(dim, np.zeros(LANES, int), axis, freq, x2 & is_rot, MLA_ROPE // 4)


def _take_cols(w, idx):
    wz = jnp.concatenate([w, jnp.zeros((w.shape[0], 1), w.dtype)], axis=1)
    return wz[:, np.where(idx < 0, w.shape[1], idx)]


def _na_bias_table(rpb):
    cols = np.arange(GRID_W)
    c0 = np.clip(cols - NA_COLS // 2, 0, GRID_W - NA_COLS)
    kc = np.arange(GRID_W)
    inside = (kc[None, :] >= c0[:, None]) & (kc[None, :] < c0[:, None] + NA_COLS)
    dc = kc[None, :] - cols[:, None] + (NA_COLS - 1)
    onehot = (np.arange(2 * NA_COLS - 1)[:, None, None] == dc[None]) & inside[None]
    sel = jnp.einsum("hjd,dck->hjck", rpb * LOG2E, jnp.asarray(onehot, F32),
                     precision=lax.Precision.HIGHEST)
    t = jnp.where(jnp.asarray(inside)[None, None], sel, NEG_BIG)
    return jnp.concatenate([t[:, :-1], t[:, 1:]], axis=-1).astype(F32)


def kernel(x, c, ctx, c_ctx, l0_ada_w, l0_ada_b, l0_norm_mix, l0_norm_ffn, l0_w_in, l0_pool_w, l0_pool_scale, l0_q_gain, l0_k_gain, l0_w_out, l0_ffn_w_gate, l0_ffn_w_up, l0_ffn_w_down, l1_ada_w, l1_ada_b, l1_norm_mix, l1_norm_ffn, l1_w_in, l1_mla_q_a_gain, l1_mla_kv_a_gain, l1_mla_w_uq, l1_mla_w_ukv, l1_mla_q_gain, l1_mla_k_gain, l1_na_q_gain, l1_na_k_gain, l1_na_rpb, l1_w_out, l1_ffn_w_gate, l1_ffn_w_up, l1_ffn_w_down):
    b, s, d = x.shape
    tc = ctx.shape[1]
    tm_lat = 512
    tp_lat = 1024
    tq_lat = 1024

    cond = jnp.concatenate([c, c_ctx[None, :], jnp.zeros((7, d), F32)], axis=0)

    def mods(ada_w, ada_b):
        return _ada(cond, ada_w, ada_b).reshape(cond.shape[0], 6, 1, d)

    def ffn_weights(wg, wu, wd):
        return wg.astype(BF16), wu.astype(BF16), wd.astype(BF16)

    row1 = lambda v: v.reshape(1, -1)
    tile4 = lambda v, mul: (jnp.tile(v, SLAB // v.shape[0]) * mul).reshape(1, -1)
    sh64 = jnp.asarray(np.kron(np.eye(SLAB // HEAD_DIM), np.ones((HEAD_DIM, HEAD_DIM))), BF16)

    def slab_gain(g, lay, mul):
        per_lane = jnp.where(jnp.asarray(lay.dim >= 0), g[np.maximum(lay.dim, 0)], 0.0) * mul
        return jnp.tile(per_lane, SLAB // LANES).reshape(1, -1)

    def slab_same_head(lay):
        head = np.concatenate([np.where(lay.dim >= 0, blk * 2 + lay.slot, -1)
                               for blk in range(SLAB // LANES)])
        return jnp.asarray((head[:, None] == head[None, :]) & (head[:, None] >= 0), BF16)

    def pair_order(w_q, axis):
        shp = w_q.shape
        split = shp[:axis] + (GQA_KV_HEADS // 2, 2, GQA_GROUP, HEAD_DIM) + shp[axis + 1:]
        perm = list(range(len(split)))
        perm[axis + 1], perm[axis + 2] = perm[axis + 2], perm[axis + 1]
        return w_q.reshape(split).transpose(perm).reshape(shp)

    mods0 = mods(l0_ada_w, l0_ada_b)
    glay = _gqa_layout()
    q_cols = np.concatenate([((2 * m + glay.slot) * GQA_GROUP + g) * HEAD_DIM + glay.dim
                             for m in range(GQA_KV_HEADS // 2) for g in range(GQA_GROUP)])
    k_cols = np.concatenate([(2 * m + glay.slot) * HEAD_DIM + glay.dim
                             for m in range(GQA_KV_HEADS // 2)])
    q0, k0, v0 = POOL_DIM, POOL_DIM + GQA_Q_DIM, POOL_DIM + GQA_Q_DIM + GQA_KV_DIM
    w_in0 = jnp.concatenate([l0_w_in[:, :q0], l0_w_in[:, q0:k0][:, q_cols],
                             l0_w_in[:, k0:v0][:, k_cols], l0_w_in[:, v0:]], axis=1).astype(BF16)
    cos0, sin0 = glay.rope_tables(s)
    ones_c = jnp.ones((tc, LANES), F32)
    zeros_c = jnp.zeros((tc, LANES), F32)
    qg4 = slab_gain(l0_q_gain, glay, HEAD_DIM ** 0.5 * GQA_QSCALE)
    kg4 = slab_gain(l0_k_gain, glay, HEAD_DIM ** 0.5)
    sh_gqa = slab_same_head(glay)
    gmask_a = jnp.asarray((glay.slot == 0)[None], BF16)
    gmask_b = 1 - gmask_a

    a_l, q_l, k_l, v_l = _inproj0(x, mods0, None, row1(l0_norm_mix), w_in0, sh_gqa, qg4, kg4,
                                  cos0, sin0, rope=True, tm=tp_lat)
    a_c, q_c, k_c, v_c = _inproj0(ctx, mods0, b, row1(l0_norm_mix), w_in0, sh_gqa, qg4, kg4,
                                  ones_c, zeros_c, rope=False, tm=tc)
    n_kv_pairs = GQA_KV_HEADS // 2
    attn_l = _pair_attn(q_l, gmask_a, gmask_b, k_c, v_c, k_l, v_l, dk=LANES, n_pairs=n_kv_pairs,
                        q_per_pair=GQA_GROUP, blocks_per_step=1, tq=tq_lat, name="gqa_lat")
    attn_c = _pair_attn(q_c, gmask_a, gmask_b, k_c, v_c, None, None, dk=LANES, n_pairs=n_kv_pairs,
                        q_per_pair=GQA_GROUP, blocks_per_step=GQA_GROUP, tq=tc, name="gqa_ctx")

    eye = jnp.eye(POOL_GROUPS, dtype=F32)
    pw_bd = (eye[:, None, :, None] * l0_pool_w[:, :, None, :]).reshape(POOL_DIM, POOL_DIM).astype(BF16)
    pool_l = _pool(a_l, pw_bd, row1(l0_pool_scale))
    pool_c = _pool(a_c, pw_bd, row1(l0_pool_scale))

    w_out0_pool = l0_w_out[:POOL_DIM].astype(BF16)
    w_out0_attn = pair_order(l0_w_out[POOL_DIM:], 0).astype(BF16)
    ffn0 = ffn_weights(l0_ffn_w_gate, l0_ffn_w_up, l0_ffn_w_down)

    x1 = _post(x, mods0, None, pool_l, attn_l, w_out0_pool, w_out0_attn, row1(l0_norm_ffn), *ffn0,
               tm=tm_lat, name="post0_lat")
    xc = _post(ctx, mods0, b, pool_c, attn_c, w_out0_pool, w_out0_attn, row1(l0_norm_ffn), *ffn0,
               tm=tc, name="post0_ctx")

    mods1 = mods(l1_ada_w, l1_ada_b)
    mlay = _mla_layout()
    heads = np.arange(MLA_HEADS)[:, None]
    o_kr = MLA_Q_LORA + MLA_KV_LORA
    kr_block = _take_cols(l1_w_in, np.where(mlay.dim >= MLA_NOPE, o_kr + mlay.dim - MLA_NOPE, -1))
    w_in1 = jnp.concatenate([l1_w_in[:, :o_kr], kr_block, l1_w_in[:, o_kr + MLA_ROPE:]],
                            axis=1).astype(BF16)
    wuq = _take_cols(l1_mla_w_uq, np.where(mlay.dim >= 0, heads * MLA_QK + mlay.dim, -1).reshape(-1)
                     ).astype(BF16)
    kv_w = MLA_NOPE + MLA_V
    k_idx = np.where((mlay.dim >= 0) & (mlay.dim < MLA_NOPE), heads * kv_w + mlay.dim, -1).reshape(-1)
    v_idx = (heads * kv_w + MLA_NOPE + np.arange(MLA_V)[None, :]).reshape(-1)
    wukv = _take_cols(l1_mla_w_ukv, np.concatenate([k_idx, v_idx])).astype(BF16)

    shmla = slab_same_head(mlay)
    mqg = slab_gain(l1_mla_q_gain, mlay, MLA_QK ** 0.5 * MLA_QSCALE)
    mkg = slab_gain(l1_mla_k_gain, mlay, MLA_QK ** 0.5)
    nqg = tile4(l1_na_q_gain, NA_HEAD_DIM ** 0.5 * NA_QSCALE)
    nkg = tile4(l1_na_k_gain, NA_HEAD_DIM ** 0.5)
    cos1, sin1 = mlay.rope_tables(s)
    common = (row1(l1_norm_mix), w_in1, row1(l1_mla_q_a_gain), row1(l1_mla_kv_a_gain), wuq, wukv,
              sh64, shmla, mqg, mkg, nqg, nkg)
    mq, mk, mv, nq, nk, nv = _inproj1(x1, mods1, None, *common, cos1, sin1, rope=True, want_q=True,
                                      tm=tp_lat)
    mkc, mvc, nkc, nvc = _inproj1(xc, mods1, b, *common, ones_c, zeros_c, rope=False, want_q=False,
                                  tm=tc)

    no_mask = jnp.ones((1, MLA_PAIR_W), BF16)
    o_mla = _pair_attn(mq, no_mask, no_mask, mkc, mvc, mk, mv, dk=MLA_PAIR_W,
                       n_pairs=MLA_HEADS // 2, q_per_pair=1, blocks_per_step=1, tq=tq_lat, name="mla")
    o_na = _na_attn(nq, nkc, nvc, nk, nv, _na_bias_table(l1_na_rpb), rows_per_step=8)

    w_out1 = l1_w_out.astype(BF16)
    ffn1 = ffn_weights(l1_ffn_w_gate, l1_ffn_w_up, l1_ffn_w_down)
    return _post(x1, mods1, None, o_mla, o_na, w_out1[:MLA_HEADS * MLA_V], w_out1[MLA_HEADS * MLA_V:],
                 row1(l1_norm_ffn), *ffn1, tm=tm_lat, name="post1_lat")
```

```python
import functools

import numpy as np
import jax
import jax.numpy as jnp
from jax import lax
from jax.experimental import pallas as pl
from jax.experimental.pallas import tpu as pltpu

F32 = jnp.float32
BF16 = jnp.bfloat16

D_MODEL = 1024
GRID_W = 64
ROPE_THETA = 10000.0
EPS = 1e-6
LANES = 128
SLAB = 256

POOL_GROUPS = 4
POOL_CH = 64
POOL_DIM = POOL_GROUPS * POOL_CH
POOL_HALO = 16

HEAD_DIM = 64
GQA_HEADS = 12
GQA_KV_HEADS = 4
GQA_GROUP = GQA_HEADS // GQA_KV_HEADS
GQA_Q_DIM = GQA_HEADS * HEAD_DIM
GQA_KV_DIM = GQA_KV_HEADS * HEAD_DIM

MLA_HEADS = 8
MLA_NOPE = 64
MLA_ROPE = 32
MLA_QK = MLA_NOPE + MLA_ROPE
MLA_V = 64
MLA_Q_LORA = 384
MLA_KV_LORA = 256
MLA_PAIR_W = 2 * LANES

NA_HEADS = 8
NA_HEAD_DIM = 64
NA_DIM = NA_HEADS * NA_HEAD_DIM
NA_ROWS = 8
NA_COLS = 16
NA_BLOCK_ROWS = 4
NA_KEY_TILES = 3
NA_BLOCKS_PER_GROUP = 4

FFN_DIM = -(-8 * D_MODEL // (3 * 256)) * 256
FFN_CHUNK = 256

NEG_BIG = -1e30
LOG2E = 1.4426950408889634
GQA_QSCALE = HEAD_DIM ** -0.5 * LOG2E
MLA_QSCALE = MLA_QK ** -0.5 * LOG2E
NA_QSCALE = NA_HEAD_DIM ** -0.5 * LOG2E
PROJ_SUB = 256
ATTN_SUB = 256
ATTN_KEY_CHUNK = 256

VMEM_LIMIT = 56 * 1024 * 1024

_NT = (((1,), (1,)), ((), ()))


def _params(n_axes):
    return pltpu.CompilerParams(dimension_semantics=("arbitrary",) * n_axes,
                                vmem_limit_bytes=VMEM_LIMIT)


def _modulate(x, gain, shift, scale):
    ms = jnp.mean(x * x, axis=-1, keepdims=True)
    return x * lax.rsqrt(ms + EPS) * gain * (1.0 + scale) + shift


def _row_rmsnorm(x, gain):
    ms = jnp.mean(x * x, axis=-1, keepdims=True)
    return x * lax.rsqrt(ms + EPS) * gain


def _lane(shape):
    return lax.broadcasted_iota(jnp.int32, shape, 1)


def _head_rmsnorm(slab, same_head, gain, n):
    ssq = jnp.dot((slab * slab).astype(BF16), same_head, preferred_element_type=F32)
    return slab * lax.rsqrt(ssq + n * EPS) * gain


def _rope(blk, cos, sin_signed):
    return blk * cos + pltpu.roll(blk, LANES // 2, 1) * sin_signed


def _ada_kernel(c_ref, w_ref, b_ref, o_ref):
    cnd = c_ref[...]
    act = cnd * jax.nn.sigmoid(cnd)
    o_ref[...] = jnp.dot(act.astype(BF16), w_ref[...].astype(BF16),
                         preferred_element_type=F32) + b_ref[...]


def _ada(cond, w, b):
    n_rows, d = cond.shape
    n = w.shape[1]
    tn = 1024
    return pl.pallas_call(
        _ada_kernel,
        grid=(n // tn,),
        in_specs=[pl.BlockSpec((n_rows, d), lambda j: (0, 0)),
                  pl.BlockSpec((d, tn), lambda j: (0, j)),
                  pl.BlockSpec((1, tn), lambda j: (0, j))],
        out_specs=pl.BlockSpec((n_rows, tn), lambda j: (0, j)),
        out_shape=jax.ShapeDtypeStruct((n_rows, n), F32),
        compiler_params=_params(1),
        name="ada",
    )(cond, w, b.reshape(1, n))


def _slabs(src, c0, width):
    return [src[:, c0 + j * SLAB:c0 + (j + 1) * SLAB] for j in range(width // SLAB)]


def _norm_rope_store(slabs, same_head, gain, n, dst_ref, rows, rope_tables=None):
    for j, slab in enumerate(slabs):
        slab = _head_rmsnorm(slab, same_head, gain, n)
        for blk in range(SLAB // LANES):
            part = slab[:, blk * LANES:(blk + 1) * LANES]
            if rope_tables is not None:
                part = _rope(part, *rope_tables)
            o = j * SLAB + blk * LANES
            dst_ref[0, rows, o:o + LANES] = part.astype(BF16)


def _software_pipeline(n_rows, stages):
    sub = min(n_rows, PROJ_SUB)
    tiles = [slice(r, r + sub) for r in range(0, n_rows, sub)]
    states = [{} for _ in tiles]
    for it in range(len(tiles) + len(stages) - 1):
        for s, stage in enumerate(stages):
            t = it - s
            if stage is not None and 0 <= t < len(tiles):
                stage(tiles[t], states[t])


def _inproj0_kernel(x_ref, sh_ref, sc_ref, g_ref, w_ref, same_ref, qg_ref, kg_ref, cos_ref, sin_ref,
                    a_ref, q_ref, k_ref, v_ref, *, rope):
    q0 = POOL_DIM
    k0 = q0 + GQA_Q_DIM
    v0 = k0 + GQA_KV_DIM

    def project(rows, st):
        h = _modulate(x_ref[0, rows, :], g_ref[...], sh_ref[0, 0], sc_ref[0, 0]).astype(BF16)
        st["p"] = jnp.dot(h, w_ref[...], preferred_element_type=F32)

    def epilogue(rows, st):
        p = st["p"]
        a_ref[0, rows, :] = p[:, :POOL_DIM]
        tables = (cos_ref[rows, :], sin_ref[rows, :]) if rope else None
        for c0, width, gain_ref, dst in ((q0, GQA_Q_DIM, qg_ref, q_ref), (k0, GQA_KV_DIM, kg_ref, k_ref)):
            _norm_rope_store(_slabs(p, c0, width), same_ref[...], gain_ref[...], HEAD_DIM, dst, rows,
                             tables)
        v_ref[0, rows, :] = p[:, v0:v0 + GQA_KV_DIM].astype(BF16)

    _software_pipeline(x_ref.shape[1], [project, None, epilogue])


def _inproj0(x, mods, ctx_row, gain, w, same_head, q_gain, k_gain, cos, sin, *, rope, tm):
    b, t, d = x.shape
    row = lambda bi, i: (bi, i, 0)
    return pl.pallas_call(
        functools.partial(_inproj0_kernel, rope=rope),
        grid=(b, t // tm),
        in_specs=[pl.BlockSpec((1, tm, d), row),
                  _mod_spec(mods, 0, ctx_row), _mod_spec(mods, 1, ctx_row),
                  _resident(gain), _resident(w), _resident(same_head),
                  _resident(q_gain), _resident(k_gain),
                  pl.BlockSpec((tm, LANES), lambda bi, i: (i, 0)),
                  pl.BlockSpec((tm, LANES), lambda bi, i: (i, 0))],
        out_specs=[pl.BlockSpec((1, tm, POOL_DIM), row),
                   pl.BlockSpec((1, tm, GQA_Q_DIM), row),
                   pl.BlockSpec((1, tm, GQA_KV_DIM), row),
                   pl.BlockSpec((1, tm, GQA_KV_DIM), row)],
        out_shape=[jax.ShapeDtypeStruct((b, t, POOL_DIM), F32),
                   jax.ShapeDtypeStruct((b, t, GQA_Q_DIM), BF16),
                   jax.ShapeDtypeStruct((b, t, GQA_KV_DIM), BF16),
                   jax.ShapeDtypeStruct((b, t, GQA_KV_DIM), BF16)],
        compiler_params=_params(2),
        name="inproj0_rope" if rope else "inproj0_ctx",
    )(x, mods, mods, gain, w, same_head, q_gain, k_gain, cos, sin)


def _inproj1_kernel(x_ref, sh_ref, sc_ref, g_ref, w_ref, qag_ref, kvag_ref, wuq_ref, wukv_ref,
                    sh64_ref, shmla_ref, mqg_ref, mkg_ref, nqg_ref, nkg_ref, cos_ref, sin_ref,
                    *out_refs, rope, want_q):
    if want_q:
        mq_ref, mk_ref, mv_ref, nq_ref, nk_ref, nv_ref = out_refs
    else:
        mk_ref, mv_ref, nk_ref, nv_ref = out_refs
    o_ckv = MLA_Q_LORA
    o_kr = o_ckv + MLA_KV_LORA
    o_nq = o_kr + LANES
    o_nk = o_nq + NA_DIM
    o_nv = o_nk + NA_DIM
    mla_w = MLA_HEADS * LANES

    def project(rows, st):
        h = _modulate(x_ref[0, rows, :], g_ref[...], sh_ref[0, 0], sc_ref[0, 0]).astype(BF16)
        st["p"] = jnp.dot(h, w_ref[...], preferred_element_type=F32)

    def up_project(rows, st):
        p = st["p"]
        if want_q:
            st["qq"] = jnp.dot(_row_rmsnorm(p[:, :MLA_Q_LORA], qag_ref[...]).astype(BF16),
                               wuq_ref[...], preferred_element_type=F32)
        st["kv"] = jnp.dot(_row_rmsnorm(p[:, o_ckv:o_ckv + MLA_KV_LORA], kvag_ref[...]).astype(BF16),
                           wukv_ref[...], preferred_element_type=F32)

    def epilogue(rows, st):
        p, kv = st["p"], st["kv"]
        tables = (cos_ref[rows, :], sin_ref[rows, :]) if rope else None
        if want_q:
            _norm_rope_store(_slabs(st["qq"], 0, mla_w), shmla_ref[...], mqg_ref[...], MLA_QK,
                             mq_ref, rows, tables)
            _norm_rope_store(_slabs(p, o_nq, NA_DIM), sh64_ref[...], nqg_ref[...], NA_HEAD_DIM,
                             nq_ref, rows)
        kr_blk = p[:, o_kr:o_kr + LANES]
        kr_slab = jnp.concatenate([kr_blk, kr_blk], axis=1)
        _norm_rope_store([slab + kr_slab for slab in _slabs(kv, 0, mla_w)], shmla_ref[...],
                         mkg_ref[...], MLA_QK, mk_ref, rows, tables)
        mv_ref[0, rows, :] = kv[:, mla_w:mla_w + MLA_HEADS * MLA_V].astype(BF16)
        _norm_rope_store(_slabs(p, o_nk, NA_DIM), sh64_ref[...], nkg_ref[...], NA_HEAD_DIM,
                         nk_ref, rows)
        nv_ref[0, rows, :] = p[:, o_nv:o_nv + NA_DIM].astype(BF16)

    _software_pipeline(x_ref.shape[1], [project, up_project, epilogue])


def _inproj1(x, mods, ctx_row, gain, w, qag, kvag, wuq, wukv, sh64, shmla, mqg, mkg, nqg, nkg,
             cos, sin, *, rope, want_q, tm):
    b, t, d = x.shape
    row = lambda bi, i: (bi, i, 0)
    full = _resident
    mla_w = (MLA_HEADS // 2) * MLA_PAIR_W
    outs = [(mla_w, "mk"), (MLA_HEADS * MLA_V, "mv"), (NA_DIM, "nk"), (NA_DIM, "nv")]
    if want_q:
        outs = [(mla_w, "mq")] + outs[:2] + [(NA_DIM, "nq")] + outs[2:]
    return pl.pallas_call(
        functools.partial(_inproj1_kernel, rope=rope, want_q=want_q),
        grid=(b, t // tm),
        in_specs=[pl.BlockSpec((1, tm, d), row),
                  _mod_spec(mods, 0, ctx_row), _mod_spec(mods, 1, ctx_row),
                  full(gain), full(w), full(qag), full(kvag), full(wuq), full(wukv),
                  full(sh64), full(shmla), full(mqg), full(mkg), full(nqg), full(nkg),
                  pl.BlockSpec((tm, LANES), lambda bi, i: (i, 0)),
                  pl.BlockSpec((tm, LANES), lambda bi, i: (i, 0))],
        out_specs=[pl.BlockSpec((1, tm, n), row) for n, _ in outs],
        out_shape=[jax.ShapeDtypeStruct((b, t, n), BF16) for n, _ in outs],
        compiler_params=_params(2),
        name="inproj1_lat" if want_q else "inproj1_ctx",
    )(x, mods, mods, gain, w, qag, kvag, wuq, wukv, sh64, shmla, mqg, mkg, nqg, nkg, cos, sin)


def _fill_pair_kv_t(k, v, mask_a, mask_b, ka_ref, kb_ref, vat_ref, vbt_ref, r0):
    n = k.shape[0]
    if mask_a is None:
        ka_ref[r0:r0 + n, :] = k[:, :LANES]
        kb_ref[r0:r0 + n, :] = k[:, LANES:]
    else:
        ka_ref[r0:r0 + n, :] = k * mask_a
        kb_ref[r0:r0 + n, :] = k * mask_b
    vt = v.astype(F32).T
    first = lax.broadcasted_iota(jnp.int32, vt.shape, 0) < HEAD_DIM
    vat_ref[:, r0:r0 + n] = jnp.where(first, vt, 1.0).astype(BF16)
    vbt_ref[:, r0:r0 + n] = jnp.where(first, 1.0, vt).astype(BF16)


def _pair_softmax_pv(q_tiles, ka_ref, kb_ref, vat_ref, vbt_ref, key_chunks=None):
    n_keys = ka_ref.shape[0]
    chunk = min(n_keys, ATTN_KEY_CHUNK)
    if key_chunks is None:
        key_chunks = [[(k0, None) for k0 in range(0, n_keys, chunk)]] * len(q_tiles)
    n_chunks = len(key_chunks[0])
    chains = [(t, head, q, k_ref, vt_ref) for t, (qa, qb) in enumerate(q_tiles)
              for head, (q, k_ref, vt_ref) in enumerate(((qa, ka_ref, vat_ref), (qb, kb_ref, vbt_ref)))]

    def scores(j):
        out = []
        for t, head, q, k_ref, _ in chains:
            k0, bias = key_chunks[t][j]
            sc = lax.dot_general(k_ref[k0:k0 + chunk, :], q, _NT, preferred_element_type=F32)
            out.append(sc if bias is None else sc + bias(head))
        return out

    s = scores(0)
    run_max = [None] * len(chains)
    acc = [None] * len(chains)
    for j in range(n_chunks):
        s_next = scores(j + 1) if j + 1 < n_chunks else None
        for c, (t, _, _, _, vt_ref) in enumerate(chains):
            k0 = key_chunks[t][j][0]
            mx = s[c].max(axis=0, keepdims=True)
            if j > 0:
                mx = jnp.maximum(mx, run_max[c])
            p = jnp.exp2((s[c] - mx).astype(BF16))
            pv = jnp.dot(vt_ref[:, k0:k0 + chunk], p, preferred_element_type=F32)
            acc[c] = pv if j == 0 else acc[c] * jnp.exp2(run_max[c] - mx) + pv
            run_max[c] = mx
        s = s_next
    outs = []
    for t in range(len(q_tiles)):
        oa, ob = acc[2 * t], acc[2 * t + 1]
        top = oa[:HEAD_DIM] / oa[HEAD_DIM:HEAD_DIM + 1]
        bot = ob[HEAD_DIM:] / ob[0:1]
        outs.append(jnp.concatenate([top, bot], axis=0).T)
    return outs


def _pair_attn_kernel(*refs, use_lat, steps_per_pair, dk, split):
    if use_lat:
        q_ref, ma_ref, mb_ref, kc_ref, vc_ref, kl_ref, vl_ref, o_ref, ka, kb, vat, vbt = refs
        segs = ((kc_ref, vc_ref), (kl_ref, vl_ref))
    else:
        q_ref, ma_ref, mb_ref, kc_ref, vc_ref, o_ref, ka, kb, vat, vbt = refs
        segs = ((kc_ref, vc_ref),)

    @pl.when(jnp.logical_and(pl.program_id(2) == 0, pl.program_id(1) % steps_per_pair == 0))
    def _():
        r0 = 0
        for k_ref, v_ref in segs:
            masks = (None, None) if split else (ma_ref[...], mb_ref[...])
            _fill_pair_kv_t(k_ref[0], v_ref[0], *masks, ka, kb, vat, vbt, r0)
            r0 += k_ref.shape[1]

    sub = min(q_ref.shape[1], ATTN_SUB)
    tiles = [(r0, blk) for blk in range(q_ref.shape[2] // dk) for r0 in range(0, q_ref.shape[1], sub)]

    def q_pair(r0, blk):
        q = q_ref[0, r0:r0 + sub, blk * dk:(blk + 1) * dk]
        return (q[:, :LANES], q[:, LANES:]) if split else (q, q)

    outs = _pair_softmax_pv([q_pair(r0, blk) for r0, blk in tiles], ka, kb, vat, vbt)
    for (r0, blk), o in zip(tiles, outs):
        o_ref[0, r0:r0 + sub, blk * LANES:(blk + 1) * LANES] = o.astype(o_ref.dtype)


def _pair_attn(q, mask_a, mask_b, kc, vc, kl, vl, *, dk, n_pairs, q_per_pair, blocks_per_step, tq,
               name):
    b, t, _ = q.shape
    split = dk == 2 * LANES
    dk_head = dk // 2 if split else dk
    assert q_per_pair % blocks_per_step == 0
    steps_per_pair = q_per_pair // blocks_per_step
    n_qblk = n_pairs * steps_per_pair
    use_lat = kl is not None
    qmap = lambda bi, c, i: (bi, i, c)
    kvmap = lambda bi, c, i: (bi, 0, c // steps_per_pair)
    const = lambda bi, c, i: (0, 0)
    in_specs = [pl.BlockSpec((1, tq, blocks_per_step * dk), qmap),
                pl.BlockSpec((1, dk), const),
                pl.BlockSpec((1, dk), const),
                pl.BlockSpec((1, kc.shape[1], dk), kvmap),
                pl.BlockSpec((1, vc.shape[1], LANES), kvmap)]
    args = [q, mask_a, mask_b, kc, vc]
    if use_lat:
        in_specs += [pl.BlockSpec((1, kl.shape[1], dk), kvmap),
                     pl.BlockSpec((1, vl.shape[1], LANES), kvmap)]
        args += [kl, vl]
    n_keys = kc.shape[1] + (kl.shape[1] if use_lat else 0)
    return pl.pallas_call(
        functools.partial(_pair_attn_kernel, use_lat=use_lat, steps_per_pair=steps_per_pair, dk=dk,
                          split=split),
        grid=(b, n_qblk, t // tq),
        in_specs=in_specs,
        out_specs=pl.BlockSpec((1, tq, blocks_per_step * LANES), qmap),
        out_shape=jax.ShapeDtypeStruct((b, t, n_pairs * q_per_pair * LANES), BF16),
        scratch_shapes=[pltpu.VMEM((n_keys, dk_head), BF16), pltpu.VMEM((n_keys, dk_head), BF16),
                        pltpu.VMEM((LANES, n_keys), BF16), pltpu.VMEM((LANES, n_keys), BF16)],
        compiler_params=_params(3),
        name=name,
    )(*args)


def _na_block_tiles(u, n_blocks):
    t0 = min(max(u - 1, 0), n_blocks - NA_KEY_TILES)
    return t0, (0 if u == 0 else 2 if u == n_blocks - 1 else 1)


def _na_kernel(q_ref, kc_ref, vc_ref, k_ref, v_ref, bias_ref, o_ref, ka, kb, vat, vbt, *, n_rows):
    tc = kc_ref.shape[1]
    lane = _lane((1, LANES))
    mask_a = jnp.where(lane < NA_HEAD_DIM, 1.0, 0.0).astype(BF16)
    mask_b = jnp.where(lane < NA_HEAD_DIM, 0.0, 1.0).astype(BF16)
    _fill_pair_kv_t(kc_ref[0], vc_ref[0], mask_a, mask_b, ka, kb, vat, vbt, 0)
    _fill_pair_kv_t(k_ref[0], v_ref[0], mask_a, mask_b, ka, kb, vat, vbt, tc)

    n_blocks = n_rows // NA_BLOCK_ROWS
    blk_q = NA_BLOCK_ROWS * GRID_W
    for g0 in range(0, n_blocks, NA_BLOCKS_PER_GROUP):
        blocks = range(g0, min(g0 + NA_BLOCKS_PER_GROUP, n_blocks))
        q_tiles, key_chunks = [], []
        for u in blocks:
            q = q_ref[0, u * blk_q:(u + 1) * blk_q, :]
            q_tiles.append((q, q))
            t0, cls = _na_block_tiles(u, n_blocks)
            chunks = [(0, None)]
            for ch in range(NA_KEY_TILES):
                chunks.append((tc + (t0 + ch) * blk_q,
                               functools.partial(lambda head, cls, ch: bias_ref[head, cls, ch],
                                                 cls=cls, ch=ch)))
            key_chunks.append(chunks)
        outs = _pair_softmax_pv(q_tiles, ka, kb, vat, vbt, key_chunks)
        for u, o in zip(blocks, outs):
            o_ref[0, u * blk_q:(u + 1) * blk_q, :] = o.astype(o_ref.dtype)


def _na_attn(q, kc, vc, k, v, bias):
    b, s, _ = q.shape
    n_rows = s // GRID_W
    n_pairs = NA_HEADS // 2
    n_keys = kc.shape[1] + s
    assert kc.shape[1] == ATTN_KEY_CHUNK and NA_BLOCK_ROWS * GRID_W == ATTN_KEY_CHUNK
    blk = lambda rows: pl.BlockSpec((1, rows, LANES), lambda m, bi: (bi, 0, m))
    return pl.pallas_call(
        functools.partial(_na_kernel, n_rows=n_rows),
        grid=(n_pairs, b),
        in_specs=[blk(s), blk(kc.shape[1]), blk(vc.shape[1]), blk(s), blk(s),
                  pl.BlockSpec((2,) + bias.shape[1:], lambda m, bi: (m, 0, 0, 0, 0))],
        out_specs=blk(s),
        out_shape=jax.ShapeDtypeStruct((b, s, NA_DIM), BF16),
        scratch_shapes=[pltpu.VMEM((n_keys, LANES), BF16), pltpu.VMEM((n_keys, LANES), BF16),
                        pltpu.VMEM((LANES, n_keys), BF16), pltpu.VMEM((LANES, n_keys), BF16)],
        compiler_params=_params(2),
        name="na_attn",
    )(q, kc, vc, k, v, bias)


def _pool_kernel(a_ref, pw_ref, ps_ref, o_ref, pad_ref, *, t_len, chunk):
    pad_ref[0:POOL_HALO, :] = jnp.zeros((POOL_HALO, POOL_DIM), F32)
    pad_ref[POOL_HALO + t_len:2 * POOL_HALO + t_len, :] = jnp.zeros((POOL_HALO, POOL_DIM), F32)
    pad_ref[POOL_HALO:POOL_HALO + t_len, :] = a_ref[0]
    lane = _lane((chunk, POOL_DIM))
    g0 = lane < POOL_CH
    g1 = lane < 2 * POOL_CH
    g2 = lane < 3 * POOL_CH
    half_w = jnp.where(g0, 1, jnp.where(g1, 2, jnp.where(g2, 4, 8)))
    for c in range(t_len // chunk):
        base = c * chunk

        def ld(off, base=base):
            return pad_ref[POOL_HALO + base + off:POOL_HALO + base + off + chunk, :]

        a0 = ld(0)
        w2 = ld(-1) + a0
        w4 = w2 + ld(-2) + ld(1)
        w8 = w4 + ld(-4) + ld(-3) + ld(2) + ld(3)
        w16 = w8
        for off in (-8, -7, -6, -5, 4, 5, 6, 7):
            w16 = w16 + ld(off)
        tok = base + lax.broadcasted_iota(jnp.int32, (chunk, POOL_DIM), 0)
        cnt = (jnp.minimum(tok + half_w, t_len) - jnp.maximum(tok - half_w, 0)).astype(F32)
        wsum = jnp.where(g0, w2, jnp.where(g1, w4, jnp.where(g2, w8, w16)))
        dlt = wsum / cnt - a0
        y = jnp.dot(dlt.astype(BF16), pw_ref[...], preferred_element_type=F32) * ps_ref[...]
        o_ref[0, base:base + chunk, :] = y.astype(o_ref.dtype)


def _pool(a, pw_bd, pscale):
    b, t, _ = a.shape
    chunk = min(t, 256)
    return pl.pallas_call(
        functools.partial(_pool_kernel, t_len=t, chunk=chunk),
        grid=(b,),
        in_specs=[pl.BlockSpec((1, t, POOL_DIM), lambda bi: (bi, 0, 0)),
                  pl.BlockSpec((POOL_DIM, POOL_DIM), lambda bi: (0, 0)),
                  pl.BlockSpec((1, POOL_DIM), lambda bi: (0, 0))],
        out_specs=pl.BlockSpec((1, t, POOL_DIM), lambda bi: (bi, 0, 0)),
        out_shape=jax.ShapeDtypeStruct((b, t, POOL_DIM), BF16),
        scratch_shapes=[pltpu.VMEM((t + 2 * POOL_HALO, POOL_DIM), F32)],
        compiler_params=_params(1),
        name="pool",
    )(a, pw_bd, pscale)


def _post_kernel(x_ref, g1_ref, sh_ref, sc_ref, g2_ref, m0_ref, m1_ref, w0_ref, w1_ref, g_ref,
                 wg_ref, wu_ref, wd_ref, o_ref):
    mix = (jnp.dot(m0_ref[0], w0_ref[...], preferred_element_type=F32)
           + jnp.dot(m1_ref[0], w1_ref[...], preferred_element_type=F32))
    x = x_ref[0] + g1_ref[0, 0] * mix
    h = _modulate(x, g_ref[...], sh_ref[0, 0], sc_ref[0, 0]).astype(BF16)
    n_chunks = wg_ref.shape[1] // FFN_CHUNK

    def gate_up(c):
        cols = slice(c * FFN_CHUNK, (c + 1) * FFN_CHUNK)
        return (jnp.dot(h, wg_ref[:, cols], preferred_element_type=F32),
                jnp.dot(h, wu_ref[:, cols], preferred_element_type=F32))

    acc = None
    gt, up = gate_up(0)
    for c in range(n_chunks):
        nxt = gate_up(c + 1) if c + 1 < n_chunks else None
        act = (gt * jax.nn.sigmoid(gt) * up).astype(BF16)
        part = jnp.dot(act, wd_ref[c * FFN_CHUNK:(c + 1) * FFN_CHUNK, :], preferred_element_type=F32)
        acc = part if acc is None else acc + part
        if nxt is not None:
            gt, up = nxt
    o_ref[0] = x + g2_ref[0, 0] * acc


def _mod_spec(mods, k, ctx_row):
    blk = (1, 1, 1, mods.shape[-1])
    if ctx_row is None:
        return pl.BlockSpec(blk, lambda bi, i: (bi, k, 0, 0))
    return pl.BlockSpec(blk, lambda bi, i: (ctx_row, k, 0, 0))


def _resident(a):
    nd = a.ndim
    return pl.BlockSpec(a.shape, lambda bi, i: (0,) * nd, pipeline_mode=pl.Buffered(1))


def _post(x, mods, ctx_row, m0, m1, w0, w1, gain, wg, wu, wd, *, tm, name):
    b, t, d = x.shape
    row = lambda bi, i: (bi, i, 0)
    return pl.pallas_call(
        _post_kernel,
        grid=(b, t // tm),
        in_specs=[pl.BlockSpec((1, tm, d), row),
                  _mod_spec(mods, 2, ctx_row), _mod_spec(mods, 3, ctx_row),
                  _mod_spec(mods, 4, ctx_row), _mod_spec(mods, 5, ctx_row),
                  pl.BlockSpec((1, tm, m0.shape[2]), row),
                  pl.BlockSpec((1, tm, m1.shape[2]), row),
                  _resident(w0), _resident(w1), _resident(gain),
                  _resident(wg), _resident(wu), _resident(wd)],
        out_specs=pl.BlockSpec((1, tm, d), row),
        out_shape=jax.ShapeDtypeStruct((b, t, d), F32),
        compiler_params=_params(2),
        name=name,
    )(x, mods, mods, mods, mods, m0, m1, w0, w1, gain, wg, wu, wd)


class _LaneLayout:
    def __init__(self, dim, slot, axis, freq, x2, n_freq):
        self.dim, self.slot, self.axis, self.freq, self.x2, self.n_freq = dim, slot, axis, freq, x2, n_freq

    def rope_tables(self, seq):
        tok = np.arange(seq)
        inv = (ROPE_THETA ** (-np.arange(self.n_freq, dtype=np.float32) * 2.0
                              / (2 * self.n_freq))).astype(np.float32).astype(np.float64)
        pos = np.stack([tok // GRID_W, tok % GRID_W]).astype(np.float64)
        rot = self.axis >= 0
        ang = pos[np.maximum(self.axis, 0)].T * inv[self.freq][None, :]
        cos = np.where(rot[None], np.cos(ang), 1.0)
        sin = np.where(rot[None], np.sin(ang) * np.where(self.x2, 1.0, -1.0)[None], 0.0)
        return jnp.asarray(cos, F32), jnp.asarray(sin, F32)


def _gqa_layout():
    lane = np.arange(LANES)
    x2, slot, axis, freq = lane // 64, (lane % 64) // 32, (lane % 32) // 16, lane % 16
    return _LaneLayout((axis * 2 + x2) * 16 + freq, slot, axis, freq, x2.astype(bool), HEAD_DIM // 4)


def _mla_layout():
    lane = np.arange(LANES)
    is_rot = (lane % 64) < 16
    x2 = lane >= 64
    axis = np.where(is_rot, (lane % 64) // 8, -1)
    freq = np.where(is_rot, lane % 8, 0)
    nope = np.where(lane < 64, lane - 16, 48 + lane - 80)
    dim = np.where(is_rot, MLA_NOPE + (np.maximum(axis, 0) * 2 + x2) * 8 + freq, nope)
    dim = np.where(lane >= MLA_QK, -1, dim)
    return _LaneLayout(dim, np.zeros(LANES, int), axis, freq, x2 & is_rot, MLA_ROPE // 4)


def _take_cols(w, idx):
    wz = jnp.concatenate([w, jnp.zeros((w.shape[0], 1), w.dtype)], axis=1)
    return wz[:, np.where(idx < 0, w.shape[1], idx)]


def _na_bias_table(rpb, n_rows):
    n_blocks = n_rows // NA_BLOCK_ROWS
    cols = np.arange(GRID_W)
    c0 = np.clip(cols - NA_COLS // 2, 0, GRID_W - NA_COLS)
    kc = np.arange(GRID_W)
    inside = (kc[None, :] >= c0[:, None]) & (kc[None, :] < c0[:, None] + NA_COLS)
    dc = kc[None, :] - cols[:, None] + (NA_COLS - 1)
    onehot = (np.arange(2 * NA_COLS - 1)[:, None, None] == dc[None]) & inside[None]
    sel = jnp.einsum("hjd,dck->hjck", rpb * LOG2E, jnp.asarray(onehot, F32),
                     precision=lax.Precision.HIGHEST)
    t = jnp.where(jnp.asarray(inside)[None, None], sel, NEG_BIG)
    j_idx = np.zeros((3, NA_KEY_TILES, NA_BLOCK_ROWS, NA_BLOCK_ROWS), np.int32)
    ok = np.zeros(j_idx.shape, bool)
    for cls, u in enumerate((0, 1, n_blocks - 1)):
        t0, cls_u = _na_block_tiles(u, n_blocks)
        assert cls_u == cls
        for ch in range(NA_KEY_TILES):
            for i in range(NA_BLOCK_ROWS):
                for dr in range(NA_BLOCK_ROWS):
                    r, kr = NA_BLOCK_ROWS * u + dr, NA_BLOCK_ROWS * (t0 + ch) + i
                    r0 = min(max(r - NA_ROWS // 2, 0), n_rows - NA_ROWS)
                    ok[cls, ch, i, dr] = r0 <= kr < r0 + NA_ROWS
                    j_idx[cls, ch, i, dr] = min(max(kr - r + NA_ROWS - 1, 0), 2 * NA_ROWS - 2)
    full = jnp.where(jnp.asarray(ok)[None, ..., None, None], t[:, j_idx], NEG_BIG)
    full = full.transpose(0, 1, 2, 3, 6, 4, 5)
    n = NA_BLOCK_ROWS * GRID_W
    return full.reshape(rpb.shape[0], 3, NA_KEY_TILES, n, n).astype(F32)


def kernel(x, c, ctx, c_ctx, l0_ada_w, l0_ada_b, l0_norm_mix, l0_norm_ffn, l0_w_in, l0_pool_w, l0_pool_scale, l0_q_gain, l0_k_gain, l0_w_out, l0_ffn_w_gate, l0_ffn_w_up, l0_ffn_w_down, l1_ada_w, l1_ada_b, l1_norm_mix, l1_norm_ffn, l1_w_in, l1_mla_q_a_gain, l1_mla_kv_a_gain, l1_mla_w_uq, l1_mla_w_ukv, l1_mla_q_gain, l1_mla_k_gain, l1_na_q_gain, l1_na_k_gain, l1_na_rpb, l1_w_out, l1_ffn_w_gate, l1_ffn_w_up, l1_ffn_w_down):
    b, s, d = x.shape
    tc = ctx.shape[1]
    tm_lat = 512
    tp_lat = 1024
    tq_lat = 1024

    cond = jnp.concatenate([c, c_ctx[None, :], jnp.zeros((7, d), F32)], axis=0)

    def mods(ada_w, ada_b):
        return _ada(cond, ada_w, ada_b).reshape(cond.shape[0], 6, 1, d)

    def ffn_weights(wg, wu, wd):
        return wg.astype(BF16), wu.astype(BF16), wd.astype(BF16)

    row1 = lambda v: v.reshape(1, -1)
    tile4 = lambda v, mul: (jnp.tile(v, SLAB // v.shape[0]) * mul).reshape(1, -1)
    sh64 = jnp.asarray(np.kron(np.eye(SLAB // HEAD_DIM), np.ones((HEAD_DIM, HEAD_DIM))), BF16)

    def slab_gain(g, lay, mul):
        per_lane = jnp.where(jnp.asarray(lay.dim >= 0), g[np.maximum(lay.dim, 0)], 0.0) * mul
        return jnp.tile(per_lane, SLAB // LANES).reshape(1, -1)

    def slab_same_head(lay):
        head = np.concatenate([np.where(lay.dim >= 0, blk * 2 + lay.slot, -1)
                               for blk in range(SLAB // LANES)])
        return jnp.asarray((head[:, None] == head[None, :]) & (head[:, None] >= 0), BF16)

    def pair_order(w_q, axis):
        shp = w_q.shape
        split = shp[:axis] + (GQA_KV_HEADS // 2, 2, GQA_GROUP, HEAD_DIM) + shp[axis + 1:]
        perm = list(range(len(split)))
        perm[axis + 1], perm[axis + 2] = perm[axis + 2], perm[axis + 1]
        return w_q.reshape(split).transpose(perm).reshape(shp)

    mods0 = mods(l0_ada_w, l0_ada_b)
    glay = _gqa_layout()
    q_cols = np.concatenate([((2 * m + glay.slot) * GQA_GROUP + g) * HEAD_DIM + glay.dim
                             for m in range(GQA_KV_HEADS // 2) for g in range(GQA_GROUP)])
    k_cols = np.concatenate([(2 * m + glay.slot) * HEAD_DIM + glay.dim
                             for m in range(GQA_KV_HEADS // 2)])
    q0, k0, v0 = POOL_DIM, POOL_DIM + GQA_Q_DIM, POOL_DIM + GQA_Q_DIM + GQA_KV_DIM
    w_in0 = jnp.concatenate([l0_w_in[:, :q0], l0_w_in[:, q0:k0][:, q_cols],
                             l0_w_in[:, k0:v0][:, k_cols], l0_w_in[:, v0:]], axis=1).astype(BF16)
    cos0, sin0 = glay.rope_tables(s)
    ones_c = jnp.ones((tc, LANES), F32)
    zeros_c = jnp.zeros((tc, LANES), F32)
    qg4 = slab_gain(l0_q_gain, glay, HEAD_DIM ** 0.5 * GQA_QSCALE)
    kg4 = slab_gain(l0_k_gain, glay, HEAD_DIM ** 0.5)
    sh_gqa = slab_same_head(glay)
    gmask_a = jnp.asarray((glay.slot == 0)[None], BF16)
    gmask_b = 1 - gmask_a

    a_l, q_l, k_l, v_l = _inproj0(x, mods0, None, row1(l0_norm_mix), w_in0, sh_gqa, qg4, kg4,
                                  cos0, sin0, rope=True, tm=tp_lat)
    a_c, q_c, k_c, v_c = _inproj0(ctx, mods0, b, row1(l0_norm_mix), w_in0, sh_gqa, qg4, kg4,
                                  ones_c, zeros_c, rope=False, tm=tc)
    n_kv_pairs = GQA_KV_HEADS // 2
    attn_l = _pair_attn(q_l, gmask_a, gmask_b, k_c, v_c, k_l, v_l, dk=LANES, n_pairs=n_kv_pairs,
                        q_per_pair=GQA_GROUP, blocks_per_step=1, tq=tq_lat, name="gqa_lat")
    attn_c = _pair_attn(q_c, gmask_a, gmask_b, k_c, v_c, None, None, dk=LANES, n_pairs=n_kv_pairs,
                        q_per_pair=GQA_GROUP, blocks_per_step=GQA_GROUP, tq=tc, name="gqa_ctx")

    eye = jnp.eye(POOL_GROUPS, dtype=F32)
    pw_bd = (eye[:, None, :, None] * l0_pool_w[:, :, None, :]).reshape(POOL_DIM, POOL_DIM).astype(BF16)
    pool_l = _pool(a_l, pw_bd, row1(l0_pool_scale))
    pool_c = _pool(a_c, pw_bd, row1(l0_pool_scale))

    w_out0_pool = l0_w_out[:POOL_DIM].astype(BF16)
    w_out0_attn = pair_order(l0_w_out[POOL_DIM:], 0).astype(BF16)
    ffn0 = ffn_weights(l0_ffn_w_gate, l0_ffn_w_up, l0_ffn_w_down)

    x1 = _post(x, mods0, None, pool_l, attn_l, w_out0_pool, w_out0_attn, row1(l0_norm_ffn), *ffn0,
               tm=tm_lat, name="post0_lat")
    xc = _post(ctx, mods0, b, pool_c, attn_c, w_out0_pool, w_out0_attn, row1(l0_norm_ffn), *ffn0,
               tm=tc, name="post0_ctx")

    mods1 = mods(l1_ada_w, l1_ada_b)
    mlay = _mla_layout()
    heads = np.arange(MLA_HEADS)[:, None]
    o_kr = MLA_Q_LORA + MLA_KV_LORA
    kr_block = _take_cols(l1_w_in, np.where(mlay.dim >= MLA_NOPE, o_kr + mlay.dim - MLA_NOPE, -1))
    w_in1 = jnp.concatenate([l1_w_in[:, :o_kr], kr_block, l1_w_in[:, o_kr + MLA_ROPE:]],
                            axis=1).astype(BF16)
    wuq = _take_cols(l1_mla_w_uq, np.where(mlay.dim >= 0, heads * MLA_QK + mlay.dim, -1).reshape(-1)
                     ).astype(BF16)
    kv_w = MLA_NOPE + MLA_V
    k_idx = np.where((mlay.dim >= 0) & (mlay.dim < MLA_NOPE), heads * kv_w + mlay.dim, -1).reshape(-1)
    v_idx = (heads * kv_w + MLA_NOPE + np.arange(MLA_V)[None, :]).reshape(-1)
    wukv = _take_cols(l1_mla_w_ukv, np.concatenate([k_idx, v_idx])).astype(BF16)

    shmla = slab_same_head(mlay)
    mqg = slab_gain(l1_mla_q_gain, mlay, MLA_QK ** 0.5 * MLA_QSCALE)
    mkg = slab_gain(l1_mla_k_gain, mlay, MLA_QK ** 0.5)
    nqg = tile4(l1_na_q_gain, NA_HEAD_DIM ** 0.5 * NA_QSCALE)
    nkg = tile4(l1_na_k_gain, NA_HEAD_DIM ** 0.5)
    cos1, sin1 = mlay.rope_tables(s)
    common = (row1(l1_norm_mix), w_in1, row1(l1_mla_q_a_gain), row1(l1_mla_kv_a_gain), wuq, wukv,
              sh64, shmla, mqg, mkg, nqg, nkg)
    mq, mk, mv, nq, nk, nv = _inproj1(x1, mods1, None, *common, cos1, sin1, rope=True, want_q=True,
                                      tm=tp_lat)
    mkc, mvc, nkc, nvc = _inproj1(xc, mods1, b, *common, ones_c, zeros_c, rope=False, want_q=False,
                                  tm=tc)

    no_mask = jnp.ones((1, MLA_PAIR_W), BF16)
    o_mla = _pair_attn(mq, no_mask, no_mask, mkc, mvc, mk, mv, dk=MLA_PAIR_W,
                       n_pairs=MLA_HEADS // 2, q_per_pair=1, blocks_per_step=1, tq=tq_lat, name="mla")
    o_na = _na_attn(nq, nkc, nvc, nk, nv, _na_bias_table(l1_na_rpb, s // GRID_W))

    w_out1 = l1_w_out.astype(BF16)
    ffn1 = ffn_weights(l1_ffn_w_gate, l1_ffn_w_up, l1_ffn_w_down)
    return _post(x1, mods1, None, o_mla, o_na, w_out1[:MLA_HEADS * MLA_V], w_out1[MLA_HEADS * MLA_V:],
                 row1(l1_norm_ffn), *ffn1, tm=tm_lat, name="post1_lat")
```

```python
import functools

import numpy as np
import jax
import jax.numpy as jnp
from jax import lax
from jax.experimental import pallas as pl
from jax.experimental.pallas import tpu as pltpu

F32 = jnp.float32
BF16 = jnp.bfloat16

D_MODEL = 1024
GRID_W = 64
ROPE_THETA = 10000.0
EPS = 1e-6
LANES = 128
SLAB = 256

POOL_GROUPS = 4
POOL_CH = 64
POOL_DIM = POOL_GROUPS * POOL_CH
POOL_HALO = 16

HEAD_DIM = 64
GQA_HEADS = 12
GQA_KV_HEADS = 4
GQA_GROUP = GQA_HEADS // GQA_KV_HEADS
GQA_Q_DIM = GQA_HEADS * HEAD_DIM
GQA_KV_DIM = GQA_KV_HEADS * HEAD_DIM

MLA_HEADS = 8
MLA_NOPE = 64
MLA_ROPE = 32
MLA_QK = MLA_NOPE + MLA_ROPE
MLA_V = 64
MLA_Q_LORA = 384
MLA_KV_LORA = 256
MLA_PAIR_W = 2 * LANES

NA_HEADS = 8
NA_HEAD_DIM = 64
NA_DIM = NA_HEADS * NA_HEAD_DIM
NA_ROWS = 8
NA_COLS = 16
NA_BLOCK_ROWS = 4
NA_KEY_TILES = 3
NA_BLOCKS_PER_GROUP = 4

FFN_DIM = -(-8 * D_MODEL // (3 * 256)) * 256
FFN_CHUNK = 256

NEG_BIG = -1e30
LOG2E = 1.4426950408889634
GQA_QSCALE = HEAD_DIM ** -0.5 * LOG2E
MLA_QSCALE = MLA_QK ** -0.5 * LOG2E
NA_QSCALE = NA_HEAD_DIM ** -0.5 * LOG2E
PROJ_SUB = 256
ATTN_SUB = 256
ATTN_KEY_CHUNK = 256

VMEM_LIMIT = 56 * 1024 * 1024

_NT = (((1,), (1,)), ((), ()))


def _params(n_axes):
    return pltpu.CompilerParams(dimension_semantics=("arbitrary",) * n_axes,
                                vmem_limit_bytes=VMEM_LIMIT)


def _modulate(x, gain, shift, scale):
    ms = jnp.mean(x * x, axis=-1, keepdims=True)
    return x * lax.rsqrt(ms + EPS) * gain * (1.0 + scale) + shift


def _row_rmsnorm(x, gain):
    ms = jnp.mean(x * x, axis=-1, keepdims=True)
    return x * lax.rsqrt(ms + EPS) * gain


def _lane(shape):
    return lax.broadcasted_iota(jnp.int32, shape, 1)


def _head_rmsnorm(slab, same_head, gain, n):
    ssq = jnp.dot((slab * slab).astype(BF16), same_head, preferred_element_type=F32)
    return slab * lax.rsqrt(ssq + n * EPS) * gain


def _rope(blk, cos, sin_signed):
    return blk * cos + pltpu.roll(blk, LANES // 2, 1) * sin_signed


def _ada_kernel(c_ref, w_ref, b_ref, o_ref):
    cnd = c_ref[...]
    act = cnd * jax.nn.sigmoid(cnd)
    o_ref[...] = jnp.dot(act.astype(BF16), w_ref[...].astype(BF16),
                         preferred_element_type=F32) + b_ref[...]


def _ada(cond, w, b):
    n_rows, d = cond.shape
    n = w.shape[1]
    tn = 1024
    return pl.pallas_call(
        _ada_kernel,
        grid=(n // tn,),
        in_specs=[pl.BlockSpec((n_rows, d), lambda j: (0, 0)),
                  pl.BlockSpec((d, tn), lambda j: (0, j)),
                  pl.BlockSpec((1, tn), lambda j: (0, j))],
        out_specs=pl.BlockSpec((n_rows, tn), lambda j: (0, j)),
        out_shape=jax.ShapeDtypeStruct((n_rows, n), F32),
        compiler_params=_params(1),
        name="ada",
    )(cond, w, b.reshape(1, n))


def _slabs(src, c0, width):
    return [src[:, c0 + j * SLAB:c0 + (j + 1) * SLAB] for j in range(width // SLAB)]


def _norm_rope_store(slabs, same_head, gain, n, dst_ref, rows, rope_tables=None):
    for j, slab in enumerate(slabs):
        slab = _head_rmsnorm(slab, same_head, gain, n)
        for blk in range(SLAB // LANES):
            part = slab[:, blk * LANES:(blk + 1) * LANES]
            if rope_tables is not None:
                part = _rope(part, *rope_tables)
            o = j * SLAB + blk * LANES
            dst_ref[0, rows, o:o + LANES] = part.astype(BF16)


def _software_pipeline(n_rows, stages):
    sub = min(n_rows, PROJ_SUB)
    tiles = [slice(r, r + sub) for r in range(0, n_rows, sub)]
    states = [{} for _ in tiles]
    for it in range(len(tiles) + len(stages) - 1):
        for s, stage in enumerate(stages):
            t = it - s
            if stage is not None and 0 <= t < len(tiles):
                stage(tiles[t], states[t])


def _inproj0_kernel(x_ref, sh_ref, sc_ref, g_ref, w_ref, same_ref, qg_ref, kg_ref, cos_ref, sin_ref,
                    a_ref, q_ref, k_ref, v_ref, *, rope):
    q0 = POOL_DIM
    k0 = q0 + GQA_Q_DIM
    v0 = k0 + GQA_KV_DIM

    def project(rows, st):
        h = _modulate(x_ref[0, rows, :], g_ref[...], sh_ref[0, 0], sc_ref[0, 0]).astype(BF16)
        st["p"] = jnp.dot(h, w_ref[...], preferred_element_type=F32)

    def epilogue(rows, st):
        p = st["p"]
        a_ref[0, rows, :] = p[:, :POOL_DIM]
        tables = (cos_ref[rows, :], sin_ref[rows, :]) if rope else None
        for c0, width, gain_ref, dst in ((q0, GQA_Q_DIM, qg_ref, q_ref), (k0, GQA_KV_DIM, kg_ref, k_ref)):
            _norm_rope_store(_slabs(p, c0, width), same_ref[...], gain_ref[...], HEAD_DIM, dst, rows,
                             tables)
        v_ref[0, rows, :] = p[:, v0:v0 + GQA_KV_DIM].astype(BF16)

    _software_pipeline(x_ref.shape[1], [project, None, epilogue])


def _inproj0(x, mods, ctx_row, gain, w, same_head, q_gain, k_gain, cos, sin, *, rope, tm):
    b, t, d = x.shape
    row = lambda bi, i: (bi, i, 0)
    return pl.pallas_call(
        functools.partial(_inproj0_kernel, rope=rope),
        grid=(b, t // tm),
        in_specs=[pl.BlockSpec((1, tm, d), row),
                  _mod_spec(mods, 0, ctx_row), _mod_spec(mods, 1, ctx_row),
                  _resident(gain), _resident(w), _resident(same_head),
                  _resident(q_gain), _resident(k_gain),
                  pl.BlockSpec((tm, LANES), lambda bi, i: (i, 0)),
                  pl.BlockSpec((tm, LANES), lambda bi, i: (i, 0))],
        out_specs=[pl.BlockSpec((1, tm, POOL_DIM), row),
                   pl.BlockSpec((1, tm, GQA_Q_DIM), row),
                   pl.BlockSpec((1, tm, GQA_KV_DIM), row),
                   pl.BlockSpec((1, tm, GQA_KV_DIM), row)],
        out_shape=[jax.ShapeDtypeStruct((b, t, POOL_DIM), F32),
                   jax.ShapeDtypeStruct((b, t, GQA_Q_DIM), BF16),
                   jax.ShapeDtypeStruct((b, t, GQA_KV_DIM), BF16),
                   jax.ShapeDtypeStruct((b, t, GQA_KV_DIM), BF16)],
        compiler_params=_params(2),
        name="inproj0_rope" if rope else "inproj0_ctx",
    )(x, mods, mods, gain, w, same_head, q_gain, k_gain, cos, sin)


def _inproj1_kernel(x_ref, sh_ref, sc_ref, g_ref, w_ref, qag_ref, kvag_ref, wuq_ref, wukv_ref,
                    sh64_ref, shmla_ref, mqg_ref, mkg_ref, nqg_ref, nkg_ref, cos_ref, sin_ref,
                    *out_refs, rope, want_q):
    if want_q:
        mq_ref, mk_ref, mv_ref, nq_ref, nk_ref, nv_ref = out_refs
    else:
        mk_ref, mv_ref, nk_ref, nv_ref = out_refs
    o_ckv = MLA_Q_LORA
    o_kr = o_ckv + MLA_KV_LORA
    o_nq = o_kr + LANES
    o_nk = o_nq + NA_DIM
    o_nv = o_nk + NA_DIM
    mla_w = MLA_HEADS * LANES

    def project(rows, st):
        h = _modulate(x_ref[0, rows, :], g_ref[...], sh_ref[0, 0], sc_ref[0, 0]).astype(BF16)
        st["p"] = jnp.dot(h, w_ref[...], preferred_element_type=F32)

    def up_project(rows, st):
        p = st["p"]
        if want_q:
            st["qq"] = jnp.dot(_row_rmsnorm(p[:, :MLA_Q_LORA], qag_ref[...]).astype(BF16),
                               wuq_ref[...], preferred_element_type=F32)
        st["kv"] = jnp.dot(_row_rmsnorm(p[:, o_ckv:o_ckv + MLA_KV_LORA], kvag_ref[...]).astype(BF16),
                           wukv_ref[...], preferred_element_type=F32)

    def epilogue(rows, st):
        p, kv = st["p"], st["kv"]
        tables = (cos_ref[rows, :], sin_ref[rows, :]) if rope else None
        if want_q:
            _norm_rope_store(_slabs(st["qq"], 0, mla_w), shmla_ref[...], mqg_ref[...], MLA_QK,
                             mq_ref, rows, tables)
            _norm_rope_store(_slabs(p, o_nq, NA_DIM), sh64_ref[...], nqg_ref[...], NA_HEAD_DIM,
                             nq_ref, rows)
        kr_blk = p[:, o_kr:o_kr + LANES]
        kr_slab = jnp.concatenate([kr_blk, kr_blk], axis=1)
        _norm_rope_store([slab + kr_slab for slab in _slabs(kv, 0, mla_w)], shmla_ref[...],
                         mkg_ref[...], MLA_QK, mk_ref, rows, tables)
        mv_ref[0, rows, :] = kv[:, mla_w:mla_w + MLA_HEADS * MLA_V].astype(BF16)
        _norm_rope_store(_slabs(p, o_nk, NA_DIM), sh64_ref[...], nkg_ref[...], NA_HEAD_DIM,
                         nk_ref, rows)
        nv_ref[0, rows, :] = p[:, o_nv:o_nv + NA_DIM].astype(BF16)

    _software_pipeline(x_ref.shape[1], [project, up_project, epilogue])


def _inproj1(x, mods, ctx_row, gain, w, qag, kvag, wuq, wukv, sh64, shmla, mqg, mkg, nqg, nkg,
             cos, sin, *, rope, want_q, tm):
    b, t, d = x.shape
    row = lambda bi, i: (bi, i, 0)
    full = _resident
    mla_w = (MLA_HEADS // 2) * MLA_PAIR_W
    outs = [(mla_w, "mk"), (MLA_HEADS * MLA_V, "mv"), (NA_DIM, "nk"), (NA_DIM, "nv")]
    if want_q:
        outs = [(mla_w, "mq")] + outs[:2] + [(NA_DIM, "nq")] + outs[2:]
    return pl.pallas_call(
        functools.partial(_inproj1_kernel, rope=rope, want_q=want_q),
        grid=(b, t // tm),
        in_specs=[pl.BlockSpec((1, tm, d), row),
                  _mod_spec(mods, 0, ctx_row), _mod_spec(mods, 1, ctx_row),
                  full(gain), full(w), full(qag), full(kvag), full(wuq), full(wukv),
                  full(sh64), full(shmla), full(mqg), full(mkg), full(nqg), full(nkg),
                  pl.BlockSpec((tm, LANES), lambda bi, i: (i, 0)),
                  pl.BlockSpec((tm, LANES), lambda bi, i: (i, 0))],
        out_specs=[pl.BlockSpec((1, tm, n), row) for n, _ in outs],
        out_shape=[jax.ShapeDtypeStruct((b, t, n), BF16) for n, _ in outs],
        compiler_params=_params(2),
        name="inproj1_lat" if want_q else "inproj1_ctx",
    )(x, mods, mods, gain, w, qag, kvag, wuq, wukv, sh64, shmla, mqg, mkg, nqg, nkg, cos, sin)


def _fill_pair_kv_t(k, v, mask_a, mask_b, ka_ref, kb_ref, vat_ref, vbt_ref, r0):
    n = k.shape[0]
    if mask_a is None:
        ka_ref[r0:r0 + n, :] = k[:, :LANES]
        kb_ref[r0:r0 + n, :] = k[:, LANES:]
    else:
        ka_ref[r0:r0 + n, :] = k * mask_a
        kb_ref[r0:r0 + n, :] = k * mask_b
    vt = v.astype(F32).T
    first = lax.broadcasted_iota(jnp.int32, vt.shape, 0) < HEAD_DIM
    vat_ref[:, r0:r0 + n] = jnp.where(first, vt, 1.0).astype(BF16)
    vbt_ref[:, r0:r0 + n] = jnp.where(first, 1.0, vt).astype(BF16)


def _pair_softmax_pv(q_tiles, ka_ref, kb_ref, vat_ref, vbt_ref, key_chunks=None):
    n_keys = ka_ref.shape[0]
    chunk = min(n_keys, ATTN_KEY_CHUNK)
    if key_chunks is None:
        key_chunks = [[(k0, None) for k0 in range(0, n_keys, chunk)]] * len(q_tiles)
    n_chunks = len(key_chunks[0])
    chains = [(t, head, q, k_ref, vt_ref) for t, (qa, qb) in enumerate(q_tiles)
              for head, (q, k_ref, vt_ref) in enumerate(((qa, ka_ref, vat_ref), (qb, kb_ref, vbt_ref)))]

    def scores(j):
        out = []
        for t, head, q, k_ref, _ in chains:
            k0, bias = key_chunks[t][j]
            sc = lax.dot_general(k_ref[k0:k0 + chunk, :], q, _NT, preferred_element_type=F32)
            out.append(sc if bias is None else sc + bias(head))
        return out

    s = scores(0)
    run_max = [None] * len(chains)
    acc = [None] * len(chains)
    for j in range(n_chunks):
        s_next = scores(j + 1) if j + 1 < n_chunks else None
        for c, (t, _, _, _, vt_ref) in enumerate(chains):
            k0 = key_chunks[t][j][0]
            mx = s[c].max(axis=0, keepdims=True)
            if j > 0:
                mx = jnp.maximum(mx, run_max[c])
            p = jnp.exp2((s[c] - mx).astype(BF16))
            pv = jnp.dot(vt_ref[:, k0:k0 + chunk], p, preferred_element_type=F32)
            acc[c] = pv if j == 0 else acc[c] * jnp.exp2(run_max[c] - mx) + pv
            run_max[c] = mx
        s = s_next
    outs = []
    for t in range(len(q_tiles)):
        oa, ob = acc[2 * t], acc[2 * t + 1]
        top = oa[:HEAD_DIM] / oa[HEAD_DIM:HEAD_DIM + 1]
        bot = ob[HEAD_DIM:] / ob[0:1]
        outs.append(jnp.concatenate([top, bot], axis=0).T)
    return outs


def _pair_attn_kernel(*refs, use_lat, steps_per_pair, dk, split):
    if use_lat:
        q_ref, ma_ref, mb_ref, kc_ref, vc_ref, kl_ref, vl_ref, o_ref, ka, kb, vat, vbt = refs
        segs = ((kc_ref, vc_ref), (kl_ref, vl_ref))
    else:
        q_ref, ma_ref, mb_ref, kc_ref, vc_ref, o_ref, ka, kb, vat, vbt = refs
        segs = ((kc_ref, vc_ref),)

    @pl.when(jnp.logical_and(pl.program_id(2) == 0, pl.program_id(1) % steps_per_pair == 0))
    def _():
        r0 = 0
        for k_ref, v_ref in segs:
            masks = (None, None) if split else (ma_ref[...], mb_ref[...])
            _fill_pair_kv_t(k_ref[0], v_ref[0], *masks, ka, kb, vat, vbt, r0)
            r0 += k_ref.shape[1]

    sub = min(q_ref.shape[1], ATTN_SUB)
    tiles = [(r0, blk) for blk in range(q_ref.shape[2] // dk) for r0 in range(0, q_ref.shape[1], sub)]

    def q_pair(r0, blk):
        q = q_ref[0, r0:r0 + sub, blk * dk:(blk + 1) * dk]
        return (q[:, :LANES], q[:, LANES:]) if split else (q, q)

    outs = _pair_softmax_pv([q_pair(r0, blk) for r0, blk in tiles], ka, kb, vat, vbt)
    for (r0, blk), o in zip(tiles, outs):
        o_ref[0, r0:r0 + sub, blk * LANES:(blk + 1) * LANES] = o.astype(o_ref.dtype)


def _pair_attn(q, mask_a, mask_b, kc, vc, kl, vl, *, dk, n_pairs, q_per_pair, blocks_per_step, tq,
               name):
    b, t, _ = q.shape
    split = dk == 2 * LANES
    dk_head = dk // 2 if split else dk
    assert q_per_pair % blocks_per_step == 0
    steps_per_pair = q_per_pair // blocks_per_step
    n_qblk = n_pairs * steps_per_pair
    use_lat = kl is not None
    qmap = lambda bi, c, i: (bi, i, c)
    kvmap = lambda bi, c, i: (bi, 0, c // steps_per_pair)
    const = lambda bi, c, i: (0, 0)
    in_specs = [pl.BlockSpec((1, tq, blocks_per_step * dk), qmap),
                pl.BlockSpec((1, dk), const),
                pl.BlockSpec((1, dk), const),
                pl.BlockSpec((1, kc.shape[1], dk), kvmap),
                pl.BlockSpec((1, vc.shape[1], LANES), kvmap)]
    args = [q, mask_a, mask_b, kc, vc]
    if use_lat:
        in_specs += [pl.BlockSpec((1, kl.shape[1], dk), kvmap),
                     pl.BlockSpec((1, vl.shape[1], LANES), kvmap)]
        args += [kl, vl]
    n_keys = kc.shape[1] + (kl.shape[1] if use_lat else 0)
    return pl.pallas_call(
        functools.partial(_pair_attn_kernel, use_lat=use_lat, steps_per_pair=steps_per_pair, dk=dk,
                          split=split),
        grid=(b, n_qblk, t // tq),
        in_specs=in_specs,
        out_specs=pl.BlockSpec((1, tq, blocks_per_step * LANES), qmap),
        out_shape=jax.ShapeDtypeStruct((b, t, n_pairs * q_per_pair * LANES), BF16),
        scratch_shapes=[pltpu.VMEM((n_keys, dk_head), BF16), pltpu.VMEM((n_keys, dk_head), BF16),
                        pltpu.VMEM((LANES, n_keys), BF16), pltpu.VMEM((LANES, n_keys), BF16)],
        compiler_params=_params(3),
        name=name,
    )(*args)


def _na_block_tiles(u, n_blocks):
    t0 = min(max(u - 1, 0), n_blocks - NA_KEY_TILES)
    return t0, (0 if u == 0 else 2 if u == n_blocks - 1 else 1)


def _na_kernel(q_ref, kc_ref, vc_ref, k_ref, v_ref, band_ref, o_ref, ka, kb, vat, vbt, bias_ref,
               *, n_rows):
    j_idx, ok = _na_bias_plan(n_rows)

    @pl.when(pl.program_id(1) == 0)
    def _():
        masked = jnp.full((GRID_W, GRID_W), NEG_BIG, F32)
        for head, cls, ch, i, dr in np.ndindex(2, *j_idx.shape):
            half = (dr % 2) * GRID_W
            piece = (band_ref[head, int(j_idx[cls, ch, i, dr]), :, half:half + GRID_W]
                     if ok[cls, ch, i, dr] else masked)
            bias_ref[head, cls, ch, i * GRID_W:(i + 1) * GRID_W, dr * GRID_W:(dr + 1) * GRID_W] = piece

    tc = kc_ref.shape[1]
    lane = _lane((1, LANES))
    mask_a = jnp.where(lane < NA_HEAD_DIM, 1.0, 0.0).astype(BF16)
    mask_b = jnp.where(lane < NA_HEAD_DIM, 0.0, 1.0).astype(BF16)
    _fill_pair_kv_t(kc_ref[0], vc_ref[0], mask_a, mask_b, ka, kb, vat, vbt, 0)
    _fill_pair_kv_t(k_ref[0], v_ref[0], mask_a, mask_b, ka, kb, vat, vbt, tc)

    n_blocks = n_rows // NA_BLOCK_ROWS
    blk_q = NA_BLOCK_ROWS * GRID_W
    for g0 in range(0, n_blocks, NA_BLOCKS_PER_GROUP):
        blocks = range(g0, min(g0 + NA_BLOCKS_PER_GROUP, n_blocks))
        q_tiles, key_chunks = [], []
        for u in blocks:
            q = q_ref[0, u * blk_q:(u + 1) * blk_q, :]
            q_tiles.append((q, q))
            t0, cls = _na_block_tiles(u, n_blocks)
            chunks = [(0, None)]
            for ch in range(NA_KEY_TILES):
                chunks.append((tc + (t0 + ch) * blk_q,
                               functools.partial(lambda head, cls, ch: bias_ref[head, cls, ch],
                                                 cls=cls, ch=ch)))
            key_chunks.append(chunks)
        outs = _pair_softmax_pv(q_tiles, ka, kb, vat, vbt, key_chunks)
        for u, o in zip(blocks, outs):
            o_ref[0, u * blk_q:(u + 1) * blk_q, :] = o.astype(o_ref.dtype)


def _na_attn(q, kc, vc, k, v, band):
    b, s, _ = q.shape
    n_rows = s // GRID_W
    n_pairs = NA_HEADS // 2
    n_keys = kc.shape[1] + s
    blk_q = NA_BLOCK_ROWS * GRID_W
    assert kc.shape[1] == ATTN_KEY_CHUNK and blk_q == ATTN_KEY_CHUNK
    blk = lambda rows: pl.BlockSpec((1, rows, LANES), lambda m, bi: (bi, 0, m))
    return pl.pallas_call(
        functools.partial(_na_kernel, n_rows=n_rows),
        grid=(n_pairs, b),
        in_specs=[blk(s), blk(kc.shape[1]), blk(vc.shape[1]), blk(s), blk(s),
                  pl.BlockSpec((2,) + band.shape[1:], lambda m, bi: (m, 0, 0, 0))],
        out_specs=blk(s),
        out_shape=jax.ShapeDtypeStruct((b, s, NA_DIM), BF16),
        scratch_shapes=[pltpu.VMEM((n_keys, LANES), BF16), pltpu.VMEM((n_keys, LANES), BF16),
                        pltpu.VMEM((LANES, n_keys), BF16), pltpu.VMEM((LANES, n_keys), BF16),
                        pltpu.VMEM((2, 3, NA_KEY_TILES, blk_q, blk_q), F32)],
        compiler_params=_params(2),
        name="na_attn",
    )(q, kc, vc, k, v, band)


def _pool_kernel(a_ref, pw_ref, ps_ref, o_ref, pad_ref, *, t_len, chunk):
    pad_ref[0:POOL_HALO, :] = jnp.zeros((POOL_HALO, POOL_DIM), F32)
    pad_ref[POOL_HALO + t_len:2 * POOL_HALO + t_len, :] = jnp.zeros((POOL_HALO, POOL_DIM), F32)
    pad_ref[POOL_HALO:POOL_HALO + t_len, :] = a_ref[0]
    lane = _lane((chunk, POOL_DIM))
    g0 = lane < POOL_CH
    g1 = lane < 2 * POOL_CH
    g2 = lane < 3 * POOL_CH
    half_w = jnp.where(g0, 1, jnp.where(g1, 2, jnp.where(g2, 4, 8)))
    for c in range(t_len // chunk):
        base = c * chunk

        def ld(off, base=base):
            return pad_ref[POOL_HALO + base + off:POOL_HALO + base + off + chunk, :]

        a0 = ld(0)
        w2 = ld(-1) + a0
        w4 = w2 + ld(-2) + ld(1)
        w8 = w4 + ld(-4) + ld(-3) + ld(2) + ld(3)
        w16 = w8
        for off in (-8, -7, -6, -5, 4, 5, 6, 7):
            w16 = w16 + ld(off)
        tok = base + lax.broadcasted_iota(jnp.int32, (chunk, POOL_DIM), 0)
        cnt = (jnp.minimum(tok + half_w, t_len) - jnp.maximum(tok - half_w, 0)).astype(F32)
        wsum = jnp.where(g0, w2, jnp.where(g1, w4, jnp.where(g2, w8, w16)))
        dlt = wsum / cnt - a0
        y = jnp.dot(dlt.astype(BF16), pw_ref[...], preferred_element_type=F32) * ps_ref[...]
        o_ref[0, base:base + chunk, :] = y.astype(o_ref.dtype)


def _pool(a, pw_bd, pscale):
    b, t, _ = a.shape
    chunk = min(t, 256)
    return pl.pallas_call(
        functools.partial(_pool_kernel, t_len=t, chunk=chunk),
        grid=(b,),
        in_specs=[pl.BlockSpec((1, t, POOL_DIM), lambda bi: (bi, 0, 0)),
                  pl.BlockSpec((POOL_DIM, POOL_DIM), lambda bi: (0, 0)),
                  pl.BlockSpec((1, POOL_DIM), lambda bi: (0, 0))],
        out_specs=pl.BlockSpec((1, t, POOL_DIM), lambda bi: (bi, 0, 0)),
        out_shape=jax.ShapeDtypeStruct((b, t, POOL_DIM), BF16),
        scratch_shapes=[pltpu.VMEM((t + 2 * POOL_HALO, POOL_DIM), F32)],
        compiler_params=_params(1),
        name="pool",
    )(a, pw_bd, pscale)


def _post_kernel(x_ref, g1_ref, sh_ref, sc_ref, g2_ref, m0_ref, m1_ref, w0_ref, w1_ref, g_ref,
                 wg_ref, wu_ref, wd_ref, o_ref):
    mix = (jnp.dot(m0_ref[0], w0_ref[...], preferred_element_type=F32)
           + jnp.dot(m1_ref[0], w1_ref[...], preferred_element_type=F32))
    x = x_ref[0] + g1_ref[0, 0] * mix
    h = _modulate(x, g_ref[...], sh_ref[0, 0], sc_ref[0, 0]).astype(BF16)
    n_chunks = wg_ref.shape[1] // FFN_CHUNK

    def gate_up(c):
        cols = slice(c * FFN_CHUNK, (c + 1) * FFN_CHUNK)
        return (jnp.dot(h, wg_ref[:, cols], preferred_element_type=F32),
                jnp.dot(h, wu_ref[:, cols], preferred_element_type=F32))

    acc = None
    gt, up = gate_up(0)
    for c in range(n_chunks):
        nxt = gate_up(c + 1) if c + 1 < n_chunks else None
        act = (gt * jax.nn.sigmoid(gt) * up).astype(BF16)
        part = jnp.dot(act, wd_ref[c * FFN_CHUNK:(c + 1) * FFN_CHUNK, :], preferred_element_type=F32)
        acc = part if acc is None else acc + part
        if nxt is not None:
            gt, up = nxt
    o_ref[0] = x + g2_ref[0, 0] * acc


def _mod_spec(mods, k, ctx_row):
    blk = (1, 1, 1, mods.shape[-1])
    if ctx_row is None:
        return pl.BlockSpec(blk, lambda bi, i: (bi, k, 0, 0))
    return pl.BlockSpec(blk, lambda bi, i: (ctx_row, k, 0, 0))


def _resident(a):
    nd = a.ndim
    return pl.BlockSpec(a.shape, lambda bi, i: (0,) * nd, pipeline_mode=pl.Buffered(1))


def _post(x, mods, ctx_row, m0, m1, w0, w1, gain, wg, wu, wd, *, tm, name):
    b, t, d = x.shape
    row = lambda bi, i: (bi, i, 0)
    return pl.pallas_call(
        _post_kernel,
        grid=(b, t // tm),
        in_specs=[pl.BlockSpec((1, tm, d), row),
                  _mod_spec(mods, 2, ctx_row), _mod_spec(mods, 3, ctx_row),
                  _mod_spec(mods, 4, ctx_row), _mod_spec(mods, 5, ctx_row),
                  pl.BlockSpec((1, tm, m0.shape[2]), row),
                  pl.BlockSpec((1, tm, m1.shape[2]), row),
                  _resident(w0), _resident(w1), _resident(gain),
                  _resident(wg), _resident(wu), _resident(wd)],
        out_specs=pl.BlockSpec((1, tm, d), row),
        out_shape=jax.ShapeDtypeStruct((b, t, d), F32),
        compiler_params=_params(2),
        name=name,
    )(x, mods, mods, mods, mods, m0, m1, w0, w1, gain, wg, wu, wd)


class _LaneLayout:
    def __init__(self, dim, slot, axis, freq, x2, n_freq):
        self.dim, self.slot, self.axis, self.freq, self.x2, self.n_freq = dim, slot, axis, freq, x2, n_freq

    def rope_tables(self, seq):
        tok = np.arange(seq)
        inv = (ROPE_THETA ** (-np.arange(self.n_freq, dtype=np.float32) * 2.0
                              / (2 * self.n_freq))).astype(np.float32).astype(np.float64)
        pos = np.stack([tok // GRID_W, tok % GRID_W]).astype(np.float64)
        rot = self.axis >= 0
        ang = pos[np.maximum(self.axis, 0)].T * inv[self.freq][None, :]
        cos = np.where(rot[None], np.cos(ang), 1.0)
        sin = np.where(rot[None], np.sin(ang) * np.where(self.x2, 1.0, -1.0)[None], 0.0)
        return jnp.asarray(cos, F32), jnp.asarray(sin, F32)


def _gqa_layout():
    lane = np.arange(LANES)
    x2, slot, axis, freq = lane // 64, (lane % 64) // 32, (lane % 32) // 16, lane % 16
    return _LaneLayout((axis * 2 + x2) * 16 + freq, slot, axis, freq, x2.astype(bool), HEAD_DIM // 4)


def _mla_layout():
    lane = np.arange(LANES)
    is_rot = (lane % 64) < 16
    x2 = lane >= 64
    axis = np.where(is_rot, (lane % 64) // 8, -1)
    freq = np.where(is_rot, lane % 8, 0)
    nope = np.where(lane < 64, lane - 16, 48 + lane - 80)
    dim = np.where(is_rot, MLA_NOPE + (np.maximum(axis, 0) * 2 + x2) * 8 + freq, nope)
    dim = np.where(lane >= MLA_QK, -1, dim)
    return _LaneLayout(dim, np.zeros(LANES, int), axis, freq, x2 & is_rot, MLA_ROPE // 4)


def _take_cols(w, idx):
    wz = jnp.concatenate([w, jnp.zeros((w.shape[0], 1), w.dtype)], axis=1)
    return wz[:, np.where(idx < 0, w.shape[1], idx)]


def _na_bias_plan(n_rows):
    n_blocks = n_rows // NA_BLOCK_ROWS
    j_idx = np.zeros((3, NA_KEY_TILES, NA_BLOCK_ROWS, NA_BLOCK_ROWS), np.int32)
    ok = np.zeros(j_idx.shape, bool)
    for cls, u in enumerate((0, 1, n_blocks - 1)):
        t0, cls_u = _na_block_tiles(u, n_blocks)
        assert cls_u == cls
        for ch in range(NA_KEY_TILES):
            for i in range(NA_BLOCK_ROWS):
                for dr in range(NA_BLOCK_ROWS):
                    r, kr = NA_BLOCK_ROWS * u + dr, NA_BLOCK_ROWS * (t0 + ch) + i
                    r0 = min(max(r - NA_ROWS // 2, 0), n_rows - NA_ROWS)
                    ok[cls, ch, i, dr] = r0 <= kr < r0 + NA_ROWS
                    j_idx[cls, ch, i, dr] = kr - r + NA_ROWS - 1
    return j_idx, ok


def _na_band_table(rpb):
    cols = np.arange(GRID_W)
    c0 = np.clip(cols - NA_COLS // 2, 0, GRID_W - NA_COLS)
    kc = np.arange(GRID_W)
    inside = (kc[None, :] >= c0[:, None]) & (kc[None, :] < c0[:, None] + NA_COLS)
    dc = kc[None, :] - cols[:, None] + (NA_COLS - 1)
    onehot = (np.arange(2 * NA_COLS - 1)[:, None, None] == dc[None]) & inside[None]
    sel = jnp.einsum("hjd,dck->hjkc", rpb * LOG2E, jnp.asarray(onehot, F32),
                     precision=lax.Precision.HIGHEST)
    band = jnp.where(jnp.asarray(inside.T)[None, None], sel, NEG_BIG).astype(F32)
    return jnp.concatenate([band, band], axis=-1)


def kernel(x, c, ctx, c_ctx, l0_ada_w, l0_ada_b, l0_norm_mix, l0_norm_ffn, l0_w_in, l0_pool_w, l0_pool_scale, l0_q_gain, l0_k_gain, l0_w_out, l0_ffn_w_gate, l0_ffn_w_up, l0_ffn_w_down, l1_ada_w, l1_ada_b, l1_norm_mix, l1_norm_ffn, l1_w_in, l1_mla_q_a_gain, l1_mla_kv_a_gain, l1_mla_w_uq, l1_mla_w_ukv, l1_mla_q_gain, l1_mla_k_gain, l1_na_q_gain, l1_na_k_gain, l1_na_rpb, l1_w_out, l1_ffn_w_gate, l1_ffn_w_up, l1_ffn_w_down):
    b, s, d = x.shape
    tc = ctx.shape[1]
    tm_lat = 512
    tp_lat = 1024
    tq_lat = 1024

    cond = jnp.concatenate([c, c_ctx[None, :], jnp.zeros((7, d), F32)], axis=0)

    def mods(ada_w, ada_b):
        return _ada(cond, ada_w, ada_b).reshape(cond.shape[0], 6, 1, d)

    def ffn_weights(wg, wu, wd):
        return wg.astype(BF16), wu.astype(BF16), wd.astype(BF16)

    row1 = lambda v: v.reshape(1, -1)
    tile4 = lambda v, mul: (jnp.tile(v, SLAB // v.shape[0]) * mul).reshape(1, -1)
    sh64 = jnp.asarray(np.kron(np.eye(SLAB // HEAD_DIM), np.ones((HEAD_DIM, HEAD_DIM))), BF16)

    def slab_gain(g, lay, mul):
        per_lane = jnp.where(jnp.asarray(lay.dim >= 0), g[np.maximum(lay.dim, 0)], 0.0) * mul
        return jnp.tile(per_lane, SLAB // LANES).reshape(1, -1)

    def slab_same_head(lay):
        head = np.concatenate([np.where(lay.dim >= 0, blk * 2 + lay.slot, -1)
                               for blk in range(SLAB // LANES)])
        return jnp.asarray((head[:, None] == head[None, :]) & (head[:, None] >= 0), BF16)

    def pair_order(w_q, axis):
        shp = w_q.shape
        split = shp[:axis] + (GQA_KV_HEADS // 2, 2, GQA_GROUP, HEAD_DIM) + shp[axis + 1:]
        perm = list(range(len(split)))
        perm[axis + 1], perm[axis + 2] = perm[axis + 2], perm[axis + 1]
        return w_q.reshape(split).transpose(perm).reshape(shp)

    mods0 = mods(l0_ada_w, l0_ada_b)
    glay = _gqa_layout()
    q_cols = np.concatenate([((2 * m + glay.slot) * GQA_GROUP + g) * HEAD_DIM + glay.dim
                             for m in range(GQA_KV_HEADS // 2) for g in range(GQA_GROUP)])
    k_cols = np.concatenate([(2 * m + glay.slot) * HEAD_DIM + glay.dim
                             for m in range(GQA_KV_HEADS // 2)])
    q0, k0, v0 = POOL_DIM, POOL_DIM + GQA_Q_DIM, POOL_DIM + GQA_Q_DIM + GQA_KV_DIM
    w_in0 = jnp.concatenate([l0_w_in[:, :q0], l0_w_in[:, q0:k0][:, q_cols],
                             l0_w_in[:, k0:v0][:, k_cols], l0_w_in[:, v0:]], axis=1).astype(BF16)
    cos0, sin0 = glay.rope_tables(s)
    ones_c = jnp.ones((tc, LANES), F32)
    zeros_c = jnp.zeros((tc, LANES), F32)
    qg4 = slab_gain(l0_q_gain, glay, HEAD_DIM ** 0.5 * GQA_QSCALE)
    kg4 = slab_gain(l0_k_gain, glay, HEAD_DIM ** 0.5)
    sh_gqa = slab_same_head(glay)
    gmask_a = jnp.asarray((glay.slot == 0)[None], BF16)
    gmask_b = 1 - gmask_a

    a_l, q_l, k_l, v_l = _inproj0(x, mods0, None, row1(l0_norm_mix), w_in0, sh_gqa, qg4, kg4,
                                  cos0, sin0, rope=True, tm=tp_lat)
    a_c, q_c, k_c, v_c = _inproj0(ctx, mods0, b, row1(l0_norm_mix), w_in0, sh_gqa, qg4, kg4,
                                  ones_c, zeros_c, rope=False, tm=tc)
    n_kv_pairs = GQA_KV_HEADS // 2
    attn_l = _pair_attn(q_l, gmask_a, gmask_b, k_c, v_c, k_l, v_l, dk=LANES, n_pairs=n_kv_pairs,
                        q_per_pair=GQA_GROUP, blocks_per_step=1, tq=tq_lat, name="gqa_lat")
    attn_c = _pair_attn(q_c, gmask_a, gmask_b, k_c, v_c, None, None, dk=LANES, n_pairs=n_kv_pairs,
                        q_per_pair=GQA_GROUP, blocks_per_step=GQA_GROUP, tq=tc, name="gqa_ctx")

    eye = jnp.eye(POOL_GROUPS, dtype=F32)
    pw_bd = (eye[:, None, :, None] * l0_pool_w[:, :, None, :]).reshape(POOL_DIM, POOL_DIM).astype(BF16)
    pool_l = _pool(a_l, pw_bd, row1(l0_pool_scale))
    pool_c = _pool(a_c, pw_bd, row1(l0_pool_scale))

    w_out0_pool = l0_w_out[:POOL_DIM].astype(BF16)
    w_out0_attn = pair_order(l0_w_out[POOL_DIM:], 0).astype(BF16)
    ffn0 = ffn_weights(l0_ffn_w_gate, l0_ffn_w_up, l0_ffn_w_down)

    x1 = _post(x, mods0, None, pool_l, attn_l, w_out0_pool, w_out0_attn, row1(l0_norm_ffn), *ffn0,
               tm=tm_lat, name="post0_lat")
    xc = _post(ctx, mods0, b, pool_c, attn_c, w_out0_pool, w_out0_attn, row1(l0_norm_ffn), *ffn0,
               tm=tc, name="post0_ctx")

    mods1 = mods(l1_ada_w, l1_ada_b)
    mlay = _mla_layout()
    heads = np.arange(MLA_HEADS)[:, None]
    o_kr = MLA_Q_LORA + MLA_KV_LORA
    kr_block = _take_cols(l1_w_in, np.where(mlay.dim >= MLA_NOPE, o_kr + mlay.dim - MLA_NOPE, -1))
    w_in1 = jnp.concatenate([l1_w_in[:, :o_kr], kr_block, l1_w_in[:, o_kr + MLA_ROPE:]],
                            axis=1).astype(BF16)
    wuq = _take_cols(l1_mla_w_uq, np.where(mlay.dim >= 0, heads * MLA_QK + mlay.dim, -1).reshape(-1)
                     ).astype(BF16)
    kv_w = MLA_NOPE + MLA_V
    k_idx = np.where((mlay.dim >= 0) & (mlay.dim < MLA_NOPE), heads * kv_w + mlay.dim, -1).reshape(-1)
    v_idx = (heads * kv_w + MLA_NOPE + np.arange(MLA_V)[None, :]).reshape(-1)
    wukv = _take_cols(l1_mla_w_ukv, np.concatenate([k_idx, v_idx])).astype(BF16)

    shmla = slab_same_head(mlay)
    mqg = slab_gain(l1_mla_q_gain, mlay, MLA_QK ** 0.5 * MLA_QSCALE)
    mkg = slab_gain(l1_mla_k_gain, mlay, MLA_QK ** 0.5)
    nqg = tile4(l1_na_q_gain, NA_HEAD_DIM ** 0.5 * NA_QSCALE)
    nkg = tile4(l1_na_k_gain, NA_HEAD_DIM ** 0.5)
    cos1, sin1 = mlay.rope_tables(s)
    common = (row1(l1_norm_mix), w_in1, row1(l1_mla_q_a_gain), row1(l1_mla_kv_a_gain), wuq, wukv,
              sh64, shmla, mqg, mkg, nqg, nkg)
    mq, mk, mv, nq, nk, nv = _inproj1(x1, mods1, None, *common, cos1, sin1, rope=True, want_q=True,
                                      tm=tp_lat)
    mkc, mvc, nkc, nvc = _inproj1(xc, mods1, b, *common, ones_c, zeros_c, rope=False, want_q=False,
                                  tm=tc)

    no_mask = jnp.ones((1, MLA_PAIR_W), BF16)
    o_mla = _pair_attn(mq, no_mask, no_mask, mkc, mvc, mk, mv, dk=MLA_PAIR_W,
                       n_pairs=MLA_HEADS // 2, q_per_pair=1, blocks_per_step=1, tq=tq_lat, name="mla")
    o_na = _na_attn(nq, nkc, nvc, nk, nv, _na_band_table(l1_na_rpb))

    w_out1 = l1_w_out.astype(BF16)
    ffn1 = ffn_weights(l1_ffn_w_gate, l1_ffn_w_up, l1_ffn_w_down)
    return _post(x1, mods1, None, o_mla, o_na, w_out1[:MLA_HEADS * MLA_V], w_out1[MLA_HEADS * MLA_V:],
                 row1(l1_norm_ffn), *ffn1, tm=tm_lat, name="post1_lat")
```

```python
import functools

import numpy as np
import jax
import jax.numpy as jnp
from jax import lax
from jax.experimental import pallas as pl
from jax.experimental.pallas import tpu as pltpu

F32 = jnp.float32
BF16 = jnp.bfloat16

D_MODEL = 1024
GRID_W = 64
ROPE_THETA = 10000.0
EPS = 1e-6
LANES = 128
SLAB = 256

POOL_GROUPS = 4
POOL_CH = 64
POOL_DIM = POOL_GROUPS * POOL_CH
POOL_HALO = 16

HEAD_DIM = 64
GQA_HEADS = 12
GQA_KV_HEADS = 4
GQA_GROUP = GQA_HEADS // GQA_KV_HEADS
GQA_Q_DIM = GQA_HEADS * HEAD_DIM
GQA_KV_DIM = GQA_KV_HEADS * HEAD_DIM

MLA_HEADS = 8
MLA_NOPE = 64
MLA_ROPE = 32
MLA_QK = MLA_NOPE + MLA_ROPE
MLA_V = 64
MLA_Q_LORA = 384
MLA_KV_LORA = 256
MLA_PAIR_W = 2 * LANES

NA_HEADS = 8
NA_HEAD_DIM = 64
NA_DIM = NA_HEADS * NA_HEAD_DIM
NA_ROWS = 8
NA_COLS = 16
NA_BLOCK_ROWS = 4
NA_KEY_TILES = 3

FFN_DIM = -(-8 * D_MODEL // (3 * 256)) * 256
FFN_CHUNK = 256

NEG_BIG = -1e30
LOG2E = 1.4426950408889634
GQA_QSCALE = HEAD_DIM ** -0.5 * LOG2E
MLA_QSCALE = MLA_QK ** -0.5 * LOG2E
NA_QSCALE = NA_HEAD_DIM ** -0.5 * LOG2E
PROJ_SUB = 256
ATTN_SUB = 256
ATTN_KEY_CHUNK = 256
ATTN_TILES_IN_FLIGHT = 2
ATTN_LOOKAHEAD = 1

VMEM_LIMIT = 56 * 1024 * 1024

_NT = (((1,), (1,)), ((), ()))


def _params(n_axes):
    return pltpu.CompilerParams(dimension_semantics=("arbitrary",) * n_axes,
                                vmem_limit_bytes=VMEM_LIMIT)


def _modulate(x, gain, shift, scale):
    ms = jnp.mean(x * x, axis=-1, keepdims=True)
    return x * lax.rsqrt(ms + EPS) * gain * (1.0 + scale) + shift


def _row_rmsnorm(x, gain):
    ms = jnp.mean(x * x, axis=-1, keepdims=True)
    return x * lax.rsqrt(ms + EPS) * gain


def _lane(shape):
    return lax.broadcasted_iota(jnp.int32, shape, 1)


def _head_rmsnorm(slab, same_head, gain, n):
    ssq = jnp.dot((slab * slab).astype(BF16), same_head, preferred_element_type=F32)
    return slab * lax.rsqrt(ssq + n * EPS) * gain


def _rope(blk, cos, sin_signed):
    return blk * cos + pltpu.roll(blk, LANES // 2, 1) * sin_signed


def _ada_kernel(c_ref, w_ref, b_ref, o_ref):
    cnd = c_ref[...]
    act = cnd * jax.nn.sigmoid(cnd)
    o_ref[...] = jnp.dot(act.astype(BF16), w_ref[...].astype(BF16),
                         preferred_element_type=F32) + b_ref[...]


def _ada(cond, w, b):
    n_rows, d = cond.shape
    n = w.shape[1]
    tn = 1024
    return pl.pallas_call(
        _ada_kernel,
        grid=(n // tn,),
        in_specs=[pl.BlockSpec((n_rows, d), lambda j: (0, 0)),
                  pl.BlockSpec((d, tn), lambda j: (0, j)),
                  pl.BlockSpec((1, tn), lambda j: (0, j))],
        out_specs=pl.BlockSpec((n_rows, tn), lambda j: (0, j)),
        out_shape=jax.ShapeDtypeStruct((n_rows, n), F32),
        compiler_params=_params(1),
        name="ada",
    )(cond, w, b.reshape(1, n))


def _slabs(src, c0, width):
    return [src[:, c0 + j * SLAB:c0 + (j + 1) * SLAB] for j in range(width // SLAB)]


def _norm_rope_store(slabs, same_head, gain, n, dst_ref, rows, rope_tables=None):
    for j, slab in enumerate(slabs):
        slab = _head_rmsnorm(slab, same_head, gain, n)
        for blk in range(SLAB // LANES):
            part = slab[:, blk * LANES:(blk + 1) * LANES]
            if rope_tables is not None:
                part = _rope(part, *rope_tables)
            o = j * SLAB + blk * LANES
            dst_ref[0, rows, o:o + LANES] = part.astype(BF16)


def _software_pipeline(n_rows, stages):
    sub = min(n_rows, PROJ_SUB)
    tiles = [slice(r, r + sub) for r in range(0, n_rows, sub)]
    states = [{} for _ in tiles]
    for it in range(len(tiles) + len(stages) - 1):
        for s, stage in enumerate(stages):
            t = it - s
            if stage is not None and 0 <= t < len(tiles):
                stage(tiles[t], states[t])


def _inproj0_kernel(x_ref, sh_ref, sc_ref, g_ref, w_ref, same_ref, qg_ref, kg_ref, cos_ref, sin_ref,
                    a_ref, q_ref, k_ref, v_ref, *, rope):
    q0 = POOL_DIM
    k0 = q0 + GQA_Q_DIM
    v0 = k0 + GQA_KV_DIM

    def project(rows, st):
        h = _modulate(x_ref[0, rows, :], g_ref[...], sh_ref[0, 0], sc_ref[0, 0]).astype(BF16)
        st["p"] = jnp.dot(h, w_ref[...], preferred_element_type=F32)

    def epilogue(rows, st):
        p = st["p"]
        a_ref[0, rows, :] = p[:, :POOL_DIM]
        tables = (cos_ref[rows, :], sin_ref[rows, :]) if rope else None
        for c0, width, gain_ref, dst in ((q0, GQA_Q_DIM, qg_ref, q_ref), (k0, GQA_KV_DIM, kg_ref, k_ref)):
            _norm_rope_store(_slabs(p, c0, width), same_ref[...], gain_ref[...], HEAD_DIM, dst, rows,
                             tables)
        v_ref[0, rows, :] = p[:, v0:v0 + GQA_KV_DIM].astype(BF16)

    _software_pipeline(x_ref.shape[1], [project, None, epilogue])


def _inproj0(x, mods, ctx_row, gain, w, same_head, q_gain, k_gain, cos, sin, *, rope, tm):
    b, t, d = x.shape
    row = lambda bi, i: (bi, i, 0)
    return pl.pallas_call(
        functools.partial(_inproj0_kernel, rope=rope),
        grid=(b, t // tm),
        in_specs=[pl.BlockSpec((1, tm, d), row),
                  _mod_spec(mods, 0, ctx_row), _mod_spec(mods, 1, ctx_row),
                  _resident(gain), _resident(w), _resident(same_head),
                  _resident(q_gain), _resident(k_gain),
                  pl.BlockSpec((tm, LANES), lambda bi, i: (i, 0)),
                  pl.BlockSpec((tm, LANES), lambda bi, i: (i, 0))],
        out_specs=[pl.BlockSpec((1, tm, POOL_DIM), row),
                   pl.BlockSpec((1, tm, GQA_Q_DIM), row),
                   pl.BlockSpec((1, tm, GQA_KV_DIM), row),
                   pl.BlockSpec((1, tm, GQA_KV_DIM), row)],
        out_shape=[jax.ShapeDtypeStruct((b, t, POOL_DIM), F32),
                   jax.ShapeDtypeStruct((b, t, GQA_Q_DIM), BF16),
                   jax.ShapeDtypeStruct((b, t, GQA_KV_DIM), BF16),
                   jax.ShapeDtypeStruct((b, t, GQA_KV_DIM), BF16)],
        compiler_params=_params(2),
        name="inproj0_rope" if rope else "inproj0_ctx",
    )(x, mods, mods, gain, w, same_head, q_gain, k_gain, cos, sin)


def _inproj1_kernel(x_ref, sh_ref, sc_ref, g_ref, w_ref, qag_ref, kvag_ref, wuq_ref, wukv_ref,
                    sh64_ref, shmla_ref, mqg_ref, mkg_ref, nqg_ref, nkg_ref, cos_ref, sin_ref,
                    *out_refs, rope, want_q):
    if want_q:
        mq_ref, mk_ref, mv_ref, nq_ref, nk_ref, nv_ref = out_refs
    else:
        mk_ref, mv_ref, nk_ref, nv_ref = out_refs
    o_ckv = MLA_Q_LORA
    o_kr = o_ckv + MLA_KV_LORA
    o_nq = o_kr + LANES
    o_nk = o_nq + NA_DIM
    o_nv = o_nk + NA_DIM
    mla_w = MLA_HEADS * LANES

    def project(rows, st):
        h = _modulate(x_ref[0, rows, :], g_ref[...], sh_ref[0, 0], sc_ref[0, 0]).astype(BF16)
        st["p"] = jnp.dot(h, w_ref[...], preferred_element_type=F32)

    def up_project(rows, st):
        p = st["p"]
        if want_q:
            st["qq"] = jnp.dot(_row_rmsnorm(p[:, :MLA_Q_LORA], qag_ref[...]).astype(BF16),
                               wuq_ref[...], preferred_element_type=F32)
        st["kv"] = jnp.dot(_row_rmsnorm(p[:, o_ckv:o_ckv + MLA_KV_LORA], kvag_ref[...]).astype(BF16),
                           wukv_ref[...], preferred_element_type=F32)

    def epilogue(rows, st):
        p, kv = st["p"], st["kv"]
        tables = (cos_ref[rows, :], sin_ref[rows, :]) if rope else None
        if want_q:
            _norm_rope_store(_slabs(st["qq"], 0, mla_w), shmla_ref[...], mqg_ref[...], MLA_QK,
                             mq_ref, rows, tables)
            _norm_rope_store(_slabs(p, o_nq, NA_DIM), sh64_ref[...], nqg_ref[...], NA_HEAD_DIM,
                             nq_ref, rows)
        kr_blk = p[:, o_kr:o_kr + LANES]
        kr_slab = jnp.concatenate([kr_blk, kr_blk], axis=1)
        _norm_rope_store([slab + kr_slab for slab in _slabs(kv, 0, mla_w)], shmla_ref[...],
                         mkg_ref[...], MLA_QK, mk_ref, rows, tables)
        mv_ref[0, rows, :] = kv[:, mla_w:mla_w + MLA_HEADS * MLA_V].astype(BF16)
        _norm_rope_store(_slabs(p, o_nk, NA_DIM), sh64_ref[...], nkg_ref[...], NA_HEAD_DIM,
                         nk_ref, rows)
        nv_ref[0, rows, :] = p[:, o_nv:o_nv + NA_DIM].astype(BF16)

    _software_pipeline(x_ref.shape[1], [project, up_project, epilogue])


def _inproj1(x, mods, ctx_row, gain, w, qag, kvag, wuq, wukv, sh64, shmla, mqg, mkg, nqg, nkg,
             cos, sin, *, rope, want_q, tm):
    b, t, d = x.shape
    row = lambda bi, i: (bi, i, 0)
    full = _resident
    mla_w = (MLA_HEADS // 2) * MLA_PAIR_W
    outs = [(mla_w, "mk"), (MLA_HEADS * MLA_V, "mv"), (NA_DIM, "nk"), (NA_DIM, "nv")]
    if want_q:
        outs = [(mla_w, "mq")] + outs[:2] + [(NA_DIM, "nq")] + outs[2:]
    return pl.pallas_call(
        functools.partial(_inproj1_kernel, rope=rope, want_q=want_q),
        grid=(b, t // tm),
        in_specs=[pl.BlockSpec((1, tm, d), row),
                  _mod_spec(mods, 0, ctx_row), _mod_spec(mods, 1, ctx_row),
                  full(gain), full(w), full(qag), full(kvag), full(wuq), full(wukv),
                  full(sh64), full(shmla), full(mqg), full(mkg), full(nqg), full(nkg),
                  pl.BlockSpec((tm, LANES), lambda bi, i: (i, 0)),
                  pl.BlockSpec((tm, LANES), lambda bi, i: (i, 0))],
        out_specs=[pl.BlockSpec((1, tm, n), row) for n, _ in outs],
        out_shape=[jax.ShapeDtypeStruct((b, t, n), BF16) for n, _ in outs],
        compiler_params=_params(2),
        name="inproj1_lat" if want_q else "inproj1_ctx",
    )(x, mods, mods, gain, w, qag, kvag, wuq, wukv, sh64, shmla, mqg, mkg, nqg, nkg, cos, sin)


def _fill_pair_kv_t(k, v, mask_a, mask_b, ka_ref, kb_ref, vat_ref, vbt_ref, r0):
    n = k.shape[0]
    if mask_a is None:
        ka_ref[r0:r0 + n, :] = k[:, :LANES]
        kb_ref[r0:r0 + n, :] = k[:, LANES:]
    else:
        ka_ref[r0:r0 + n, :] = k * mask_a
        kb_ref[r0:r0 + n, :] = k * mask_b
    vt = v.astype(F32).T
    first = lax.broadcasted_iota(jnp.int32, vt.shape, 0) < HEAD_DIM
    vat_ref[:, r0:r0 + n] = jnp.where(first, vt, 1.0).astype(BF16)
    vbt_ref[:, r0:r0 + n] = jnp.where(first, 1.0, vt).astype(BF16)


def _default_key_chunks(n_keys):
    return [(k0, min(ATTN_KEY_CHUNK, n_keys - k0), None) for k0 in range(0, n_keys, ATTN_KEY_CHUNK)]


def _pair_softmax_pv(groups, ka_ref, kb_ref, vat_ref, vbt_ref, emit):
    heads = ((ka_ref, vat_ref), (kb_ref, vbt_ref))
    jobs = [(g, j) for g, tiles in enumerate(groups) for j in range(len(tiles[0][1]))]
    queries = {}

    def scores(job):
        g, j = job
        if g not in queries:
            queries[g] = [load_q() for load_q, _, _ in groups[g]]
        out = []
        for (_, key_chunks, _), qs in zip(groups[g], queries[g]):
            k0, size, bias = key_chunks[j]
            for head, (q, (k_ref, _)) in enumerate(zip(qs, heads)):
                sc = lax.dot_general(k_ref[k0:k0 + size, :], q, _NT, preferred_element_type=F32)
                out.append(sc if bias is None else sc + bias(head))
        return out

    pending = [scores(job) for job in jobs[:ATTN_LOOKAHEAD]]
    for n, (g, j) in enumerate(jobs):
        if n + ATTN_LOOKAHEAD < len(jobs):
            pending.append(scores(jobs[n + ATTN_LOOKAHEAD]))
        s = pending.pop(0)
        if j == 0:
            run_max = [None] * len(s)
            acc = [None] * len(s)
        for t, (_, key_chunks, _) in enumerate(groups[g]):
            k0, size, _ = key_chunks[j]
            for head, (_, vt_ref) in enumerate(heads):
                c = 2 * t + head
                mx = s[c].max(axis=0, keepdims=True)
                if j > 0:
                    mx = jnp.maximum(mx, run_max[c])
                p = jnp.exp2((s[c] - mx).astype(BF16))
                pv = jnp.dot(vt_ref[:, k0:k0 + size], p, preferred_element_type=F32)
                acc[c] = pv if j == 0 else acc[c] * jnp.exp2(run_max[c] - mx) + pv
                run_max[c] = mx
        if j == len(groups[g][0][1]) - 1:
            for t, (_, _, tag) in enumerate(groups[g]):
                oa, ob = acc[2 * t], acc[2 * t + 1]
                top = oa[:HEAD_DIM] / oa[HEAD_DIM:HEAD_DIM + 1]
                bot = ob[HEAD_DIM:] / ob[0:1]
                emit(tag, jnp.concatenate([top, bot], axis=0).T)


def _pair_attn_kernel(*refs, use_lat, steps_per_pair, dk, split):
    if use_lat:
        q_ref, ma_ref, mb_ref, kc_ref, vc_ref, kl_ref, vl_ref, o_ref, ka, kb, vat, vbt = refs
        segs = ((kc_ref, vc_ref), (kl_ref, vl_ref))
    else:
        q_ref, ma_ref, mb_ref, kc_ref, vc_ref, o_ref, ka, kb, vat, vbt = refs
        segs = ((kc_ref, vc_ref),)

    @pl.when(jnp.logical_and(pl.program_id(2) == 0, pl.program_id(1) % steps_per_pair == 0))
    def _():
        r0 = 0
        for k_ref, v_ref in segs:
            masks = (None, None) if split else (ma_ref[...], mb_ref[...])
            _fill_pair_kv_t(k_ref[0], v_ref[0], *masks, ka, kb, vat, vbt, r0)
            r0 += k_ref.shape[1]

    sub = min(q_ref.shape[1], ATTN_SUB)
    key_chunks = _default_key_chunks(ka.shape[0])

    def load_q(r0, blk):
        q = q_ref[0, r0:r0 + sub, blk * dk:(blk + 1) * dk]
        return (q[:, :LANES], q[:, LANES:]) if split else (q, q)

    def emit(tag, out):
        r0, blk = tag
        o_ref[0, r0:r0 + sub, blk * LANES:(blk + 1) * LANES] = out.astype(o_ref.dtype)

    tiles = [(functools.partial(load_q, r0, blk), key_chunks, (r0, blk))
             for blk in range(q_ref.shape[2] // dk) for r0 in range(0, q_ref.shape[1], sub)]
    groups = [tiles[i:i + ATTN_TILES_IN_FLIGHT] for i in range(0, len(tiles), ATTN_TILES_IN_FLIGHT)]
    _pair_softmax_pv(groups, ka, kb, vat, vbt, emit)


def _pair_attn(q, mask_a, mask_b, kc, vc, kl, vl, *, dk, n_pairs, q_per_pair, blocks_per_step, tq,
               name):
    b, t, _ = q.shape
    split = dk == 2 * LANES
    dk_head = dk // 2 if split else dk
    assert q_per_pair % blocks_per_step == 0
    steps_per_pair = q_per_pair // blocks_per_step
    n_qblk = n_pairs * steps_per_pair
    use_lat = kl is not None
    qmap = lambda bi, c, i: (bi, i, c)
    kvmap = lambda bi, c, i: (bi, 0, c // steps_per_pair)
    const = lambda bi, c, i: (0, 0)
    in_specs = [pl.BlockSpec((1, tq, blocks_per_step * dk), qmap),
                pl.BlockSpec((1, dk), const),
                pl.BlockSpec((1, dk), const),
                pl.BlockSpec((1, kc.shape[1], dk), kvmap),
                pl.BlockSpec((1, vc.shape[1], LANES), kvmap)]
    args = [q, mask_a, mask_b, kc, vc]
    if use_lat:
        in_specs += [pl.BlockSpec((1, kl.shape[1], dk), kvmap),
                     pl.BlockSpec((1, vl.shape[1], LANES), kvmap)]
        args += [kl, vl]
    n_keys = kc.shape[1] + (kl.shape[1] if use_lat else 0)
    return pl.pallas_call(
        functools.partial(_pair_attn_kernel, use_lat=use_lat, steps_per_pair=steps_per_pair, dk=dk,
                          split=split),
        grid=(b, n_qblk, t // tq),
        in_specs=in_specs,
        out_specs=pl.BlockSpec((1, tq, blocks_per_step * LANES), qmap),
        out_shape=jax.ShapeDtypeStruct((b, t, n_pairs * q_per_pair * LANES), BF16),
        scratch_shapes=[pltpu.VMEM((n_keys, dk_head), BF16), pltpu.VMEM((n_keys, dk_head), BF16),
                        pltpu.VMEM((LANES, n_keys), BF16), pltpu.VMEM((LANES, n_keys), BF16)],
        compiler_params=_params(3),
        name=name,
    )(*args)


def _na_block_tiles(u, n_blocks):
    t0 = min(max(u - 1, 0), n_blocks - NA_KEY_TILES)
    return t0, (0 if u == 0 else 2 if u == n_blocks - 1 else 1)


def _na_kernel(q_ref, kc_ref, vc_ref, k_ref, v_ref, band_ref, o_ref, ka, kb, vat, vbt, bias_ref,
               *, n_rows):
    j_idx, ok = _na_bias_plan(n_rows)

    @pl.when(pl.program_id(1) == 0)
    def _():
        masked = jnp.full((GRID_W, GRID_W), NEG_BIG, F32)
        for head, cls, ch, i, dr in np.ndindex(2, *j_idx.shape):
            half = (dr % 2) * GRID_W
            piece = (band_ref[head, int(j_idx[cls, ch, i, dr]), :, half:half + GRID_W]
                     if ok[cls, ch, i, dr] else masked)
            bias_ref[head, cls, ch, i * GRID_W:(i + 1) * GRID_W, dr * GRID_W:(dr + 1) * GRID_W] = piece

    tc = kc_ref.shape[1]
    lane = _lane((1, LANES))
    mask_a = jnp.where(lane < NA_HEAD_DIM, 1.0, 0.0).astype(BF16)
    mask_b = jnp.where(lane < NA_HEAD_DIM, 0.0, 1.0).astype(BF16)
    _fill_pair_kv_t(kc_ref[0], vc_ref[0], mask_a, mask_b, ka, kb, vat, vbt, 0)
    _fill_pair_kv_t(k_ref[0], v_ref[0], mask_a, mask_b, ka, kb, vat, vbt, tc)

    n_blocks = n_rows // NA_BLOCK_ROWS
    blk_q = NA_BLOCK_ROWS * GRID_W

    def load_q(u):
        q = q_ref[0, u * blk_q:(u + 1) * blk_q, :]
        return q, q

    def emit(u, out):
        o_ref[0, u * blk_q:(u + 1) * blk_q, :] = out.astype(o_ref.dtype)

    tiles = []
    for u in range(n_blocks):
        t0, cls = _na_block_tiles(u, n_blocks)
        chunks = [(0, tc, None)]
        for ch in range(NA_KEY_TILES):
            chunks.append((tc + (t0 + ch) * blk_q, blk_q,
                           functools.partial(lambda head, cls, ch: bias_ref[head, cls, ch],
                                             cls=cls, ch=ch)))
        tiles.append((functools.partial(load_q, u), chunks, u))
    groups = [tiles[i:i + ATTN_TILES_IN_FLIGHT] for i in range(0, n_blocks, ATTN_TILES_IN_FLIGHT)]
    _pair_softmax_pv(groups, ka, kb, vat, vbt, emit)


def _na_attn(q, kc, vc, k, v, band):
    b, s, _ = q.shape
    n_rows = s // GRID_W
    n_pairs = NA_HEADS // 2
    n_keys = kc.shape[1] + s
    blk_q = NA_BLOCK_ROWS * GRID_W
    assert kc.shape[1] == ATTN_KEY_CHUNK and blk_q == ATTN_KEY_CHUNK
    blk = lambda rows: pl.BlockSpec((1, rows, LANES), lambda m, bi: (bi, 0, m))
    return pl.pallas_call(
        functools.partial(_na_kernel, n_rows=n_rows),
        grid=(n_pairs, b),
        in_specs=[blk(s), blk(kc.shape[1]), blk(vc.shape[1]), blk(s), blk(s),
                  pl.BlockSpec((2,) + band.shape[1:], lambda m, bi: (m, 0, 0, 0))],
        out_specs=blk(s),
        out_shape=jax.ShapeDtypeStruct((b, s, NA_DIM), BF16),
        scratch_shapes=[pltpu.VMEM((n_keys, LANES), BF16), pltpu.VMEM((n_keys, LANES), BF16),
                        pltpu.VMEM((LANES, n_keys), BF16), pltpu.VMEM((LANES, n_keys), BF16),
                        pltpu.VMEM((2, 3, NA_KEY_TILES, blk_q, blk_q), F32)],
        compiler_params=_params(2),
        name="na_attn",
    )(q, kc, vc, k, v, band)


def _pool_kernel(a_ref, pw_ref, ps_ref, o_ref, pad_ref, *, t_len, chunk):
    pad_ref[0:POOL_HALO, :] = jnp.zeros((POOL_HALO, POOL_DIM), F32)
    pad_ref[POOL_HALO + t_len:2 * POOL_HALO + t_len, :] = jnp.zeros((POOL_HALO, POOL_DIM), F32)
    pad_ref[POOL_HALO:POOL_HALO + t_len, :] = a_ref[0]
    lane = _lane((chunk, POOL_DIM))
    g0 = lane < POOL_CH
    g1 = lane < 2 * POOL_CH
    g2 = lane < 3 * POOL_CH
    half_w = jnp.where(g0, 1, jnp.where(g1, 2, jnp.where(g2, 4, 8)))
    for c in range(t_len // chunk):
        base = c * chunk

        def ld(off, base=base):
            return pad_ref[POOL_HALO + base + off:POOL_HALO + base + off + chunk, :]

        a0 = ld(0)
        w2 = ld(-1) + a0
        w4 = w2 + ld(-2) + ld(1)
        w8 = w4 + ld(-4) + ld(-3) + ld(2) + ld(3)
        w16 = w8
        for off in (-8, -7, -6, -5, 4, 5, 6, 7):
            w16 = w16 + ld(off)
        tok = base + lax.broadcasted_iota(jnp.int32, (chunk, POOL_DIM), 0)
        cnt = (jnp.minimum(tok + half_w, t_len) - jnp.maximum(tok - half_w, 0)).astype(F32)
        wsum = jnp.where(g0, w2, jnp.where(g1, w4, jnp.where(g2, w8, w16)))
        dlt = wsum / cnt - a0
        y = jnp.dot(dlt.astype(BF16), pw_ref[...], preferred_element_type=F32) * ps_ref[...]
        o_ref[0, base:base + chunk, :] = y.astype(o_ref.dtype)


def _pool(a, pw_bd, pscale):
    b, t, _ = a.shape
    chunk = min(t, 256)
    return pl.pallas_call(
        functools.partial(_pool_kernel, t_len=t, chunk=chunk),
        grid=(b,),
        in_specs=[pl.BlockSpec((1, t, POOL_DIM), lambda bi: (bi, 0, 0)),
                  pl.BlockSpec((POOL_DIM, POOL_DIM), lambda bi: (0, 0)),
                  pl.BlockSpec((1, POOL_DIM), lambda bi: (0, 0))],
        out_specs=pl.BlockSpec((1, t, POOL_DIM), lambda bi: (bi, 0, 0)),
        out_shape=jax.ShapeDtypeStruct((b, t, POOL_DIM), BF16),
        scratch_shapes=[pltpu.VMEM((t + 2 * POOL_HALO, POOL_DIM), F32)],
        compiler_params=_params(1),
        name="pool",
    )(a, pw_bd, pscale)


def _post_kernel(x_ref, g1_ref, sh_ref, sc_ref, g2_ref, m0_ref, m1_ref, w0_ref, w1_ref, g_ref,
                 wg_ref, wu_ref, wd_ref, o_ref):
    mix = (jnp.dot(m0_ref[0], w0_ref[...], preferred_element_type=F32)
           + jnp.dot(m1_ref[0], w1_ref[...], preferred_element_type=F32))
    x = x_ref[0] + g1_ref[0, 0] * mix
    h = _modulate(x, g_ref[...], sh_ref[0, 0], sc_ref[0, 0]).astype(BF16)
    n_chunks = wg_ref.shape[1] // FFN_CHUNK

    def gate_up(c):
        cols = slice(c * FFN_CHUNK, (c + 1) * FFN_CHUNK)
        return (jnp.dot(h, wg_ref[:, cols], preferred_element_type=F32),
                jnp.dot(h, wu_ref[:, cols], preferred_element_type=F32))

    acc = None
    gt, up = gate_up(0)
    for c in range(n_chunks):
        nxt = gate_up(c + 1) if c + 1 < n_chunks else None
        act = (gt * jax.nn.sigmoid(gt) * up).astype(BF16)
        part = jnp.dot(act, wd_ref[c * FFN_CHUNK:(c + 1) * FFN_CHUNK, :], preferred_element_type=F32)
        acc = part if acc is None else acc + part
        if nxt is not None:
            gt, up = nxt
    o_ref[0] = x + g2_ref[0, 0] * acc


def _mod_spec(mods, k, ctx_row):
    blk = (1, 1, 1, mods.shape[-1])
    if ctx_row is None:
        return pl.BlockSpec(blk, lambda bi, i: (bi, k, 0, 0))
    return pl.BlockSpec(blk, lambda bi, i: (ctx_row, k, 0, 0))


def _resident(a):
    nd = a.ndim
    return pl.BlockSpec(a.shape, lambda bi, i: (0,) * nd, pipeline_mode=pl.Buffered(1))


def _post(x, mods, ctx_row, m0, m1, w0, w1, gain, wg, wu, wd, *, tm, name):
    b, t, d = x.shape
    row = lambda bi, i: (bi, i, 0)
    return pl.pallas_call(
        _post_kernel,
        grid=(b, t // tm),
        in_specs=[pl.BlockSpec((1, tm, d), row),
                  _mod_spec(mods, 2, ctx_row), _mod_spec(mods, 3, ctx_row),
                  _mod_spec(mods, 4, ctx_row), _mod_spec(mods, 5, ctx_row),
                  pl.BlockSpec((1, tm, m0.shape[2]), row),
                  pl.BlockSpec((1, tm, m1.shape[2]), row),
                  _resident(w0), _resident(w1), _resident(gain),
                  _resident(wg), _resident(wu), _resident(wd)],
        out_specs=pl.BlockSpec((1, tm, d), row),
        out_shape=jax.ShapeDtypeStruct((b, t, d), F32),
        compiler_params=_params(2),
        name=name,
    )(x, mods, mods, mods, mods, m0, m1, w0, w1, gain, wg, wu, wd)


class _LaneLayout:
    def __init__(self, dim, slot, axis, freq, x2, n_freq):
        self.dim, self.slot, self.axis, self.freq, self.x2, self.n_freq = dim, slot, axis, freq, x2, n_freq

    def rope_tables(self, seq):
        tok = np.arange(seq)
        inv = (ROPE_THETA ** (-np.arange(self.n_freq, dtype=np.float32) * 2.0
                              / (2 * self.n_freq))).astype(np.float32).astype(np.float64)
        pos = np.stack([tok // GRID_W, tok % GRID_W]).astype(np.float64)
        rot = self.axis >= 0
        ang = pos[np.maximum(self.axis, 0)].T * inv[self.freq][None, :]
        cos = np.where(rot[None], np.cos(ang), 1.0)
        sin = np.where(rot[None], np.sin(ang) * np.where(self.x2, 1.0, -1.0)[None], 0.0)
        return jnp.asarray(cos, F32), jnp.asarray(sin, F32)


def _gqa_layout():
    lane = np.arange(LANES)
    x2, slot, axis, freq = lane // 64, (lane % 64) // 32, (lane % 32) // 16, lane % 16
    return _LaneLayout((axis * 2 + x2) * 16 + freq, slot, axis, freq, x2.astype(bool), HEAD_DIM // 4)


def _mla_layout():
    lane = np.arange(LANES)
    is_rot = (lane % 64) < 16
    x2 = lane >= 64
    axis = np.where(is_rot, (lane % 64) // 8, -1)
    freq = np.where(is_rot, lane % 8, 0)
    nope = np.where(lane < 64, lane - 16, 48 + lane - 80)
    dim = np.where(is_rot, MLA_NOPE + (np.maximum(axis, 0) * 2 + x2) * 8 + freq, nope)
    dim = np.where(lane >= MLA_QK, -1, dim)
    return _LaneLayout(dim, np.zeros(LANES, int), axis, freq, x2 & is_rot, MLA_ROPE // 4)


def _take_cols(w, idx):
    wz = jnp.concatenate([w, jnp.zeros((w.shape[0], 1), w.dtype)], axis=1)
    return wz[:, np.where(idx < 0, w.shape[1], idx)]


def _na_bias_plan(n_rows):
    n_blocks = n_rows // NA_BLOCK_ROWS
    j_idx = np.zeros((3, NA_KEY_TILES, NA_BLOCK_ROWS, NA_BLOCK_ROWS), np.int32)
    ok = np.zeros(j_idx.shape, bool)
    for cls, u in enumerate((0, 1, n_blocks - 1)):
        t0, cls_u = _na_block_tiles(u, n_blocks)
        assert cls_u == cls
        for ch in range(NA_KEY_TILES):
            for i in range(NA_BLOCK_ROWS):
                for dr in range(NA_BLOCK_ROWS):
                    r, kr = NA_BLOCK_ROWS * u + dr, NA_BLOCK_ROWS * (t0 + ch) + i
                    r0 = min(max(r - NA_ROWS // 2, 0), n_rows - NA_ROWS)
                    ok[cls, ch, i, dr] = r0 <= kr < r0 + NA_ROWS
                    j_idx[cls, ch, i, dr] = kr - r + NA_ROWS - 1
    return j_idx, ok


def _na_band_table(rpb):
    cols = np.arange(GRID_W)
    c0 = np.clip(cols - NA_COLS // 2, 0, GRID_W - NA_COLS)
    kc = np.arange(GRID_W)
    inside = (kc[None, :] >= c0[:, None]) & (kc[None, :] < c0[:, None] + NA_COLS)
    dc = kc[None, :] - cols[:, None] + (NA_COLS - 1)
    onehot = (np.arange(2 * NA_COLS - 1)[:, None, None] == dc[None]) & inside[None]
    sel = jnp.einsum("hjd,dck->hjkc", rpb * LOG2E, jnp.asarray(onehot, F32),
                     precision=lax.Precision.HIGHEST)
    band = jnp.where(jnp.asarray(inside.T)[None, None], sel, NEG_BIG).astype(F32)
    return jnp.concatenate([band, band], axis=-1)


def kernel(x, c, ctx, c_ctx, l0_ada_w, l0_ada_b, l0_norm_mix, l0_norm_ffn, l0_w_in, l0_pool_w, l0_pool_scale, l0_q_gain, l0_k_gain, l0_w_out, l0_ffn_w_gate, l0_ffn_w_up, l0_ffn_w_down, l1_ada_w, l1_ada_b, l1_norm_mix, l1_norm_ffn, l1_w_in, l1_mla_q_a_gain, l1_mla_kv_a_gain, l1_mla_w_uq, l1_mla_w_ukv, l1_mla_q_gain, l1_mla_k_gain, l1_na_q_gain, l1_na_k_gain, l1_na_rpb, l1_w_out, l1_ffn_w_gate, l1_ffn_w_up, l1_ffn_w_down):
    b, s, d = x.shape
    tc = ctx.shape[1]
    tm_lat = 512
    tp_lat = 1024
    tq_lat = 2048

    cond = jnp.concatenate([c, c_ctx[None, :], jnp.zeros((7, d), F32)], axis=0)

    def mods(ada_w, ada_b):
        return _ada(cond, ada_w, ada_b).reshape(cond.shape[0], 6, 1, d)

    def ffn_weights(wg, wu, wd):
        return wg.astype(BF16), wu.astype(BF16), wd.astype(BF16)

    row1 = lambda v: v.reshape(1, -1)
    tile4 = lambda v, mul: (jnp.tile(v, SLAB // v.shape[0]) * mul).reshape(1, -1)
    sh64 = jnp.asarray(np.kron(np.eye(SLAB // HEAD_DIM), np.ones((HEAD_DIM, HEAD_DIM))), BF16)

    def slab_gain(g, lay, mul):
        per_lane = jnp.where(jnp.asarray(lay.dim >= 0), g[np.maximum(lay.dim, 0)], 0.0) * mul
        return jnp.tile(per_lane, SLAB // LANES).reshape(1, -1)

    def slab_same_head(lay):
        head = np.concatenate([np.where(lay.dim >= 0, blk * 2 + lay.slot, -1)
                               for blk in range(SLAB // LANES)])
        return jnp.asarray((head[:, None] == head[None, :]) & (head[:, None] >= 0), BF16)

    def pair_order(w_q, axis):
        shp = w_q.shape
        split = shp[:axis] + (GQA_KV_HEADS // 2, 2, GQA_GROUP, HEAD_DIM) + shp[axis + 1:]
        perm = list(range(len(split)))
        perm[axis + 1], perm[axis + 2] = perm[axis + 2], perm[axis + 1]
        return w_q.reshape(split).transpose(perm).reshape(shp)

    mods0 = mods(l0_ada_w, l0_ada_b)
    glay = _gqa_layout()
    q_cols = np.concatenate([((2 * m + glay.slot) * GQA_GROUP + g) * HEAD_DIM + glay.dim
                             for m in range(GQA_KV_HEADS // 2) for g in range(GQA_GROUP)])
    k_cols = np.concatenate([(2 * m + glay.slot) * HEAD_DIM + glay.dim
                             for m in range(GQA_KV_HEADS // 2)])
    q0, k0, v0 = POOL_DIM, POOL_DIM + GQA_Q_DIM, POOL_DIM + GQA_Q_DIM + GQA_KV_DIM
    w_in0 = jnp.concatenate([l0_w_in[:, :q0], l0_w_in[:, q0:k0][:, q_cols],
                             l0_w_in[:, k0:v0][:, k_cols], l0_w_in[:, v0:]], axis=1).astype(BF16)
    cos0, sin0 = glay.rope_tables(s)
    ones_c = jnp.ones((tc, LANES), F32)
    zeros_c = jnp.zeros((tc, LANES), F32)
    qg4 = slab_gain(l0_q_gain, glay, HEAD_DIM ** 0.5 * GQA_QSCALE)
    kg4 = slab_gain(l0_k_gain, glay, HEAD_DIM ** 0.5)
    sh_gqa = slab_same_head(glay)
    gmask_a = jnp.asarray((glay.slot == 0)[None], BF16)
    gmask_b = 1 - gmask_a

    a_l, q_l, k_l, v_l = _inproj0(x, mods0, None, row1(l0_norm_mix), w_in0, sh_gqa, qg4, kg4,
                                  cos0, sin0, rope=True, tm=tp_lat)
    a_c, q_c, k_c, v_c = _inproj0(ctx, mods0, b, row1(l0_norm_mix), w_in0, sh_gqa, qg4, kg4,
                                  ones_c, zeros_c, rope=False, tm=tc)
    n_kv_pairs = GQA_KV_HEADS // 2
    attn_l = _pair_attn(q_l, gmask_a, gmask_b, k_c, v_c, k_l, v_l, dk=LANES, n_pairs=n_kv_pairs,
                        q_per_pair=GQA_GROUP, blocks_per_step=1, tq=tq_lat, name="gqa_lat")
    attn_c = _pair_attn(q_c, gmask_a, gmask_b, k_c, v_c, None, None, dk=LANES, n_pairs=n_kv_pairs,
                        q_per_pair=GQA_GROUP, blocks_per_step=GQA_GROUP, tq=tc, name="gqa_ctx")

    eye = jnp.eye(POOL_GROUPS, dtype=F32)
    pw_bd = (eye[:, None, :, None] * l0_pool_w[:, :, None, :]).reshape(POOL_DIM, POOL_DIM).astype(BF16)
    pool_l = _pool(a_l, pw_bd, row1(l0_pool_scale))
    pool_c = _pool(a_c, pw_bd, row1(l0_pool_scale))

    w_out0_pool = l0_w_out[:POOL_DIM].astype(BF16)
    w_out0_attn = pair_order(l0_w_out[POOL_DIM:], 0).astype(BF16)
    ffn0 = ffn_weights(l0_ffn_w_gate, l0_ffn_w_up, l0_ffn_w_down)

    x1 = _post(x, mods0, None, pool_l, attn_l, w_out0_pool, w_out0_attn, row1(l0_norm_ffn), *ffn0,
               tm=tm_lat, name="post0_lat")
    xc = _post(ctx, mods0, b, pool_c, attn_c, w_out0_pool, w_out0_attn, row1(l0_norm_ffn), *ffn0,
               tm=tc, name="post0_ctx")

    mods1 = mods(l1_ada_w, l1_ada_b)
    mlay = _mla_layout()
    heads = np.arange(MLA_HEADS)[:, None]
    o_kr = MLA_Q_LORA + MLA_KV_LORA
    kr_block = _take_cols(l1_w_in, np.where(mlay.dim >= MLA_NOPE, o_kr + mlay.dim - MLA_NOPE, -1))
    w_in1 = jnp.concatenate([l1_w_in[:, :o_kr], kr_block, l1_w_in[:, o_kr + MLA_ROPE:]],
                            axis=1).astype(BF16)
    wuq = _take_cols(l1_mla_w_uq, np.where(mlay.dim >= 0, heads * MLA_QK + mlay.dim, -1).reshape(-1)
                     ).astype(BF16)
    kv_w = MLA_NOPE + MLA_V
    k_idx = np.where((mlay.dim >= 0) & (mlay.dim < MLA_NOPE), heads * kv_w + mlay.dim, -1).reshape(-1)
    v_idx = (heads * kv_w + MLA_NOPE + np.arange(MLA_V)[None, :]).reshape(-1)
    wukv = _take_cols(l1_mla_w_ukv, np.concatenate([k_idx, v_idx])).astype(BF16)

    shmla = slab_same_head(mlay)
    mqg = slab_gain(l1_mla_q_gain, mlay, MLA_QK ** 0.5 * MLA_QSCALE)
    mkg = slab_gain(l1_mla_k_gain, mlay, MLA_QK ** 0.5)
    nqg = tile4(l1_na_q_gain, NA_HEAD_DIM ** 0.5 * NA_QSCALE)
    nkg = tile4(l1_na_k_gain, NA_HEAD_DIM ** 0.5)
    cos1, sin1 = mlay.rope_tables(s)
    common = (row1(l1_norm_mix), w_in1, row1(l1_mla_q_a_gain), row1(l1_mla_kv_a_gain), wuq, wukv,
              sh64, shmla, mqg, mkg, nqg, nkg)
    mq, mk, mv, nq, nk, nv = _inproj1(x1, mods1, None, *common, cos1, sin1, rope=True, want_q=True,
                                      tm=tp_lat)
    mkc, mvc, nkc, nvc = _inproj1(xc, mods1, b, *common, ones_c, zeros_c, rope=False, want_q=False,
                                  tm=tc)

    no_mask = jnp.ones((1, MLA_PAIR_W), BF16)
    o_mla = _pair_attn(mq, no_mask, no_mask, mkc, mvc, mk, mv, dk=MLA_PAIR_W,
                       n_pairs=MLA_HEADS // 2, q_per_pair=1, blocks_per_step=1, tq=tq_lat, name="mla")
    o_na = _na_attn(nq, nkc, nvc, nk, nv, _na_band_table(l1_na_rpb))

    w_out1 = l1_w_out.astype(BF16)
    ffn1 = ffn_weights(l1_ffn_w_gate, l1_ffn_w_up, l1_ffn_w_down)
    return _post(x1, mods1, None, o_mla, o_na, w_out1[:MLA_HEADS * MLA_V], w_out1[MLA_HEADS * MLA_V:],
                 row1(l1_norm_ffn), *ffn1, tm=tm_lat, name="post1_lat")
```

```python
import functools

import numpy as np
import jax
import jax.numpy as jnp
from jax import lax
from jax.experimental import pallas as pl
from jax.experimental.pallas import tpu as pltpu

F32 = jnp.float32
BF16 = jnp.bfloat16

D_MODEL = 1024
GRID_W = 64
ROPE_THETA = 10000.0
EPS = 1e-6
LANES = 128
SLAB = 256

POOL_GROUPS = 4
POOL_CH = 64
POOL_DIM = POOL_GROUPS * POOL_CH
POOL_HALO = 16

HEAD_DIM = 64
GQA_HEADS = 12
GQA_KV_HEADS = 4
GQA_GROUP = GQA_HEADS // GQA_KV_HEADS
GQA_Q_DIM = GQA_HEADS * HEAD_DIM
GQA_KV_DIM = GQA_KV_HEADS * HEAD_DIM

MLA_HEADS = 8
MLA_NOPE = 64
MLA_ROPE = 32
MLA_QK = MLA_NOPE + MLA_ROPE
MLA_V = 64
MLA_Q_LORA = 384
MLA_KV_LORA = 256
MLA_PAIR_W = 2 * LANES

NA_HEADS = 8
NA_HEAD_DIM = 64
NA_DIM = NA_HEADS * NA_HEAD_DIM
NA_ROWS = 8
NA_COLS = 16
NA_BLOCK_ROWS = 4
NA_KEY_TILES = 3

FFN_DIM = -(-8 * D_MODEL // (3 * 256)) * 256
FFN_CHUNK = 256

NEG_BIG = -1e30
LOG2E = 1.4426950408889634
GQA_QSCALE = HEAD_DIM ** -0.5 * LOG2E
MLA_QSCALE = MLA_QK ** -0.5 * LOG2E
NA_QSCALE = NA_HEAD_DIM ** -0.5 * LOG2E
PROJ_SUB = 256
ATTN_SUB = 256
ATTN_KEY_CHUNK = 256
ATTN_TILES_IN_FLIGHT = 2
NA_TILES_IN_FLIGHT = 4
ATTN_LOOKAHEAD = 1

VMEM_LIMIT = 56 * 1024 * 1024

_NT = (((1,), (1,)), ((), ()))


def _params(n_axes):
    return pltpu.CompilerParams(dimension_semantics=("arbitrary",) * n_axes,
                                vmem_limit_bytes=VMEM_LIMIT)


def _modulate(x, gain, shift, scale):
    ms = jnp.mean(x * x, axis=-1, keepdims=True)
    return x * lax.rsqrt(ms + EPS) * gain * (1.0 + scale) + shift


def _row_rmsnorm(x, gain):
    ms = jnp.mean(x * x, axis=-1, keepdims=True)
    return x * lax.rsqrt(ms + EPS) * gain


def _lane(shape):
    return lax.broadcasted_iota(jnp.int32, shape, 1)


def _head_rmsnorm(slab, same_head, gain, n):
    ssq = jnp.dot((slab * slab).astype(BF16), same_head, preferred_element_type=F32)
    return slab * lax.rsqrt(ssq + n * EPS) * gain


def _rope(blk, cos, sin_signed):
    return blk * cos + pltpu.roll(blk, LANES // 2, 1) * sin_signed


def _ada_kernel(c_ref, w_ref, b_ref, o_ref):
    cnd = c_ref[...]
    act = cnd * jax.nn.sigmoid(cnd)
    o_ref[...] = jnp.dot(act.astype(BF16), w_ref[...].astype(BF16),
                         preferred_element_type=F32) + b_ref[...]


def _ada(cond, w, b):
    n_rows, d = cond.shape
    n = w.shape[1]
    tn = 1024
    return pl.pallas_call(
        _ada_kernel,
        grid=(n // tn,),
        in_specs=[pl.BlockSpec((n_rows, d), lambda j: (0, 0)),
                  pl.BlockSpec((d, tn), lambda j: (0, j)),
                  pl.BlockSpec((1, tn), lambda j: (0, j))],
        out_specs=pl.BlockSpec((n_rows, tn), lambda j: (0, j)),
        out_shape=jax.ShapeDtypeStruct((n_rows, n), F32),
        compiler_params=_params(1),
        name="ada",
    )(cond, w, b.reshape(1, n))


def _slabs(src, c0, width):
    return [src[:, c0 + j * SLAB:c0 + (j + 1) * SLAB] for j in range(width // SLAB)]


def _norm_rope_store(slabs, same_head, gain, n, dst_ref, rows, rope_tables=None):
    for j, slab in enumerate(slabs):
        slab = _head_rmsnorm(slab, same_head, gain, n)
        for blk in range(SLAB // LANES):
            part = slab[:, blk * LANES:(blk + 1) * LANES]
            if rope_tables is not None:
                part = _rope(part, *rope_tables)
            o = j * SLAB + blk * LANES
            dst_ref[0, rows, o:o + LANES] = part.astype(BF16)


def _software_pipeline(n_rows, stages):
    sub = min(n_rows, PROJ_SUB)
    tiles = [slice(r, r + sub) for r in range(0, n_rows, sub)]
    states = [{} for _ in tiles]
    for it in range(len(tiles) + len(stages) - 1):
        for s, stage in enumerate(stages):
            t = it - s
            if stage is not None and 0 <= t < len(tiles):
                stage(tiles[t], states[t])


def _inproj0_kernel(x_ref, sh_ref, sc_ref, g_ref, w_ref, same_ref, qg_ref, kg_ref, cos_ref, sin_ref,
                    a_ref, q_ref, k_ref, v_ref, *, rope):
    q0 = POOL_DIM
    k0 = q0 + GQA_Q_DIM
    v0 = k0 + GQA_KV_DIM

    def project(rows, st):
        h = _modulate(x_ref[0, rows, :], g_ref[...], sh_ref[0, 0], sc_ref[0, 0]).astype(BF16)
        st["p"] = jnp.dot(h, w_ref[...], preferred_element_type=F32)

    def epilogue(rows, st):
        p = st["p"]
        a_ref[0, rows, :] = p[:, :POOL_DIM]
        tables = (cos_ref[rows, :], sin_ref[rows, :]) if rope else None
        for c0, width, gain_ref, dst in ((q0, GQA_Q_DIM, qg_ref, q_ref), (k0, GQA_KV_DIM, kg_ref, k_ref)):
            _norm_rope_store(_slabs(p, c0, width), same_ref[...], gain_ref[...], HEAD_DIM, dst, rows,
                             tables)
        v_ref[0, rows, :] = p[:, v0:v0 + GQA_KV_DIM].astype(BF16)

    _software_pipeline(x_ref.shape[1], [project, None, epilogue])


def _inproj0(x, mods, ctx_row, gain, w, same_head, q_gain, k_gain, cos, sin, *, rope, tm):
    b, t, d = x.shape
    row = lambda bi, i: (bi, i, 0)
    return pl.pallas_call(
        functools.partial(_inproj0_kernel, rope=rope),
        grid=(b, t // tm),
        in_specs=[pl.BlockSpec((1, tm, d), row),
                  _mod_spec(mods, 0, ctx_row), _mod_spec(mods, 1, ctx_row),
                  _resident(gain), _resident(w), _resident(same_head),
                  _resident(q_gain), _resident(k_gain),
                  pl.BlockSpec((tm, LANES), lambda bi, i: (i, 0)),
                  pl.BlockSpec((tm, LANES), lambda bi, i: (i, 0))],
        out_specs=[pl.BlockSpec((1, tm, POOL_DIM), row),
                   pl.BlockSpec((1, tm, GQA_Q_DIM), row),
                   pl.BlockSpec((1, tm, GQA_KV_DIM), row),
                   pl.BlockSpec((1, tm, GQA_KV_DIM), row)],
        out_shape=[jax.ShapeDtypeStruct((b, t, POOL_DIM), F32),
                   jax.ShapeDtypeStruct((b, t, GQA_Q_DIM), BF16),
                   jax.ShapeDtypeStruct((b, t, GQA_KV_DIM), BF16),
                   jax.ShapeDtypeStruct((b, t, GQA_KV_DIM), BF16)],
        compiler_params=_params(2),
        name="inproj0_rope" if rope else "inproj0_ctx",
    )(x, mods, mods, gain, w, same_head, q_gain, k_gain, cos, sin)


def _inproj1_kernel(x_ref, sh_ref, sc_ref, g_ref, w_ref, qag_ref, kvag_ref, wuq_ref, wukv_ref,
                    sh64_ref, shmla_ref, mqg_ref, mkg_ref, nqg_ref, nkg_ref, cos_ref, sin_ref,
                    *out_refs, rope, want_q):
    if want_q:
        mq_ref, mk_ref, mv_ref, nq_ref, nk_ref, nv_ref = out_refs
    else:
        mk_ref, mv_ref, nk_ref, nv_ref = out_refs
    o_ckv = MLA_Q_LORA
    o_kr = o_ckv + MLA_KV_LORA
    o_nq = o_kr + LANES
    o_nk = o_nq + NA_DIM
    o_nv = o_nk + NA_DIM
    mla_w = MLA_HEADS * LANES

    def project(rows, st):
        h = _modulate(x_ref[0, rows, :], g_ref[...], sh_ref[0, 0], sc_ref[0, 0]).astype(BF16)
        st["p"] = jnp.dot(h, w_ref[...], preferred_element_type=F32)

    def up_project(rows, st):
        p = st["p"]
        if want_q:
            st["qq"] = jnp.dot(_row_rmsnorm(p[:, :MLA_Q_LORA], qag_ref[...]).astype(BF16),
                               wuq_ref[...], preferred_element_type=F32)
        st["kv"] = jnp.dot(_row_rmsnorm(p[:, o_ckv:o_ckv + MLA_KV_LORA], kvag_ref[...]).astype(BF16),
                           wukv_ref[...], preferred_element_type=F32)

    def epilogue(rows, st):
        p, kv = st["p"], st["kv"]
        tables = (cos_ref[rows, :], sin_ref[rows, :]) if rope else None
        if want_q:
            _norm_rope_store(_slabs(st["qq"], 0, mla_w), shmla_ref[...], mqg_ref[...], MLA_QK,
                             mq_ref, rows, tables)
            _norm_rope_store(_slabs(p, o_nq, NA_DIM), sh64_ref[...], nqg_ref[...], NA_HEAD_DIM,
                             nq_ref, rows)
        kr_blk = p[:, o_kr:o_kr + LANES]
        kr_slab = jnp.concatenate([kr_blk, kr_blk], axis=1)
        _norm_rope_store([slab + kr_slab for slab in _slabs(kv, 0, mla_w)], shmla_ref[...],
                         mkg_ref[...], MLA_QK, mk_ref, rows, tables)
        mv_ref[0, rows, :] = kv[:, mla_w:mla_w + MLA_HEADS * MLA_V].astype(BF16)
        _norm_rope_store(_slabs(p, o_nk, NA_DIM), sh64_ref[...], nkg_ref[...], NA_HEAD_DIM,
                         nk_ref, rows)
        nv_ref[0, rows, :] = p[:, o_nv:o_nv + NA_DIM].astype(BF16)

    _software_pipeline(x_ref.shape[1], [project, up_project, epilogue])


def _inproj1(x, mods, ctx_row, gain, w, qag, kvag, wuq, wukv, sh64, shmla, mqg, mkg, nqg, nkg,
             cos, sin, *, rope, want_q, tm):
    b, t, d = x.shape
    row = lambda bi, i: (bi, i, 0)
    full = _resident
    mla_w = (MLA_HEADS // 2) * MLA_PAIR_W
    outs = [(mla_w, "mk"), (MLA_HEADS * MLA_V, "mv"), (NA_DIM, "nk"), (NA_DIM, "nv")]
    if want_q:
        outs = [(mla_w, "mq")] + outs[:2] + [(NA_DIM, "nq")] + outs[2:]
    return pl.pallas_call(
        functools.partial(_inproj1_kernel, rope=rope, want_q=want_q),
        grid=(b, t // tm),
        in_specs=[pl.BlockSpec((1, tm, d), row),
                  _mod_spec(mods, 0, ctx_row), _mod_spec(mods, 1, ctx_row),
                  full(gain), full(w), full(qag), full(kvag), full(wuq), full(wukv),
                  full(sh64), full(shmla), full(mqg), full(mkg), full(nqg), full(nkg),
                  pl.BlockSpec((tm, LANES), lambda bi, i: (i, 0)),
                  pl.BlockSpec((tm, LANES), lambda bi, i: (i, 0))],
        out_specs=[pl.BlockSpec((1, tm, n), row) for n, _ in outs],
        out_shape=[jax.ShapeDtypeStruct((b, t, n), BF16) for n, _ in outs],
        compiler_params=_params(2),
        name="inproj1_lat" if want_q else "inproj1_ctx",
    )(x, mods, mods, gain, w, qag, kvag, wuq, wukv, sh64, shmla, mqg, mkg, nqg, nkg, cos, sin)


def _fill_pair_kv_t(k, v, mask_a, mask_b, ka_ref, kb_ref, vat_ref, vbt_ref, r0):
    n = k.shape[0]
    if mask_a is None:
        ka_ref[r0:r0 + n, :] = k[:, :LANES]
        kb_ref[r0:r0 + n, :] = k[:, LANES:]
    else:
        ka_ref[r0:r0 + n, :] = k * mask_a
        kb_ref[r0:r0 + n, :] = k * mask_b
    vt = v.astype(F32).T
    first = lax.broadcasted_iota(jnp.int32, vt.shape, 0) < HEAD_DIM
    vat_ref[:, r0:r0 + n] = jnp.where(first, vt, 1.0).astype(BF16)
    vbt_ref[:, r0:r0 + n] = jnp.where(first, 1.0, vt).astype(BF16)


def _default_key_chunks(n_keys):
    return [(k0, min(ATTN_KEY_CHUNK, n_keys - k0), None) for k0 in range(0, n_keys, ATTN_KEY_CHUNK)]


def _pair_softmax_pv(groups, ka_ref, kb_ref, vat_ref, vbt_ref, emit):
    heads = ((ka_ref, vat_ref), (kb_ref, vbt_ref))
    jobs = [(g, j) for g, tiles in enumerate(groups) for j in range(len(tiles[0][1]))]
    queries = {}

    def scores(job):
        g, j = job
        if g not in queries:
            queries[g] = [load_q() for load_q, _, _ in groups[g]]
        out = []
        for (_, key_chunks, _), qs in zip(groups[g], queries[g]):
            k0, size, bias = key_chunks[j]
            for head, (q, (k_ref, _)) in enumerate(zip(qs, heads)):
                sc = lax.dot_general(k_ref[k0:k0 + size, :], q, _NT, preferred_element_type=F32)
                out.append((sc if bias is None else sc + bias(head)).astype(BF16))
        return out

    pending = [scores(job) for job in jobs[:ATTN_LOOKAHEAD]]
    for n, (g, j) in enumerate(jobs):
        if n + ATTN_LOOKAHEAD < len(jobs):
            pending.append(scores(jobs[n + ATTN_LOOKAHEAD]))
        s = pending.pop(0)
        if j == 0:
            run_max = [None] * len(s)
            acc = [None] * len(s)
        for t, (_, key_chunks, _) in enumerate(groups[g]):
            k0, size, _ = key_chunks[j]
            for head, (_, vt_ref) in enumerate(heads):
                c = 2 * t + head
                mx = s[c].max(axis=0, keepdims=True)
                if j > 0:
                    mx = jnp.maximum(mx, run_max[c])
                p = jnp.exp2(s[c] - mx)
                pv = jnp.dot(vt_ref[:, k0:k0 + size], p, preferred_element_type=F32)
                if j > 0:
                    acc[c] = acc[c] * jnp.exp2(run_max[c].astype(F32) - mx.astype(F32)) + pv
                else:
                    acc[c] = pv
                run_max[c] = mx
        if j == len(groups[g][0][1]) - 1:
            for t, (_, _, tag) in enumerate(groups[g]):
                oa, ob = acc[2 * t], acc[2 * t + 1]
                top = oa[:HEAD_DIM] / oa[HEAD_DIM:HEAD_DIM + 1]
                bot = ob[HEAD_DIM:] / ob[0:1]
                emit(tag, jnp.concatenate([top, bot], axis=0).T)


def _pair_attn_kernel(*refs, use_lat, steps_per_pair, dk, split):
    if use_lat:
        q_ref, ma_ref, mb_ref, kc_ref, vc_ref, kl_ref, vl_ref, o_ref, ka, kb, vat, vbt = refs
        segs = ((kc_ref, vc_ref), (kl_ref, vl_ref))
    else:
        q_ref, ma_ref, mb_ref, kc_ref, vc_ref, o_ref, ka, kb, vat, vbt = refs
        segs = ((kc_ref, vc_ref),)

    @pl.when(jnp.logical_and(pl.program_id(2) == 0, pl.program_id(1) % steps_per_pair == 0))
    def _():
        r0 = 0
        for k_ref, v_ref in segs:
            masks = (None, None) if split else (ma_ref[...], mb_ref[...])
            _fill_pair_kv_t(k_ref[0], v_ref[0], *masks, ka, kb, vat, vbt, r0)
            r0 += k_ref.shape[1]

    sub = min(q_ref.shape[1], ATTN_SUB)
    key_chunks = _default_key_chunks(ka.shape[0])

    def load_q(r0, blk):
        q = q_ref[0, r0:r0 + sub, blk * dk:(blk + 1) * dk]
        return (q[:, :LANES], q[:, LANES:]) if split else (q, q)

    def emit(tag, out):
        r0, blk = tag
        o_ref[0, r0:r0 + sub, blk * LANES:(blk + 1) * LANES] = out.astype(o_ref.dtype)

    tiles = [(functools.partial(load_q, r0, blk), key_chunks, (r0, blk))
             for blk in range(q_ref.shape[2] // dk) for r0 in range(0, q_ref.shape[1], sub)]
    groups = [tiles[i:i + ATTN_TILES_IN_FLIGHT] for i in range(0, len(tiles), ATTN_TILES_IN_FLIGHT)]
    _pair_softmax_pv(groups, ka, kb, vat, vbt, emit)


def _pair_attn(q, mask_a, mask_b, kc, vc, kl, vl, *, dk, n_pairs, q_per_pair, blocks_per_step, tq,
               name):
    b, t, _ = q.shape
    split = dk == 2 * LANES
    dk_head = dk // 2 if split else dk
    assert q_per_pair % blocks_per_step == 0
    steps_per_pair = q_per_pair // blocks_per_step
    n_qblk = n_pairs * steps_per_pair
    use_lat = kl is not None
    qmap = lambda bi, c, i: (bi, i, c)
    kvmap = lambda bi, c, i: (bi, 0, c // steps_per_pair)
    const = lambda bi, c, i: (0, 0)
    in_specs = [pl.BlockSpec((1, tq, blocks_per_step * dk), qmap),
                pl.BlockSpec((1, dk), const),
                pl.BlockSpec((1, dk), const),
                pl.BlockSpec((1, kc.shape[1], dk), kvmap),
                pl.BlockSpec((1, vc.shape[1], LANES), kvmap)]
    args = [q, mask_a, mask_b, kc, vc]
    if use_lat:
        in_specs += [pl.BlockSpec((1, kl.shape[1], dk), kvmap),
                     pl.BlockSpec((1, vl.shape[1], LANES), kvmap)]
        args += [kl, vl]
    n_keys = kc.shape[1] + (kl.shape[1] if use_lat else 0)
    return pl.pallas_call(
        functools.partial(_pair_attn_kernel, use_lat=use_lat, steps_per_pair=steps_per_pair, dk=dk,
                          split=split),
        grid=(b, n_qblk, t // tq),
        in_specs=in_specs,
        out_specs=pl.BlockSpec((1, tq, blocks_per_step * LANES), qmap),
        out_shape=jax.ShapeDtypeStruct((b, t, n_pairs * q_per_pair * LANES), BF16),
        scratch_shapes=[pltpu.VMEM((n_keys, dk_head), BF16), pltpu.VMEM((n_keys, dk_head), BF16),
                        pltpu.VMEM((LANES, n_keys), BF16), pltpu.VMEM((LANES, n_keys), BF16)],
        compiler_params=_params(3),
        name=name,
    )(*args)


def _na_block_tiles(u, n_blocks):
    t0 = min(max(u - 1, 0), n_blocks - NA_KEY_TILES)
    return t0, (0 if u == 0 else 2 if u == n_blocks - 1 else 1)


def _na_kernel(q_ref, kc_ref, vc_ref, k_ref, v_ref, band_ref, o_ref, ka, kb, vat, vbt, bias_ref,
               *, n_rows):
    j_idx, ok = _na_bias_plan(n_rows)

    @pl.when(pl.program_id(1) == 0)
    def _():
        masked = jnp.full((GRID_W, GRID_W), NEG_BIG, F32)
        for head, cls, ch, i, dr in np.ndindex(2, *j_idx.shape):
            half = (dr % 2) * GRID_W
            piece = (band_ref[head, int(j_idx[cls, ch, i, dr]), :, half:half + GRID_W]
                     if ok[cls, ch, i, dr] else masked)
            bias_ref[head, cls, ch, i * GRID_W:(i + 1) * GRID_W, dr * GRID_W:(dr + 1) * GRID_W] = piece

    tc = kc_ref.shape[1]
    lane = _lane((1, LANES))
    mask_a = jnp.where(lane < NA_HEAD_DIM, 1.0, 0.0).astype(BF16)
    mask_b = jnp.where(lane < NA_HEAD_DIM, 0.0, 1.0).astype(BF16)
    _fill_pair_kv_t(kc_ref[0], vc_ref[0], mask_a, mask_b, ka, kb, vat, vbt, 0)
    _fill_pair_kv_t(k_ref[0], v_ref[0], mask_a, mask_b, ka, kb, vat, vbt, tc)

    n_blocks = n_rows // NA_BLOCK_ROWS
    blk_q = NA_BLOCK_ROWS * GRID_W

    def load_q(u):
        q = q_ref[0, u * blk_q:(u + 1) * blk_q, :]
        return q, q

    def emit(u, out):
        o_ref[0, u * blk_q:(u + 1) * blk_q, :] = out.astype(o_ref.dtype)

    tiles = []
    for u in range(n_blocks):
        t0, cls = _na_block_tiles(u, n_blocks)
        chunks = [(0, tc, None)]
        for ch in range(NA_KEY_TILES):
            chunks.append((tc + (t0 + ch) * blk_q, blk_q,
                           functools.partial(lambda head, cls, ch: bias_ref[head, cls, ch],
                                             cls=cls, ch=ch)))
        tiles.append((functools.partial(load_q, u), chunks, u))
    groups = [tiles[i:i + NA_TILES_IN_FLIGHT] for i in range(0, n_blocks, NA_TILES_IN_FLIGHT)]
    _pair_softmax_pv(groups, ka, kb, vat, vbt, emit)


def _na_attn(q, kc, vc, k, v, band):
    b, s, _ = q.shape
    n_rows = s // GRID_W
    n_pairs = NA_HEADS // 2
    n_keys = kc.shape[1] + s
    blk_q = NA_BLOCK_ROWS * GRID_W
    assert kc.shape[1] == ATTN_KEY_CHUNK and blk_q == ATTN_KEY_CHUNK
    blk = lambda rows: pl.BlockSpec((1, rows, LANES), lambda m, bi: (bi, 0, m))
    return pl.pallas_call(
        functools.partial(_na_kernel, n_rows=n_rows),
        grid=(n_pairs, b),
        in_specs=[blk(s), blk(kc.shape[1]), blk(vc.shape[1]), blk(s), blk(s),
                  pl.BlockSpec((2,) + band.shape[1:], lambda m, bi: (m, 0, 0, 0))],
        out_specs=blk(s),
        out_shape=jax.ShapeDtypeStruct((b, s, NA_DIM), BF16),
        scratch_shapes=[pltpu.VMEM((n_keys, LANES), BF16), pltpu.VMEM((n_keys, LANES), BF16),
                        pltpu.VMEM((LANES, n_keys), BF16), pltpu.VMEM((LANES, n_keys), BF16),
                        pltpu.VMEM((2, 3, NA_KEY_TILES, blk_q, blk_q), F32)],
        compiler_params=_params(2),
        name="na_attn",
    )(q, kc, vc, k, v, band)


def _pool_kernel(a_ref, pw_ref, ps_ref, o_ref, pad_ref, *, t_len, chunk):
    pad_ref[0:POOL_HALO, :] = jnp.zeros((POOL_HALO, POOL_DIM), F32)
    pad_ref[POOL_HALO + t_len:2 * POOL_HALO + t_len, :] = jnp.zeros((POOL_HALO, POOL_DIM), F32)
    pad_ref[POOL_HALO:POOL_HALO + t_len, :] = a_ref[0]
    lane = _lane((chunk, POOL_DIM))
    g0 = lane < POOL_CH
    g1 = lane < 2 * POOL_CH
    g2 = lane < 3 * POOL_CH
    half_w = jnp.where(g0, 1, jnp.where(g1, 2, jnp.where(g2, 4, 8)))
    for c in range(t_len // chunk):
        base = c * chunk

        def ld(off, base=base):
            return pad_ref[POOL_HALO + base + off:POOL_HALO + base + off + chunk, :]

        a0 = ld(0)
        w2 = ld(-1) + a0
        w4 = w2 + ld(-2) + ld(1)
        w8 = w4 + ld(-4) + ld(-3) + ld(2) + ld(3)
        w16 = w8
        for off in (-8, -7, -6, -5, 4, 5, 6, 7):
            w16 = w16 + ld(off)
        tok = base + lax.broadcasted_iota(jnp.int32, (chunk, POOL_DIM), 0)
        cnt = (jnp.minimum(tok + half_w, t_len) - jnp.maximum(tok - half_w, 0)).astype(F32)
        wsum = jnp.where(g0, w2, jnp.where(g1, w4, jnp.where(g2, w8, w16)))
        dlt = wsum / cnt - a0
        y = jnp.dot(dlt.astype(BF16), pw_ref[...], preferred_element_type=F32) * ps_ref[...]
        o_ref[0, base:base + chunk, :] = y.astype(o_ref.dtype)


def _pool(a, pw_bd, pscale):
    b, t, _ = a.shape
    chunk = min(t, 256)
    return pl.pallas_call(
        functools.partial(_pool_kernel, t_len=t, chunk=chunk),
        grid=(b,),
        in_specs=[pl.BlockSpec((1, t, POOL_DIM), lambda bi: (bi, 0, 0)),
                  pl.BlockSpec((POOL_DIM, POOL_DIM), lambda bi: (0, 0)),
                  pl.BlockSpec((1, POOL_DIM), lambda bi: (0, 0))],
        out_specs=pl.BlockSpec((1, t, POOL_DIM), lambda bi: (bi, 0, 0)),
        out_shape=jax.ShapeDtypeStruct((b, t, POOL_DIM), BF16),
        scratch_shapes=[pltpu.VMEM((t + 2 * POOL_HALO, POOL_DIM), F32)],
        compiler_params=_params(1),
        name="pool",
    )(a, pw_bd, pscale)


def _post_kernel(x_ref, g1_ref, sh_ref, sc_ref, g2_ref, m0_ref, m1_ref, w0_ref, w1_ref, g_ref,
                 wg_ref, wu_ref, wd_ref, o_ref):
    mix = (jnp.dot(m0_ref[0], w0_ref[...], preferred_element_type=F32)
           + jnp.dot(m1_ref[0], w1_ref[...], preferred_element_type=F32))
    x = x_ref[0] + g1_ref[0, 0] * mix
    h = _modulate(x, g_ref[...], sh_ref[0, 0], sc_ref[0, 0]).astype(BF16)
    n_chunks = wg_ref.shape[1] // FFN_CHUNK

    def gate_up(c):
        cols = slice(c * FFN_CHUNK, (c + 1) * FFN_CHUNK)
        return (jnp.dot(h, wg_ref[:, cols], preferred_element_type=F32),
                jnp.dot(h, wu_ref[:, cols], preferred_element_type=F32))

    acc = None
    gt, up = gate_up(0)
    for c in range(n_chunks):
        nxt = gate_up(c + 1) if c + 1 < n_chunks else None
        act = (gt * jax.nn.sigmoid(gt) * up).astype(BF16)
        part = jnp.dot(act, wd_ref[c * FFN_CHUNK:(c + 1) * FFN_CHUNK, :], preferred_element_type=F32)
        acc = part if acc is None else acc + part
        if nxt is not None:
            gt, up = nxt
    o_ref[0] = x + g2_ref[0, 0] * acc


def _mod_spec(mods, k, ctx_row):
    blk = (1, 1, 1, mods.shape[-1])
    if ctx_row is None:
        return pl.BlockSpec(blk, lambda bi, i: (bi, k, 0, 0))
    return pl.BlockSpec(blk, lambda bi, i: (ctx_row, k, 0, 0))


def _resident(a):
    nd = a.ndim
    return pl.BlockSpec(a.shape, lambda bi, i: (0,) * nd, pipeline_mode=pl.Buffered(1))


def _post(x, mods, ctx_row, m0, m1, w0, w1, gain, wg, wu, wd, *, tm, name):
    b, t, d = x.shape
    row = lambda bi, i: (bi, i, 0)
    return pl.pallas_call(
        _post_kernel,
        grid=(b, t // tm),
        in_specs=[pl.BlockSpec((1, tm, d), row),
                  _mod_spec(mods, 2, ctx_row), _mod_spec(mods, 3, ctx_row),
                  _mod_spec(mods, 4, ctx_row), _mod_spec(mods, 5, ctx_row),
                  pl.BlockSpec((1, tm, m0.shape[2]), row),
                  pl.BlockSpec((1, tm, m1.shape[2]), row),
                  _resident(w0), _resident(w1), _resident(gain),
                  _resident(wg), _resident(wu), _resident(wd)],
        out_specs=pl.BlockSpec((1, tm, d), row),
        out_shape=jax.ShapeDtypeStruct((b, t, d), F32),
        compiler_params=_params(2),
        name=name,
    )(x, mods, mods, mods, mods, m0, m1, w0, w1, gain, wg, wu, wd)


class _LaneLayout:
    def __init__(self, dim, slot, axis, freq, x2, n_freq):
        self.dim, self.slot, self.axis, self.freq, self.x2, self.n_freq = dim, slot, axis, freq, x2, n_freq

    def rope_tables(self, seq):
        tok = np.arange(seq)
        inv = (ROPE_THETA ** (-np.arange(self.n_freq, dtype=np.float32) * 2.0
                              / (2 * self.n_freq))).astype(np.float32).astype(np.float64)
        pos = np.stack([tok // GRID_W, tok % GRID_W]).astype(np.float64)
        rot = self.axis >= 0
        ang = pos[np.maximum(self.axis, 0)].T * inv[self.freq][None, :]
        cos = np.where(rot[None], np.cos(ang), 1.0)
        sin = np.where(rot[None], np.sin(ang) * np.where(self.x2, 1.0, -1.0)[None], 0.0)
        return jnp.asarray(cos, F32), jnp.asarray(sin, F32)


def _gqa_layout():
    lane = np.arange(LANES)
    x2, slot, axis, freq = lane // 64, (lane % 64) // 32, (lane % 32) // 16, lane % 16
    return _LaneLayout((axis * 2 + x2) * 16 + freq, slot, axis, freq, x2.astype(bool), HEAD_DIM // 4)


def _mla_layout():
    lane = np.arange(LANES)
    is_rot = (lane % 64) < 16
    x2 = lane >= 64
    axis = np.where(is_rot, (lane % 64) // 8, -1)
    freq = np.where(is_rot, lane % 8, 0)
    nope = np.where(lane < 64, lane - 16, 48 + lane - 80)
    dim = np.where(is_rot, MLA_NOPE + (np.maximum(axis, 0) * 2 + x2) * 8 + freq, nope)
    dim = np.where(lane >= MLA_QK, -1, dim)
    return _LaneLayout(dim, np.zeros(LANES, int), axis, freq, x2 & is_rot, MLA_ROPE // 4)


def _take_cols(w, idx):
    return jnp.where(jnp.asarray(idx >= 0)[None, :], w[:, np.maximum(idx, 0)], 0.0)


def _na_bias_plan(n_rows):
    n_blocks = n_rows // NA_BLOCK_ROWS
    j_idx = np.zeros((3, NA_KEY_TILES, NA_BLOCK_ROWS, NA_BLOCK_ROWS), np.int32)
    ok = np.zeros(j_idx.shape, bool)
    for cls, u in enumerate((0, 1, n_blocks - 1)):
        t0, cls_u = _na_block_tiles(u, n_blocks)
        assert cls_u == cls
        for ch in range(NA_KEY_TILES):
            for i in range(NA_BLOCK_ROWS):
                for dr in range(NA_BLOCK_ROWS):
                    r, kr = NA_BLOCK_ROWS * u + dr, NA_BLOCK_ROWS * (t0 + ch) + i
                    r0 = min(max(r - NA_ROWS // 2, 0), n_rows - NA_ROWS)
                    ok[cls, ch, i, dr] = r0 <= kr < r0 + NA_ROWS
                    j_idx[cls, ch, i, dr] = kr - r + NA_ROWS - 1
    return j_idx, ok


def _na_band_table(rpb):
    cols = np.arange(GRID_W)
    c0 = np.clip(cols - NA_COLS // 2, 0, GRID_W - NA_COLS)
    kc = np.arange(GRID_W)
    inside = (kc[None, :] >= c0[:, None]) & (kc[None, :] < c0[:, None] + NA_COLS)
    dc = kc[None, :] - cols[:, None] + (NA_COLS - 1)
    onehot = (np.arange(2 * NA_COLS - 1)[:, None, None] == dc[None]) & inside[None]
    sel = jnp.einsum("hjd,dck->hjkc", rpb * LOG2E, jnp.asarray(onehot, F32),
                     precision=lax.Precision.HIGHEST)
    band = jnp.where(jnp.asarray(inside.T)[None, None], sel, NEG_BIG).astype(F32)
    return jnp.concatenate([band, band], axis=-1)


def kernel(x, c, ctx, c_ctx, l0_ada_w, l0_ada_b, l0_norm_mix, l0_norm_ffn, l0_w_in, l0_pool_w, l0_pool_scale, l0_q_gain, l0_k_gain, l0_w_out, l0_ffn_w_gate, l0_ffn_w_up, l0_ffn_w_down, l1_ada_w, l1_ada_b, l1_norm_mix, l1_norm_ffn, l1_w_in, l1_mla_q_a_gain, l1_mla_kv_a_gain, l1_mla_w_uq, l1_mla_w_ukv, l1_mla_q_gain, l1_mla_k_gain, l1_na_q_gain, l1_na_k_gain, l1_na_rpb, l1_w_out, l1_ffn_w_gate, l1_ffn_w_up, l1_ffn_w_down):
    b, s, d = x.shape
    tc = ctx.shape[1]
    tm_lat = 512
    tp_lat = 1024
    tq_lat = 2048

    cond = jnp.concatenate([c, c_ctx[None, :], jnp.zeros((7, d), F32)], axis=0)

    def mods(ada_w, ada_b):
        return _ada(cond, ada_w, ada_b).reshape(cond.shape[0], 6, 1, d)

    def ffn_weights(wg, wu, wd):
        return wg.astype(BF16), wu.astype(BF16), wd.astype(BF16)

    row1 = lambda v: v.reshape(1, -1)
    tile4 = lambda v, mul: (jnp.tile(v, SLAB // v.shape[0]) * mul).reshape(1, -1)
    sh64 = jnp.asarray(np.kron(np.eye(SLAB // HEAD_DIM), np.ones((HEAD_DIM, HEAD_DIM))), BF16)

    def slab_gain(g, lay, mul):
        per_lane = jnp.where(jnp.asarray(lay.dim >= 0), g[np.maximum(lay.dim, 0)], 0.0) * mul
        return jnp.tile(per_lane, SLAB // LANES).reshape(1, -1)

    def slab_same_head(lay):
        head = np.concatenate([np.where(lay.dim >= 0, blk * 2 + lay.slot, -1)
                               for blk in range(SLAB // LANES)])
        return jnp.asarray((head[:, None] == head[None, :]) & (head[:, None] >= 0), BF16)

    def pair_order(w_q, axis):
        shp = w_q.shape
        split = shp[:axis] + (GQA_KV_HEADS // 2, 2, GQA_GROUP, HEAD_DIM) + shp[axis + 1:]
        perm = list(range(len(split)))
        perm[axis + 1], perm[axis + 2] = perm[axis + 2], perm[axis + 1]
        return w_q.reshape(split).transpose(perm).reshape(shp)

    mods0 = mods(l0_ada_w, l0_ada_b)
    glay = _gqa_layout()
    q_cols = np.concatenate([((2 * m + glay.slot) * GQA_GROUP + g) * HEAD_DIM + glay.dim
                             for m in range(GQA_KV_HEADS // 2) for g in range(GQA_GROUP)])
    k_cols = np.concatenate([(2 * m + glay.slot) * HEAD_DIM + glay.dim
                             for m in range(GQA_KV_HEADS // 2)])
    q0, k0, v0 = POOL_DIM, POOL_DIM + GQA_Q_DIM, POOL_DIM + GQA_Q_DIM + GQA_KV_DIM
    w_in0 = l0_w_in[:, np.concatenate([np.arange(q0), q0 + q_cols, k0 + k_cols,
                                       np.arange(v0, l0_w_in.shape[1])])].astype(BF16)
    cos0, sin0 = glay.rope_tables(s)
    ones_c = jnp.ones((tc, LANES), F32)
    zeros_c = jnp.zeros((tc, LANES), F32)
    qg4 = slab_gain(l0_q_gain, glay, HEAD_DIM ** 0.5 * GQA_QSCALE)
    kg4 = slab_gain(l0_k_gain, glay, HEAD_DIM ** 0.5)
    sh_gqa = slab_same_head(glay)
    gmask_a = jnp.asarray((glay.slot == 0)[None], BF16)
    gmask_b = 1 - gmask_a

    a_l, q_l, k_l, v_l = _inproj0(x, mods0, None, row1(l0_norm_mix), w_in0, sh_gqa, qg4, kg4,
                                  cos0, sin0, rope=True, tm=tp_lat)
    a_c, q_c, k_c, v_c = _inproj0(ctx, mods0, b, row1(l0_norm_mix), w_in0, sh_gqa, qg4, kg4,
                                  ones_c, zeros_c, rope=False, tm=tc)
    n_kv_pairs = GQA_KV_HEADS // 2
    attn_l = _pair_attn(q_l, gmask_a, gmask_b, k_c, v_c, k_l, v_l, dk=LANES, n_pairs=n_kv_pairs,
                        q_per_pair=GQA_GROUP, blocks_per_step=1, tq=tq_lat, name="gqa_lat")
    attn_c = _pair_attn(q_c, gmask_a, gmask_b, k_c, v_c, None, None, dk=LANES, n_pairs=n_kv_pairs,
                        q_per_pair=GQA_GROUP, blocks_per_step=GQA_GROUP, tq=tc, name="gqa_ctx")

    eye = jnp.eye(POOL_GROUPS, dtype=F32)
    pw_bd = (eye[:, None, :, None] * l0_pool_w[:, :, None, :]).reshape(POOL_DIM, POOL_DIM).astype(BF16)
    pool_l = _pool(a_l, pw_bd, row1(l0_pool_scale))
    pool_c = _pool(a_c, pw_bd, row1(l0_pool_scale))

    w_out0_pool = l0_w_out[:POOL_DIM].astype(BF16)
    w_out0_attn = pair_order(l0_w_out[POOL_DIM:], 0).astype(BF16)
    ffn0 = ffn_weights(l0_ffn_w_gate, l0_ffn_w_up, l0_ffn_w_down)

    x1 = _post(x, mods0, None, pool_l, attn_l, w_out0_pool, w_out0_attn, row1(l0_norm_ffn), *ffn0,
               tm=tm_lat, name="post0_lat")
    xc = _post(ctx, mods0, b, pool_c, attn_c, w_out0_pool, w_out0_attn, row1(l0_norm_ffn), *ffn0,
               tm=tc, name="post0_ctx")

    mods1 = mods(l1_ada_w, l1_ada_b)
    mlay = _mla_layout()
    heads = np.arange(MLA_HEADS)[:, None]
    o_kr = MLA_Q_LORA + MLA_KV_LORA
    kr_cols = np.where(mlay.dim >= MLA_NOPE, o_kr + mlay.dim - MLA_NOPE, -1)
    w_in1 = _take_cols(l1_w_in, np.concatenate([np.arange(o_kr), kr_cols,
                                                np.arange(o_kr + MLA_ROPE, l1_w_in.shape[1])])
                       ).astype(BF16)
    wuq = _take_cols(l1_mla_w_uq, np.where(mlay.dim >= 0, heads * MLA_QK + mlay.dim, -1).reshape(-1)
                     ).astype(BF16)
    kv_w = MLA_NOPE + MLA_V
    k_idx = np.where((mlay.dim >= 0) & (mlay.dim < MLA_NOPE), heads * kv_w + mlay.dim, -1).reshape(-1)
    v_idx = (heads * kv_w + MLA_NOPE + np.arange(MLA_V)[None, :]).reshape(-1)
    wukv = _take_cols(l1_mla_w_ukv, np.concatenate([k_idx, v_idx])).astype(BF16)

    shmla = slab_same_head(mlay)
    mqg = slab_gain(l1_mla_q_gain, mlay, MLA_QK ** 0.5 * MLA_QSCALE)
    mkg = slab_gain(l1_mla_k_gain, mlay, MLA_QK ** 0.5)
    nqg = tile4(l1_na_q_gain, NA_HEAD_DIM ** 0.5 * NA_QSCALE)
    nkg = tile4(l1_na_k_gain, NA_HEAD_DIM ** 0.5)
    cos1, sin1 = mlay.rope_tables(s)
    common = (row1(l1_norm_mix), w_in1, row1(l1_mla_q_a_gain), row1(l1_mla_kv_a_gain), wuq, wukv,
              sh64, shmla, mqg, mkg, nqg, nkg)
    mq, mk, mv, nq, nk, nv = _inproj1(x1, mods1, None, *common, cos1, sin1, rope=True, want_q=True,
                                      tm=tp_lat)
    mkc, mvc, nkc, nvc = _inproj1(xc, mods1, b, *common, ones_c, zeros_c, rope=False, want_q=False,
                                  tm=tc)

    no_mask = jnp.ones((1, MLA_PAIR_W), BF16)
    o_mla = _pair_attn(mq, no_mask, no_mask, mkc, mvc, mk, mv, dk=MLA_PAIR_W,
                       n_pairs=MLA_HEADS // 2, q_per_pair=1, blocks_per_step=1, tq=tq_lat, name="mla")
    o_na = _na_attn(nq, nkc, nvc, nk, nv, _na_band_table(l1_na_rpb))

    w_out1 = l1_w_out.astype(BF16)
    ffn1 = ffn_weights(l1_ffn_w_gate, l1_ffn_w_up, l1_ffn_w_down)
    return _post(x1, mods1, None, o_mla, o_na, w_out1[:MLA_HEADS * MLA_V], w_out1[MLA_HEADS * MLA_V:],
                 row1(l1_norm_ffn), *ffn1, tm=tm_lat, name="post1_lat")
```

```python
import functools

import numpy as np
import jax
import jax.numpy as jnp
from jax import lax
from jax.experimental import pallas as pl
from jax.experimental.pallas import tpu as pltpu

F32 = jnp.float32
BF16 = jnp.bfloat16

D_MODEL = 1024
GRID_W = 64
ROPE_THETA = 10000.0
EPS = 1e-6
LANES = 128
SLAB = 256

POOL_GROUPS = 4
POOL_CH = 64
POOL_DIM = POOL_GROUPS * POOL_CH
POOL_HALO = 16

HEAD_DIM = 64
GQA_HEADS = 12
GQA_KV_HEADS = 4
GQA_GROUP = GQA_HEADS // GQA_KV_HEADS
GQA_Q_DIM = GQA_HEADS * HEAD_DIM
GQA_KV_DIM = GQA_KV_HEADS * HEAD_DIM

MLA_HEADS = 8
MLA_NOPE = 64
MLA_ROPE = 32
MLA_QK = MLA_NOPE + MLA_ROPE
MLA_V = 64
MLA_Q_LORA = 384
MLA_KV_LORA = 256
MLA_PAIR_W = 2 * LANES

NA_HEADS = 8
NA_HEAD_DIM = 64
NA_DIM = NA_HEADS * NA_HEAD_DIM
NA_ROWS = 8
NA_COLS = 16
NA_BLOCK_ROWS = 4
NA_KEY_TILES = 3

FFN_DIM = -(-8 * D_MODEL // (3 * 256)) * 256
FFN_CHUNK = 256

NEG_BIG = -1e30
LOG2E = 1.4426950408889634
GQA_QSCALE = HEAD_DIM ** -0.5 * LOG2E
MLA_QSCALE = MLA_QK ** -0.5 * LOG2E
NA_QSCALE = NA_HEAD_DIM ** -0.5 * LOG2E
POST_SUB = 256
PROJ_SUB = 256
ATTN_SUB = 256
ATTN_KEY_CHUNK = 256
ATTN_TILES_IN_FLIGHT = 2
NA_TILES_IN_FLIGHT = 4
ATTN_LOOKAHEAD = 1

VMEM_LIMIT = 56 * 1024 * 1024

_NT = (((1,), (1,)), ((), ()))


def _params(n_axes):
    return pltpu.CompilerParams(dimension_semantics=("arbitrary",) * n_axes,
                                vmem_limit_bytes=VMEM_LIMIT)


def _modulate(x, gain, shift, scale):
    ms = jnp.mean(x * x, axis=-1, keepdims=True)
    return x * lax.rsqrt(ms + EPS) * gain * (1.0 + scale) + shift


def _row_rmsnorm(x, gain):
    ms = jnp.mean(x * x, axis=-1, keepdims=True)
    return x * lax.rsqrt(ms + EPS) * gain


def _lane(shape):
    return lax.broadcasted_iota(jnp.int32, shape, 1)


def _head_rmsnorm(slab, same_head, gain, n):
    ssq = jnp.dot((slab * slab).astype(BF16), same_head, preferred_element_type=F32)
    return slab * lax.rsqrt(ssq + n * EPS) * gain


def _rope(blk, cos, sin_signed):
    return blk * cos + pltpu.roll(blk, LANES // 2, 1) * sin_signed


def _ada_kernel(c_ref, w_ref, b_ref, o_ref):
    cnd = c_ref[...]
    act = cnd * jax.nn.sigmoid(cnd)
    o_ref[...] = jnp.dot(act.astype(BF16), w_ref[...].astype(BF16),
                         preferred_element_type=F32) + b_ref[...]


def _ada(cond, w, b):
    n_rows, d = cond.shape
    n = w.shape[1]
    tn = 2048
    return pl.pallas_call(
        _ada_kernel,
        grid=(n // tn,),
        in_specs=[pl.BlockSpec((n_rows, d), lambda j: (0, 0)),
                  pl.BlockSpec((d, tn), lambda j: (0, j)),
                  pl.BlockSpec((1, tn), lambda j: (0, j))],
        out_specs=pl.BlockSpec((n_rows, tn), lambda j: (0, j)),
        out_shape=jax.ShapeDtypeStruct((n_rows, n), F32),
        compiler_params=_params(1),
        name="ada",
    )(cond, w, b.reshape(1, n))


def _slabs(src, c0, width):
    return [src[:, c0 + j * SLAB:c0 + (j + 1) * SLAB] for j in range(width // SLAB)]


def _norm_rope_store(slabs, same_head, gain, n, dst_ref, rows, rope_tables=None):
    for j, slab in enumerate(slabs):
        slab = _head_rmsnorm(slab, same_head, gain, n)
        for blk in range(SLAB // LANES):
            part = slab[:, blk * LANES:(blk + 1) * LANES]
            if rope_tables is not None:
                part = _rope(part, *rope_tables)
            o = j * SLAB + blk * LANES
            dst_ref[0, rows, o:o + LANES] = part.astype(BF16)


def _software_pipeline(n_rows, stages):
    sub = min(n_rows, PROJ_SUB)
    tiles = [slice(r, r + sub) for r in range(0, n_rows, sub)]
    states = [{} for _ in tiles]
    for it in range(len(tiles) + len(stages) - 1):
        for s, stage in enumerate(stages):
            t = it - s
            if stage is not None and 0 <= t < len(tiles):
                stage(tiles[t], states[t])


def _inproj0_kernel(x_ref, sh_ref, sc_ref, g_ref, w_ref, same_ref, qg_ref, kg_ref, cos_ref, sin_ref,
                    a_ref, q_ref, k_ref, v_ref, *, rope):
    q0 = POOL_DIM
    k0 = q0 + GQA_Q_DIM
    v0 = k0 + GQA_KV_DIM

    def project(rows, st):
        h = _modulate(x_ref[0, rows, :], g_ref[...], sh_ref[0, 0], sc_ref[0, 0]).astype(BF16)
        st["p"] = jnp.dot(h, w_ref[...], preferred_element_type=F32)

    def epilogue(rows, st):
        p = st["p"]
        a_ref[0, rows, :] = p[:, :POOL_DIM]
        tables = (cos_ref[rows, :], sin_ref[rows, :]) if rope else None
        for c0, width, gain_ref, dst in ((q0, GQA_Q_DIM, qg_ref, q_ref), (k0, GQA_KV_DIM, kg_ref, k_ref)):
            _norm_rope_store(_slabs(p, c0, width), same_ref[...], gain_ref[...], HEAD_DIM, dst, rows,
                             tables)
        v_ref[0, rows, :] = p[:, v0:v0 + GQA_KV_DIM].astype(BF16)

    _software_pipeline(x_ref.shape[1], [project, None, epilogue])


def _inproj0(x, mods, ctx_row, gain, w, same_head, q_gain, k_gain, cos, sin, *, rope, tm):
    b, t, d = x.shape
    row = lambda bi, i: (bi, i, 0)
    return pl.pallas_call(
        functools.partial(_inproj0_kernel, rope=rope),
        grid=(b, t // tm),
        in_specs=[pl.BlockSpec((1, tm, d), row),
                  _mod_spec(mods, 0, ctx_row), _mod_spec(mods, 1, ctx_row),
                  _resident(gain), _resident(w), _resident(same_head),
                  _resident(q_gain), _resident(k_gain),
                  pl.BlockSpec((tm, LANES), lambda bi, i: (i, 0)),
                  pl.BlockSpec((tm, LANES), lambda bi, i: (i, 0))],
        out_specs=[pl.BlockSpec((1, tm, POOL_DIM), row),
                   pl.BlockSpec((1, tm, GQA_Q_DIM), row),
                   pl.BlockSpec((1, tm, GQA_KV_DIM), row),
                   pl.BlockSpec((1, tm, GQA_KV_DIM), row)],
        out_shape=[jax.ShapeDtypeStruct((b, t, POOL_DIM), F32),
                   jax.ShapeDtypeStruct((b, t, GQA_Q_DIM), BF16),
                   jax.ShapeDtypeStruct((b, t, GQA_KV_DIM), BF16),
                   jax.ShapeDtypeStruct((b, t, GQA_KV_DIM), BF16)],
        compiler_params=_params(2),
        name="inproj0_rope" if rope else "inproj0_ctx",
    )(x, mods, mods, gain, w, same_head, q_gain, k_gain, cos, sin)


def _inproj1_kernel(x_ref, sh_ref, sc_ref, g_ref, w_ref, qag_ref, kvag_ref, wuq_ref, wukv_ref,
                    sh64_ref, shmla_ref, mqg_ref, mkg_ref, nqg_ref, nkg_ref, cos_ref, sin_ref,
                    *out_refs, rope, want_q):
    if want_q:
        mq_ref, mk_ref, mv_ref, nq_ref, nk_ref, nv_ref = out_refs
    else:
        mk_ref, mv_ref, nk_ref, nv_ref = out_refs
    o_ckv = MLA_Q_LORA
    o_kr = o_ckv + MLA_KV_LORA
    o_nq = o_kr + LANES
    o_nk = o_nq + NA_DIM
    o_nv = o_nk + NA_DIM
    mla_w = MLA_HEADS * LANES

    def project(rows, st):
        h = _modulate(x_ref[0, rows, :], g_ref[...], sh_ref[0, 0], sc_ref[0, 0]).astype(BF16)
        st["p"] = jnp.dot(h, w_ref[...], preferred_element_type=F32)

    def up_project(rows, st):
        p = st["p"]
        if want_q:
            st["qq"] = jnp.dot(_row_rmsnorm(p[:, :MLA_Q_LORA], qag_ref[...]).astype(BF16),
                               wuq_ref[...], preferred_element_type=F32)
        st["kv"] = jnp.dot(_row_rmsnorm(p[:, o_ckv:o_ckv + MLA_KV_LORA], kvag_ref[...]).astype(BF16),
                           wukv_ref[...], preferred_element_type=F32)

    def epilogue(rows, st):
        p, kv = st["p"], st["kv"]
        tables = (cos_ref[rows, :], sin_ref[rows, :]) if rope else None
        if want_q:
            _norm_rope_store(_slabs(st["qq"], 0, mla_w), shmla_ref[...], mqg_ref[...], MLA_QK,
                             mq_ref, rows, tables)
            _norm_rope_store(_slabs(p, o_nq, NA_DIM), sh64_ref[...], nqg_ref[...], NA_HEAD_DIM,
                             nq_ref, rows)
        kr_blk = p[:, o_kr:o_kr + LANES]
        kr_slab = jnp.concatenate([kr_blk, kr_blk], axis=1)
        _norm_rope_store([slab + kr_slab for slab in _slabs(kv, 0, mla_w)], shmla_ref[...],
                         mkg_ref[...], MLA_QK, mk_ref, rows, tables)
        mv_ref[0, rows, :] = kv[:, mla_w:mla_w + MLA_HEADS * MLA_V].astype(BF16)
        _norm_rope_store(_slabs(p, o_nk, NA_DIM), sh64_ref[...], nkg_ref[...], NA_HEAD_DIM,
                         nk_ref, rows)
        nv_ref[0, rows, :] = p[:, o_nv:o_nv + NA_DIM].astype(BF16)

    _software_pipeline(x_ref.shape[1], [project, up_project, epilogue])


def _inproj1(x, mods, ctx_row, gain, w, qag, kvag, wuq, wukv, sh64, shmla, mqg, mkg, nqg, nkg,
             cos, sin, *, rope, want_q, tm):
    b, t, d = x.shape
    row = lambda bi, i: (bi, i, 0)
    full = _resident
    mla_w = (MLA_HEADS // 2) * MLA_PAIR_W
    outs = [(mla_w, "mk"), (MLA_HEADS * MLA_V, "mv"), (NA_DIM, "nk"), (NA_DIM, "nv")]
    if want_q:
        outs = [(mla_w, "mq")] + outs[:2] + [(NA_DIM, "nq")] + outs[2:]
    return pl.pallas_call(
        functools.partial(_inproj1_kernel, rope=rope, want_q=want_q),
        grid=(b, t // tm),
        in_specs=[pl.BlockSpec((1, tm, d), row),
                  _mod_spec(mods, 0, ctx_row), _mod_spec(mods, 1, ctx_row),
                  full(gain), full(w), full(qag), full(kvag), full(wuq), full(wukv),
                  full(sh64), full(shmla), full(mqg), full(mkg), full(nqg), full(nkg),
                  pl.BlockSpec((tm, LANES), lambda bi, i: (i, 0)),
                  pl.BlockSpec((tm, LANES), lambda bi, i: (i, 0))],
        out_specs=[pl.BlockSpec((1, tm, n), row) for n, _ in outs],
        out_shape=[jax.ShapeDtypeStruct((b, t, n), BF16) for n, _ in outs],
        compiler_params=_params(2),
        name="inproj1_lat" if want_q else "inproj1_ctx",
    )(x, mods, mods, gain, w, qag, kvag, wuq, wukv, sh64, shmla, mqg, mkg, nqg, nkg, cos, sin)


def _fill_pair_kv_t(k, v, mask_a, mask_b, ka_ref, kb_ref, vat_ref, vbt_ref, r0):
    n = k.shape[0]
    if mask_a is None:
        ka_ref[r0:r0 + n, :] = k[:, :LANES]
        kb_ref[r0:r0 + n, :] = k[:, LANES:]
    else:
        ka_ref[r0:r0 + n, :] = k * mask_a
        kb_ref[r0:r0 + n, :] = k * mask_b
    vt = v.astype(F32).T
    first = lax.broadcasted_iota(jnp.int32, vt.shape, 0) < HEAD_DIM
    vat_ref[:, r0:r0 + n] = jnp.where(first, vt, 1.0).astype(BF16)
    vbt_ref[:, r0:r0 + n] = jnp.where(first, 1.0, vt).astype(BF16)


def _default_key_chunks(n_keys):
    return [(k0, min(ATTN_KEY_CHUNK, n_keys - k0), None) for k0 in range(0, n_keys, ATTN_KEY_CHUNK)]


def _pair_softmax_pv(groups, ka_ref, kb_ref, vat_ref, vbt_ref, emit):
    heads = ((ka_ref, vat_ref), (kb_ref, vbt_ref))
    jobs = [(g, j) for g, tiles in enumerate(groups) for j in range(len(tiles[0][1]))]
    queries = {}

    def scores(job):
        g, j = job
        if g not in queries:
            queries[g] = [load_q() for load_q, _, _ in groups[g]]
        out = []
        for (_, key_chunks, _), qs in zip(groups[g], queries[g]):
            k0, size, bias = key_chunks[j]
            for head, (q, (k_ref, _)) in enumerate(zip(qs, heads)):
                sc = lax.dot_general(k_ref[k0:k0 + size, :], q, _NT, preferred_element_type=F32)
                out.append((sc if bias is None else sc + bias(head)).astype(BF16))
        return out

    pending = [scores(job) for job in jobs[:ATTN_LOOKAHEAD]]
    for n, (g, j) in enumerate(jobs):
        if n + ATTN_LOOKAHEAD < len(jobs):
            pending.append(scores(jobs[n + ATTN_LOOKAHEAD]))
        s = pending.pop(0)
        if j == 0:
            run_max = [None] * len(s)
            acc = [None] * len(s)
        for t, (_, key_chunks, _) in enumerate(groups[g]):
            k0, size, _ = key_chunks[j]
            for head, (_, vt_ref) in enumerate(heads):
                c = 2 * t + head
                mx = s[c].max(axis=0, keepdims=True)
                if j > 0:
                    mx = jnp.maximum(mx, run_max[c])
                p = jnp.exp2(s[c] - mx)
                pv = jnp.dot(vt_ref[:, k0:k0 + size], p, preferred_element_type=F32)
                if j > 0:
                    acc[c] = acc[c] * jnp.exp2(run_max[c].astype(F32) - mx.astype(F32)) + pv
                else:
                    acc[c] = pv
                run_max[c] = mx
        if j == len(groups[g][0][1]) - 1:
            for t, (_, _, tag) in enumerate(groups[g]):
                oa, ob = acc[2 * t], acc[2 * t + 1]
                top = oa[:HEAD_DIM] / oa[HEAD_DIM:HEAD_DIM + 1]
                bot = ob[HEAD_DIM:] / ob[0:1]
                emit(tag, jnp.concatenate([top, bot], axis=0).T)


def _pair_attn_kernel(*refs, use_lat, steps_per_pair, dk, split):
    if use_lat:
        q_ref, ma_ref, mb_ref, kc_ref, vc_ref, kl_ref, vl_ref, o_ref, ka, kb, vat, vbt = refs
        segs = ((kc_ref, vc_ref), (kl_ref, vl_ref))
    else:
        q_ref, ma_ref, mb_ref, kc_ref, vc_ref, o_ref, ka, kb, vat, vbt = refs
        segs = ((kc_ref, vc_ref),)

    @pl.when(jnp.logical_and(pl.program_id(2) == 0, pl.program_id(1) % steps_per_pair == 0))
    def _():
        r0 = 0
        for k_ref, v_ref in segs:
            masks = (None, None) if split else (ma_ref[...], mb_ref[...])
            _fill_pair_kv_t(k_ref[0], v_ref[0], *masks, ka, kb, vat, vbt, r0)
            r0 += k_ref.shape[1]

    sub = min(q_ref.shape[1], ATTN_SUB)
    key_chunks = _default_key_chunks(ka.shape[0])

    def load_q(r0, blk):
        q = q_ref[0, r0:r0 + sub, blk * dk:(blk + 1) * dk]
        return (q[:, :LANES], q[:, LANES:]) if split else (q, q)

    def emit(tag, out):
        r0, blk = tag
        o_ref[0, r0:r0 + sub, blk * LANES:(blk + 1) * LANES] = out.astype(o_ref.dtype)

    tiles = [(functools.partial(load_q, r0, blk), key_chunks, (r0, blk))
             for blk in range(q_ref.shape[2] // dk) for r0 in range(0, q_ref.shape[1], sub)]
    groups = [tiles[i:i + ATTN_TILES_IN_FLIGHT] for i in range(0, len(tiles), ATTN_TILES_IN_FLIGHT)]
    _pair_softmax_pv(groups, ka, kb, vat, vbt, emit)


def _pair_attn(q, mask_a, mask_b, kc, vc, kl, vl, *, dk, n_pairs, q_per_pair, blocks_per_step, tq,
               name):
    b, t, _ = q.shape
    split = dk == 2 * LANES
    dk_head = dk // 2 if split else dk
    assert q_per_pair % blocks_per_step == 0
    steps_per_pair = q_per_pair // blocks_per_step
    n_qblk = n_pairs * steps_per_pair
    use_lat = kl is not None
    qmap = lambda bi, c, i: (bi, i, c)
    kvmap = lambda bi, c, i: (bi, 0, c // steps_per_pair)
    const = lambda bi, c, i: (0, 0)
    in_specs = [pl.BlockSpec((1, tq, blocks_per_step * dk), qmap),
                pl.BlockSpec((1, dk), const),
                pl.BlockSpec((1, dk), const),
                pl.BlockSpec((1, kc.shape[1], dk), kvmap),
                pl.BlockSpec((1, vc.shape[1], LANES), kvmap)]
    args = [q, mask_a, mask_b, kc, vc]
    if use_lat:
        in_specs += [pl.BlockSpec((1, kl.shape[1], dk), kvmap),
                     pl.BlockSpec((1, vl.shape[1], LANES), kvmap)]
        args += [kl, vl]
    n_keys = kc.shape[1] + (kl.shape[1] if use_lat else 0)
    return pl.pallas_call(
        functools.partial(_pair_attn_kernel, use_lat=use_lat, steps_per_pair=steps_per_pair, dk=dk,
                          split=split),
        grid=(b, n_qblk, t // tq),
        in_specs=in_specs,
        out_specs=pl.BlockSpec((1, tq, blocks_per_step * LANES), qmap),
        out_shape=jax.ShapeDtypeStruct((b, t, n_pairs * q_per_pair * LANES), BF16),
        scratch_shapes=[pltpu.VMEM((n_keys, dk_head), BF16), pltpu.VMEM((n_keys, dk_head), BF16),
                        pltpu.VMEM((LANES, n_keys), BF16), pltpu.VMEM((LANES, n_keys), BF16)],
        compiler_params=_params(3),
        name=name,
    )(*args)


def _na_block_tiles(u, n_blocks):
    t0 = min(max(u - 1, 0), n_blocks - NA_KEY_TILES)
    return t0, (0 if u == 0 else 2 if u == n_blocks - 1 else 1)


def _na_kernel(q_ref, kc_ref, vc_ref, k_ref, v_ref, band_ref, o_ref, ka, kb, vat, vbt, bias_ref,
               *, n_rows):
    j_idx, ok = _na_bias_plan(n_rows)

    @pl.when(pl.program_id(1) == 0)
    def _():
        masked = jnp.full((GRID_W, GRID_W), NEG_BIG, F32)
        for head, cls, ch, i, dr in np.ndindex(2, *j_idx.shape):
            half = (dr % 2) * GRID_W
            piece = (band_ref[head, int(j_idx[cls, ch, i, dr]), :, half:half + GRID_W]
                     if ok[cls, ch, i, dr] else masked)
            bias_ref[head, cls, ch, i * GRID_W:(i + 1) * GRID_W, dr * GRID_W:(dr + 1) * GRID_W] = piece

    tc = kc_ref.shape[1]
    lane = _lane((1, LANES))
    mask_a = jnp.where(lane < NA_HEAD_DIM, 1.0, 0.0).astype(BF16)
    mask_b = jnp.where(lane < NA_HEAD_DIM, 0.0, 1.0).astype(BF16)
    _fill_pair_kv_t(kc_ref[0], vc_ref[0], mask_a, mask_b, ka, kb, vat, vbt, 0)
    _fill_pair_kv_t(k_ref[0], v_ref[0], mask_a, mask_b, ka, kb, vat, vbt, tc)

    n_blocks = n_rows // NA_BLOCK_ROWS
    blk_q = NA_BLOCK_ROWS * GRID_W

    def load_q(u):
        q = q_ref[0, u * blk_q:(u + 1) * blk_q, :]
        return q, q

    def emit(u, out):
        o_ref[0, u * blk_q:(u + 1) * blk_q, :] = out.astype(o_ref.dtype)

    by_len = {}
    for u in range(n_blocks):
        t0, cls = _na_block_tiles(u, n_blocks)
        chunks = [(0, tc, None)]
        for ch in range(NA_KEY_TILES):
            if ok[cls, ch].any():
                chunks.append((tc + (t0 + ch) * blk_q, blk_q,
                               functools.partial(lambda head, cls, ch: bias_ref[head, cls, ch],
                                                 cls=cls, ch=ch)))
        by_len.setdefault(len(chunks), []).append((functools.partial(load_q, u), chunks, u))
    groups = [tiles[i:i + NA_TILES_IN_FLIGHT] for tiles in by_len.values()
              for i in range(0, len(tiles), NA_TILES_IN_FLIGHT)]
    _pair_softmax_pv(groups, ka, kb, vat, vbt, emit)


def _na_attn(q, kc, vc, k, v, band):
    b, s, _ = q.shape
    n_rows = s // GRID_W
    n_pairs = NA_HEADS // 2
    n_keys = kc.shape[1] + s
    blk_q = NA_BLOCK_ROWS * GRID_W
    assert kc.shape[1] == ATTN_KEY_CHUNK and blk_q == ATTN_KEY_CHUNK
    blk = lambda rows: pl.BlockSpec((1, rows, LANES), lambda m, bi: (bi, 0, m))
    return pl.pallas_call(
        functools.partial(_na_kernel, n_rows=n_rows),
        grid=(n_pairs, b),
        in_specs=[blk(s), blk(kc.shape[1]), blk(vc.shape[1]), blk(s), blk(s),
                  pl.BlockSpec((2,) + band.shape[1:], lambda m, bi: (m, 0, 0, 0))],
        out_specs=blk(s),
        out_shape=jax.ShapeDtypeStruct((b, s, NA_DIM), BF16),
        scratch_shapes=[pltpu.VMEM((n_keys, LANES), BF16), pltpu.VMEM((n_keys, LANES), BF16),
                        pltpu.VMEM((LANES, n_keys), BF16), pltpu.VMEM((LANES, n_keys), BF16),
                        pltpu.VMEM((2, 3, NA_KEY_TILES, blk_q, blk_q), F32)],
        compiler_params=_params(2),
        name="na_attn",
    )(q, kc, vc, k, v, band)


def _pool_kernel(a_ref, pw_ref, ps_ref, o_ref, pad_ref, *, t_len, chunk):
    pad_ref[0:POOL_HALO, :] = jnp.zeros((POOL_HALO, POOL_DIM), F32)
    pad_ref[POOL_HALO + t_len:2 * POOL_HALO + t_len, :] = jnp.zeros((POOL_HALO, POOL_DIM), F32)
    pad_ref[POOL_HALO:POOL_HALO + t_len, :] = a_ref[0]
    lane = _lane((chunk, POOL_DIM))
    g0 = lane < POOL_CH
    g1 = lane < 2 * POOL_CH
    g2 = lane < 3 * POOL_CH
    half_w = jnp.where(g0, 1, jnp.where(g1, 2, jnp.where(g2, 4, 8)))
    for c in range(t_len // chunk):
        base = c * chunk

        def ld(off, base=base):
            return pad_ref[POOL_HALO + base + off:POOL_HALO + base + off + chunk, :]

        a0 = ld(0)
        w2 = ld(-1) + a0
        w4 = w2 + ld(-2) + ld(1)
        w8 = w4 + ld(-4) + ld(-3) + ld(2) + ld(3)
        w16 = w8
        for off in (-8, -7, -6, -5, 4, 5, 6, 7):
            w16 = w16 + ld(off)
        tok = base + lax.broadcasted_iota(jnp.int32, (chunk, POOL_DIM), 0)
        cnt = (jnp.minimum(tok + half_w, t_len) - jnp.maximum(tok - half_w, 0)).astype(F32)
        wsum = jnp.where(g0, w2, jnp.where(g1, w4, jnp.where(g2, w8, w16)))
        dlt = wsum / cnt - a0
        y = jnp.dot(dlt.astype(BF16), pw_ref[...], preferred_element_type=F32) * ps_ref[...]
        o_ref[0, base:base + chunk, :] = y.astype(o_ref.dtype)


def _pool(a, pw_bd, pscale):
    b, t, _ = a.shape
    chunk = min(t, 256)
    return pl.pallas_call(
        functools.partial(_pool_kernel, t_len=t, chunk=chunk),
        grid=(b,),
        in_specs=[pl.BlockSpec((1, t, POOL_DIM), lambda bi: (bi, 0, 0)),
                  pl.BlockSpec((POOL_DIM, POOL_DIM), lambda bi: (0, 0)),
                  pl.BlockSpec((1, POOL_DIM), lambda bi: (0, 0))],
        out_specs=pl.BlockSpec((1, t, POOL_DIM), lambda bi: (bi, 0, 0)),
        out_shape=jax.ShapeDtypeStruct((b, t, POOL_DIM), BF16),
        scratch_shapes=[pltpu.VMEM((t + 2 * POOL_HALO, POOL_DIM), F32)],
        compiler_params=_params(1),
        name="pool",
    )(a, pw_bd, pscale)


def _post_kernel(x_ref, g1_ref, sh_ref, sc_ref, g2_ref, m0_ref, m1_ref, w0_ref, w1_ref, g_ref,
                 wg_ref, wu_ref, wd_ref, o_ref):
    n_rows = x_ref.shape[1]
    sub = min(n_rows, POST_SUB)
    tiles = [slice(r, r + sub) for r in range(0, n_rows, sub)]
    n_chunks = wg_ref.shape[1] // FFN_CHUNK

    mixes = [jnp.dot(m0_ref[0, rows, :], w0_ref[...], preferred_element_type=F32)
             + jnp.dot(m1_ref[0, rows, :], w1_ref[...], preferred_element_type=F32) for rows in tiles]
    xs = [x_ref[0, rows, :] + g1_ref[0, 0] * mix for rows, mix in zip(tiles, mixes)]
    hs = [_modulate(x, g_ref[...], sh_ref[0, 0], sc_ref[0, 0]).astype(BF16) for x in xs]

    def gate_up(c):
        cols = slice(c * FFN_CHUNK, (c + 1) * FFN_CHUNK)
        return [(jnp.dot(h, wg_ref[:, cols], preferred_element_type=F32),
                 jnp.dot(h, wu_ref[:, cols], preferred_element_type=F32)) for h in hs]

    accs = [None] * len(tiles)
    cur = gate_up(0)
    for c in range(n_chunks):
        nxt = gate_up(c + 1) if c + 1 < n_chunks else None
        for t, (gt, up) in enumerate(cur):
            act = (gt * jax.nn.sigmoid(gt) * up).astype(BF16)
            part = jnp.dot(act, wd_ref[c * FFN_CHUNK:(c + 1) * FFN_CHUNK, :],
                           preferred_element_type=F32)
            accs[t] = part if accs[t] is None else accs[t] + part
        cur = nxt
    for rows, x, acc in zip(tiles, xs, accs):
        o_ref[0, rows, :] = x + g2_ref[0, 0] * acc


def _mod_spec(mods, k, ctx_row):
    blk = (1, 1, 1, mods.shape[-1])
    if ctx_row is None:
        return pl.BlockSpec(blk, lambda bi, i: (bi, k, 0, 0))
    return pl.BlockSpec(blk, lambda bi, i: (ctx_row, k, 0, 0))


def _resident(a):
    nd = a.ndim
    return pl.BlockSpec(a.shape, lambda bi, i: (0,) * nd, pipeline_mode=pl.Buffered(1))


def _post(x, mods, ctx_row, m0, m1, w0, w1, gain, wg, wu, wd, *, tm, name):
    b, t, d = x.shape
    row = lambda bi, i: (bi, i, 0)
    return pl.pallas_call(
        _post_kernel,
        grid=(b, t // tm),
        in_specs=[pl.BlockSpec((1, tm, d), row),
                  _mod_spec(mods, 2, ctx_row), _mod_spec(mods, 3, ctx_row),
                  _mod_spec(mods, 4, ctx_row), _mod_spec(mods, 5, ctx_row),
                  pl.BlockSpec((1, tm, m0.shape[2]), row),
                  pl.BlockSpec((1, tm, m1.shape[2]), row),
                  _resident(w0), _resident(w1), _resident(gain),
                  _resident(wg), _resident(wu), _resident(wd)],
        out_specs=pl.BlockSpec((1, tm, d), row),
        out_shape=jax.ShapeDtypeStruct((b, t, d), F32),
        compiler_params=_params(2),
        name=name,
    )(x, mods, mods, mods, mods, m0, m1, w0, w1, gain, wg, wu, wd)


class _LaneLayout:
    def __init__(self, dim, slot, axis, freq, x2, n_freq):
        self.dim, self.slot, self.axis, self.freq, self.x2, self.n_freq = dim, slot, axis, freq, x2, n_freq

    def rope_tables(self, seq):
        tok = np.arange(seq)
        inv = (ROPE_THETA ** (-np.arange(self.n_freq, dtype=np.float32) * 2.0
                              / (2 * self.n_freq))).astype(np.float32).astype(np.float64)
        pos = np.stack([tok // GRID_W, tok % GRID_W]).astype(np.float64)
        rot = self.axis >= 0
        ang = pos[np.maximum(self.axis, 0)].T * inv[self.freq][None, :]
        cos = np.where(rot[None], np.cos(ang), 1.0)
        sin = np.where(rot[None], np.sin(ang) * np.where(self.x2, 1.0, -1.0)[None], 0.0)
        return jnp.asarray(cos, F32), jnp.asarray(sin, F32)


def _gqa_layout():
    lane = np.arange(LANES)
    x2, slot, axis, freq = lane // 64, (lane % 64) // 32, (lane % 32) // 16, lane % 16
    return _LaneLayout((axis * 2 + x2) * 16 + freq, slot, axis, freq, x2.astype(bool), HEAD_DIM // 4)


def _mla_layout():
    lane = np.arange(LANES)
    is_rot = (lane % 64) < 16
    x2 = lane >= 64
    axis = np.where(is_rot, (lane % 64) // 8, -1)
    freq = np.where(is_rot, lane % 8, 0)
    nope = np.where(lane < 64, lane - 16, 48 + lane - 80)
    dim = np.where(is_rot, MLA_NOPE + (np.maximum(axis, 0) * 2 + x2) * 8 + freq, nope)
    dim = np.where(lane >= MLA_QK, -1, dim)
    return _LaneLayout(dim, np.zeros(LANES, int), axis, freq, x2 & is_rot, MLA_ROPE // 4)


def _take_cols(w, idx):
    return jnp.where(jnp.asarray(idx >= 0)[None, :], w[:, np.maximum(idx, 0)], 0.0)


def _na_bias_plan(n_rows):
    n_blocks = n_rows // NA_BLOCK_ROWS
    j_idx = np.zeros((3, NA_KEY_TILES, NA_BLOCK_ROWS, NA_BLOCK_ROWS), np.int32)
    ok = np.zeros(j_idx.shape, bool)
    for cls, u in enumerate((0, 1, n_blocks - 1)):
        t0, cls_u = _na_block_tiles(u, n_blocks)
        assert cls_u == cls
        for ch in range(NA_KEY_TILES):
            for i in range(NA_BLOCK_ROWS):
                for dr in range(NA_BLOCK_ROWS):
                    r, kr = NA_BLOCK_ROWS * u + dr, NA_BLOCK_ROWS * (t0 + ch) + i
                    r0 = min(max(r - NA_ROWS // 2, 0), n_rows - NA_ROWS)
                    ok[cls, ch, i, dr] = r0 <= kr < r0 + NA_ROWS
                    j_idx[cls, ch, i, dr] = kr - r + NA_ROWS - 1
    return j_idx, ok


def _na_band_table(rpb):
    cols = np.arange(GRID_W)
    c0 = np.clip(cols - NA_COLS // 2, 0, GRID_W - NA_COLS)
    kc = np.arange(GRID_W)
    inside = (kc[None, :] >= c0[:, None]) & (kc[None, :] < c0[:, None] + NA_COLS)
    dc = kc[None, :] - cols[:, None] + (NA_COLS - 1)
    onehot = (np.arange(2 * NA_COLS - 1)[:, None, None] == dc[None]) & inside[None]
    sel = jnp.einsum("hjd,dck->hjkc", rpb * LOG2E, jnp.asarray(onehot, F32),
                     precision=lax.Precision.HIGHEST)
    band = jnp.where(jnp.asarray(inside.T)[None, None], sel, NEG_BIG).astype(F32)
    return jnp.concatenate([band, band], axis=-1)


def kernel(x, c, ctx, c_ctx, l0_ada_w, l0_ada_b, l0_norm_mix, l0_norm_ffn, l0_w_in, l0_pool_w, l0_pool_scale, l0_q_gain, l0_k_gain, l0_w_out, l0_ffn_w_gate, l0_ffn_w_up, l0_ffn_w_down, l1_ada_w, l1_ada_b, l1_norm_mix, l1_norm_ffn, l1_w_in, l1_mla_q_a_gain, l1_mla_kv_a_gain, l1_mla_w_uq, l1_mla_w_ukv, l1_mla_q_gain, l1_mla_k_gain, l1_na_q_gain, l1_na_k_gain, l1_na_rpb, l1_w_out, l1_ffn_w_gate, l1_ffn_w_up, l1_ffn_w_down):
    b, s, d = x.shape
    tc = ctx.shape[1]
    tm_lat = 512
    tp_lat = 1024
    tq_lat = 2048

    cond = jnp.concatenate([c, c_ctx[None, :], jnp.zeros((7, d), F32)], axis=0)

    def mods(ada_w, ada_b):
        return _ada(cond, ada_w, ada_b).reshape(cond.shape[0], 6, 1, d)

    def ffn_weights(wg, wu, wd):
        return wg.astype(BF16), wu.astype(BF16), wd.astype(BF16)

    row1 = lambda v: v.reshape(1, -1)
    tile4 = lambda v, mul: (jnp.tile(v, SLAB // v.shape[0]) * mul).reshape(1, -1)
    sh64 = jnp.asarray(np.kron(np.eye(SLAB // HEAD_DIM), np.ones((HEAD_DIM, HEAD_DIM))), BF16)

    def slab_gain(g, lay, mul):
        per_lane = jnp.where(jnp.asarray(lay.dim >= 0), g[np.maximum(lay.dim, 0)], 0.0) * mul
        return jnp.tile(per_lane, SLAB // LANES).reshape(1, -1)

    def slab_same_head(lay):
        head = np.concatenate([np.where(lay.dim >= 0, blk * 2 + lay.slot, -1)
                               for blk in range(SLAB // LANES)])
        return jnp.asarray((head[:, None] == head[None, :]) & (head[:, None] >= 0), BF16)

    def pair_order(w_q, axis):
        shp = w_q.shape
        split = shp[:axis] + (GQA_KV_HEADS // 2, 2, GQA_GROUP, HEAD_DIM) + shp[axis + 1:]
        perm = list(range(len(split)))
        perm[axis + 1], perm[axis + 2] = perm[axis + 2], perm[axis + 1]
        return w_q.reshape(split).transpose(perm).reshape(shp)

    mods0 = mods(l0_ada_w, l0_ada_b)
    glay = _gqa_layout()
    q_cols = np.concatenate([((2 * m + glay.slot) * GQA_GROUP + g) * HEAD_DIM + glay.dim
                             for m in range(GQA_KV_HEADS // 2) for g in range(GQA_GROUP)])
    k_cols = np.concatenate([(2 * m + glay.slot) * HEAD_DIM + glay.dim
                             for m in range(GQA_KV_HEADS // 2)])
    q0, k0, v0 = POOL_DIM, POOL_DIM + GQA_Q_DIM, POOL_DIM + GQA_Q_DIM + GQA_KV_DIM
    w_in0 = l0_w_in[:, np.concatenate([np.arange(q0), q0 + q_cols, k0 + k_cols,
                                       np.arange(v0, l0_w_in.shape[1])])].astype(BF16)
    cos0, sin0 = glay.rope_tables(s)
    n_ctx = b * tc
    flat = lambda a: a.reshape(1, n_ctx, a.shape[-1])
    unflat = lambda a: a.reshape(b, tc, a.shape[-1])
    ones_c = jnp.ones((n_ctx, LANES), F32)
    zeros_c = jnp.zeros((n_ctx, LANES), F32)
    qg4 = slab_gain(l0_q_gain, glay, HEAD_DIM ** 0.5 * GQA_QSCALE)
    kg4 = slab_gain(l0_k_gain, glay, HEAD_DIM ** 0.5)
    sh_gqa = slab_same_head(glay)
    gmask_a = jnp.asarray((glay.slot == 0)[None], BF16)
    gmask_b = 1 - gmask_a

    a_l, q_l, k_l, v_l = _inproj0(x, mods0, None, row1(l0_norm_mix), w_in0, sh_gqa, qg4, kg4,
                                  cos0, sin0, rope=True, tm=tp_lat)
    a_c, q_c, k_c, v_c = map(unflat, _inproj0(flat(ctx), mods0, b, row1(l0_norm_mix), w_in0, sh_gqa,
                                               qg4, kg4, ones_c, zeros_c, rope=False,
                                               tm=min(n_ctx, tp_lat)))
    n_kv_pairs = GQA_KV_HEADS // 2
    attn_l = _pair_attn(q_l, gmask_a, gmask_b, k_c, v_c, k_l, v_l, dk=LANES, n_pairs=n_kv_pairs,
                        q_per_pair=GQA_GROUP, blocks_per_step=1, tq=tq_lat, name="gqa_lat")
    attn_c = _pair_attn(q_c, gmask_a, gmask_b, k_c, v_c, None, None, dk=LANES, n_pairs=n_kv_pairs,
                        q_per_pair=GQA_GROUP, blocks_per_step=GQA_GROUP, tq=tc, name="gqa_ctx")

    eye = jnp.eye(POOL_GROUPS, dtype=F32)
    pw_bd = (eye[:, None, :, None] * l0_pool_w[:, :, None, :]).reshape(POOL_DIM, POOL_DIM).astype(BF16)
    pool_l = _pool(a_l, pw_bd, row1(l0_pool_scale))
    pool_c = _pool(a_c, pw_bd, row1(l0_pool_scale))

    w_out0_pool = l0_w_out[:POOL_DIM].astype(BF16)
    w_out0_attn = pair_order(l0_w_out[POOL_DIM:], 0).astype(BF16)
    ffn0 = ffn_weights(l0_ffn_w_gate, l0_ffn_w_up, l0_ffn_w_down)

    x1 = _post(x, mods0, None, pool_l, attn_l, w_out0_pool, w_out0_attn, row1(l0_norm_ffn), *ffn0,
               tm=tm_lat, name="post0_lat")
    xc = _post(flat(ctx), mods0, b, flat(pool_c), flat(attn_c), w_out0_pool, w_out0_attn,
               row1(l0_norm_ffn), *ffn0, tm=min(n_ctx, tm_lat), name="post0_ctx")

    mods1 = mods(l1_ada_w, l1_ada_b)
    mlay = _mla_layout()
    heads = np.arange(MLA_HEADS)[:, None]
    o_kr = MLA_Q_LORA + MLA_KV_LORA
    kr_cols = np.where(mlay.dim >= MLA_NOPE, o_kr + mlay.dim - MLA_NOPE, -1)
    w_in1 = _take_cols(l1_w_in, np.concatenate([np.arange(o_kr), kr_cols,
                                                np.arange(o_kr + MLA_ROPE, l1_w_in.shape[1])])
                       ).astype(BF16)
    wuq = _take_cols(l1_mla_w_uq, np.where(mlay.dim >= 0, heads * MLA_QK + mlay.dim, -1).reshape(-1)
                     ).astype(BF16)
    kv_w = MLA_NOPE + MLA_V
    k_idx = np.where((mlay.dim >= 0) & (mlay.dim < MLA_NOPE), heads * kv_w + mlay.dim, -1).reshape(-1)
    v_idx = (heads * kv_w + MLA_NOPE + np.arange(MLA_V)[None, :]).reshape(-1)
    wukv = _take_cols(l1_mla_w_ukv, np.concatenate([k_idx, v_idx])).astype(BF16)

    shmla = slab_same_head(mlay)
    mqg = slab_gain(l1_mla_q_gain, mlay, MLA_QK ** 0.5 * MLA_QSCALE)
    mkg = slab_gain(l1_mla_k_gain, mlay, MLA_QK ** 0.5)
    nqg = tile4(l1_na_q_gain, NA_HEAD_DIM ** 0.5 * NA_QSCALE)
    nkg = tile4(l1_na_k_gain, NA_HEAD_DIM ** 0.5)
    cos1, sin1 = mlay.rope_tables(s)
    common = (row1(l1_norm_mix), w_in1, row1(l1_mla_q_a_gain), row1(l1_mla_kv_a_gain), wuq, wukv,
              sh64, shmla, mqg, mkg, nqg, nkg)
    mq, mk, mv, nq, nk, nv = _inproj1(x1, mods1, None, *common, cos1, sin1, rope=True, want_q=True,
                                      tm=tp_lat)
    mkc, mvc, nkc, nvc = map(unflat, _inproj1(xc, mods1, b, *common, ones_c, zeros_c, rope=False,
                                              want_q=False, tm=min(n_ctx, tp_lat)))

    no_mask = jnp.ones((1, MLA_PAIR_W), BF16)
    o_mla = _pair_attn(mq, no_mask, no_mask, mkc, mvc, mk, mv, dk=MLA_PAIR_W,
                       n_pairs=MLA_HEADS // 2, q_per_pair=1, blocks_per_step=1, tq=tq_lat, name="mla")
    o_na = _na_attn(nq, nkc, nvc, nk, nv, _na_band_table(l1_na_rpb))

    w_out1 = l1_w_out.astype(BF16)
    ffn1 = ffn_weights(l1_ffn_w_gate, l1_ffn_w_up, l1_ffn_w_down)
    return _post(x1, mods1, None, o_mla, o_na, w_out1[:MLA_HEADS * MLA_V], w_out1[MLA_HEADS * MLA_V:],
                 row1(l1_norm_ffn), *ffn1, tm=tm_lat, name="post1_lat")
```

```python
import functools

import numpy as np
import jax
import jax.numpy as jnp
from jax import lax
from jax.experimental import pallas as pl
from jax.experimental.pallas import tpu as pltpu

F32 = jnp.float32
BF16 = jnp.bfloat16

D_MODEL = 1024
GRID_W = 64
ROPE_THETA = 10000.0
EPS = 1e-6
LANES = 128
SLAB = 256

POOL_GROUPS = 4
POOL_CH = 64
POOL_DIM = POOL_GROUPS * POOL_CH
POOL_HALO = 16

HEAD_DIM = 64
GQA_HEADS = 12
GQA_KV_HEADS = 4
GQA_GROUP = GQA_HEADS // GQA_KV_HEADS
GQA_Q_DIM = GQA_HEADS * HEAD_DIM
GQA_KV_DIM = GQA_KV_HEADS * HEAD_DIM

MLA_HEADS = 8
MLA_NOPE = 64
MLA_ROPE = 32
MLA_QK = MLA_NOPE + MLA_ROPE
MLA_V = 64
MLA_Q_LORA = 384
MLA_KV_LORA = 256
MLA_PAIR_W = 2 * LANES

NA_HEADS = 8
NA_HEAD_DIM = 64
NA_DIM = NA_HEADS * NA_HEAD_DIM
NA_ROWS = 8
NA_COLS = 16
NA_BLOCK_ROWS = 4
NA_KEY_TILES = 3

FFN_DIM = -(-8 * D_MODEL // (3 * 256)) * 256
FFN_CHUNK = 256

NEG_BIG = -1e30
LOG2E = 1.4426950408889634
GQA_QSCALE = HEAD_DIM ** -0.5 * LOG2E
MLA_QSCALE = MLA_QK ** -0.5 * LOG2E
NA_QSCALE = NA_HEAD_DIM ** -0.5 * LOG2E
POST_SUB = 256
PROJ_SUB = 256
ATTN_SUB = 256
ATTN_KEY_CHUNK = 256
ATTN_TILES_IN_FLIGHT = 2
NA_TILES_IN_FLIGHT = 4
ATTN_LOOKAHEAD = 1

VMEM_LIMIT = 56 * 1024 * 1024

_NT = (((1,), (1,)), ((), ()))


def _params(n_axes):
    return pltpu.CompilerParams(dimension_semantics=("arbitrary",) * n_axes,
                                vmem_limit_bytes=VMEM_LIMIT)


def _modulate(x, gain, shift, scale):
    ms = jnp.mean(x * x, axis=-1, keepdims=True)
    return x * lax.rsqrt(ms + EPS) * gain * (1.0 + scale) + shift


def _row_rmsnorm(x, gain):
    ms = jnp.mean(x * x, axis=-1, keepdims=True)
    return x * lax.rsqrt(ms + EPS) * gain


def _lane(shape):
    return lax.broadcasted_iota(jnp.int32, shape, 1)


def _head_rmsnorm(slab, same_head, gain, n):
    ssq = jnp.dot((slab * slab).astype(BF16), same_head, preferred_element_type=F32)
    return slab * lax.rsqrt(ssq + n * EPS) * gain


def _rope(blk, cos, sin_signed):
    return blk * cos + pltpu.roll(blk, LANES // 2, 1) * sin_signed


def _ada_kernel(c_ref, w_ref, b_ref, o_ref):
    cnd = c_ref[...]
    act = cnd * jax.nn.sigmoid(cnd)
    o_ref[...] = jnp.dot(act.astype(BF16), w_ref[...].astype(BF16),
                         preferred_element_type=F32) + b_ref[...]


def _ada(cond, w, b):
    n_rows, d = cond.shape
    n = w.shape[1]
    tn = 2048
    return pl.pallas_call(
        _ada_kernel,
        grid=(n // tn,),
        in_specs=[pl.BlockSpec((n_rows, d), lambda j: (0, 0)),
                  pl.BlockSpec((d, tn), lambda j: (0, j)),
                  pl.BlockSpec((1, tn), lambda j: (0, j))],
        out_specs=pl.BlockSpec((n_rows, tn), lambda j: (0, j)),
        out_shape=jax.ShapeDtypeStruct((n_rows, n), F32),
        compiler_params=_params(1),
        name="ada",
    )(cond, w, b.reshape(1, n))


def _slabs(src, c0, width):
    return [src[:, c0 + j * SLAB:c0 + (j + 1) * SLAB] for j in range(width // SLAB)]


def _norm_rope_store(slabs, same_head, gain, n, dst_ref, rows, rope_tables=None):
    for j, slab in enumerate(slabs):
        slab = _head_rmsnorm(slab, same_head, gain, n)
        for blk in range(SLAB // LANES):
            part = slab[:, blk * LANES:(blk + 1) * LANES]
            if rope_tables is not None:
                part = _rope(part, *rope_tables)
            o = j * SLAB + blk * LANES
            dst_ref[0, rows, o:o + LANES] = part.astype(BF16)


def _software_pipeline(n_rows, stages):
    sub = min(n_rows, PROJ_SUB)
    tiles = [slice(r, r + sub) for r in range(0, n_rows, sub)]
    states = [{} for _ in tiles]
    for it in range(len(tiles) + len(stages) - 1):
        for s, stage in enumerate(stages):
            t = it - s
            if stage is not None and 0 <= t < len(tiles):
                stage(tiles[t], states[t])


def _inproj0_kernel(x_ref, sh_ref, sc_ref, g_ref, w_ref, same_ref, qg_ref, kg_ref, cos_ref, sin_ref,
                    a_ref, q_ref, k_ref, v_ref, *, rope):
    q0 = POOL_DIM
    k0 = q0 + GQA_Q_DIM
    v0 = k0 + GQA_KV_DIM

    def project(rows, st):
        h = _modulate(x_ref[0, rows, :], g_ref[...], sh_ref[0, 0], sc_ref[0, 0]).astype(BF16)
        st["p"] = jnp.dot(h, w_ref[...], preferred_element_type=F32)

    def epilogue(rows, st):
        p = st["p"]
        a_ref[0, rows, :] = p[:, :POOL_DIM]
        tables = (cos_ref[rows, :], sin_ref[rows, :]) if rope else None
        for c0, width, gain_ref, dst in ((q0, GQA_Q_DIM, qg_ref, q_ref), (k0, GQA_KV_DIM, kg_ref, k_ref)):
            _norm_rope_store(_slabs(p, c0, width), same_ref[...], gain_ref[...], HEAD_DIM, dst, rows,
                             tables)
        v_ref[0, rows, :] = p[:, v0:v0 + GQA_KV_DIM].astype(BF16)

    _software_pipeline(x_ref.shape[1], [project, None, epilogue])


def _inproj0(x, mods, ctx_row, gain, w, same_head, q_gain, k_gain, cos, sin, *, rope, tm):
    b, t, d = x.shape
    row = lambda bi, i: (bi, i, 0)
    return pl.pallas_call(
        functools.partial(_inproj0_kernel, rope=rope),
        grid=(b, t // tm),
        in_specs=[pl.BlockSpec((1, tm, d), row),
                  _mod_spec(mods, 0, ctx_row), _mod_spec(mods, 1, ctx_row),
                  _resident(gain), _resident(w), _resident(same_head),
                  _resident(q_gain), _resident(k_gain),
                  pl.BlockSpec((tm, LANES), lambda bi, i: (i, 0)),
                  pl.BlockSpec((tm, LANES), lambda bi, i: (i, 0))],
        out_specs=[pl.BlockSpec((1, tm, POOL_DIM), row),
                   pl.BlockSpec((1, tm, GQA_Q_DIM), row),
                   pl.BlockSpec((1, tm, GQA_KV_DIM), row),
                   pl.BlockSpec((1, tm, GQA_KV_DIM), row)],
        out_shape=[jax.ShapeDtypeStruct((b, t, POOL_DIM), F32),
                   jax.ShapeDtypeStruct((b, t, GQA_Q_DIM), BF16),
                   jax.ShapeDtypeStruct((b, t, GQA_KV_DIM), BF16),
                   jax.ShapeDtypeStruct((b, t, GQA_KV_DIM), BF16)],
        compiler_params=_params(2),
        name="inproj0_rope" if rope else "inproj0_ctx",
    )(x, mods, mods, gain, w, same_head, q_gain, k_gain, cos, sin)


def _inproj1_kernel(x_ref, sh_ref, sc_ref, g_ref, w_ref, qag_ref, kvag_ref, wuq_ref, wukv_ref,
                    sh64_ref, shmla_ref, mqg_ref, mkg_ref, nqg_ref, nkg_ref, cos_ref, sin_ref,
                    *out_refs, rope, want_q):
    if want_q:
        mq_ref, mk_ref, mv_ref, nq_ref, nk_ref, nv_ref = out_refs
    else:
        mk_ref, mv_ref, nk_ref, nv_ref = out_refs
    o_ckv = MLA_Q_LORA
    o_kr = o_ckv + MLA_KV_LORA
    o_nq = o_kr + LANES
    o_nk = o_nq + NA_DIM
    o_nv = o_nk + NA_DIM
    mla_w = MLA_HEADS * LANES

    def project(rows, st):
        h = _modulate(x_ref[0, rows, :], g_ref[...], sh_ref[0, 0], sc_ref[0, 0]).astype(BF16)
        st["p"] = jnp.dot(h, w_ref[...], preferred_element_type=F32)

    def up_project(rows, st):
        p = st["p"]
        if want_q:
            st["qq"] = jnp.dot(_row_rmsnorm(p[:, :MLA_Q_LORA], qag_ref[...]).astype(BF16),
                               wuq_ref[...], preferred_element_type=F32)
        st["kv"] = jnp.dot(_row_rmsnorm(p[:, o_ckv:o_ckv + MLA_KV_LORA], kvag_ref[...]).astype(BF16),
                           wukv_ref[...], preferred_element_type=F32)

    def epilogue(rows, st):
        p, kv = st["p"], st["kv"]
        tables = (cos_ref[rows, :], sin_ref[rows, :]) if rope else None
        if want_q:
            _norm_rope_store(_slabs(st["qq"], 0, mla_w), shmla_ref[...], mqg_ref[...], MLA_QK,
                             mq_ref, rows, tables)
            _norm_rope_store(_slabs(p, o_nq, NA_DIM), sh64_ref[...], nqg_ref[...], NA_HEAD_DIM,
                             nq_ref, rows)
        kr_blk = p[:, o_kr:o_kr + LANES]
        kr_slab = jnp.concatenate([kr_blk, kr_blk], axis=1)
        _norm_rope_store([slab + kr_slab for slab in _slabs(kv, 0, mla_w)], shmla_ref[...],
                         mkg_ref[...], MLA_QK, mk_ref, rows, tables)
        mv_ref[0, rows, :] = kv[:, mla_w:mla_w + MLA_HEADS * MLA_V].astype(BF16)
        _norm_rope_store(_slabs(p, o_nk, NA_DIM), sh64_ref[...], nkg_ref[...], NA_HEAD_DIM,
                         nk_ref, rows)
        nv_ref[0, rows, :] = p[:, o_nv:o_nv + NA_DIM].astype(BF16)

    _software_pipeline(x_ref.shape[1], [project, up_project, epilogue])


def _inproj1(x, mods, ctx_row, gain, w, qag, kvag, wuq, wukv, sh64, shmla, mqg, mkg, nqg, nkg,
             cos, sin, *, rope, want_q, tm):
    b, t, d = x.shape
    row = lambda bi, i: (bi, i, 0)
    full = _resident
    mla_w = (MLA_HEADS // 2) * MLA_PAIR_W
    outs = [(mla_w, "mk"), (MLA_HEADS * MLA_V, "mv"), (NA_DIM, "nk"), (NA_DIM, "nv")]
    if want_q:
        outs = [(mla_w, "mq")] + outs[:2] + [(NA_DIM, "nq")] + outs[2:]
    return pl.pallas_call(
        functools.partial(_inproj1_kernel, rope=rope, want_q=want_q),
        grid=(b, t // tm),
        in_specs=[pl.BlockSpec((1, tm, d), row),
                  _mod_spec(mods, 0, ctx_row), _mod_spec(mods, 1, ctx_row),
                  full(gain), full(w), full(qag), full(kvag), full(wuq), full(wukv),
                  full(sh64), full(shmla), full(mqg), full(mkg), full(nqg), full(nkg),
                  pl.BlockSpec((tm, LANES), lambda bi, i: (i, 0)),
                  pl.BlockSpec((tm, LANES), lambda bi, i: (i, 0))],
        out_specs=[pl.BlockSpec((1, tm, n), row) for n, _ in outs],
        out_shape=[jax.ShapeDtypeStruct((b, t, n), BF16) for n, _ in outs],
        compiler_params=_params(2),
        name="inproj1_lat" if want_q else "inproj1_ctx",
    )(x, mods, mods, gain, w, qag, kvag, wuq, wukv, sh64, shmla, mqg, mkg, nqg, nkg, cos, sin)


def _transpose_on_mxu(x):
    n = x.shape[1]
    eye = lax.broadcasted_iota(jnp.int32, (n, n), 0) == lax.broadcasted_iota(jnp.int32, (n, n), 1)
    return lax.dot_general(jnp.where(eye, 1.0, 0.0).astype(BF16), x, _NT, preferred_element_type=F32)


def _fill_pair_kv_t(k, v, mask_a, mask_b, ka_ref, kb_ref, vat_ref, vbt_ref, r0):
    n = k.shape[0]
    if mask_a is None:
        ka_ref[r0:r0 + n, :] = k[:, :LANES]
        kb_ref[r0:r0 + n, :] = k[:, LANES:]
    else:
        ka_ref[r0:r0 + n, :] = k * mask_a
        kb_ref[r0:r0 + n, :] = k * mask_b
    vt = _transpose_on_mxu(v)
    first = lax.broadcasted_iota(jnp.int32, vt.shape, 0) < HEAD_DIM
    vat_ref[:, r0:r0 + n] = jnp.where(first, vt, 1.0).astype(BF16)
    vbt_ref[:, r0:r0 + n] = jnp.where(first, 1.0, vt).astype(BF16)


def _default_key_chunks(n_keys):
    return [(k0, min(ATTN_KEY_CHUNK, n_keys - k0), None) for k0 in range(0, n_keys, ATTN_KEY_CHUNK)]


def _pair_softmax_pv(groups, ka_ref, kb_ref, vat_ref, vbt_ref, emit, out_transpose_on_mxu=False):
    heads = ((ka_ref, vat_ref), (kb_ref, vbt_ref))
    jobs = [(g, j) for g, tiles in enumerate(groups) for j in range(len(tiles[0][1]))]
    queries = {}

    def scores(job):
        g, j = job
        if g not in queries:
            queries[g] = [load_q() for load_q, _, _ in groups[g]]
        out = []
        for (_, key_chunks, _), qs in zip(groups[g], queries[g]):
            k0, size, bias = key_chunks[j]
            for head, (q, (k_ref, _)) in enumerate(zip(qs, heads)):
                sc = lax.dot_general(k_ref[k0:k0 + size, :], q, _NT, preferred_element_type=F32)
                out.append((sc if bias is None else sc + bias(head)).astype(BF16))
        return out

    pending = [scores(job) for job in jobs[:ATTN_LOOKAHEAD]]
    for n, (g, j) in enumerate(jobs):
        if n + ATTN_LOOKAHEAD < len(jobs):
            pending.append(scores(jobs[n + ATTN_LOOKAHEAD]))
        s = pending.pop(0)
        if j == 0:
            run_max = [None] * len(s)
            acc = [None] * len(s)
        for t, (_, key_chunks, _) in enumerate(groups[g]):
            k0, size, _ = key_chunks[j]
            for head, (_, vt_ref) in enumerate(heads):
                c = 2 * t + head
                mx = s[c].max(axis=0, keepdims=True)
                if j > 0:
                    mx = jnp.maximum(mx, run_max[c])
                p = jnp.exp2(s[c] - mx)
                pv = jnp.dot(vt_ref[:, k0:k0 + size], p, preferred_element_type=F32)
                if j > 0:
                    acc[c] = acc[c] * jnp.exp2(run_max[c].astype(F32) - mx.astype(F32)) + pv
                else:
                    acc[c] = pv
                run_max[c] = mx
        if j == len(groups[g][0][1]) - 1:
            for t, (_, _, tag) in enumerate(groups[g]):
                oa, ob = acc[2 * t], acc[2 * t + 1]
                top = oa[:HEAD_DIM] / oa[HEAD_DIM:HEAD_DIM + 1]
                bot = ob[HEAD_DIM:] / ob[0:1]
                out_t = jnp.concatenate([top, bot], axis=0)
                emit(tag, _transpose_on_mxu(out_t.astype(BF16)) if out_transpose_on_mxu else out_t.T)


def _pair_attn_kernel(*refs, use_lat, steps_per_pair, dk, split, out_transpose_on_mxu):
    if use_lat:
        q_ref, ma_ref, mb_ref, kc_ref, vc_ref, kl_ref, vl_ref, o_ref, ka, kb, vat, vbt = refs
        segs = ((kc_ref, vc_ref), (kl_ref, vl_ref))
    else:
        q_ref, ma_ref, mb_ref, kc_ref, vc_ref, o_ref, ka, kb, vat, vbt = refs
        segs = ((kc_ref, vc_ref),)

    @pl.when(jnp.logical_and(pl.program_id(2) == 0, pl.program_id(1) % steps_per_pair == 0))
    def _():
        r0 = 0
        for k_ref, v_ref in segs:
            masks = (None, None) if split else (ma_ref[...], mb_ref[...])
            _fill_pair_kv_t(k_ref[0], v_ref[0], *masks, ka, kb, vat, vbt, r0)
            r0 += k_ref.shape[1]

    sub = min(q_ref.shape[1], ATTN_SUB)
    key_chunks = _default_key_chunks(ka.shape[0])

    def load_q(r0, blk):
        q = q_ref[0, r0:r0 + sub, blk * dk:(blk + 1) * dk]
        return (q[:, :LANES], q[:, LANES:]) if split else (q, q)

    def emit(tag, out):
        r0, blk = tag
        o_ref[0, r0:r0 + sub, blk * LANES:(blk + 1) * LANES] = out.astype(o_ref.dtype)

    tiles = [(functools.partial(load_q, r0, blk), key_chunks, (r0, blk))
             for blk in range(q_ref.shape[2] // dk) for r0 in range(0, q_ref.shape[1], sub)]
    groups = [tiles[i:i + ATTN_TILES_IN_FLIGHT] for i in range(0, len(tiles), ATTN_TILES_IN_FLIGHT)]
    _pair_softmax_pv(groups, ka, kb, vat, vbt, emit, out_transpose_on_mxu)


def _pair_attn(q, mask_a, mask_b, kc, vc, kl, vl, *, dk, n_pairs, q_per_pair, blocks_per_step, tq,
               name, out_transpose_on_mxu=False):
    b, t, _ = q.shape
    split = dk == 2 * LANES
    dk_head = dk // 2 if split else dk
    assert q_per_pair % blocks_per_step == 0
    steps_per_pair = q_per_pair // blocks_per_step
    n_qblk = n_pairs * steps_per_pair
    use_lat = kl is not None
    qmap = lambda bi, c, i: (bi, i, c)
    kvmap = lambda bi, c, i: (bi, 0, c // steps_per_pair)
    const = lambda bi, c, i: (0, 0)
    in_specs = [pl.BlockSpec((1, tq, blocks_per_step * dk), qmap),
                pl.BlockSpec((1, dk), const),
                pl.BlockSpec((1, dk), const),
                pl.BlockSpec((1, kc.shape[1], dk), kvmap),
                pl.BlockSpec((1, vc.shape[1], LANES), kvmap)]
    args = [q, mask_a, mask_b, kc, vc]
    if use_lat:
        in_specs += [pl.BlockSpec((1, kl.shape[1], dk), kvmap),
                     pl.BlockSpec((1, vl.shape[1], LANES), kvmap)]
        args += [kl, vl]
    n_keys = kc.shape[1] + (kl.shape[1] if use_lat else 0)
    return pl.pallas_call(
        functools.partial(_pair_attn_kernel, use_lat=use_lat, steps_per_pair=steps_per_pair, dk=dk,
                          split=split, out_transpose_on_mxu=out_transpose_on_mxu),
        grid=(b, n_qblk, t // tq),
        in_specs=in_specs,
        out_specs=pl.BlockSpec((1, tq, blocks_per_step * LANES), qmap),
        out_shape=jax.ShapeDtypeStruct((b, t, n_pairs * q_per_pair * LANES), BF16),
        scratch_shapes=[pltpu.VMEM((n_keys, dk_head), BF16), pltpu.VMEM((n_keys, dk_head), BF16),
                        pltpu.VMEM((LANES, n_keys), BF16), pltpu.VMEM((LANES, n_keys), BF16)],
        compiler_params=_params(3),
        name=name,
    )(*args)


def _na_block_tiles(u, n_blocks):
    t0 = min(max(u - 1, 0), n_blocks - NA_KEY_TILES)
    return t0, (0 if u == 0 else 2 if u == n_blocks - 1 else 1)


def _na_kernel(q_ref, kc_ref, vc_ref, k_ref, v_ref, band_ref, o_ref, ka, kb, vat, vbt, bias_ref,
               *, n_rows):
    j_idx, ok = _na_bias_plan(n_rows)

    @pl.when(pl.program_id(1) == 0)
    def _():
        masked = jnp.full((GRID_W, GRID_W), NEG_BIG, F32)
        for head, cls, ch, i, dr in np.ndindex(2, *j_idx.shape):
            half = (dr % 2) * GRID_W
            piece = (band_ref[head, int(j_idx[cls, ch, i, dr]), :, half:half + GRID_W]
                     if ok[cls, ch, i, dr] else masked)
            bias_ref[head, cls, ch, i * GRID_W:(i + 1) * GRID_W, dr * GRID_W:(dr + 1) * GRID_W] = piece

    tc = kc_ref.shape[1]
    lane = _lane((1, LANES))
    mask_a = jnp.where(lane < NA_HEAD_DIM, 1.0, 0.0).astype(BF16)
    mask_b = jnp.where(lane < NA_HEAD_DIM, 0.0, 1.0).astype(BF16)
    _fill_pair_kv_t(kc_ref[0], vc_ref[0], mask_a, mask_b, ka, kb, vat, vbt, 0)
    _fill_pair_kv_t(k_ref[0], v_ref[0], mask_a, mask_b, ka, kb, vat, vbt, tc)

    n_blocks = n_rows // NA_BLOCK_ROWS
    blk_q = NA_BLOCK_ROWS * GRID_W

    def load_q(u):
        q = q_ref[0, u * blk_q:(u + 1) * blk_q, :]
        return q, q

    def emit(u, out):
        o_ref[0, u * blk_q:(u + 1) * blk_q, :] = out.astype(o_ref.dtype)

    by_len = {}
    for u in range(n_blocks):
        t0, cls = _na_block_tiles(u, n_blocks)
        chunks = [(0, tc, None)]
        for ch in range(NA_KEY_TILES):
            if ok[cls, ch].any():
                chunks.append((tc + (t0 + ch) * blk_q, blk_q,
                               functools.partial(lambda head, cls, ch: bias_ref[head, cls, ch],
                                                 cls=cls, ch=ch)))
        by_len.setdefault(len(chunks), []).append((functools.partial(load_q, u), chunks, u))
    groups = [tiles[i:i + NA_TILES_IN_FLIGHT] for tiles in by_len.values()
              for i in range(0, len(tiles), NA_TILES_IN_FLIGHT)]
    _pair_softmax_pv(groups, ka, kb, vat, vbt, emit)


def _na_attn(q, kc, vc, k, v, band):
    b, s, _ = q.shape
    n_rows = s // GRID_W
    n_pairs = NA_HEADS // 2
    n_keys = kc.shape[1] + s
    blk_q = NA_BLOCK_ROWS * GRID_W
    assert kc.shape[1] == ATTN_KEY_CHUNK and blk_q == ATTN_KEY_CHUNK
    blk = lambda rows: pl.BlockSpec((1, rows, LANES), lambda m, bi: (bi, 0, m))
    return pl.pallas_call(
        functools.partial(_na_kernel, n_rows=n_rows),
        grid=(n_pairs, b),
        in_specs=[blk(s), blk(kc.shape[1]), blk(vc.shape[1]), blk(s), blk(s),
                  pl.BlockSpec((2,) + band.shape[1:], lambda m, bi: (m, 0, 0, 0))],
        out_specs=blk(s),
        out_shape=jax.ShapeDtypeStruct((b, s, NA_DIM), BF16),
        scratch_shapes=[pltpu.VMEM((n_keys, LANES), BF16), pltpu.VMEM((n_keys, LANES), BF16),
                        pltpu.VMEM((LANES, n_keys), BF16), pltpu.VMEM((LANES, n_keys), BF16),
                        pltpu.VMEM((2, 3, NA_KEY_TILES, blk_q, blk_q), F32)],
        compiler_params=_params(2),
        name="na_attn",
    )(q, kc, vc, k, v, band)


def _pool_kernel(a_ref, pw_ref, ps_ref, o_ref, pad_ref, *, t_len, chunk):
    pad_ref[0:POOL_HALO, :] = jnp.zeros((POOL_HALO, POOL_DIM), F32)
    pad_ref[POOL_HALO + t_len:2 * POOL_HALO + t_len, :] = jnp.zeros((POOL_HALO, POOL_DIM), F32)
    pad_ref[POOL_HALO:POOL_HALO + t_len, :] = a_ref[0]
    lane = _lane((chunk, POOL_DIM))
    g0 = lane < POOL_CH
    g1 = lane < 2 * POOL_CH
    g2 = lane < 3 * POOL_CH
    half_w = jnp.where(g0, 1, jnp.where(g1, 2, jnp.where(g2, 4, 8)))
    for c in range(t_len // chunk):
        base = c * chunk

        def ld(off, base=base):
            return pad_ref[POOL_HALO + base + off:POOL_HALO + base + off + chunk, :]

        a0 = ld(0)
        w2 = ld(-1) + a0
        w4 = w2 + ld(-2) + ld(1)
        w8 = w4 + ld(-4) + ld(-3) + ld(2) + ld(3)
        w16 = w8
        for off in (-8, -7, -6, -5, 4, 5, 6, 7):
            w16 = w16 + ld(off)
        tok = base + lax.broadcasted_iota(jnp.int32, (chunk, POOL_DIM), 0)
        cnt = (jnp.minimum(tok + half_w, t_len) - jnp.maximum(tok - half_w, 0)).astype(F32)
        wsum = jnp.where(g0, w2, jnp.where(g1, w4, jnp.where(g2, w8, w16)))
        dlt = wsum / cnt - a0
        y = jnp.dot(dlt.astype(BF16), pw_ref[...], preferred_element_type=F32) * ps_ref[...]
        o_ref[0, base:base + chunk, :] = y.astype(o_ref.dtype)


def _pool(a, pw_bd, pscale):
    b, t, _ = a.shape
    chunk = min(t, 256)
    return pl.pallas_call(
        functools.partial(_pool_kernel, t_len=t, chunk=chunk),
        grid=(b,),
        in_specs=[pl.BlockSpec((1, t, POOL_DIM), lambda bi: (bi, 0, 0)),
                  pl.BlockSpec((POOL_DIM, POOL_DIM), lambda bi: (0, 0)),
                  pl.BlockSpec((1, POOL_DIM), lambda bi: (0, 0))],
        out_specs=pl.BlockSpec((1, t, POOL_DIM), lambda bi: (bi, 0, 0)),
        out_shape=jax.ShapeDtypeStruct((b, t, POOL_DIM), BF16),
        scratch_shapes=[pltpu.VMEM((t + 2 * POOL_HALO, POOL_DIM), F32)],
        compiler_params=_params(1),
        name="pool",
    )(a, pw_bd, pscale)


def _post_kernel(x_ref, g1_ref, sh_ref, sc_ref, g2_ref, m0_ref, m1_ref, w0_ref, w1_ref, g_ref,
                 wg_ref, wu_ref, wd_ref, o_ref):
    n_rows = x_ref.shape[1]
    sub = min(n_rows, POST_SUB)
    tiles = [slice(r, r + sub) for r in range(0, n_rows, sub)]
    n_chunks = wg_ref.shape[1] // FFN_CHUNK

    mixes = [jnp.dot(m0_ref[0, rows, :], w0_ref[...], preferred_element_type=F32)
             + jnp.dot(m1_ref[0, rows, :], w1_ref[...], preferred_element_type=F32) for rows in tiles]
    xs = [x_ref[0, rows, :] + g1_ref[0, 0] * mix for rows, mix in zip(tiles, mixes)]
    hs = [_modulate(x, g_ref[...], sh_ref[0, 0], sc_ref[0, 0]).astype(BF16) for x in xs]

    def gate_up(c):
        cols = slice(c * FFN_CHUNK, (c + 1) * FFN_CHUNK)
        return [(jnp.dot(h, wg_ref[:, cols], preferred_element_type=F32),
                 jnp.dot(h, wu_ref[:, cols], preferred_element_type=F32)) for h in hs]

    accs = [None] * len(tiles)
    cur = gate_up(0)
    for c in range(n_chunks):
        nxt = gate_up(c + 1) if c + 1 < n_chunks else None
        for t, (gt, up) in enumerate(cur):
            act = (gt * jax.nn.sigmoid(gt) * up).astype(BF16)
            part = jnp.dot(act, wd_ref[c * FFN_CHUNK:(c + 1) * FFN_CHUNK, :],
                           preferred_element_type=F32)
            accs[t] = part if accs[t] is None else accs[t] + part
        cur = nxt
    for rows, x, acc in zip(tiles, xs, accs):
        o_ref[0, rows, :] = x + g2_ref[0, 0] * acc


def _mod_spec(mods, k, ctx_row):
    blk = (1, 1, 1, mods.shape[-1])
    if ctx_row is None:
        return pl.BlockSpec(blk, lambda bi, i: (bi, k, 0, 0))
    return pl.BlockSpec(blk, lambda bi, i: (ctx_row, k, 0, 0))


def _resident(a):
    nd = a.ndim
    return pl.BlockSpec(a.shape, lambda bi, i: (0,) * nd, pipeline_mode=pl.Buffered(1))


def _post(x, mods, ctx_row, m0, m1, w0, w1, gain, wg, wu, wd, *, tm, name):
    b, t, d = x.shape
    row = lambda bi, i: (bi, i, 0)
    return pl.pallas_call(
        _post_kernel,
        grid=(b, t // tm),
        in_specs=[pl.BlockSpec((1, tm, d), row),
                  _mod_spec(mods, 2, ctx_row), _mod_spec(mods, 3, ctx_row),
                  _mod_spec(mods, 4, ctx_row), _mod_spec(mods, 5, ctx_row),
                  pl.BlockSpec((1, tm, m0.shape[2]), row),
                  pl.BlockSpec((1, tm, m1.shape[2]), row),
                  _resident(w0), _resident(w1), _resident(gain),
                  _resident(wg), _resident(wu), _resident(wd)],
        out_specs=pl.BlockSpec((1, tm, d), row),
        out_shape=jax.ShapeDtypeStruct((b, t, d), F32),
        compiler_params=_params(2),
        name=name,
    )(x, mods, mods, mods, mods, m0, m1, w0, w1, gain, wg, wu, wd)


class _LaneLayout:
    def __init__(self, dim, slot, axis, freq, x2, n_freq):
        self.dim, self.slot, self.axis, self.freq, self.x2, self.n_freq = dim, slot, axis, freq, x2, n_freq

    def rope_tables(self, seq):
        tok = np.arange(seq)
        inv = (ROPE_THETA ** (-np.arange(self.n_freq, dtype=np.float32) * 2.0
                              / (2 * self.n_freq))).astype(np.float32).astype(np.float64)
        pos = np.stack([tok // GRID_W, tok % GRID_W]).astype(np.float64)
        rot = self.axis >= 0
        ang = pos[np.maximum(self.axis, 0)].T * inv[self.freq][None, :]
        cos = np.where(rot[None], np.cos(ang), 1.0)
        sin = np.where(rot[None], np.sin(ang) * np.where(self.x2, 1.0, -1.0)[None], 0.0)
        return jnp.asarray(cos, F32), jnp.asarray(sin, F32)


def _gqa_layout():
    lane = np.arange(LANES)
    x2, slot, axis, freq = lane // 64, (lane % 64) // 32, (lane % 32) // 16, lane % 16
    return _LaneLayout((axis * 2 + x2) * 16 + freq, slot, axis, freq, x2.astype(bool), HEAD_DIM // 4)


def _mla_layout():
    lane = np.arange(LANES)
    is_rot = (lane % 64) < 16
    x2 = lane >= 64
    axis = np.where(is_rot, (lane % 64) // 8, -1)
    freq = np.where(is_rot, lane % 8, 0)
    nope = np.where(lane < 64, lane - 16, 48 + lane - 80)
    dim = np.where(is_rot, MLA_NOPE + (np.maximum(axis, 0) * 2 + x2) * 8 + freq, nope)
    dim = np.where(lane >= MLA_QK, -1, dim)
    return _LaneLayout(dim, np.zeros(LANES, int), axis, freq, x2 & is_rot, MLA_ROPE // 4)


def _take_cols(w, idx):
    return jnp.where(jnp.asarray(idx >= 0)[None, :], w[:, np.maximum(idx, 0)], 0.0)


def _na_bias_plan(n_rows):
    n_blocks = n_rows // NA_BLOCK_ROWS
    j_idx = np.zeros((3, NA_KEY_TILES, NA_BLOCK_ROWS, NA_BLOCK_ROWS), np.int32)
    ok = np.zeros(j_idx.shape, bool)
    for cls, u in enumerate((0, 1, n_blocks - 1)):
        t0, cls_u = _na_block_tiles(u, n_blocks)
        assert cls_u == cls
        for ch in range(NA_KEY_TILES):
            for i in range(NA_BLOCK_ROWS):
                for dr in range(NA_BLOCK_ROWS):
                    r, kr = NA_BLOCK_ROWS * u + dr, NA_BLOCK_ROWS * (t0 + ch) + i
                    r0 = min(max(r - NA_ROWS // 2, 0), n_rows - NA_ROWS)
                    ok[cls, ch, i, dr] = r0 <= kr < r0 + NA_ROWS
                    j_idx[cls, ch, i, dr] = kr - r + NA_ROWS - 1
    return j_idx, ok


def _na_band_table(rpb):
    cols = np.arange(GRID_W)
    c0 = np.clip(cols - NA_COLS // 2, 0, GRID_W - NA_COLS)
    kc = np.arange(GRID_W)
    inside = (kc[None, :] >= c0[:, None]) & (kc[None, :] < c0[:, None] + NA_COLS)
    dc = kc[None, :] - cols[:, None] + (NA_COLS - 1)
    onehot = (np.arange(2 * NA_COLS - 1)[:, None, None] == dc[None]) & inside[None]
    sel = jnp.einsum("hjd,dck->hjkc", rpb * LOG2E, jnp.asarray(onehot, F32),
                     precision=lax.Precision.HIGHEST)
    band = jnp.where(jnp.asarray(inside.T)[None, None], sel, NEG_BIG).astype(F32)
    return jnp.concatenate([band, band], axis=-1)


def kernel(x, c, ctx, c_ctx, l0_ada_w, l0_ada_b, l0_norm_mix, l0_norm_ffn, l0_w_in, l0_pool_w, l0_pool_scale, l0_q_gain, l0_k_gain, l0_w_out, l0_ffn_w_gate, l0_ffn_w_up, l0_ffn_w_down, l1_ada_w, l1_ada_b, l1_norm_mix, l1_norm_ffn, l1_w_in, l1_mla_q_a_gain, l1_mla_kv_a_gain, l1_mla_w_uq, l1_mla_w_ukv, l1_mla_q_gain, l1_mla_k_gain, l1_na_q_gain, l1_na_k_gain, l1_na_rpb, l1_w_out, l1_ffn_w_gate, l1_ffn_w_up, l1_ffn_w_down):
    b, s, d = x.shape
    tc = ctx.shape[1]
    tm_lat = 512
    tp_lat = 1024
    tq_lat = 2048

    cond = jnp.concatenate([c, c_ctx[None, :], jnp.zeros((7, d), F32)], axis=0)

    def mods(ada_w, ada_b):
        return _ada(cond, ada_w, ada_b).reshape(cond.shape[0], 6, 1, d)

    def ffn_weights(wg, wu, wd):
        return wg.astype(BF16), wu.astype(BF16), wd.astype(BF16)

    row1 = lambda v: v.reshape(1, -1)
    tile4 = lambda v, mul: (jnp.tile(v, SLAB // v.shape[0]) * mul).reshape(1, -1)
    sh64 = jnp.asarray(np.kron(np.eye(SLAB // HEAD_DIM), np.ones((HEAD_DIM, HEAD_DIM))), BF16)

    def slab_gain(g, lay, mul):
        per_lane = jnp.where(jnp.asarray(lay.dim >= 0), g[np.maximum(lay.dim, 0)], 0.0) * mul
        return jnp.tile(per_lane, SLAB // LANES).reshape(1, -1)

    def slab_same_head(lay):
        head = np.concatenate([np.where(lay.dim >= 0, blk * 2 + lay.slot, -1)
                               for blk in range(SLAB // LANES)])
        return jnp.asarray((head[:, None] == head[None, :]) & (head[:, None] >= 0), BF16)

    def pair_order(w_q, axis):
        shp = w_q.shape
        split = shp[:axis] + (GQA_KV_HEADS // 2, 2, GQA_GROUP, HEAD_DIM) + shp[axis + 1:]
        perm = list(range(len(split)))
        perm[axis + 1], perm[axis + 2] = perm[axis + 2], perm[axis + 1]
        return w_q.reshape(split).transpose(perm).reshape(shp)

    mods0 = mods(l0_ada_w, l0_ada_b)
    glay = _gqa_layout()
    q_cols = np.concatenate([((2 * m + glay.slot) * GQA_GROUP + g) * HEAD_DIM + glay.dim
                             for m in range(GQA_KV_HEADS // 2) for g in range(GQA_GROUP)])
    k_cols = np.concatenate([(2 * m + glay.slot) * HEAD_DIM + glay.dim
                             for m in range(GQA_KV_HEADS // 2)])
    q0, k0, v0 = POOL_DIM, POOL_DIM + GQA_Q_DIM, POOL_DIM + GQA_Q_DIM + GQA_KV_DIM
    w_in0 = l0_w_in[:, np.concatenate([np.arange(q0), q0 + q_cols, k0 + k_cols,
                                       np.arange(v0, l0_w_in.shape[1])])].astype(BF16)
    cos0, sin0 = glay.rope_tables(s)
    n_ctx = b * tc
    flat = lambda a: a.reshape(1, n_ctx, a.shape[-1])
    unflat = lambda a: a.reshape(b, tc, a.shape[-1])
    ones_c = jnp.ones((n_ctx, LANES), F32)
    zeros_c = jnp.zeros((n_ctx, LANES), F32)
    qg4 = slab_gain(l0_q_gain, glay, HEAD_DIM ** 0.5 * GQA_QSCALE)
    kg4 = slab_gain(l0_k_gain, glay, HEAD_DIM ** 0.5)
    sh_gqa = slab_same_head(glay)
    gmask_a = jnp.asarray((glay.slot == 0)[None], BF16)
    gmask_b = 1 - gmask_a

    a_l, q_l, k_l, v_l = _inproj0(x, mods0, None, row1(l0_norm_mix), w_in0, sh_gqa, qg4, kg4,
                                  cos0, sin0, rope=True, tm=tp_lat)
    a_c, q_c, k_c, v_c = map(unflat, _inproj0(flat(ctx), mods0, b, row1(l0_norm_mix), w_in0, sh_gqa,
                                               qg4, kg4, ones_c, zeros_c, rope=False,
                                               tm=min(n_ctx, tp_lat)))
    n_kv_pairs = GQA_KV_HEADS // 2
    attn_l = _pair_attn(q_l, gmask_a, gmask_b, k_c, v_c, k_l, v_l, dk=LANES, n_pairs=n_kv_pairs,
                        q_per_pair=GQA_GROUP, blocks_per_step=1, tq=tq_lat, name="gqa_lat",
                        out_transpose_on_mxu=True)
    attn_c = _pair_attn(q_c, gmask_a, gmask_b, k_c, v_c, None, None, dk=LANES, n_pairs=n_kv_pairs,
                        q_per_pair=GQA_GROUP, blocks_per_step=GQA_GROUP, tq=tc, name="gqa_ctx")

    eye = jnp.eye(POOL_GROUPS, dtype=F32)
    pw_bd = (eye[:, None, :, None] * l0_pool_w[:, :, None, :]).reshape(POOL_DIM, POOL_DIM).astype(BF16)
    pool_l = _pool(a_l, pw_bd, row1(l0_pool_scale))
    pool_c = _pool(a_c, pw_bd, row1(l0_pool_scale))

    w_out0_pool = l0_w_out[:POOL_DIM].astype(BF16)
    w_out0_attn = pair_order(l0_w_out[POOL_DIM:], 0).astype(BF16)
    ffn0 = ffn_weights(l0_ffn_w_gate, l0_ffn_w_up, l0_ffn_w_down)

    x1 = _post(x, mods0, None, pool_l, attn_l, w_out0_pool, w_out0_attn, row1(l0_norm_ffn), *ffn0,
               tm=tm_lat, name="post0_lat")
    xc = _post(flat(ctx), mods0, b, flat(pool_c), flat(attn_c), w_out0_pool, w_out0_attn,
               row1(l0_norm_ffn), *ffn0, tm=min(n_ctx, tm_lat), name="post0_ctx")

    mods1 = mods(l1_ada_w, l1_ada_b)
    mlay = _mla_layout()
    heads = np.arange(MLA_HEADS)[:, None]
    o_kr = MLA_Q_LORA + MLA_KV_LORA
    kr_cols = np.where(mlay.dim >= MLA_NOPE, o_kr + mlay.dim - MLA_NOPE, -1)
    w_in1 = jnp.concatenate([l1_w_in[:, :o_kr].astype(BF16), _take_cols(l1_w_in, kr_cols).astype(BF16),
                             l1_w_in[:, o_kr + MLA_ROPE:].astype(BF16)], axis=1)
    wuq = _take_cols(l1_mla_w_uq, np.where(mlay.dim >= 0, heads * MLA_QK + mlay.dim, -1).reshape(-1)
                     ).astype(BF16)
    kv_w = MLA_NOPE + MLA_V
    k_idx = np.where((mlay.dim >= 0) & (mlay.dim < MLA_NOPE), heads * kv_w + mlay.dim, -1).reshape(-1)
    v_idx = (heads * kv_w + MLA_NOPE + np.arange(MLA_V)[None, :]).reshape(-1)
    wukv = _take_cols(l1_mla_w_ukv, np.concatenate([k_idx, v_idx])).astype(BF16)

    shmla = slab_same_head(mlay)
    mqg = slab_gain(l1_mla_q_gain, mlay, MLA_QK ** 0.5 * MLA_QSCALE)
    mkg = slab_gain(l1_mla_k_gain, mlay, MLA_QK ** 0.5)
    nqg = tile4(l1_na_q_gain, NA_HEAD_DIM ** 0.5 * NA_QSCALE)
    nkg = tile4(l1_na_k_gain, NA_HEAD_DIM ** 0.5)
    cos1, sin1 = mlay.rope_tables(s)
    common = (row1(l1_norm_mix), w_in1, row1(l1_mla_q_a_gain), row1(l1_mla_kv_a_gain), wuq, wukv,
              sh64, shmla, mqg, mkg, nqg, nkg)
    mq, mk, mv, nq, nk, nv = _inproj1(x1, mods1, None, *common, cos1, sin1, rope=True, want_q=True,
                                      tm=tp_lat)
    mkc, mvc, nkc, nvc = map(unflat, _inproj1(xc, mods1, b, *common, ones_c, zeros_c, rope=False,
                                              want_q=False, tm=min(n_ctx, tp_lat)))

    no_mask = jnp.ones((1, MLA_PAIR_W), BF16)
    o_mla = _pair_attn(mq, no_mask, no_mask, mkc, mvc, mk, mv, dk=MLA_PAIR_W,
                       n_pairs=MLA_HEADS // 2, q_per_pair=1, blocks_per_step=1, tq=tq_lat, name="mla")
    o_na = _na_attn(nq, nkc, nvc, nk, nv, _na_band_table(l1_na_rpb))

    w_out1 = l1_w_out.astype(BF16)
    ffn1 = ffn_weights(l1_ffn_w_gate, l1_ffn_w_up, l1_ffn_w_down)
    return _post(x1, mods1, None, o_mla, o_na, w_out1[:MLA_HEADS * MLA_V], w_out1[MLA_HEADS * MLA_V:],
                 row1(l1_norm_ffn), *ffn1, tm=tm_lat, name="post1_lat")
```

```python
import functools

import numpy as np
import jax
import jax.numpy as jnp
from jax import lax
from jax.experimental import pallas as pl
from jax.experimental.pallas import tpu as pltpu

F32 = jnp.float32
BF16 = jnp.bfloat16

D_MODEL = 1024
GRID_W = 64
ROPE_THETA = 10000.0
EPS = 1e-6
LANES = 128
SLAB = 256

POOL_GROUPS = 4
POOL_CH = 64
POOL_DIM = POOL_GROUPS * POOL_CH
POOL_HALO = 16

HEAD_DIM = 64
GQA_HEADS = 12
GQA_KV_HEADS = 4
GQA_GROUP = GQA_HEADS // GQA_KV_HEADS
GQA_Q_DIM = GQA_HEADS * HEAD_DIM
GQA_KV_DIM = GQA_KV_HEADS * HEAD_DIM

MLA_HEADS = 8
MLA_NOPE = 64
MLA_ROPE = 32
MLA_QK = MLA_NOPE + MLA_ROPE
MLA_V = 64
MLA_Q_LORA = 384
MLA_KV_LORA = 256
MLA_PAIR_W = 2 * LANES

NA_HEADS = 8
NA_HEAD_DIM = 64
NA_DIM = NA_HEADS * NA_HEAD_DIM
NA_ROWS = 8
NA_COLS = 16
NA_BLOCK_ROWS = 4
NA_KEY_TILES = 3

FFN_DIM = -(-8 * D_MODEL // (3 * 256)) * 256
FFN_CHUNK = 256

NEG_BIG = -1e30
LOG2E = 1.4426950408889634
GQA_QSCALE = HEAD_DIM ** -0.5 * LOG2E
MLA_QSCALE = MLA_QK ** -0.5 * LOG2E
NA_QSCALE = NA_HEAD_DIM ** -0.5 * LOG2E
POST_SUB = 256
PROJ_SUB = 256
ATTN_SUB = 256
ATTN_KEY_CHUNK = 256
ATTN_TILES_IN_FLIGHT = 2
ATTN_LOOKAHEAD = 1

VMEM_LIMIT = 56 * 1024 * 1024

_NT = (((1,), (1,)), ((), ()))


def _params(n_axes):
    return pltpu.CompilerParams(dimension_semantics=("arbitrary",) * n_axes,
                                vmem_limit_bytes=VMEM_LIMIT)


def _modulate(x, gain, shift, scale):
    ms = jnp.mean(x * x, axis=-1, keepdims=True)
    return x * lax.rsqrt(ms + EPS) * (gain * (1.0 + scale)) + shift


def _row_rmsnorm(x, gain):
    ms = jnp.mean(x * x, axis=-1, keepdims=True)
    return x * lax.rsqrt(ms + EPS) * gain


def _lane(shape):
    return lax.broadcasted_iota(jnp.int32, shape, 1)


def _head_rmsnorm(slab, same_head, gain, n):
    ssq = jnp.dot((slab * slab).astype(BF16), same_head, preferred_element_type=F32)
    return slab * lax.rsqrt(ssq + n * EPS) * gain


def _rope(blk, cos, sin_signed):
    return blk * cos + pltpu.roll(blk, LANES // 2, 1) * sin_signed


def _ada_kernel(c_ref, w_ref, b_ref, o_ref):
    cnd = c_ref[...]
    act = cnd * jax.nn.sigmoid(cnd)
    o_ref[...] = jnp.dot(act.astype(BF16), w_ref[...].astype(BF16),
                         preferred_element_type=F32) + b_ref[...]


def _ada(cond, w, b):
    n_rows, d = cond.shape
    n = w.shape[1]
    tn = 2048
    return pl.pallas_call(
        _ada_kernel,
        grid=(n // tn,),
        in_specs=[pl.BlockSpec((n_rows, d), lambda j: (0, 0)),
                  pl.BlockSpec((d, tn), lambda j: (0, j)),
                  pl.BlockSpec((1, tn), lambda j: (0, j))],
        out_specs=pl.BlockSpec((n_rows, tn), lambda j: (0, j)),
        out_shape=jax.ShapeDtypeStruct((n_rows, n), F32),
        compiler_params=_params(1),
        name="ada",
    )(cond, w, b.reshape(1, n))


def _slabs(src, c0, width):
    return [src[:, c0 + j * SLAB:c0 + (j + 1) * SLAB] for j in range(width // SLAB)]


def _norm_rope_store(slabs, same_head, gain, n, dst_ref, rows, rope_tables=None):
    for j, slab in enumerate(slabs):
        slab = _head_rmsnorm(slab, same_head, gain, n)
        for blk in range(SLAB // LANES):
            part = slab[:, blk * LANES:(blk + 1) * LANES]
            if rope_tables is not None:
                part = _rope(part, *rope_tables)
            o = j * SLAB + blk * LANES
            dst_ref[0, rows, o:o + LANES] = part.astype(BF16)


def _software_pipeline(n_rows, stages):
    sub = min(n_rows, PROJ_SUB)
    tiles = [slice(r, r + sub) for r in range(0, n_rows, sub)]
    states = [{} for _ in tiles]
    for it in range(len(tiles) + len(stages) - 1):
        for s, stage in enumerate(stages):
            t = it - s
            if stage is not None and 0 <= t < len(tiles):
                stage(tiles[t], states[t])


def _inproj0_kernel(x_ref, sh_ref, sc_ref, g_ref, w_ref, same_ref, qg_ref, kg_ref, cos_ref, sin_ref,
                    a_ref, q_ref, k_ref, v_ref, *, rope):
    q0 = POOL_DIM
    k0 = q0 + GQA_Q_DIM
    v0 = k0 + GQA_KV_DIM

    def project(rows, st):
        h = _modulate(x_ref[0, rows, :], g_ref[...], sh_ref[0, 0], sc_ref[0, 0]).astype(BF16)
        st["p"] = jnp.dot(h, w_ref[...], preferred_element_type=F32)

    def epilogue(rows, st):
        p = st["p"]
        a_ref[0, rows, :] = p[:, :POOL_DIM]
        tables = (cos_ref[rows, :], sin_ref[rows, :]) if rope else None
        for c0, width, gain_ref, dst in ((q0, GQA_Q_DIM, qg_ref, q_ref), (k0, GQA_KV_DIM, kg_ref, k_ref)):
            _norm_rope_store(_slabs(p, c0, width), same_ref[...], gain_ref[...], HEAD_DIM, dst, rows,
                             tables)
        v_ref[0, rows, :] = p[:, v0:v0 + GQA_KV_DIM].astype(BF16)

    _software_pipeline(x_ref.shape[1], [project, None, epilogue])


def _inproj0(x, mods, ctx_row, gain, w, same_head, q_gain, k_gain, cos, sin, *, rope, tm):
    b, t, d = x.shape
    row = lambda bi, i: (bi, i, 0)
    return pl.pallas_call(
        functools.partial(_inproj0_kernel, rope=rope),
        grid=(b, t // tm),
        in_specs=[pl.BlockSpec((1, tm, d), row),
                  _mod_spec(mods, 0, ctx_row), _mod_spec(mods, 1, ctx_row),
                  _resident(gain), _resident(w), _resident(same_head),
                  _resident(q_gain), _resident(k_gain),
                  pl.BlockSpec((tm, LANES), lambda bi, i: (i, 0)),
                  pl.BlockSpec((tm, LANES), lambda bi, i: (i, 0))],
        out_specs=[pl.BlockSpec((1, tm, POOL_DIM), row),
                   pl.BlockSpec((1, tm, GQA_Q_DIM), row),
                   pl.BlockSpec((1, tm, GQA_KV_DIM), row),
                   pl.BlockSpec((1, tm, GQA_KV_DIM), row)],
        out_shape=[jax.ShapeDtypeStruct((b, t, POOL_DIM), F32),
                   jax.ShapeDtypeStruct((b, t, GQA_Q_DIM), BF16),
                   jax.ShapeDtypeStruct((b, t, GQA_KV_DIM), BF16),
                   jax.ShapeDtypeStruct((b, t, GQA_KV_DIM), BF16)],
        compiler_params=_params(2),
        name="inproj0_rope" if rope else "inproj0_ctx",
    )(x, mods, mods, gain, w, same_head, q_gain, k_gain, cos, sin)


def _inproj1_kernel(x_ref, sh_ref, sc_ref, g_ref, w_ref, qag_ref, kvag_ref, wuq_ref, wukv_ref,
                    sh64_ref, shmla_ref, mqg_ref, mkg_ref, nqg_ref, nkg_ref, cos_ref, sin_ref,
                    *out_refs, rope, want_q):
    if want_q:
        mq_ref, mk_ref, mv_ref, nq_ref, nk_ref, nv_ref = out_refs
    else:
        mk_ref, mv_ref, nk_ref, nv_ref = out_refs
    o_ckv = MLA_Q_LORA
    o_kr = o_ckv + MLA_KV_LORA
    o_nq = o_kr + LANES
    o_nk = o_nq + NA_DIM
    o_nv = o_nk + NA_DIM
    mla_w = MLA_HEADS * LANES

    def project(rows, st):
        h = _modulate(x_ref[0, rows, :], g_ref[...], sh_ref[0, 0], sc_ref[0, 0]).astype(BF16)
        st["p"] = jnp.dot(h, w_ref[...], preferred_element_type=F32)

    def up_project(rows, st):
        p = st["p"]
        if want_q:
            st["qq"] = jnp.dot(_row_rmsnorm(p[:, :MLA_Q_LORA], qag_ref[...]).astype(BF16),
                               wuq_ref[...], preferred_element_type=F32)
        st["kv"] = jnp.dot(_row_rmsnorm(p[:, o_ckv:o_ckv + MLA_KV_LORA], kvag_ref[...]).astype(BF16),
                           wukv_ref[...], preferred_element_type=F32)

    def epilogue(rows, st):
        p, kv = st["p"], st["kv"]
        tables = (cos_ref[rows, :], sin_ref[rows, :]) if rope else None
        if want_q:
            _norm_rope_store(_slabs(st["qq"], 0, mla_w), shmla_ref[...], mqg_ref[...], MLA_QK,
                             mq_ref, rows, tables)
            _norm_rope_store(_slabs(p, o_nq, NA_DIM), sh64_ref[...], nqg_ref[...], NA_HEAD_DIM,
                             nq_ref, rows)
        kr_blk = p[:, o_kr:o_kr + LANES]
        kr_slab = jnp.concatenate([kr_blk, kr_blk], axis=1)
        _norm_rope_store([slab + kr_slab for slab in _slabs(kv, 0, mla_w)], shmla_ref[...],
                         mkg_ref[...], MLA_QK, mk_ref, rows, tables)
        mv_ref[0, rows, :] = kv[:, mla_w:mla_w + MLA_HEADS * MLA_V].astype(BF16)
        _norm_rope_store(_slabs(p, o_nk, NA_DIM), sh64_ref[...], nkg_ref[...], NA_HEAD_DIM,
                         nk_ref, rows)
        nv_ref[0, rows, :] = p[:, o_nv:o_nv + NA_DIM].astype(BF16)

    _software_pipeline(x_ref.shape[1], [project, up_project, epilogue])


def _inproj1(x, mods, ctx_row, gain, w, qag, kvag, wuq, wukv, sh64, shmla, mqg, mkg, nqg, nkg,
             cos, sin, *, rope, want_q, tm):
    b, t, d = x.shape
    row = lambda bi, i: (bi, i, 0)
    full = _resident
    mla_w = (MLA_HEADS // 2) * MLA_PAIR_W
    outs = [(mla_w, "mk"), (MLA_HEADS * MLA_V, "mv"), (NA_DIM, "nk"), (NA_DIM, "nv")]
    if want_q:
        outs = [(mla_w, "mq")] + outs[:2] + [(NA_DIM, "nq")] + outs[2:]
    return pl.pallas_call(
        functools.partial(_inproj1_kernel, rope=rope, want_q=want_q),
        grid=(b, t // tm),
        in_specs=[pl.BlockSpec((1, tm, d), row),
                  _mod_spec(mods, 0, ctx_row), _mod_spec(mods, 1, ctx_row),
                  full(gain), full(w), full(qag), full(kvag), full(wuq), full(wukv),
                  full(sh64), full(shmla), full(mqg), full(mkg), full(nqg), full(nkg),
                  pl.BlockSpec((tm, LANES), lambda bi, i: (i, 0)),
                  pl.BlockSpec((tm, LANES), lambda bi, i: (i, 0))],
        out_specs=[pl.BlockSpec((1, tm, n), row) for n, _ in outs],
        out_shape=[jax.ShapeDtypeStruct((b, t, n), BF16) for n, _ in outs],
        compiler_params=_params(2),
        name="inproj1_lat" if want_q else "inproj1_ctx",
    )(x, mods, mods, gain, w, qag, kvag, wuq, wukv, sh64, shmla, mqg, mkg, nqg, nkg, cos, sin)


def _transpose_on_mxu(x):
    n = x.shape[1]
    eye = lax.broadcasted_iota(jnp.int32, (n, n), 0) == lax.broadcasted_iota(jnp.int32, (n, n), 1)
    return lax.dot_general(jnp.where(eye, 1.0, 0.0).astype(BF16), x, _NT, preferred_element_type=F32)


def _fill_pair_kv_t(k, v, mask_a, mask_b, ka_ref, kb_ref, vat_ref, vbt_ref, r0):
    n = k.shape[0]
    if mask_a is None:
        ka_ref[r0:r0 + n, :] = k[:, :LANES]
        kb_ref[r0:r0 + n, :] = k[:, LANES:]
    else:
        ka_ref[r0:r0 + n, :] = k * mask_a
        kb_ref[r0:r0 + n, :] = k * mask_b
    vt = _transpose_on_mxu(v)
    first = lax.broadcasted_iota(jnp.int32, vt.shape, 0) < HEAD_DIM
    vat_ref[:, r0:r0 + n] = jnp.where(first, vt, 1.0).astype(BF16)
    vbt_ref[:, r0:r0 + n] = jnp.where(first, 1.0, vt).astype(BF16)


def _default_key_chunks(n_keys):
    return [(k0, min(ATTN_KEY_CHUNK, n_keys - k0), None) for k0 in range(0, n_keys, ATTN_KEY_CHUNK)]


def _pair_softmax_pv(groups, ka_ref, kb_ref, vat_ref, vbt_ref, emit, out_transpose_on_mxu=False):
    heads = ((ka_ref, vat_ref), (kb_ref, vbt_ref))
    jobs = [(g, j) for g, tiles in enumerate(groups) for j in range(len(tiles[0][1]))]
    queries = {}

    def scores(job):
        g, j = job
        if g not in queries:
            queries[g] = [load_q() for load_q, _, _ in groups[g]]
        out = []
        for (_, key_chunks, _), qs in zip(groups[g], queries[g]):
            k0, size, bias = key_chunks[j]
            for head, (q, (k_ref, _)) in enumerate(zip(qs, heads)):
                sc = lax.dot_general(k_ref[k0:k0 + size, :], q, _NT, preferred_element_type=F32)
                out.append((sc if bias is None else sc + bias(head)).astype(BF16))
        return out

    pending = [scores(job) for job in jobs[:ATTN_LOOKAHEAD]]
    for n, (g, j) in enumerate(jobs):
        if n + ATTN_LOOKAHEAD < len(jobs):
            pending.append(scores(jobs[n + ATTN_LOOKAHEAD]))
        s = pending.pop(0)
        if j == 0:
            run_max = [None] * len(s)
            acc = [None] * len(s)
        for t, (_, key_chunks, _) in enumerate(groups[g]):
            k0, size, _ = key_chunks[j]
            for head, (_, vt_ref) in enumerate(heads):
                c = 2 * t + head
                mx = s[c].max(axis=0, keepdims=True)
                if j > 0:
                    mx = jnp.maximum(mx, run_max[c])
                p = jnp.exp2(s[c] - mx)
                pv = jnp.dot(vt_ref[:, k0:k0 + size], p, preferred_element_type=F32)
                if j > 0:
                    acc[c] = acc[c] * jnp.exp2(run_max[c].astype(F32) - mx.astype(F32)) + pv
                else:
                    acc[c] = pv
                run_max[c] = mx
        if j == len(groups[g][0][1]) - 1:
            for t, (_, _, tag) in enumerate(groups[g]):
                oa, ob = acc[2 * t], acc[2 * t + 1]
                top = oa[:HEAD_DIM] / oa[HEAD_DIM:HEAD_DIM + 1]
                bot = ob[HEAD_DIM:] / ob[0:1]
                out_t = jnp.concatenate([top, bot], axis=0)
                emit(tag, _transpose_on_mxu(out_t.astype(BF16)) if out_transpose_on_mxu else out_t.T)


def _pair_attn_kernel(*refs, use_lat, steps_per_pair, dk, split, out_transpose_on_mxu):
    if use_lat:
        q_ref, ma_ref, mb_ref, kc_ref, vc_ref, kl_ref, vl_ref, o_ref, ka, kb, vat, vbt = refs
        segs = ((kc_ref, vc_ref), (kl_ref, vl_ref))
    else:
        q_ref, ma_ref, mb_ref, kc_ref, vc_ref, o_ref, ka, kb, vat, vbt = refs
        segs = ((kc_ref, vc_ref),)

    @pl.when(jnp.logical_and(pl.program_id(2) == 0, pl.program_id(1) % steps_per_pair == 0))
    def _():
        r0 = 0
        for k_ref, v_ref in segs:
            masks = (None, None) if split else (ma_ref[...], mb_ref[...])
            _fill_pair_kv_t(k_ref[0], v_ref[0], *masks, ka, kb, vat, vbt, r0)
            r0 += k_ref.shape[1]

    sub = min(q_ref.shape[1], ATTN_SUB)
    key_chunks = _default_key_chunks(ka.shape[0])

    def load_q(r0, blk):
        q = q_ref[0, r0:r0 + sub, blk * dk:(blk + 1) * dk]
        return (q[:, :LANES], q[:, LANES:]) if split else (q, q)

    def emit(tag, out):
        r0, blk = tag
        o_ref[0, r0:r0 + sub, blk * LANES:(blk + 1) * LANES] = out.astype(o_ref.dtype)

    tiles = [(functools.partial(load_q, r0, blk), key_chunks, (r0, blk))
             for blk in range(q_ref.shape[2] // dk) for r0 in range(0, q_ref.shape[1], sub)]
    groups = [tiles[i:i + ATTN_TILES_IN_FLIGHT] for i in range(0, len(tiles), ATTN_TILES_IN_FLIGHT)]
    _pair_softmax_pv(groups, ka, kb, vat, vbt, emit, out_transpose_on_mxu)


def _pair_attn(q, mask_a, mask_b, kc, vc, kl, vl, *, dk, n_pairs, q_per_pair, blocks_per_step, tq,
               name, out_transpose_on_mxu=False):
    b, t, _ = q.shape
    split = dk == 2 * LANES
    dk_head = dk // 2 if split else dk
    assert q_per_pair % blocks_per_step == 0
    steps_per_pair = q_per_pair // blocks_per_step
    n_qblk = n_pairs * steps_per_pair
    use_lat = kl is not None
    qmap = lambda bi, c, i: (bi, i, c)
    kvmap = lambda bi, c, i: (bi, 0, c // steps_per_pair)
    const = lambda bi, c, i: (0, 0)
    in_specs = [pl.BlockSpec((1, tq, blocks_per_step * dk), qmap),
                pl.BlockSpec((1, dk), const),
                pl.BlockSpec((1, dk), const),
                pl.BlockSpec((1, kc.shape[1], dk), kvmap),
                pl.BlockSpec((1, vc.shape[1], LANES), kvmap)]
    args = [q, mask_a, mask_b, kc, vc]
    if use_lat:
        in_specs += [pl.BlockSpec((1, kl.shape[1], dk), kvmap),
                     pl.BlockSpec((1, vl.shape[1], LANES), kvmap)]
        args += [kl, vl]
    n_keys = kc.shape[1] + (kl.shape[1] if use_lat else 0)
    return pl.pallas_call(
        functools.partial(_pair_attn_kernel, use_lat=use_lat, steps_per_pair=steps_per_pair, dk=dk,
                          split=split, out_transpose_on_mxu=out_transpose_on_mxu),
        grid=(b, n_qblk, t // tq),
        in_specs=in_specs,
        out_specs=pl.BlockSpec((1, tq, blocks_per_step * LANES), qmap),
        out_shape=jax.ShapeDtypeStruct((b, t, n_pairs * q_per_pair * LANES), BF16),
        scratch_shapes=[pltpu.VMEM((n_keys, dk_head), BF16), pltpu.VMEM((n_keys, dk_head), BF16),
                        pltpu.VMEM((LANES, n_keys), BF16), pltpu.VMEM((LANES, n_keys), BF16)],
        compiler_params=_params(3),
        name=name,
    )(*args)


def _na_block_tiles(u, n_blocks):
    t0 = min(max(u - 1, 0), n_blocks - NA_KEY_TILES)
    return t0, (0 if u == 0 else 2 if u == n_blocks - 1 else 1)


def _na_kernel(q_ref, kc_ref, vc_ref, k_ref, v_ref, band_ref, o_ref, ka, kb, vat, vbt, bias_ref,
               *, n_rows):
    j_idx, ok = _na_bias_plan(n_rows)

    @pl.when(pl.program_id(1) == 0)
    def _():
        masked = jnp.full((GRID_W, GRID_W), NEG_BIG, F32)
        for head, cls, ch, i, dr in np.ndindex(2, *j_idx.shape):
            half = (dr % 2) * GRID_W
            piece = (band_ref[head, int(j_idx[cls, ch, i, dr]), :, half:half + GRID_W]
                     if ok[cls, ch, i, dr] else masked)
            bias_ref[head, cls, ch, i * GRID_W:(i + 1) * GRID_W, dr * GRID_W:(dr + 1) * GRID_W] = piece

    tc = kc_ref.shape[1]
    lane = _lane((1, LANES))
    mask_a = jnp.where(lane < NA_HEAD_DIM, 1.0, 0.0).astype(BF16)
    mask_b = jnp.where(lane < NA_HEAD_DIM, 0.0, 1.0).astype(BF16)
    _fill_pair_kv_t(kc_ref[0], vc_ref[0], mask_a, mask_b, ka, kb, vat, vbt, 0)
    _fill_pair_kv_t(k_ref[0], v_ref[0], mask_a, mask_b, ka, kb, vat, vbt, tc)

    n_blocks = n_rows // NA_BLOCK_ROWS
    blk_q = NA_BLOCK_ROWS * GRID_W

    def load_q(u):
        q = q_ref[0, u * blk_q:(u + 1) * blk_q, :]
        return q, q

    def emit(u, out):
        o_ref[0, u * blk_q:(u + 1) * blk_q, :] = out.astype(o_ref.dtype)

    by_len = {}
    for u in range(n_blocks):
        t0, cls = _na_block_tiles(u, n_blocks)
        chunks = [(0, tc, None)]
        for ch in range(NA_KEY_TILES):
            if ok[cls, ch].any():
                chunks.append((tc + (t0 + ch) * blk_q, blk_q,
                               functools.partial(lambda head, cls, ch: bias_ref[head, cls, ch],
                                                 cls=cls, ch=ch)))
        by_len.setdefault(len(chunks), []).append((functools.partial(load_q, u), chunks, u))
    groups = [tiles[i:i + ATTN_TILES_IN_FLIGHT] for tiles in by_len.values()
              for i in range(0, len(tiles), ATTN_TILES_IN_FLIGHT)]
    _pair_softmax_pv(groups, ka, kb, vat, vbt, emit)


def _na_attn(q, kc, vc, k, v, band):
    b, s, _ = q.shape
    n_rows = s // GRID_W
    n_pairs = NA_HEADS // 2
    n_keys = kc.shape[1] + s
    blk_q = NA_BLOCK_ROWS * GRID_W
    assert kc.shape[1] == ATTN_KEY_CHUNK and blk_q == ATTN_KEY_CHUNK
    blk = lambda rows: pl.BlockSpec((1, rows, LANES), lambda m, bi: (bi, 0, m))
    return pl.pallas_call(
        functools.partial(_na_kernel, n_rows=n_rows),
        grid=(n_pairs, b),
        in_specs=[blk(s), blk(kc.shape[1]), blk(vc.shape[1]), blk(s), blk(s),
                  pl.BlockSpec((2,) + band.shape[1:], lambda m, bi: (m, 0, 0, 0))],
        out_specs=blk(s),
        out_shape=jax.ShapeDtypeStruct((b, s, NA_DIM), BF16),
        scratch_shapes=[pltpu.VMEM((n_keys, LANES), BF16), pltpu.VMEM((n_keys, LANES), BF16),
                        pltpu.VMEM((LANES, n_keys), BF16), pltpu.VMEM((LANES, n_keys), BF16),
                        pltpu.VMEM((2, 3, NA_KEY_TILES, blk_q, blk_q), F32)],
        compiler_params=_params(2),
        name="na_attn",
    )(q, kc, vc, k, v, band)


def _pool_kernel(a_ref, pw_ref, ps_ref, o_ref, pad_ref, *, t_len, chunk):
    pad_ref[0:POOL_HALO, :] = jnp.zeros((POOL_HALO, POOL_DIM), F32)
    pad_ref[POOL_HALO + t_len:2 * POOL_HALO + t_len, :] = jnp.zeros((POOL_HALO, POOL_DIM), F32)
    pad_ref[POOL_HALO:POOL_HALO + t_len, :] = a_ref[0]
    lane = _lane((chunk, POOL_DIM))
    g0 = lane < POOL_CH
    g1 = lane < 2 * POOL_CH
    g2 = lane < 3 * POOL_CH
    half_w = jnp.where(g0, 1, jnp.where(g1, 2, jnp.where(g2, 4, 8)))
    for c in range(t_len // chunk):
        base = c * chunk

        def ld(off, base=base):
            return pad_ref[POOL_HALO + base + off:POOL_HALO + base + off + chunk, :]

        a0 = ld(0)
        w2 = ld(-1) + a0
        w4 = w2 + ld(-2) + ld(1)
        w8 = w4 + ld(-4) + ld(-3) + ld(2) + ld(3)
        w16 = w8
        for off in (-8, -7, -6, -5, 4, 5, 6, 7):
            w16 = w16 + ld(off)
        tok = base + lax.broadcasted_iota(jnp.int32, (chunk, POOL_DIM), 0)
        cnt = (jnp.minimum(tok + half_w, t_len) - jnp.maximum(tok - half_w, 0)).astype(F32)
        wsum = jnp.where(g0, w2, jnp.where(g1, w4, jnp.where(g2, w8, w16)))
        dlt = wsum / cnt - a0
        y = jnp.dot(dlt.astype(BF16), pw_ref[...], preferred_element_type=F32) * ps_ref[...]
        o_ref[0, base:base + chunk, :] = y.astype(o_ref.dtype)


def _pool(a, pw_bd, pscale):
    b, t, _ = a.shape
    chunk = min(t, 256)
    return pl.pallas_call(
        functools.partial(_pool_kernel, t_len=t, chunk=chunk),
        grid=(b,),
        in_specs=[pl.BlockSpec((1, t, POOL_DIM), lambda bi: (bi, 0, 0)),
                  pl.BlockSpec((POOL_DIM, POOL_DIM), lambda bi: (0, 0)),
                  pl.BlockSpec((1, POOL_DIM), lambda bi: (0, 0))],
        out_specs=pl.BlockSpec((1, t, POOL_DIM), lambda bi: (bi, 0, 0)),
        out_shape=jax.ShapeDtypeStruct((b, t, POOL_DIM), BF16),
        scratch_shapes=[pltpu.VMEM((t + 2 * POOL_HALO, POOL_DIM), F32)],
        compiler_params=_params(1),
        name="pool",
    )(a, pw_bd, pscale)


def _post_kernel(x_ref, g1_ref, sh_ref, sc_ref, g2_ref, m0_ref, m1_ref, w0_ref, w1_ref, g_ref,
                 wg_ref, wu_ref, wd_ref, o_ref):
    n_rows = x_ref.shape[1]
    sub = min(n_rows, POST_SUB)
    tiles = [slice(r, r + sub) for r in range(0, n_rows, sub)]
    n_chunks = wg_ref.shape[1] // FFN_CHUNK

    mixes = [jnp.dot(m0_ref[0, rows, :], w0_ref[...], preferred_element_type=F32)
             + jnp.dot(m1_ref[0, rows, :], w1_ref[...], preferred_element_type=F32) for rows in tiles]
    xs = [x_ref[0, rows, :] + g1_ref[0, 0] * mix for rows, mix in zip(tiles, mixes)]
    hs = [_modulate(x, g_ref[...], sh_ref[0, 0], sc_ref[0, 0]).astype(BF16) for x in xs]

    def gate_up(c):
        cols = slice(c * FFN_CHUNK, (c + 1) * FFN_CHUNK)
        return [(jnp.dot(h, wg_ref[:, cols], preferred_element_type=F32),
                 jnp.dot(h, wu_ref[:, cols], preferred_element_type=F32)) for h in hs]

    accs = [None] * len(tiles)
    cur = gate_up(0)
    for c in range(n_chunks):
        nxt = gate_up(c + 1) if c + 1 < n_chunks else None
        for t, (gt, up) in enumerate(cur):
            act = (gt * jax.nn.sigmoid(gt) * up).astype(BF16)
            part = jnp.dot(act, wd_ref[c * FFN_CHUNK:(c + 1) * FFN_CHUNK, :],
                           preferred_element_type=F32)
            accs[t] = part if accs[t] is None else accs[t] + part
        cur = nxt
    for rows, x, acc in zip(tiles, xs, accs):
        o_ref[0, rows, :] = x + g2_ref[0, 0] * acc


def _mod_spec(mods, k, ctx_row):
    blk = (1, 1, 1, mods.shape[-1])
    if ctx_row is None:
        return pl.BlockSpec(blk, lambda bi, i: (bi, k, 0, 0))
    return pl.BlockSpec(blk, lambda bi, i: (ctx_row, k, 0, 0))


def _resident(a):
    nd = a.ndim
    return pl.BlockSpec(a.shape, lambda bi, i: (0,) * nd, pipeline_mode=pl.Buffered(1))


def _post(x, mods, ctx_row, m0, m1, w0, w1, gain, wg, wu, wd, *, tm, name):
    b, t, d = x.shape
    row = lambda bi, i: (bi, i, 0)
    return pl.pallas_call(
        _post_kernel,
        grid=(b, t // tm),
        in_specs=[pl.BlockSpec((1, tm, d), row),
                  _mod_spec(mods, 2, ctx_row), _mod_spec(mods, 3, ctx_row),
                  _mod_spec(mods, 4, ctx_row), _mod_spec(mods, 5, ctx_row),
                  pl.BlockSpec((1, tm, m0.shape[2]), row),
                  pl.BlockSpec((1, tm, m1.shape[2]), row),
                  _resident(w0), _resident(w1), _resident(gain),
                  _resident(wg), _resident(wu), _resident(wd)],
        out_specs=pl.BlockSpec((1, tm, d), row),
        out_shape=jax.ShapeDtypeStruct((b, t, d), F32),
        compiler_params=_params(2),
        name=name,
    )(x, mods, mods, mods, mods, m0, m1, w0, w1, gain, wg, wu, wd)


class _LaneLayout:
    def __init__(self, dim, slot, axis, freq, x2, n_freq):
        self.dim, self.slot, self.axis, self.freq, self.x2, self.n_freq = dim, slot, axis, freq, x2, n_freq

    def rope_tables(self, seq):
        tok = np.arange(seq)
        inv = (ROPE_THETA ** (-np.arange(self.n_freq, dtype=np.float32) * 2.0
                              / (2 * self.n_freq))).astype(np.float32).astype(np.float64)
        pos = np.stack([tok // GRID_W, tok % GRID_W]).astype(np.float64)
        rot = self.axis >= 0
        ang = pos[np.maximum(self.axis, 0)].T * inv[self.freq][None, :]
        cos = np.where(rot[None], np.cos(ang), 1.0)
        sin = np.where(rot[None], np.sin(ang) * np.where(self.x2, 1.0, -1.0)[None], 0.0)
        return jnp.asarray(cos, F32), jnp.asarray(sin, F32)


def _gqa_layout():
    lane = np.arange(LANES)
    x2, slot, axis, freq = lane // 64, (lane % 64) // 32, (lane % 32) // 16, lane % 16
    return _LaneLayout((axis * 2 + x2) * 16 + freq, slot, axis, freq, x2.astype(bool), HEAD_DIM // 4)


def _mla_layout():
    lane = np.arange(LANES)
    is_rot = (lane % 64) < 16
    x2 = lane >= 64
    axis = np.where(is_rot, (lane % 64) // 8, -1)
    freq = np.where(is_rot, lane % 8, 0)
    nope = np.where(lane < 64, lane - 16, 48 + lane - 80)
    dim = np.where(is_rot, MLA_NOPE + (np.maximum(axis, 0) * 2 + x2) * 8 + freq, nope)
    dim = np.where(lane >= MLA_QK, -1, dim)
    return _LaneLayout(dim, np.zeros(LANES, int), axis, freq, x2 & is_rot, MLA_ROPE // 4)


def _take_cols(w, idx):
    return jnp.where(jnp.asarray(idx >= 0)[None, :], w[:, np.maximum(idx, 0)], 0.0)


def _na_bias_plan(n_rows):
    n_blocks = n_rows // NA_BLOCK_ROWS
    j_idx = np.zeros((3, NA_KEY_TILES, NA_BLOCK_ROWS, NA_BLOCK_ROWS), np.int32)
    ok = np.zeros(j_idx.shape, bool)
    for cls, u in enumerate((0, 1, n_blocks - 1)):
        t0, cls_u = _na_block_tiles(u, n_blocks)
        assert cls_u == cls
        for ch in range(NA_KEY_TILES):
            for i in range(NA_BLOCK_ROWS):
                for dr in range(NA_BLOCK_ROWS):
                    r, kr = NA_BLOCK_ROWS * u + dr, NA_BLOCK_ROWS * (t0 + ch) + i
                    r0 = min(max(r - NA_ROWS // 2, 0), n_rows - NA_ROWS)
                    ok[cls, ch, i, dr] = r0 <= kr < r0 + NA_ROWS
                    j_idx[cls, ch, i, dr] = kr - r + NA_ROWS - 1
    return j_idx, ok


def _na_band_table(rpb):
    cols = np.arange(GRID_W)
    c0 = np.clip(cols - NA_COLS // 2, 0, GRID_W - NA_COLS)
    kc = np.arange(GRID_W)
    inside = (kc[None, :] >= c0[:, None]) & (kc[None, :] < c0[:, None] + NA_COLS)
    dc = kc[None, :] - cols[:, None] + (NA_COLS - 1)
    onehot = (np.arange(2 * NA_COLS - 1)[:, None, None] == dc[None]) & inside[None]
    sel = jnp.einsum("hjd,dck->hjkc", rpb * LOG2E, jnp.asarray(onehot, F32),
                     precision=lax.Precision.HIGHEST)
    band = jnp.where(jnp.asarray(inside.T)[None, None], sel, NEG_BIG).astype(F32)
    return jnp.concatenate([band, band], axis=-1)


def kernel(x, c, ctx, c_ctx, l0_ada_w, l0_ada_b, l0_norm_mix, l0_norm_ffn, l0_w_in, l0_pool_w, l0_pool_scale, l0_q_gain, l0_k_gain, l0_w_out, l0_ffn_w_gate, l0_ffn_w_up, l0_ffn_w_down, l1_ada_w, l1_ada_b, l1_norm_mix, l1_norm_ffn, l1_w_in, l1_mla_q_a_gain, l1_mla_kv_a_gain, l1_mla_w_uq, l1_mla_w_ukv, l1_mla_q_gain, l1_mla_k_gain, l1_na_q_gain, l1_na_k_gain, l1_na_rpb, l1_w_out, l1_ffn_w_gate, l1_ffn_w_up, l1_ffn_w_down):
    b, s, d = x.shape
    tc = ctx.shape[1]
    tm_lat = 512
    tp_lat = 1024
    tq_lat = 2048

    cond = jnp.concatenate([c, c_ctx[None, :], jnp.zeros((7, d), F32)], axis=0)

    def mods(ada_w, ada_b):
        return _ada(cond, ada_w, ada_b).reshape(cond.shape[0], 6, 1, d)

    def ffn_weights(wg, wu, wd):
        return wg.astype(BF16), wu.astype(BF16), wd.astype(BF16)

    row1 = lambda v: v.reshape(1, -1)
    tile4 = lambda v, mul: (jnp.tile(v, SLAB // v.shape[0]) * mul).reshape(1, -1)
    sh64 = jnp.asarray(np.kron(np.eye(SLAB // HEAD_DIM), np.ones((HEAD_DIM, HEAD_DIM))), BF16)

    def slab_gain(g, lay, mul):
        per_lane = jnp.where(jnp.asarray(lay.dim >= 0), g[np.maximum(lay.dim, 0)], 0.0) * mul
        return jnp.tile(per_lane, SLAB // LANES).reshape(1, -1)

    def slab_same_head(lay):
        head = np.concatenate([np.where(lay.dim >= 0, blk * 2 + lay.slot, -1)
                               for blk in range(SLAB // LANES)])
        return jnp.asarray((head[:, None] == head[None, :]) & (head[:, None] >= 0), BF16)

    def pair_order(w_q, axis):
        shp = w_q.shape
        split = shp[:axis] + (GQA_KV_HEADS // 2, 2, GQA_GROUP, HEAD_DIM) + shp[axis + 1:]
        perm = list(range(len(split)))
        perm[axis + 1], perm[axis + 2] = perm[axis + 2], perm[axis + 1]
        return w_q.reshape(split).transpose(perm).reshape(shp)

    mods0 = mods(l0_ada_w, l0_ada_b)
    glay = _gqa_layout()
    q_cols = np.concatenate([((2 * m + glay.slot) * GQA_GROUP + g) * HEAD_DIM + glay.dim
                             for m in range(GQA_KV_HEADS // 2) for g in range(GQA_GROUP)])
    k_cols = np.concatenate([(2 * m + glay.slot) * HEAD_DIM + glay.dim
                             for m in range(GQA_KV_HEADS // 2)])
    q0, k0, v0 = POOL_DIM, POOL_DIM + GQA_Q_DIM, POOL_DIM + GQA_Q_DIM + GQA_KV_DIM
    w_in0 = l0_w_in[:, np.concatenate([np.arange(q0), q0 + q_cols, k0 + k_cols,
                                       np.arange(v0, l0_w_in.shape[1])])].astype(BF16)
    cos0, sin0 = glay.rope_tables(s)
    n_ctx = b * tc
    flat = lambda a: a.reshape(1, n_ctx, a.shape[-1])
    unflat = lambda a: a.reshape(b, tc, a.shape[-1])
    ones_c = jnp.ones((n_ctx, LANES), F32)
    zeros_c = jnp.zeros((n_ctx, LANES), F32)
    qg4 = slab_gain(l0_q_gain, glay, HEAD_DIM ** 0.5 * GQA_QSCALE)
    kg4 = slab_gain(l0_k_gain, glay, HEAD_DIM ** 0.5)
    sh_gqa = slab_same_head(glay)
    gmask_a = jnp.asarray((glay.slot == 0)[None], BF16)
    gmask_b = 1 - gmask_a

    a_l, q_l, k_l, v_l = _inproj0(x, mods0, None, row1(l0_norm_mix), w_in0, sh_gqa, qg4, kg4,
                                  cos0, sin0, rope=True, tm=tp_lat)
    a_c, q_c, k_c, v_c = map(unflat, _inproj0(flat(ctx), mods0, b, row1(l0_norm_mix), w_in0, sh_gqa,
                                               qg4, kg4, ones_c, zeros_c, rope=False,
                                               tm=min(n_ctx, tp_lat)))
    n_kv_pairs = GQA_KV_HEADS // 2
    attn_l = _pair_attn(q_l, gmask_a, gmask_b, k_c, v_c, k_l, v_l, dk=LANES, n_pairs=n_kv_pairs,
                        q_per_pair=GQA_GROUP, blocks_per_step=1, tq=tq_lat, name="gqa_lat",
                        out_transpose_on_mxu=True)
    attn_c = _pair_attn(q_c, gmask_a, gmask_b, k_c, v_c, None, None, dk=LANES, n_pairs=n_kv_pairs,
                        q_per_pair=GQA_GROUP, blocks_per_step=GQA_GROUP, tq=tc, name="gqa_ctx")

    eye = jnp.eye(POOL_GROUPS, dtype=F32)
    pw_bd = (eye[:, None, :, None] * l0_pool_w[:, :, None, :]).reshape(POOL_DIM, POOL_DIM).astype(BF16)
    pool_l = _pool(a_l, pw_bd, row1(l0_pool_scale))
    pool_c = _pool(a_c, pw_bd, row1(l0_pool_scale))

    w_out0_pool = l0_w_out[:POOL_DIM].astype(BF16)
    w_out0_attn = pair_order(l0_w_out[POOL_DIM:], 0).astype(BF16)
    ffn0 = ffn_weights(l0_ffn_w_gate, l0_ffn_w_up, l0_ffn_w_down)

    x1 = _post(x, mods0, None, pool_l, attn_l, w_out0_pool, w_out0_attn, row1(l0_norm_ffn), *ffn0,
               tm=tm_lat, name="post0_lat")
    xc = _post(flat(ctx), mods0, b, flat(pool_c), flat(attn_c), w_out0_pool, w_out0_attn,
               row1(l0_norm_ffn), *ffn0, tm=min(n_ctx, tm_lat), name="post0_ctx")

    mods1 = mods(l1_ada_w, l1_ada_b)
    mlay = _mla_layout()
    heads = np.arange(MLA_HEADS)[:, None]
    o_kr = MLA_Q_LORA + MLA_KV_LORA
    kr_cols = np.where(mlay.dim >= MLA_NOPE, o_kr + mlay.dim - MLA_NOPE, -1)
    w_in1 = jnp.concatenate([l1_w_in[:, :o_kr].astype(BF16), _take_cols(l1_w_in, kr_cols).astype(BF16),
                             l1_w_in[:, o_kr + MLA_ROPE:].astype(BF16)], axis=1)
    wuq = _take_cols(l1_mla_w_uq, np.where(mlay.dim >= 0, heads * MLA_QK + mlay.dim, -1).reshape(-1)
                     ).astype(BF16)
    kv_w = MLA_NOPE + MLA_V
    k_idx = np.where((mlay.dim >= 0) & (mlay.dim < MLA_NOPE), heads * kv_w + mlay.dim, -1).reshape(-1)
    v_idx = (heads * kv_w + MLA_NOPE + np.arange(MLA_V)[None, :]).reshape(-1)
    wukv = _take_cols(l1_mla_w_ukv, np.concatenate([k_idx, v_idx])).astype(BF16)

    shmla = slab_same_head(mlay)
    mqg = slab_gain(l1_mla_q_gain, mlay, MLA_QK ** 0.5 * MLA_QSCALE)
    mkg = slab_gain(l1_mla_k_gain, mlay, MLA_QK ** 0.5)
    nqg = tile4(l1_na_q_gain, NA_HEAD_DIM ** 0.5 * NA_QSCALE)
    nkg = tile4(l1_na_k_gain, NA_HEAD_DIM ** 0.5)
    cos1, sin1 = mlay.rope_tables(s)
    common = (row1(l1_norm_mix), w_in1, row1(l1_mla_q_a_gain), row1(l1_mla_kv_a_gain), wuq, wukv,
              sh64, shmla, mqg, mkg, nqg, nkg)
    mq, mk, mv, nq, nk, nv = _inproj1(x1, mods1, None, *common, cos1, sin1, rope=True, want_q=True,
                                      tm=tp_lat)
    mkc, mvc, nkc, nvc = map(unflat, _inproj1(xc, mods1, b, *common, ones_c, zeros_c, rope=False,
                                              want_q=False, tm=min(n_ctx, tp_lat)))

    no_mask = jnp.ones((1, MLA_PAIR_W), BF16)
    o_mla = _pair_attn(mq, no_mask, no_mask, mkc, mvc, mk, mv, dk=MLA_PAIR_W,
                       n_pairs=MLA_HEADS // 2, q_per_pair=1, blocks_per_step=1, tq=tq_lat, name="mla")
    o_na = _na_attn(nq, nkc, nvc, nk, nv, _na_band_table(l1_na_rpb))

    w_out1 = l1_w_out.astype(BF16)
    ffn1 = ffn_weights(l1_ffn_w_gate, l1_ffn_w_up, l1_ffn_w_down)
    return _post(x1, mods1, None, o_mla, o_na, w_out1[:MLA_HEADS * MLA_V], w_out1[MLA_HEADS * MLA_V:],
                 row1(l1_norm_ffn), *ffn1, tm=tm_lat, name="post1_lat")
```

```python
import functools

import numpy as np
import jax
import jax.numpy as jnp
from jax import lax
from jax.experimental import pallas as pl
from jax.experimental.pallas import tpu as pltpu

F32 = jnp.float32
BF16 = jnp.bfloat16

D_MODEL = 1024
GRID_W = 64
ROPE_THETA = 10000.0
EPS = 1e-6
LANES = 128
SLAB = 256

POOL_GROUPS = 4
POOL_CH = 64
POOL_DIM = POOL_GROUPS * POOL_CH
POOL_HALO = 16

HEAD_DIM = 64
GQA_HEADS = 12
GQA_KV_HEADS = 4
GQA_GROUP = GQA_HEADS // GQA_KV_HEADS
GQA_Q_DIM = GQA_HEADS * HEAD_DIM
GQA_KV_DIM = GQA_KV_HEADS * HEAD_DIM

MLA_HEADS = 8
MLA_NOPE = 64
MLA_ROPE = 32
MLA_QK = MLA_NOPE + MLA_ROPE
MLA_V = 64
MLA_Q_LORA = 384
MLA_KV_LORA = 256
MLA_PAIR_W = 2 * LANES

NA_HEADS = 8
NA_HEAD_DIM = 64
NA_DIM = NA_HEADS * NA_HEAD_DIM
NA_ROWS = 8
NA_COLS = 16
NA_BLOCK_ROWS = 4
NA_KEY_TILES = 3

FFN_DIM = -(-8 * D_MODEL // (3 * 256)) * 256
FFN_CHUNK = 256

NEG_BIG = -1e30
LOG2E = 1.4426950408889634
GQA_QSCALE = HEAD_DIM ** -0.5 * LOG2E
MLA_QSCALE = MLA_QK ** -0.5 * LOG2E
NA_QSCALE = NA_HEAD_DIM ** -0.5 * LOG2E
POST_SUB = 256
PROJ_SUB = 256
ATTN_SUB = 256
ATTN_KEY_CHUNK = 256
ATTN_TILES_IN_FLIGHT = 2
ATTN_LOOKAHEAD = 1

VMEM_LIMIT = 56 * 1024 * 1024

_NT = (((1,), (1,)), ((), ()))


def _params(n_axes):
    return pltpu.CompilerParams(dimension_semantics=("arbitrary",) * n_axes,
                                vmem_limit_bytes=VMEM_LIMIT)


def _modulate(x, gain, shift, scale):
    ms = jnp.mean(x * x, axis=-1, keepdims=True)
    return x * lax.rsqrt(ms + EPS) * (gain * (1.0 + scale)) + shift


def _row_rmsnorm(x, gain):
    ms = jnp.mean(x * x, axis=-1, keepdims=True)
    return x * lax.rsqrt(ms + EPS) * gain


def _lane(shape):
    return lax.broadcasted_iota(jnp.int32, shape, 1)


def _head_rmsnorm(slab, same_head, gain, n):
    ssq = jnp.dot((slab * slab).astype(BF16), same_head, preferred_element_type=F32)
    return slab * lax.rsqrt(ssq + n * EPS) * gain


def _rope(blk, cos, sin_signed):
    return blk * cos + pltpu.roll(blk, LANES // 2, 1) * sin_signed


def _ada_kernel(c_ref, w_ref, b_ref, o_ref):
    cnd = c_ref[...]
    act = cnd * jax.nn.sigmoid(cnd)
    o_ref[...] = jnp.dot(act.astype(BF16), w_ref[...].astype(BF16),
                         preferred_element_type=F32) + b_ref[...]


def _ada(cond, w, b):
    n_rows, d = cond.shape
    n = w.shape[1]
    tn = 2048
    return pl.pallas_call(
        _ada_kernel,
        grid=(n // tn,),
        in_specs=[pl.BlockSpec((n_rows, d), lambda j: (0, 0)),
                  pl.BlockSpec((d, tn), lambda j: (0, j)),
                  pl.BlockSpec((1, tn), lambda j: (0, j))],
        out_specs=pl.BlockSpec((n_rows, tn), lambda j: (0, j)),
        out_shape=jax.ShapeDtypeStruct((n_rows, n), F32),
        compiler_params=_params(1),
        name="ada",
    )(cond, w, b.reshape(1, n))


def _slabs(src, c0, width):
    return [src[:, c0 + j * SLAB:c0 + (j + 1) * SLAB] for j in range(width // SLAB)]


def _norm_rope_store(slabs, same_head, gain, n, dst_ref, rows, rope_tables=None):
    for j, slab in enumerate(slabs):
        slab = _head_rmsnorm(slab, same_head, gain, n)
        for blk in range(SLAB // LANES):
            part = slab[:, blk * LANES:(blk + 1) * LANES]
            if rope_tables is not None:
                part = _rope(part, *rope_tables)
            o = j * SLAB + blk * LANES
            dst_ref[0, rows, o:o + LANES] = part.astype(BF16)


def _software_pipeline(n_rows, stages):
    sub = min(n_rows, PROJ_SUB)
    tiles = [slice(r, r + sub) for r in range(0, n_rows, sub)]
    states = [{} for _ in tiles]
    for it in range(len(tiles) + len(stages) - 1):
        for s, stage in enumerate(stages):
            t = it - s
            if 0 <= t < len(tiles):
                stage(tiles[t], states[t])


def _inproj0_kernel(x_ref, sh_ref, sc_ref, g_ref, w_ref, same_ref, qg_ref, kg_ref, cos_ref, sin_ref,
                    a_ref, q_ref, k_ref, v_ref, *, rope):
    q0 = POOL_DIM
    k0 = q0 + GQA_Q_DIM
    v0 = k0 + GQA_KV_DIM

    def project(rows, st):
        h = _modulate(x_ref[0, rows, :], g_ref[...], sh_ref[0, 0], sc_ref[0, 0]).astype(BF16)
        st["p"] = jnp.dot(h, w_ref[...], preferred_element_type=F32)

    def epilogue(rows, st):
        p = st["p"]
        a_ref[0, rows, :] = p[:, :POOL_DIM]
        tables = (cos_ref[rows, :], sin_ref[rows, :]) if rope else None
        for c0, width, gain_ref, dst in ((q0, GQA_Q_DIM, qg_ref, q_ref), (k0, GQA_KV_DIM, kg_ref, k_ref)):
            _norm_rope_store(_slabs(p, c0, width), same_ref[...], gain_ref[...], HEAD_DIM, dst, rows,
                             tables)
        v_ref[0, rows, :] = p[:, v0:v0 + GQA_KV_DIM].astype(BF16)

    _software_pipeline(x_ref.shape[1], [project, epilogue])


def _inproj0(x, mods, ctx_row, gain, w, same_head, q_gain, k_gain, cos, sin, *, rope, tm):
    b, t, d = x.shape
    row = lambda bi, i: (bi, i, 0)
    return pl.pallas_call(
        functools.partial(_inproj0_kernel, rope=rope),
        grid=(b, t // tm),
        in_specs=[pl.BlockSpec((1, tm, d), row),
                  _mod_spec(mods, 0, ctx_row), _mod_spec(mods, 1, ctx_row),
                  _resident(gain), _resident(w), _resident(same_head),
                  _resident(q_gain), _resident(k_gain),
                  pl.BlockSpec((tm, LANES), lambda bi, i: (i, 0)),
                  pl.BlockSpec((tm, LANES), lambda bi, i: (i, 0))],
        out_specs=[pl.BlockSpec((1, tm, POOL_DIM), row),
                   pl.BlockSpec((1, tm, GQA_Q_DIM), row),
                   pl.BlockSpec((1, tm, GQA_KV_DIM), row),
                   pl.BlockSpec((1, tm, GQA_KV_DIM), row)],
        out_shape=[jax.ShapeDtypeStruct((b, t, POOL_DIM), F32),
                   jax.ShapeDtypeStruct((b, t, GQA_Q_DIM), BF16),
                   jax.ShapeDtypeStruct((b, t, GQA_KV_DIM), BF16),
                   jax.ShapeDtypeStruct((b, t, GQA_KV_DIM), BF16)],
        compiler_params=_params(2),
        name="inproj0_rope" if rope else "inproj0_ctx",
    )(x, mods, mods, gain, w, same_head, q_gain, k_gain, cos, sin)


def _inproj1_kernel(x_ref, sh_ref, sc_ref, g_ref, w_ref, qag_ref, kvag_ref, wuq_ref, wukv_ref,
                    sh64_ref, shmla_ref, mqg_ref, mkg_ref, nqg_ref, nkg_ref, cos_ref, sin_ref,
                    *out_refs, rope, want_q):
    if want_q:
        mq_ref, mk_ref, mv_ref, nq_ref, nk_ref, nv_ref = out_refs
    else:
        mk_ref, mv_ref, nk_ref, nv_ref = out_refs
    o_ckv = MLA_Q_LORA
    o_kr = o_ckv + MLA_KV_LORA
    o_nq = o_kr + LANES
    o_nk = o_nq + NA_DIM
    o_nv = o_nk + NA_DIM
    mla_w = MLA_HEADS * LANES

    def project(rows, st):
        h = _modulate(x_ref[0, rows, :], g_ref[...], sh_ref[0, 0], sc_ref[0, 0]).astype(BF16)
        st["p"] = jnp.dot(h, w_ref[...], preferred_element_type=F32)

    def up_project(rows, st):
        p = st["p"]
        if want_q:
            st["qq"] = jnp.dot(_row_rmsnorm(p[:, :MLA_Q_LORA], qag_ref[...]).astype(BF16),
                               wuq_ref[...], preferred_element_type=F32)
        st["kv"] = jnp.dot(_row_rmsnorm(p[:, o_ckv:o_ckv + MLA_KV_LORA], kvag_ref[...]).astype(BF16),
                           wukv_ref[...], preferred_element_type=F32)

    def epilogue(rows, st):
        p, kv = st["p"], st["kv"]
        tables = (cos_ref[rows, :], sin_ref[rows, :]) if rope else None
        if want_q:
            _norm_rope_store(_slabs(st["qq"], 0, mla_w), shmla_ref[...], mqg_ref[...], MLA_QK,
                             mq_ref, rows, tables)
            _norm_rope_store(_slabs(p, o_nq, NA_DIM), sh64_ref[...], nqg_ref[...], NA_HEAD_DIM,
                             nq_ref, rows)
        kr_blk = p[:, o_kr:o_kr + LANES]
        kr_slab = jnp.concatenate([kr_blk, kr_blk], axis=1)
        _norm_rope_store([slab + kr_slab for slab in _slabs(kv, 0, mla_w)], shmla_ref[...],
                         mkg_ref[...], MLA_QK, mk_ref, rows, tables)
        mv_ref[0, rows, :] = kv[:, mla_w:mla_w + MLA_HEADS * MLA_V].astype(BF16)
        _norm_rope_store(_slabs(p, o_nk, NA_DIM), sh64_ref[...], nkg_ref[...], NA_HEAD_DIM,
                         nk_ref, rows)
        nv_ref[0, rows, :] = p[:, o_nv:o_nv + NA_DIM].astype(BF16)

    _software_pipeline(x_ref.shape[1], [project, up_project, epilogue])


def _inproj1(x, mods, ctx_row, gain, w, qag, kvag, wuq, wukv, sh64, shmla, mqg, mkg, nqg, nkg,
             cos, sin, *, rope, want_q, tm):
    b, t, d = x.shape
    row = lambda bi, i: (bi, i, 0)
    full = _resident
    mla_w = (MLA_HEADS // 2) * MLA_PAIR_W
    outs = [(mla_w, "mk"), (MLA_HEADS * MLA_V, "mv"), (NA_DIM, "nk"), (NA_DIM, "nv")]
    if want_q:
        outs = [(mla_w, "mq")] + outs[:2] + [(NA_DIM, "nq")] + outs[2:]
    return pl.pallas_call(
        functools.partial(_inproj1_kernel, rope=rope, want_q=want_q),
        grid=(b, t // tm),
        in_specs=[pl.BlockSpec((1, tm, d), row),
                  _mod_spec(mods, 0, ctx_row), _mod_spec(mods, 1, ctx_row),
                  full(gain), full(w), full(qag), full(kvag), full(wuq), full(wukv),
                  full(sh64), full(shmla), full(mqg), full(mkg), full(nqg), full(nkg),
                  pl.BlockSpec((tm, LANES), lambda bi, i: (i, 0)),
                  pl.BlockSpec((tm, LANES), lambda bi, i: (i, 0))],
        out_specs=[pl.BlockSpec((1, tm, n), row) for n, _ in outs],
        out_shape=[jax.ShapeDtypeStruct((b, t, n), BF16) for n, _ in outs],
        compiler_params=_params(2),
        name="inproj1_lat" if want_q else "inproj1_ctx",
    )(x, mods, mods, gain, w, qag, kvag, wuq, wukv, sh64, shmla, mqg, mkg, nqg, nkg, cos, sin)


def _transpose_on_mxu(x):
    n = x.shape[1]
    eye = lax.broadcasted_iota(jnp.int32, (n, n), 0) == lax.broadcasted_iota(jnp.int32, (n, n), 1)
    return lax.dot_general(jnp.where(eye, 1.0, 0.0).astype(BF16), x, _NT, preferred_element_type=F32)


def _fill_pair_kv_t(k, v, mask_a, mask_b, ka_ref, kb_ref, vat_ref, vbt_ref, r0):
    n = k.shape[0]
    if mask_a is None:
        ka_ref[r0:r0 + n, :] = k[:, :LANES]
        kb_ref[r0:r0 + n, :] = k[:, LANES:]
    else:
        ka_ref[r0:r0 + n, :] = k * mask_a
        kb_ref[r0:r0 + n, :] = k * mask_b
    vt = _transpose_on_mxu(v)
    first = lax.broadcasted_iota(jnp.int32, vt.shape, 0) < HEAD_DIM
    vat_ref[:, r0:r0 + n] = jnp.where(first, vt, 1.0).astype(BF16)
    vbt_ref[:, r0:r0 + n] = jnp.where(first, 1.0, vt).astype(BF16)


def _default_key_chunks(n_keys):
    return [(k0, min(ATTN_KEY_CHUNK, n_keys - k0), None) for k0 in range(0, n_keys, ATTN_KEY_CHUNK)]


def _pair_softmax_pv(groups, ka_ref, kb_ref, vat_ref, vbt_ref, emit):
    heads = ((ka_ref, vat_ref), (kb_ref, vbt_ref))
    jobs = [(g, j) for g, tiles in enumerate(groups) for j in range(len(tiles[0][1]))]
    queries = {}

    def scores(job):
        g, j = job
        if g not in queries:
            queries[g] = [load_q() for load_q, _, _ in groups[g]]
        out = []
        for (_, key_chunks, _), qs in zip(groups[g], queries[g]):
            k0, size, bias = key_chunks[j]
            for head, (q, (k_ref, _)) in enumerate(zip(qs, heads)):
                sc = lax.dot_general(k_ref[k0:k0 + size, :], q, _NT, preferred_element_type=F32)
                out.append((sc if bias is None else sc + bias(head)).astype(BF16))
        return out

    pending = [scores(job) for job in jobs[:ATTN_LOOKAHEAD]]
    for n, (g, j) in enumerate(jobs):
        if n + ATTN_LOOKAHEAD < len(jobs):
            pending.append(scores(jobs[n + ATTN_LOOKAHEAD]))
        s = pending.pop(0)
        if j == 0:
            run_max = [None] * len(s)
            acc = [None] * len(s)
        for t, (_, key_chunks, _) in enumerate(groups[g]):
            k0, size, _ = key_chunks[j]
            for head, (_, vt_ref) in enumerate(heads):
                c = 2 * t + head
                mx = s[c].max(axis=0, keepdims=True)
                if j > 0:
                    mx = jnp.maximum(mx, run_max[c])
                p = jnp.exp2(s[c] - mx)
                pv = jnp.dot(vt_ref[:, k0:k0 + size], p, preferred_element_type=F32)
                if j > 0:
                    acc[c] = acc[c] * jnp.exp2(run_max[c].astype(F32) - mx.astype(F32)) + pv
                else:
                    acc[c] = pv
                run_max[c] = mx
        if j == len(groups[g][0][1]) - 1:
            for t, (_, _, tag) in enumerate(groups[g]):
                oa, ob = acc[2 * t], acc[2 * t + 1]
                top = oa[:HEAD_DIM] / oa[HEAD_DIM:HEAD_DIM + 1]
                bot = ob[HEAD_DIM:] / ob[0:1]
                emit(tag, jnp.concatenate([top, bot], axis=0).T)


def _pair_attn_kernel(*refs, use_lat, steps_per_pair, dk, split):
    if use_lat:
        q_ref, ma_ref, mb_ref, kc_ref, vc_ref, kl_ref, vl_ref, o_ref, ka, kb, vat, vbt = refs
        segs = ((kc_ref, vc_ref), (kl_ref, vl_ref))
    else:
        q_ref, ma_ref, mb_ref, kc_ref, vc_ref, o_ref, ka, kb, vat, vbt = refs
        segs = ((kc_ref, vc_ref),)

    @pl.when(jnp.logical_and(pl.program_id(2) == 0, pl.program_id(1) % steps_per_pair == 0))
    def _():
        r0 = 0
        for k_ref, v_ref in segs:
            masks = (None, None) if split else (ma_ref[...], mb_ref[...])
            _fill_pair_kv_t(k_ref[0], v_ref[0], *masks, ka, kb, vat, vbt, r0)
            r0 += k_ref.shape[1]

    sub = min(q_ref.shape[1], ATTN_SUB)
    key_chunks = _default_key_chunks(ka.shape[0])

    def load_q(r0, blk):
        q = q_ref[0, r0:r0 + sub, blk * dk:(blk + 1) * dk]
        return (q[:, :LANES], q[:, LANES:]) if split else (q, q)

    def emit(tag, out):
        r0, blk = tag
        o_ref[0, r0:r0 + sub, blk * LANES:(blk + 1) * LANES] = out.astype(o_ref.dtype)

    tiles = [(functools.partial(load_q, r0, blk), key_chunks, (r0, blk))
             for blk in range(q_ref.shape[2] // dk) for r0 in range(0, q_ref.shape[1], sub)]
    groups = [tiles[i:i + ATTN_TILES_IN_FLIGHT] for i in range(0, len(tiles), ATTN_TILES_IN_FLIGHT)]
    _pair_softmax_pv(groups, ka, kb, vat, vbt, emit)


def _pair_attn(q, mask_a, mask_b, kc, vc, kl, vl, *, dk, n_pairs, q_per_pair, blocks_per_step, tq,
               name):
    b, t, _ = q.shape
    split = dk == 2 * LANES
    dk_head = dk // 2 if split else dk
    assert q_per_pair % blocks_per_step == 0
    steps_per_pair = q_per_pair // blocks_per_step
    n_qblk = n_pairs * steps_per_pair
    use_lat = kl is not None
    qmap = lambda bi, c, i: (bi, i, c)
    kvmap = lambda bi, c, i: (bi, 0, c // steps_per_pair)
    const = lambda bi, c, i: (0, 0)
    in_specs = [pl.BlockSpec((1, tq, blocks_per_step * dk), qmap),
                pl.BlockSpec((1, dk), const),
                pl.BlockSpec((1, dk), const),
                pl.BlockSpec((1, kc.shape[1], dk), kvmap),
                pl.BlockSpec((1, vc.shape[1], LANES), kvmap)]
    args = [q, mask_a, mask_b, kc, vc]
    if use_lat:
        in_specs += [pl.BlockSpec((1, kl.shape[1], dk), kvmap),
                     pl.BlockSpec((1, vl.shape[1], LANES), kvmap)]
        args += [kl, vl]
    n_keys = kc.shape[1] + (kl.shape[1] if use_lat else 0)
    return pl.pallas_call(
        functools.partial(_pair_attn_kernel, use_lat=use_lat, steps_per_pair=steps_per_pair, dk=dk,
                          split=split),
        grid=(b, n_qblk, t // tq),
        in_specs=in_specs,
        out_specs=pl.BlockSpec((1, tq, blocks_per_step * LANES), qmap),
        out_shape=jax.ShapeDtypeStruct((b, t, n_pairs * q_per_pair * LANES), BF16),
        scratch_shapes=[pltpu.VMEM((n_keys, dk_head), BF16), pltpu.VMEM((n_keys, dk_head), BF16),
                        pltpu.VMEM((LANES, n_keys), BF16), pltpu.VMEM((LANES, n_keys), BF16)],
        compiler_params=_params(3),
        name=name,
    )(*args)


def _na_block_tiles(u, n_blocks):
    t0 = min(max(u - 1, 0), n_blocks - NA_KEY_TILES)
    return t0, (0 if u == 0 else 2 if u == n_blocks - 1 else 1)


def _na_kernel(q_ref, kc_ref, vc_ref, k_ref, v_ref, band_ref, o_ref, ka, kb, vat, vbt, bias_ref,
               *, n_rows):
    j_idx, ok = _na_bias_plan(n_rows)

    @pl.when(pl.program_id(1) == 0)
    def _():
        masked = jnp.full((GRID_W, GRID_W), NEG_BIG, F32)
        for head, cls, ch, i, dr in np.ndindex(2, *j_idx.shape):
            half = (dr % 2) * GRID_W
            piece = (band_ref[head, int(j_idx[cls, ch, i, dr]), :, half:half + GRID_W]
                     if ok[cls, ch, i, dr] else masked)
            bias_ref[head, cls, ch, i * GRID_W:(i + 1) * GRID_W, dr * GRID_W:(dr + 1) * GRID_W] = piece

    tc = kc_ref.shape[1]
    lane = _lane((1, LANES))
    mask_a = jnp.where(lane < NA_HEAD_DIM, 1.0, 0.0).astype(BF16)
    mask_b = jnp.where(lane < NA_HEAD_DIM, 0.0, 1.0).astype(BF16)
    _fill_pair_kv_t(kc_ref[0], vc_ref[0], mask_a, mask_b, ka, kb, vat, vbt, 0)
    _fill_pair_kv_t(k_ref[0], v_ref[0], mask_a, mask_b, ka, kb, vat, vbt, tc)

    n_blocks = n_rows // NA_BLOCK_ROWS
    blk_q = NA_BLOCK_ROWS * GRID_W

    def load_q(u):
        q = q_ref[0, u * blk_q:(u + 1) * blk_q, :]
        return q, q

    def emit(u, out):
        o_ref[0, u * blk_q:(u + 1) * blk_q, :] = out.astype(o_ref.dtype)

    by_len = {}
    for u in range(n_blocks):
        t0, cls = _na_block_tiles(u, n_blocks)
        chunks = [(0, tc, None)]
        for ch in range(NA_KEY_TILES):
            if ok[cls, ch].any():
                chunks.append((tc + (t0 + ch) * blk_q, blk_q,
                               functools.partial(lambda head, cls, ch: bias_ref[head, cls, ch],
                                                 cls=cls, ch=ch)))
        by_len.setdefault(len(chunks), []).append((functools.partial(load_q, u), chunks, u))
    groups = [tiles[i:i + ATTN_TILES_IN_FLIGHT] for tiles in by_len.values()
              for i in range(0, len(tiles), ATTN_TILES_IN_FLIGHT)]
    _pair_softmax_pv(groups, ka, kb, vat, vbt, emit)


def _na_attn(q, kc, vc, k, v, band):
    b, s, _ = q.shape
    n_rows = s // GRID_W
    n_pairs = NA_HEADS // 2
    n_keys = kc.shape[1] + s
    blk_q = NA_BLOCK_ROWS * GRID_W
    assert kc.shape[1] == ATTN_KEY_CHUNK and blk_q == ATTN_KEY_CHUNK
    blk = lambda rows: pl.BlockSpec((1, rows, LANES), lambda m, bi: (bi, 0, m))
    return pl.pallas_call(
        functools.partial(_na_kernel, n_rows=n_rows),
        grid=(n_pairs, b),
        in_specs=[blk(s), blk(kc.shape[1]), blk(vc.shape[1]), blk(s), blk(s),
                  pl.BlockSpec((2,) + band.shape[1:], lambda m, bi: (m, 0, 0, 0))],
        out_specs=blk(s),
        out_shape=jax.ShapeDtypeStruct((b, s, NA_DIM), BF16),
        scratch_shapes=[pltpu.VMEM((n_keys, LANES), BF16), pltpu.VMEM((n_keys, LANES), BF16),
                        pltpu.VMEM((LANES, n_keys), BF16), pltpu.VMEM((LANES, n_keys), BF16),
                        pltpu.VMEM((2, 3, NA_KEY_TILES, blk_q, blk_q), F32)],
        compiler_params=_params(2),
        name="na_attn",
    )(q, kc, vc, k, v, band)


def _pool_kernel(a_ref, pw_ref, ps_ref, o_ref, pad_ref, *, t_len, chunk):
    pad_ref[0:POOL_HALO, :] = jnp.zeros((POOL_HALO, POOL_DIM), F32)
    pad_ref[POOL_HALO + t_len:2 * POOL_HALO + t_len, :] = jnp.zeros((POOL_HALO, POOL_DIM), F32)
    pad_ref[POOL_HALO:POOL_HALO + t_len, :] = a_ref[0]
    lane = _lane((chunk, POOL_DIM))
    g0 = lane < POOL_CH
    g1 = lane < 2 * POOL_CH
    g2 = lane < 3 * POOL_CH
    half_w = jnp.where(g0, 1, jnp.where(g1, 2, jnp.where(g2, 4, 8)))
    for c in range(t_len // chunk):
        base = c * chunk

        def ld(off, base=base):
            return pad_ref[POOL_HALO + base + off:POOL_HALO + base + off + chunk, :]

        a0 = ld(0)
        w2 = ld(-1) + a0
        w4 = w2 + ld(-2) + ld(1)
        w8 = w4 + ld(-4) + ld(-3) + ld(2) + ld(3)
        w16 = w8
        for off in (-8, -7, -6, -5, 4, 5, 6, 7):
            w16 = w16 + ld(off)
        tok = base + lax.broadcasted_iota(jnp.int32, (chunk, POOL_DIM), 0)
        cnt = (jnp.minimum(tok + half_w, t_len) - jnp.maximum(tok - half_w, 0)).astype(F32)
        wsum = jnp.where(g0, w2, jnp.where(g1, w4, jnp.where(g2, w8, w16)))
        dlt = wsum / cnt - a0
        y = jnp.dot(dlt.astype(BF16), pw_ref[...], preferred_element_type=F32) * ps_ref[...]
        o_ref[0, base:base + chunk, :] = y.astype(o_ref.dtype)


def _pool(a, pw_bd, pscale):
    b, t, _ = a.shape
    chunk = min(t, 256)
    return pl.pallas_call(
        functools.partial(_pool_kernel, t_len=t, chunk=chunk),
        grid=(b,),
        in_specs=[pl.BlockSpec((1, t, POOL_DIM), lambda bi: (bi, 0, 0)),
                  pl.BlockSpec((POOL_DIM, POOL_DIM), lambda bi: (0, 0)),
                  pl.BlockSpec((1, POOL_DIM), lambda bi: (0, 0))],
        out_specs=pl.BlockSpec((1, t, POOL_DIM), lambda bi: (bi, 0, 0)),
        out_shape=jax.ShapeDtypeStruct((b, t, POOL_DIM), BF16),
        scratch_shapes=[pltpu.VMEM((t + 2 * POOL_HALO, POOL_DIM), F32)],
        compiler_params=_params(1),
        name="pool",
    )(a, pw_bd, pscale)


def _post_kernel(x_ref, g1_ref, sh_ref, sc_ref, g2_ref, m0_ref, m1_ref, w0_ref, w1_ref, g_ref,
                 wg_ref, wu_ref, wd_ref, o_ref):
    n_rows = x_ref.shape[1]
    sub = min(n_rows, POST_SUB)
    tiles = [slice(r, r + sub) for r in range(0, n_rows, sub)]
    n_chunks = wg_ref.shape[1] // FFN_CHUNK

    mixes = [jnp.dot(m0_ref[0, rows, :], w0_ref[...], preferred_element_type=F32)
             + jnp.dot(m1_ref[0, rows, :], w1_ref[...], preferred_element_type=F32) for rows in tiles]
    xs = [x_ref[0, rows, :] + g1_ref[0, 0] * mix for rows, mix in zip(tiles, mixes)]
    hs = [_modulate(x, g_ref[...], sh_ref[0, 0], sc_ref[0, 0]).astype(BF16) for x in xs]

    def gate_up(c):
        cols = slice(c * FFN_CHUNK, (c + 1) * FFN_CHUNK)
        return [(jnp.dot(h, wg_ref[:, cols], preferred_element_type=F32),
                 jnp.dot(h, wu_ref[:, cols], preferred_element_type=F32)) for h in hs]

    accs = [None] * len(tiles)
    cur = gate_up(0)
    for c in range(n_chunks):
        nxt = gate_up(c + 1) if c + 1 < n_chunks else None
        for t, (gt, up) in enumerate(cur):
            act = (gt * jax.nn.sigmoid(gt) * up).astype(BF16)
            part = jnp.dot(act, wd_ref[c * FFN_CHUNK:(c + 1) * FFN_CHUNK, :],
                           preferred_element_type=F32)
            accs[t] = part if accs[t] is None else accs[t] + part
        cur = nxt
    for rows, x, acc in zip(tiles, xs, accs):
        o_ref[0, rows, :] = x + g2_ref[0, 0] * acc


def _mod_spec(mods, k, ctx_row):
    blk = (1, 1, 1, mods.shape[-1])
    if ctx_row is None:
        return pl.BlockSpec(blk, lambda bi, i: (bi, k, 0, 0))
    return pl.BlockSpec(blk, lambda bi, i: (ctx_row, k, 0, 0))


def _resident(a):
    nd = a.ndim
    return pl.BlockSpec(a.shape, lambda bi, i: (0,) * nd, pipeline_mode=pl.Buffered(1))


def _post(x, mods, ctx_row, m0, m1, w0, w1, gain, wg, wu, wd, *, tm, name):
    b, t, d = x.shape
    row = lambda bi, i: (bi, i, 0)
    return pl.pallas_call(
        _post_kernel,
        grid=(b, t // tm),
        in_specs=[pl.BlockSpec((1, tm, d), row),
                  _mod_spec(mods, 2, ctx_row), _mod_spec(mods, 3, ctx_row),
                  _mod_spec(mods, 4, ctx_row), _mod_spec(mods, 5, ctx_row),
                  pl.BlockSpec((1, tm, m0.shape[2]), row),
                  pl.BlockSpec((1, tm, m1.shape[2]), row),
                  _resident(w0), _resident(w1), _resident(gain),
                  _resident(wg), _resident(wu), _resident(wd)],
        out_specs=pl.BlockSpec((1, tm, d), row),
        out_shape=jax.ShapeDtypeStruct((b, t, d), F32),
        compiler_params=_params(2),
        name=name,
    )(x, mods, mods, mods, mods, m0, m1, w0, w1, gain, wg, wu, wd)


class _LaneLayout:
    def __init__(self, dim, slot, axis, freq, x2, n_freq):
        self.dim, self.slot, self.axis, self.freq, self.x2, self.n_freq = dim, slot, axis, freq, x2, n_freq

    def rope_tables(self, seq):
        tok = np.arange(seq)
        inv = (ROPE_THETA ** (-np.arange(self.n_freq, dtype=np.float32) * 2.0
                              / (2 * self.n_freq))).astype(np.float32).astype(np.float64)
        pos = np.stack([tok // GRID_W, tok % GRID_W]).astype(np.float64)
        rot = self.axis >= 0
        ang = pos[np.maximum(self.axis, 0)].T * inv[self.freq][None, :]
        cos = np.where(rot[None], np.cos(ang), 1.0)
        sin = np.where(rot[None], np.sin(ang) * np.where(self.x2, 1.0, -1.0)[None], 0.0)
        return jnp.asarray(cos, F32), jnp.asarray(sin, F32)


def _gqa_layout():
    lane = np.arange(LANES)
    x2, slot, axis, freq = lane // 64, (lane % 64) // 32, (lane % 32) // 16, lane % 16
    return _LaneLayout((axis * 2 + x2) * 16 + freq, slot, axis, freq, x2.astype(bool), HEAD_DIM // 4)


def _mla_layout():
    lane = np.arange(LANES)
    is_rot = (lane % 64) < 16
    x2 = lane >= 64
    axis = np.where(is_rot, (lane % 64) // 8, -1)
    freq = np.where(is_rot, lane % 8, 0)
    nope = np.where(lane < 64, lane - 16, 48 + lane - 80)
    dim = np.where(is_rot, MLA_NOPE + (np.maximum(axis, 0) * 2 + x2) * 8 + freq, nope)
    dim = np.where(lane >= MLA_QK, -1, dim)
    return _LaneLayout(dim, np.zeros(LANES, int), axis, freq, x2 & is_rot, MLA_ROPE // 4)


def _take_cols(w, idx):
    return jnp.where(jnp.asarray(idx >= 0)[None, :], w[:, np.maximum(idx, 0)], 0.0)


def _na_bias_plan(n_rows):
    n_blocks = n_rows // NA_BLOCK_ROWS
    j_idx = np.zeros((3, NA_KEY_TILES, NA_BLOCK_ROWS, NA_BLOCK_ROWS), np.int32)
    ok = np.zeros(j_idx.shape, bool)
    for cls, u in enumerate((0, 1, n_blocks - 1)):
        t0, cls_u = _na_block_tiles(u, n_blocks)
        assert cls_u == cls
        for ch in range(NA_KEY_TILES):
            for i in range(NA_BLOCK_ROWS):
                for dr in range(NA_BLOCK_ROWS):
                    r, kr = NA_BLOCK_ROWS * u + dr, NA_BLOCK_ROWS * (t0 + ch) + i
                    r0 = min(max(r - NA_ROWS // 2, 0), n_rows - NA_ROWS)
                    ok[cls, ch, i, dr] = r0 <= kr < r0 + NA_ROWS
                    j_idx[cls, ch, i, dr] = kr - r + NA_ROWS - 1
    return j_idx, ok


def _na_band_table(rpb):
    cols = np.arange(GRID_W)
    c0 = np.clip(cols - NA_COLS // 2, 0, GRID_W - NA_COLS)
    kc = np.arange(GRID_W)
    inside = (kc[None, :] >= c0[:, None]) & (kc[None, :] < c0[:, None] + NA_COLS)
    dc = kc[None, :] - cols[:, None] + (NA_COLS - 1)
    onehot = (np.arange(2 * NA_COLS - 1)[:, None, None] == dc[None]) & inside[None]
    sel = jnp.einsum("hjd,dck->hjkc", rpb * LOG2E, jnp.asarray(onehot, F32),
                     precision=lax.Precision.HIGHEST)
    band = jnp.where(jnp.asarray(inside.T)[None, None], sel, NEG_BIG).astype(F32)
    return jnp.concatenate([band, band], axis=-1)


def kernel(x, c, ctx, c_ctx, l0_ada_w, l0_ada_b, l0_norm_mix, l0_norm_ffn, l0_w_in, l0_pool_w, l0_pool_scale, l0_q_gain, l0_k_gain, l0_w_out, l0_ffn_w_gate, l0_ffn_w_up, l0_ffn_w_down, l1_ada_w, l1_ada_b, l1_norm_mix, l1_norm_ffn, l1_w_in, l1_mla_q_a_gain, l1_mla_kv_a_gain, l1_mla_w_uq, l1_mla_w_ukv, l1_mla_q_gain, l1_mla_k_gain, l1_na_q_gain, l1_na_k_gain, l1_na_rpb, l1_w_out, l1_ffn_w_gate, l1_ffn_w_up, l1_ffn_w_down):
    b, s, d = x.shape
    tc = ctx.shape[1]
    tm_lat = 512
    tp_lat = 1024
    tq_lat = 2048

    cond = jnp.concatenate([c, c_ctx[None, :], jnp.zeros((7, d), F32)], axis=0)

    def mods(ada_w, ada_b):
        return _ada(cond, ada_w, ada_b).reshape(cond.shape[0], 6, 1, d)

    def ffn_weights(wg, wu, wd):
        return wg.astype(BF16), wu.astype(BF16), wd.astype(BF16)

    row1 = lambda v: v.reshape(1, -1)
    tile4 = lambda v, mul: (jnp.tile(v, SLAB // v.shape[0]) * mul).reshape(1, -1)
    sh64 = jnp.asarray(np.kron(np.eye(SLAB // HEAD_DIM), np.ones((HEAD_DIM, HEAD_DIM))), BF16)

    def slab_gain(g, lay, mul):
        per_lane = jnp.where(jnp.asarray(lay.dim >= 0), g[np.maximum(lay.dim, 0)], 0.0) * mul
        return jnp.tile(per_lane, SLAB // LANES).reshape(1, -1)

    def slab_same_head(lay):
        head = np.concatenate([np.where(lay.dim >= 0, blk * 2 + lay.slot, -1)
                               for blk in range(SLAB // LANES)])
        return jnp.asarray((head[:, None] == head[None, :]) & (head[:, None] >= 0), BF16)

    def pair_order(w_q, axis):
        shp = w_q.shape
        split = shp[:axis] + (GQA_KV_HEADS // 2, 2, GQA_GROUP, HEAD_DIM) + shp[axis + 1:]
        perm = list(range(len(split)))
        perm[axis + 1], perm[axis + 2] = perm[axis + 2], perm[axis + 1]
        return w_q.reshape(split).transpose(perm).reshape(shp)

    mods0 = mods(l0_ada_w, l0_ada_b)
    glay = _gqa_layout()
    q_cols = np.concatenate([((2 * m + glay.slot) * GQA_GROUP + g) * HEAD_DIM + glay.dim
                             for m in range(GQA_KV_HEADS // 2) for g in range(GQA_GROUP)])
    k_cols = np.concatenate([(2 * m + glay.slot) * HEAD_DIM + glay.dim
                             for m in range(GQA_KV_HEADS // 2)])
    q0, k0, v0 = POOL_DIM, POOL_DIM + GQA_Q_DIM, POOL_DIM + GQA_Q_DIM + GQA_KV_DIM
    w_in0 = l0_w_in[:, np.concatenate([np.arange(q0), q0 + q_cols, k0 + k_cols,
                                       np.arange(v0, l0_w_in.shape[1])])].astype(BF16)
    cos0, sin0 = glay.rope_tables(s)
    n_ctx = b * tc
    flat = lambda a: a.reshape(1, n_ctx, a.shape[-1])
    unflat = lambda a: a.reshape(b, tc, a.shape[-1])
    ones_c = jnp.ones((n_ctx, LANES), F32)
    zeros_c = jnp.zeros((n_ctx, LANES), F32)
    qg4 = slab_gain(l0_q_gain, glay, HEAD_DIM ** 0.5 * GQA_QSCALE)
    kg4 = slab_gain(l0_k_gain, glay, HEAD_DIM ** 0.5)
    sh_gqa = slab_same_head(glay)
    gmask_a = jnp.asarray((glay.slot == 0)[None], BF16)
    gmask_b = 1 - gmask_a

    a_l, q_l, k_l, v_l = _inproj0(x, mods0, None, row1(l0_norm_mix), w_in0, sh_gqa, qg4, kg4,
                                  cos0, sin0, rope=True, tm=tp_lat)
    a_c, q_c, k_c, v_c = map(unflat, _inproj0(flat(ctx), mods0, b, row1(l0_norm_mix), w_in0, sh_gqa,
                                               qg4, kg4, ones_c, zeros_c, rope=False,
                                               tm=min(n_ctx, tp_lat)))
    n_kv_pairs = GQA_KV_HEADS // 2
    attn_l = _pair_attn(q_l, gmask_a, gmask_b, k_c, v_c, k_l, v_l, dk=LANES, n_pairs=n_kv_pairs,
                        q_per_pair=GQA_GROUP, blocks_per_step=GQA_GROUP, tq=tq_lat, name="gqa_lat")
    attn_c = _pair_attn(q_c, gmask_a, gmask_b, k_c, v_c, None, None, dk=LANES, n_pairs=n_kv_pairs,
                        q_per_pair=GQA_GROUP, blocks_per_step=GQA_GROUP, tq=tc, name="gqa_ctx")

    eye = jnp.eye(POOL_GROUPS, dtype=F32)
    pw_bd = (eye[:, None, :, None] * l0_pool_w[:, :, None, :]).reshape(POOL_DIM, POOL_DIM).astype(BF16)
    pool_l = _pool(a_l, pw_bd, row1(l0_pool_scale))
    pool_c = _pool(a_c, pw_bd, row1(l0_pool_scale))

    w_out0_pool = l0_w_out[:POOL_DIM].astype(BF16)
    w_out0_attn = pair_order(l0_w_out[POOL_DIM:], 0).astype(BF16)
    ffn0 = ffn_weights(l0_ffn_w_gate, l0_ffn_w_up, l0_ffn_w_down)

    x1 = _post(x, mods0, None, pool_l, attn_l, w_out0_pool, w_out0_attn, row1(l0_norm_ffn), *ffn0,
               tm=tm_lat, name="post0_lat")
    xc = _post(flat(ctx), mods0, b, flat(pool_c), flat(attn_c), w_out0_pool, w_out0_attn,
               row1(l0_norm_ffn), *ffn0, tm=min(n_ctx, tm_lat), name="post0_ctx")

    mods1 = mods(l1_ada_w, l1_ada_b)
    mlay = _mla_layout()
    heads = np.arange(MLA_HEADS)[:, None]
    o_kr = MLA_Q_LORA + MLA_KV_LORA
    kr_cols = np.where(mlay.dim >= MLA_NOPE, o_kr + mlay.dim - MLA_NOPE, -1)
    w_in1 = jnp.concatenate([l1_w_in[:, :o_kr].astype(BF16), _take_cols(l1_w_in, kr_cols).astype(BF16),
                             l1_w_in[:, o_kr + MLA_ROPE:].astype(BF16)], axis=1)
    wuq = _take_cols(l1_mla_w_uq, np.where(mlay.dim >= 0, heads * MLA_QK + mlay.dim, -1).reshape(-1)
                     ).astype(BF16)
    kv_w = MLA_NOPE + MLA_V
    k_idx = np.where((mlay.dim >= 0) & (mlay.dim < MLA_NOPE), heads * kv_w + mlay.dim, -1).reshape(-1)
    v_idx = (heads * kv_w + MLA_NOPE + np.arange(MLA_V)[None, :]).reshape(-1)
    wukv = _take_cols(l1_mla_w_ukv, np.concatenate([k_idx, v_idx])).astype(BF16)

    shmla = slab_same_head(mlay)
    mqg = slab_gain(l1_mla_q_gain, mlay, MLA_QK ** 0.5 * MLA_QSCALE)
    mkg = slab_gain(l1_mla_k_gain, mlay, MLA_QK ** 0.5)
    nqg = tile4(l1_na_q_gain, NA_HEAD_DIM ** 0.5 * NA_QSCALE)
    nkg = tile4(l1_na_k_gain, NA_HEAD_DIM ** 0.5)
    cos1, sin1 = mlay.rope_tables(s)
    common = (row1(l1_norm_mix), w_in1, row1(l1_mla_q_a_gain), row1(l1_mla_kv_a_gain), wuq, wukv,
              sh64, shmla, mqg, mkg, nqg, nkg)
    mq, mk, mv, nq, nk, nv = _inproj1(x1, mods1, None, *common, cos1, sin1, rope=True, want_q=True,
                                      tm=tp_lat)
    mkc, mvc, nkc, nvc = map(unflat, _inproj1(xc, mods1, b, *common, ones_c, zeros_c, rope=False,
                                              want_q=False, tm=min(n_ctx, tp_lat)))

    no_mask = jnp.ones((1, MLA_PAIR_W), BF16)
    o_mla = _pair_attn(mq, no_mask, no_mask, mkc, mvc, mk, mv, dk=MLA_PAIR_W,
                       n_pairs=MLA_HEADS // 2, q_per_pair=1, blocks_per_step=1, tq=tq_lat, name="mla")
    o_na = _na_attn(nq, nkc, nvc, nk, nv, _na_band_table(l1_na_rpb))

    w_out1 = l1_w_out.astype(BF16)
    ffn1 = ffn_weights(l1_ffn_w_gate, l1_ffn_w_up, l1_ffn_w_down)
    return _post(x1, mods1, None, o_mla, o_na, w_out1[:MLA_HEADS * MLA_V], w_out1[MLA_HEADS * MLA_V:],
                 row1(l1_norm_ffn), *ffn1, tm=tm_lat, name="post1_lat")
```

```python
import functools

import numpy as np
import jax
import jax.numpy as jnp
from jax import lax
from jax.experimental import pallas as pl
from jax.experimental.pallas import tpu as pltpu

F32 = jnp.float32
BF16 = jnp.bfloat16

D_MODEL = 1024
GRID_W = 64
ROPE_THETA = 10000.0
EPS = 1e-6
LANES = 128
SLAB = 256

POOL_GROUPS = 4
POOL_CH = 64
POOL_DIM = POOL_GROUPS * POOL_CH
POOL_HALO = 16

HEAD_DIM = 64
GQA_HEADS = 12
GQA_KV_HEADS = 4
GQA_GROUP = GQA_HEADS // GQA_KV_HEADS
GQA_Q_DIM = GQA_HEADS * HEAD_DIM
GQA_KV_DIM = GQA_KV_HEADS * HEAD_DIM

MLA_HEADS = 8
MLA_NOPE = 64
MLA_ROPE = 32
MLA_QK = MLA_NOPE + MLA_ROPE
MLA_V = 64
MLA_Q_LORA = 384
MLA_KV_LORA = 256
MLA_PAIR_W = 2 * LANES

NA_HEADS = 8
NA_HEAD_DIM = 64
NA_DIM = NA_HEADS * NA_HEAD_DIM
NA_ROWS = 8
NA_COLS = 16
NA_BLOCK_ROWS = 4
NA_KEY_TILES = 3

FFN_DIM = -(-8 * D_MODEL // (3 * 256)) * 256
FFN_CHUNK = 256

NEG_BIG = -1e30
LOG2E = 1.4426950408889634
GQA_QSCALE = HEAD_DIM ** -0.5 * LOG2E
MLA_QSCALE = MLA_QK ** -0.5 * LOG2E
NA_QSCALE = NA_HEAD_DIM ** -0.5 * LOG2E
POST_SUB = 256
PROJ0_SUB = 128
PROJ1_SUB = 256
ATTN_SUB = 256
ATTN_KEY_CHUNK = 256
ATTN_TILES_IN_FLIGHT = 2
ATTN_LOOKAHEAD = 1

VMEM_LIMIT = 56 * 1024 * 1024

_NT = (((1,), (1,)), ((), ()))


def _params(n_axes):
    return pltpu.CompilerParams(dimension_semantics=("arbitrary",) * n_axes,
                                vmem_limit_bytes=VMEM_LIMIT)


def _modulate(x, gain, shift, scale):
    ms = jnp.mean(x * x, axis=-1, keepdims=True)
    return x * lax.rsqrt(ms + EPS) * (gain * (1.0 + scale)) + shift


def _row_rmsnorm(x, gain):
    ms = jnp.mean(x * x, axis=-1, keepdims=True)
    return x * lax.rsqrt(ms + EPS) * gain


def _lane(shape):
    return lax.broadcasted_iota(jnp.int32, shape, 1)


def _head_rmsnorm(slab, same_head, gain, n):
    ssq = jnp.dot((slab * slab).astype(BF16), same_head, preferred_element_type=F32)
    return slab * lax.rsqrt(ssq + n * EPS) * gain


def _rope(blk, cos, sin_signed):
    return blk * cos + pltpu.roll(blk, LANES // 2, 1) * sin_signed


def _ada_kernel(c_ref, w_ref, b_ref, o_ref):
    cnd = c_ref[...]
    act = cnd * jax.nn.sigmoid(cnd)
    o_ref[...] = jnp.dot(act.astype(BF16), w_ref[...].astype(BF16),
                         preferred_element_type=F32) + b_ref[...]


def _ada(cond, w, b):
    n_rows, d = cond.shape
    n = w.shape[1]
    tn = 2048
    return pl.pallas_call(
        _ada_kernel,
        grid=(n // tn,),
        in_specs=[pl.BlockSpec((n_rows, d), lambda j: (0, 0)),
                  pl.BlockSpec((d, tn), lambda j: (0, j)),
                  pl.BlockSpec((1, tn), lambda j: (0, j))],
        out_specs=pl.BlockSpec((n_rows, tn), lambda j: (0, j)),
        out_shape=jax.ShapeDtypeStruct((n_rows, n), F32),
        compiler_params=_params(1),
        name="ada",
    )(cond, w, b.reshape(1, n))


def _slabs(src, c0, width):
    return [src[:, c0 + j * SLAB:c0 + (j + 1) * SLAB] for j in range(width // SLAB)]


def _norm_rope_store(slabs, same_head, gain, n, dst_ref, rows, rope_tables=None):
    for j, slab in enumerate(slabs):
        slab = _head_rmsnorm(slab, same_head, gain, n)
        for blk in range(SLAB // LANES):
            part = slab[:, blk * LANES:(blk + 1) * LANES]
            if rope_tables is not None:
                part = _rope(part, *rope_tables)
            o = j * SLAB + blk * LANES
            dst_ref[0, rows, o:o + LANES] = part.astype(BF16)


def _software_pipeline(n_rows, sub, stages):
    sub = min(n_rows, sub)
    tiles = [slice(r, r + sub) for r in range(0, n_rows, sub)]
    states = [{} for _ in tiles]
    for it in range(len(tiles) + len(stages) - 1):
        for s, stage in reversed(list(enumerate(stages))):
            t = it - s
            if 0 <= t < len(tiles):
                stage(tiles[t], states[t])


def _inproj0_kernel(x_ref, sh_ref, sc_ref, g_ref, w_ref, same_ref, qg_ref, kg_ref, cos_ref, sin_ref,
                    a_ref, q_ref, k_ref, v_ref, *, rope):
    q0 = POOL_DIM
    k0 = q0 + GQA_Q_DIM
    v0 = k0 + GQA_KV_DIM

    def project(rows, st):
        h = _modulate(x_ref[0, rows, :], g_ref[...], sh_ref[0, 0], sc_ref[0, 0]).astype(BF16)
        st["p"] = jnp.dot(h, w_ref[...], preferred_element_type=F32)

    def epilogue(rows, st):
        p = st["p"]
        a_ref[0, rows, :] = p[:, :POOL_DIM]
        tables = (cos_ref[rows, :], sin_ref[rows, :]) if rope else None
        for c0, width, gain_ref, dst in ((q0, GQA_Q_DIM, qg_ref, q_ref), (k0, GQA_KV_DIM, kg_ref, k_ref)):
            _norm_rope_store(_slabs(p, c0, width), same_ref[...], gain_ref[...], HEAD_DIM, dst, rows,
                             tables)
        v_ref[0, rows, :] = p[:, v0:v0 + GQA_KV_DIM].astype(BF16)

    _software_pipeline(x_ref.shape[1], PROJ0_SUB, [project, epilogue])


def _inproj0(x, mods, ctx_row, gain, w, same_head, q_gain, k_gain, cos, sin, *, rope, tm):
    b, t, d = x.shape
    row = lambda bi, i: (bi, i, 0)
    return pl.pallas_call(
        functools.partial(_inproj0_kernel, rope=rope),
        grid=(b, t // tm),
        in_specs=[pl.BlockSpec((1, tm, d), row),
                  _mod_spec(mods, 0, ctx_row), _mod_spec(mods, 1, ctx_row),
                  _resident(gain), _resident(w), _resident(same_head),
                  _resident(q_gain), _resident(k_gain),
                  pl.BlockSpec((tm, LANES), lambda bi, i: (i, 0)),
                  pl.BlockSpec((tm, LANES), lambda bi, i: (i, 0))],
        out_specs=[pl.BlockSpec((1, tm, POOL_DIM), row),
                   pl.BlockSpec((1, tm, GQA_Q_DIM), row),
                   pl.BlockSpec((1, tm, GQA_KV_DIM), row),
                   pl.BlockSpec((1, tm, GQA_KV_DIM), row)],
        out_shape=[jax.ShapeDtypeStruct((b, t, POOL_DIM), F32),
                   jax.ShapeDtypeStruct((b, t, GQA_Q_DIM), BF16),
                   jax.ShapeDtypeStruct((b, t, GQA_KV_DIM), BF16),
                   jax.ShapeDtypeStruct((b, t, GQA_KV_DIM), BF16)],
        compiler_params=_params(2),
        name="inproj0_rope" if rope else "inproj0_ctx",
    )(x, mods, mods, gain, w, same_head, q_gain, k_gain, cos, sin)


def _inproj1_kernel(x_ref, sh_ref, sc_ref, g_ref, w_ref, qag_ref, kvag_ref, wuq_ref, wukv_ref,
                    sh64_ref, shmla_ref, mqg_ref, mkg_ref, nqg_ref, nkg_ref, cos_ref, sin_ref,
                    *out_refs, rope, want_q):
    if want_q:
        mq_ref, mk_ref, mv_ref, nq_ref, nk_ref, nv_ref = out_refs
    else:
        mk_ref, mv_ref, nk_ref, nv_ref = out_refs
    o_ckv = MLA_Q_LORA
    o_kr = o_ckv + MLA_KV_LORA
    o_nq = o_kr + LANES
    o_nk = o_nq + NA_DIM
    o_nv = o_nk + NA_DIM
    mla_w = MLA_HEADS * LANES

    def project(rows, st):
        h = _modulate(x_ref[0, rows, :], g_ref[...], sh_ref[0, 0], sc_ref[0, 0]).astype(BF16)
        st["p"] = jnp.dot(h, w_ref[...], preferred_element_type=F32)

    def up_project(rows, st):
        p = st["p"]
        if want_q:
            st["qq"] = jnp.dot(_row_rmsnorm(p[:, :MLA_Q_LORA], qag_ref[...]).astype(BF16),
                               wuq_ref[...], preferred_element_type=F32)
        st["kv"] = jnp.dot(_row_rmsnorm(p[:, o_ckv:o_ckv + MLA_KV_LORA], kvag_ref[...]).astype(BF16),
                           wukv_ref[...], preferred_element_type=F32)

    def rope_tables(rows):
        return (cos_ref[rows, :], sin_ref[rows, :]) if rope else None

    def epilogue_q(rows, st):
        _norm_rope_store(_slabs(st["qq"], 0, mla_w), shmla_ref[...], mqg_ref[...], MLA_QK,
                         mq_ref, rows, rope_tables(rows))
        _norm_rope_store(_slabs(st["p"], o_nq, NA_DIM), sh64_ref[...], nqg_ref[...], NA_HEAD_DIM,
                         nq_ref, rows)

    def epilogue_kv(rows, st):
        p, kv = st["p"], st["kv"]
        tables = rope_tables(rows)
        kr_blk = p[:, o_kr:o_kr + LANES]
        kr_slab = jnp.concatenate([kr_blk, kr_blk], axis=1)
        _norm_rope_store([slab + kr_slab for slab in _slabs(kv, 0, mla_w)], shmla_ref[...],
                         mkg_ref[...], MLA_QK, mk_ref, rows, tables)
        mv_ref[0, rows, :] = kv[:, mla_w:mla_w + MLA_HEADS * MLA_V].astype(BF16)
        _norm_rope_store(_slabs(p, o_nk, NA_DIM), sh64_ref[...], nkg_ref[...], NA_HEAD_DIM,
                         nk_ref, rows)
        nv_ref[0, rows, :] = p[:, o_nv:o_nv + NA_DIM].astype(BF16)

    stages = [project, up_project] + ([epilogue_q] if want_q else []) + [epilogue_kv]
    _software_pipeline(x_ref.shape[1], PROJ1_SUB, stages)


def _inproj1(x, mods, ctx_row, gain, w, qag, kvag, wuq, wukv, sh64, shmla, mqg, mkg, nqg, nkg,
             cos, sin, *, rope, want_q, tm):
    b, t, d = x.shape
    row = lambda bi, i: (bi, i, 0)
    full = _resident
    mla_w = (MLA_HEADS // 2) * MLA_PAIR_W
    outs = [(mla_w, "mk"), (MLA_HEADS * MLA_V, "mv"), (NA_DIM, "nk"), (NA_DIM, "nv")]
    if want_q:
        outs = [(mla_w, "mq")] + outs[:2] + [(NA_DIM, "nq")] + outs[2:]
    return pl.pallas_call(
        functools.partial(_inproj1_kernel, rope=rope, want_q=want_q),
        grid=(b, t // tm),
        in_specs=[pl.BlockSpec((1, tm, d), row),
                  _mod_spec(mods, 0, ctx_row), _mod_spec(mods, 1, ctx_row),
                  full(gain), full(w), full(qag), full(kvag), full(wuq), full(wukv),
                  full(sh64), full(shmla), full(mqg), full(mkg), full(nqg), full(nkg),
                  pl.BlockSpec((tm, LANES), lambda bi, i: (i, 0)),
                  pl.BlockSpec((tm, LANES), lambda bi, i: (i, 0))],
        out_specs=[pl.BlockSpec((1, tm, n), row) for n, _ in outs],
        out_shape=[jax.ShapeDtypeStruct((b, t, n), BF16) for n, _ in outs],
        compiler_params=_params(2),
        name="inproj1_lat" if want_q else "inproj1_ctx",
    )(x, mods, mods, gain, w, qag, kvag, wuq, wukv, sh64, shmla, mqg, mkg, nqg, nkg, cos, sin)


def _transpose_on_mxu(x):
    n = x.shape[1]
    eye = lax.broadcasted_iota(jnp.int32, (n, n), 0) == lax.broadcasted_iota(jnp.int32, (n, n), 1)
    return lax.dot_general(jnp.where(eye, 1.0, 0.0).astype(BF16), x, _NT, preferred_element_type=F32)


def _fill_pair_kv_t(k, v, mask_a, mask_b, ka_ref, kb_ref, vat_ref, vbt_ref, r0):
    n = k.shape[0]
    if mask_a is None:
        ka_ref[r0:r0 + n, :] = k[:, :LANES]
        kb_ref[r0:r0 + n, :] = k[:, LANES:]
    else:
        ka_ref[r0:r0 + n, :] = k * mask_a
        kb_ref[r0:r0 + n, :] = k * mask_b
    vt = _transpose_on_mxu(v)
    first = lax.broadcasted_iota(jnp.int32, vt.shape, 0) < HEAD_DIM
    vat_ref[:, r0:r0 + n] = jnp.where(first, vt, 1.0).astype(BF16)
    vbt_ref[:, r0:r0 + n] = jnp.where(first, 1.0, vt).astype(BF16)


def _default_key_chunks(n_keys):
    return [(k0, min(ATTN_KEY_CHUNK, n_keys - k0), None) for k0 in range(0, n_keys, ATTN_KEY_CHUNK)]


def _pair_softmax_pv(groups, ka_ref, kb_ref, vat_ref, vbt_ref, emit):
    heads = ((ka_ref, vat_ref), (kb_ref, vbt_ref))
    jobs = [(g, j) for g, tiles in enumerate(groups) for j in range(len(tiles[0][1]))]
    queries = {}

    def scores(job):
        g, j = job
        if g not in queries:
            queries[g] = [load_q() for load_q, _, _ in groups[g]]
        out = []
        for (_, key_chunks, _), qs in zip(groups[g], queries[g]):
            k0, size, bias = key_chunks[j]
            for head, (q, (k_ref, _)) in enumerate(zip(qs, heads)):
                sc = lax.dot_general(k_ref[k0:k0 + size, :], q, _NT, preferred_element_type=F32)
                out.append((sc if bias is None else sc + bias(head)).astype(BF16))
        return out

    pending = [scores(job) for job in jobs[:ATTN_LOOKAHEAD]]
    for n, (g, j) in enumerate(jobs):
        if n + ATTN_LOOKAHEAD < len(jobs):
            pending.append(scores(jobs[n + ATTN_LOOKAHEAD]))
        s = pending.pop(0)
        if j == 0:
            run_max = [None] * len(s)
            acc = [None] * len(s)
        for t, (_, key_chunks, _) in enumerate(groups[g]):
            k0, size, _ = key_chunks[j]
            for head, (_, vt_ref) in enumerate(heads):
                c = 2 * t + head
                mx = s[c].max(axis=0, keepdims=True)
                if j > 0:
                    mx = jnp.maximum(mx, run_max[c])
                p = jnp.exp2(s[c] - mx)
                pv = jnp.dot(vt_ref[:, k0:k0 + size], p, preferred_element_type=F32)
                if j > 0:
                    acc[c] = acc[c] * jnp.exp2(run_max[c].astype(F32) - mx.astype(F32)) + pv
                else:
                    acc[c] = pv
                run_max[c] = mx
        if j == len(groups[g][0][1]) - 1:
            for t, (_, _, tag) in enumerate(groups[g]):
                oa, ob = acc[2 * t], acc[2 * t + 1]
                top = oa[:HEAD_DIM] / oa[HEAD_DIM:HEAD_DIM + 1]
                bot = ob[HEAD_DIM:] / ob[0:1]
                emit(tag, jnp.concatenate([top, bot], axis=0).T)


def _pair_attn_kernel(*refs, use_lat, steps_per_pair, dk, split):
    if use_lat:
        q_ref, ma_ref, mb_ref, kc_ref, vc_ref, kl_ref, vl_ref, o_ref, ka, kb, vat, vbt = refs
        segs = ((kc_ref, vc_ref), (kl_ref, vl_ref))
    else:
        q_ref, ma_ref, mb_ref, kc_ref, vc_ref, o_ref, ka, kb, vat, vbt = refs
        segs = ((kc_ref, vc_ref),)

    @pl.when(jnp.logical_and(pl.program_id(2) == 0, pl.program_id(1) % steps_per_pair == 0))
    def _():
        r0 = 0
        for k_ref, v_ref in segs:
            masks = (None, None) if split else (ma_ref[...], mb_ref[...])
            _fill_pair_kv_t(k_ref[0], v_ref[0], *masks, ka, kb, vat, vbt, r0)
            r0 += k_ref.shape[1]

    sub = min(q_ref.shape[1], ATTN_SUB)
    key_chunks = _default_key_chunks(ka.shape[0])

    def load_q(r0, blk):
        q = q_ref[0, r0:r0 + sub, blk * dk:(blk + 1) * dk]
        return (q[:, :LANES], q[:, LANES:]) if split else (q, q)

    def emit(tag, out):
        r0, blk = tag
        o_ref[0, r0:r0 + sub, blk * LANES:(blk + 1) * LANES] = out.astype(o_ref.dtype)

    tiles = [(functools.partial(load_q, r0, blk), key_chunks, (r0, blk))
             for blk in range(q_ref.shape[2] // dk) for r0 in range(0, q_ref.shape[1], sub)]
    groups = [tiles[i:i + ATTN_TILES_IN_FLIGHT] for i in range(0, len(tiles), ATTN_TILES_IN_FLIGHT)]
    _pair_softmax_pv(groups, ka, kb, vat, vbt, emit)


def _pair_attn(q, mask_a, mask_b, kc, vc, kl, vl, *, dk, n_pairs, q_per_pair, blocks_per_step, tq,
               name):
    b, t, _ = q.shape
    split = dk == 2 * LANES
    dk_head = dk // 2 if split else dk
    assert q_per_pair % blocks_per_step == 0
    steps_per_pair = q_per_pair // blocks_per_step
    n_qblk = n_pairs * steps_per_pair
    use_lat = kl is not None
    qmap = lambda bi, c, i: (bi, i, c)
    kvmap = lambda bi, c, i: (bi, 0, c // steps_per_pair)
    const = lambda bi, c, i: (0, 0)
    in_specs = [pl.BlockSpec((1, tq, blocks_per_step * dk), qmap),
                pl.BlockSpec((1, dk), const),
                pl.BlockSpec((1, dk), const),
                pl.BlockSpec((1, kc.shape[1], dk), kvmap),
                pl.BlockSpec((1, vc.shape[1], LANES), kvmap)]
    args = [q, mask_a, mask_b, kc, vc]
    if use_lat:
        in_specs += [pl.BlockSpec((1, kl.shape[1], dk), kvmap),
                     pl.BlockSpec((1, vl.shape[1], LANES), kvmap)]
        args += [kl, vl]
    n_keys = kc.shape[1] + (kl.shape[1] if use_lat else 0)
    return pl.pallas_call(
        functools.partial(_pair_attn_kernel, use_lat=use_lat, steps_per_pair=steps_per_pair, dk=dk,
                          split=split),
        grid=(b, n_qblk, t // tq),
        in_specs=in_specs,
        out_specs=pl.BlockSpec((1, tq, blocks_per_step * LANES), qmap),
        out_shape=jax.ShapeDtypeStruct((b, t, n_pairs * q_per_pair * LANES), BF16),
        scratch_shapes=[pltpu.VMEM((n_keys, dk_head), BF16), pltpu.VMEM((n_keys, dk_head), BF16),
                        pltpu.VMEM((LANES, n_keys), BF16), pltpu.VMEM((LANES, n_keys), BF16)],
        compiler_params=_params(3),
        name=name,
    )(*args)


def _na_block_tiles(u, n_blocks):
    t0 = min(max(u - 1, 0), n_blocks - NA_KEY_TILES)
    return t0, (0 if u == 0 else 2 if u == n_blocks - 1 else 1)


def _na_kernel(q_ref, kc_ref, vc_ref, k_ref, v_ref, band_ref, o_ref, ka, kb, vat, vbt, bias_ref,
               *, n_rows):
    j_idx, ok = _na_bias_plan(n_rows)

    @pl.when(pl.program_id(1) == 0)
    def _():
        masked = jnp.full((GRID_W, GRID_W), NEG_BIG, F32)
        for head, cls, ch, i, dr in np.ndindex(2, *j_idx.shape):
            half = (dr % 2) * GRID_W
            piece = (band_ref[head, int(j_idx[cls, ch, i, dr]), :, half:half + GRID_W]
                     if ok[cls, ch, i, dr] else masked)
            bias_ref[head, cls, ch, i * GRID_W:(i + 1) * GRID_W, dr * GRID_W:(dr + 1) * GRID_W] = piece

    tc = kc_ref.shape[1]
    lane = _lane((1, LANES))
    mask_a = jnp.where(lane < NA_HEAD_DIM, 1.0, 0.0).astype(BF16)
    mask_b = jnp.where(lane < NA_HEAD_DIM, 0.0, 1.0).astype(BF16)
    _fill_pair_kv_t(kc_ref[0], vc_ref[0], mask_a, mask_b, ka, kb, vat, vbt, 0)
    _fill_pair_kv_t(k_ref[0], v_ref[0], mask_a, mask_b, ka, kb, vat, vbt, tc)

    n_blocks = n_rows // NA_BLOCK_ROWS
    blk_q = NA_BLOCK_ROWS * GRID_W

    def load_q(u):
        q = q_ref[0, u * blk_q:(u + 1) * blk_q, :]
        return q, q

    def emit(u, out):
        o_ref[0, u * blk_q:(u + 1) * blk_q, :] = out.astype(o_ref.dtype)

    by_len = {}
    for u in range(n_blocks):
        t0, cls = _na_block_tiles(u, n_blocks)
        chunks = [(0, tc, None)]
        for ch in range(NA_KEY_TILES):
            if ok[cls, ch].any():
                chunks.append((tc + (t0 + ch) * blk_q, blk_q,
                               functools.partial(lambda head, cls, ch: bias_ref[head, cls, ch],
                                                 cls=cls, ch=ch)))
        by_len.setdefault(len(chunks), []).append((functools.partial(load_q, u), chunks, u))
    groups = [tiles[i:i + ATTN_TILES_IN_FLIGHT] for tiles in by_len.values()
              for i in range(0, len(tiles), ATTN_TILES_IN_FLIGHT)]
    _pair_softmax_pv(groups, ka, kb, vat, vbt, emit)


def _na_attn(q, kc, vc, k, v, band):
    b, s, _ = q.shape
    n_rows = s // GRID_W
    n_pairs = NA_HEADS // 2
    n_keys = kc.shape[1] + s
    blk_q = NA_BLOCK_ROWS * GRID_W
    assert kc.shape[1] == ATTN_KEY_CHUNK and blk_q == ATTN_KEY_CHUNK
    blk = lambda rows: pl.BlockSpec((1, rows, LANES), lambda m, bi: (bi, 0, m))
    return pl.pallas_call(
        functools.partial(_na_kernel, n_rows=n_rows),
        grid=(n_pairs, b),
        in_specs=[blk(s), blk(kc.shape[1]), blk(vc.shape[1]), blk(s), blk(s),
                  pl.BlockSpec((2,) + band.shape[1:], lambda m, bi: (m, 0, 0, 0))],
        out_specs=blk(s),
        out_shape=jax.ShapeDtypeStruct((b, s, NA_DIM), BF16),
        scratch_shapes=[pltpu.VMEM((n_keys, LANES), BF16), pltpu.VMEM((n_keys, LANES), BF16),
                        pltpu.VMEM((LANES, n_keys), BF16), pltpu.VMEM((LANES, n_keys), BF16),
                        pltpu.VMEM((2, 3, NA_KEY_TILES, blk_q, blk_q), F32)],
        compiler_params=_params(2),
        name="na_attn",
    )(q, kc, vc, k, v, band)


def _pool_kernel(a_ref, pw_ref, ps_ref, o_ref, pad_ref, *, t_len, chunk):
    pad_ref[0:POOL_HALO, :] = jnp.zeros((POOL_HALO, POOL_DIM), F32)
    pad_ref[POOL_HALO + t_len:2 * POOL_HALO + t_len, :] = jnp.zeros((POOL_HALO, POOL_DIM), F32)
    pad_ref[POOL_HALO:POOL_HALO + t_len, :] = a_ref[0]
    lane = _lane((chunk, POOL_DIM))
    g0 = lane < POOL_CH
    g1 = lane < 2 * POOL_CH
    g2 = lane < 3 * POOL_CH
    half_w = jnp.where(g0, 1, jnp.where(g1, 2, jnp.where(g2, 4, 8)))
    for c in range(t_len // chunk):
        base = c * chunk

        def ld(off, base=base):
            return pad_ref[POOL_HALO + base + off:POOL_HALO + base + off + chunk, :]

        a0 = ld(0)
        w2 = ld(-1) + a0
        w4 = w2 + ld(-2) + ld(1)
        w8 = w4 + ld(-4) + ld(-3) + ld(2) + ld(3)
        w16 = w8
        for off in (-8, -7, -6, -5, 4, 5, 6, 7):
            w16 = w16 + ld(off)
        tok = base + lax.broadcasted_iota(jnp.int32, (chunk, POOL_DIM), 0)
        cnt = (jnp.minimum(tok + half_w, t_len) - jnp.maximum(tok - half_w, 0)).astype(F32)
        wsum = jnp.where(g0, w2, jnp.where(g1, w4, jnp.where(g2, w8, w16)))
        dlt = wsum / cnt - a0
        y = jnp.dot(dlt.astype(BF16), pw_ref[...], preferred_element_type=F32) * ps_ref[...]
        o_ref[0, base:base + chunk, :] = y.astype(o_ref.dtype)


def _pool(a, pw_bd, pscale):
    b, t, _ = a.shape
    chunk = min(t, 256)
    return pl.pallas_call(
        functools.partial(_pool_kernel, t_len=t, chunk=chunk),
        grid=(b,),
        in_specs=[pl.BlockSpec((1, t, POOL_DIM), lambda bi: (bi, 0, 0)),
                  pl.BlockSpec((POOL_DIM, POOL_DIM), lambda bi: (0, 0)),
                  pl.BlockSpec((1, POOL_DIM), lambda bi: (0, 0))],
        out_specs=pl.BlockSpec((1, t, POOL_DIM), lambda bi: (bi, 0, 0)),
        out_shape=jax.ShapeDtypeStruct((b, t, POOL_DIM), BF16),
        scratch_shapes=[pltpu.VMEM((t + 2 * POOL_HALO, POOL_DIM), F32)],
        compiler_params=_params(1),
        name="pool",
    )(a, pw_bd, pscale)


def _post_kernel(x_ref, g1_ref, sh_ref, sc_ref, g2_ref, m0_ref, m1_ref, w0_ref, w1_ref, g_ref,
                 wg_ref, wu_ref, wd_ref, o_ref):
    n_rows = x_ref.shape[1]
    sub = min(n_rows, POST_SUB)
    tiles = [slice(r, r + sub) for r in range(0, n_rows, sub)]
    n_chunks = wg_ref.shape[1] // FFN_CHUNK

    mixes = [jnp.dot(m0_ref[0, rows, :], w0_ref[...], preferred_element_type=F32)
             + jnp.dot(m1_ref[0, rows, :], w1_ref[...], preferred_element_type=F32) for rows in tiles]
    xs = [x_ref[0, rows, :] + g1_ref[0, 0] * mix for rows, mix in zip(tiles, mixes)]
    hs = [_modulate(x, g_ref[...], sh_ref[0, 0], sc_ref[0, 0]).astype(BF16) for x in xs]

    def gate_up(c):
        cols = slice(c * FFN_CHUNK, (c + 1) * FFN_CHUNK)
        return [(jnp.dot(h, wg_ref[:, cols], preferred_element_type=F32),
                 jnp.dot(h, wu_ref[:, cols], preferred_element_type=F32)) for h in hs]

    accs = [None] * len(tiles)
    cur = gate_up(0)
    for c in range(n_chunks):
        nxt = gate_up(c + 1) if c + 1 < n_chunks else None
        for t, (gt, up) in enumerate(cur):
            act = (gt * jax.nn.sigmoid(gt) * up).astype(BF16)
            part = jnp.dot(act, wd_ref[c * FFN_CHUNK:(c + 1) * FFN_CHUNK, :],
                           preferred_element_type=F32)
            accs[t] = part if accs[t] is None else accs[t] + part
        cur = nxt
    for rows, x, acc in zip(tiles, xs, accs):
        o_ref[0, rows, :] = x + g2_ref[0, 0] * acc


def _mod_spec(mods, k, ctx_row):
    blk = (1, 1, 1, mods.shape[-1])
    if ctx_row is None:
        return pl.BlockSpec(blk, lambda bi, i: (bi, k, 0, 0))
    return pl.BlockSpec(blk, lambda bi, i: (ctx_row, k, 0, 0))


def _resident(a):
    nd = a.ndim
    return pl.BlockSpec(a.shape, lambda bi, i: (0,) * nd, pipeline_mode=pl.Buffered(1))


def _post(x, mods, ctx_row, m0, m1, w0, w1, gain, wg, wu, wd, *, tm, name):
    b, t, d = x.shape
    row = lambda bi, i: (bi, i, 0)
    return pl.pallas_call(
        _post_kernel,
        grid=(b, t // tm),
        in_specs=[pl.BlockSpec((1, tm, d), row),
                  _mod_spec(mods, 2, ctx_row), _mod_spec(mods, 3, ctx_row),
                  _mod_spec(mods, 4, ctx_row), _mod_spec(mods, 5, ctx_row),
                  pl.BlockSpec((1, tm, m0.shape[2]), row),
                  pl.BlockSpec((1, tm, m1.shape[2]), row),
                  _resident(w0), _resident(w1), _resident(gain),
                  _resident(wg), _resident(wu), _resident(wd)],
        out_specs=pl.BlockSpec((1, tm, d), row),
        out_shape=jax.ShapeDtypeStruct((b, t, d), F32),
        compiler_params=_params(2),
        name=name,
    )(x, mods, mods, mods, mods, m0, m1, w0, w1, gain, wg, wu, wd)


class _LaneLayout:
    def __init__(self, dim, slot, axis, freq, x2, n_freq):
        self.dim, self.slot, self.axis, self.freq, self.x2, self.n_freq = dim, slot, axis, freq, x2, n_freq

    def rope_tables(self, seq):
        tok = np.arange(seq)
        inv = (ROPE_THETA ** (-np.arange(self.n_freq, dtype=np.float32) * 2.0
                              / (2 * self.n_freq))).astype(np.float32).astype(np.float64)
        pos = np.stack([tok // GRID_W, tok % GRID_W]).astype(np.float64)
        rot = self.axis >= 0
        ang = pos[np.maximum(self.axis, 0)].T * inv[self.freq][None, :]
        cos = np.where(rot[None], np.cos(ang), 1.0)
        sin = np.where(rot[None], np.sin(ang) * np.where(self.x2, 1.0, -1.0)[None], 0.0)
        return jnp.asarray(cos, F32), jnp.asarray(sin, F32)


def _gqa_layout():
    lane = np.arange(LANES)
    x2, slot, axis, freq = lane // 64, (lane % 64) // 32, (lane % 32) // 16, lane % 16
    return _LaneLayout((axis * 2 + x2) * 16 + freq, slot, axis, freq, x2.astype(bool), HEAD_DIM // 4)


def _mla_layout():
    lane = np.arange(LANES)
    is_rot = (lane % 64) < 16
    x2 = lane >= 64
    axis = np.where(is_rot, (lane % 64) // 8, -1)
    freq = np.where(is_rot, lane % 8, 0)
    nope = np.where(lane < 64, lane - 16, 48 + lane - 80)
    dim = np.where(is_rot, MLA_NOPE + (np.maximum(axis, 0) * 2 + x2) * 8 + freq, nope)
    dim = np.where(lane >= MLA_QK, -1, dim)
    return _LaneLayout(dim, np.zeros(LANES, int), axis, freq, x2 & is_rot, MLA_ROPE // 4)


def _take_cols(w, idx):
    return jnp.where(jnp.asarray(idx >= 0)[None, :], w[:, np.maximum(idx, 0)], 0.0)


def _na_bias_plan(n_rows):
    n_blocks = n_rows // NA_BLOCK_ROWS
    j_idx = np.zeros((3, NA_KEY_TILES, NA_BLOCK_ROWS, NA_BLOCK_ROWS), np.int32)
    ok = np.zeros(j_idx.shape, bool)
    for cls, u in enumerate((0, 1, n_blocks - 1)):
        t0, cls_u = _na_block_tiles(u, n_blocks)
        assert cls_u == cls
        for ch in range(NA_KEY_TILES):
            for i in range(NA_BLOCK_ROWS):
                for dr in range(NA_BLOCK_ROWS):
                    r, kr = NA_BLOCK_ROWS * u + dr, NA_BLOCK_ROWS * (t0 + ch) + i
                    r0 = min(max(r - NA_ROWS // 2, 0), n_rows - NA_ROWS)
                    ok[cls, ch, i, dr] = r0 <= kr < r0 + NA_ROWS
                    j_idx[cls, ch, i, dr] = kr - r + NA_ROWS - 1
    return j_idx, ok


def _na_band_table(rpb):
    cols = np.arange(GRID_W)
    c0 = np.clip(cols - NA_COLS // 2, 0, GRID_W - NA_COLS)
    kc = np.arange(GRID_W)
    inside = (kc[None, :] >= c0[:, None]) & (kc[None, :] < c0[:, None] + NA_COLS)
    dc = kc[None, :] - cols[:, None] + (NA_COLS - 1)
    onehot = (np.arange(2 * NA_COLS - 1)[:, None, None] == dc[None]) & inside[None]
    sel = jnp.einsum("hjd,dck->hjkc", rpb * LOG2E, jnp.asarray(onehot, F32),
                     precision=lax.Precision.HIGHEST)
    band = jnp.where(jnp.asarray(inside.T)[None, None], sel, NEG_BIG).astype(F32)
    return jnp.concatenate([band, band], axis=-1)


def kernel(x, c, ctx, c_ctx, l0_ada_w, l0_ada_b, l0_norm_mix, l0_norm_ffn, l0_w_in, l0_pool_w, l0_pool_scale, l0_q_gain, l0_k_gain, l0_w_out, l0_ffn_w_gate, l0_ffn_w_up, l0_ffn_w_down, l1_ada_w, l1_ada_b, l1_norm_mix, l1_norm_ffn, l1_w_in, l1_mla_q_a_gain, l1_mla_kv_a_gain, l1_mla_w_uq, l1_mla_w_ukv, l1_mla_q_gain, l1_mla_k_gain, l1_na_q_gain, l1_na_k_gain, l1_na_rpb, l1_w_out, l1_ffn_w_gate, l1_ffn_w_up, l1_ffn_w_down):
    b, s, d = x.shape
    tc = ctx.shape[1]
    tm_lat = 512
    tp_lat = 1024
    tq_lat = 2048

    cond = jnp.concatenate([c, c_ctx[None, :], jnp.zeros((7, d), F32)], axis=0)

    def mods(ada_w, ada_b):
        return _ada(cond, ada_w, ada_b).reshape(cond.shape[0], 6, 1, d)

    def ffn_weights(wg, wu, wd):
        return wg.astype(BF16), wu.astype(BF16), wd.astype(BF16)

    row1 = lambda v: v.reshape(1, -1)
    tile4 = lambda v, mul: (jnp.tile(v, SLAB // v.shape[0]) * mul).reshape(1, -1)
    sh64 = jnp.asarray(np.kron(np.eye(SLAB // HEAD_DIM), np.ones((HEAD_DIM, HEAD_DIM))), BF16)

    def slab_gain(g, lay, mul):
        per_lane = jnp.where(jnp.asarray(lay.dim >= 0), g[np.maximum(lay.dim, 0)], 0.0) * mul
        return jnp.tile(per_lane, SLAB // LANES).reshape(1, -1)

    def slab_same_head(lay):
        head = np.concatenate([np.where(lay.dim >= 0, blk * 2 + lay.slot, -1)
                               for blk in range(SLAB // LANES)])
        return jnp.asarray((head[:, None] == head[None, :]) & (head[:, None] >= 0), BF16)

    def pair_order(w_q, axis):
        shp = w_q.shape
        split = shp[:axis] + (GQA_KV_HEADS // 2, 2, GQA_GROUP, HEAD_DIM) + shp[axis + 1:]
        perm = list(range(len(split)))
        perm[axis + 1], perm[axis + 2] = perm[axis + 2], perm[axis + 1]
        return w_q.reshape(split).transpose(perm).reshape(shp)

    mods0 = mods(l0_ada_w, l0_ada_b)
    glay = _gqa_layout()
    q_cols = np.concatenate([((2 * m + glay.slot) * GQA_GROUP + g) * HEAD_DIM + glay.dim
                             for m in range(GQA_KV_HEADS // 2) for g in range(GQA_GROUP)])
    k_cols = np.concatenate([(2 * m + glay.slot) * HEAD_DIM + glay.dim
                             for m in range(GQA_KV_HEADS // 2)])
    q0, k0, v0 = POOL_DIM, POOL_DIM + GQA_Q_DIM, POOL_DIM + GQA_Q_DIM + GQA_KV_DIM
    w_in0 = l0_w_in[:, np.concatenate([np.arange(q0), q0 + q_cols, k0 + k_cols,
                                       np.arange(v0, l0_w_in.shape[1])])].astype(BF16)
    cos0, sin0 = glay.rope_tables(s)
    n_ctx = b * tc
    flat = lambda a: a.reshape(1, n_ctx, a.shape[-1])
    unflat = lambda a: a.reshape(b, tc, a.shape[-1])
    ones_c = jnp.ones((n_ctx, LANES), F32)
    zeros_c = jnp.zeros((n_ctx, LANES), F32)
    qg4 = slab_gain(l0_q_gain, glay, HEAD_DIM ** 0.5 * GQA_QSCALE)
    kg4 = slab_gain(l0_k_gain, glay, HEAD_DIM ** 0.5)
    sh_gqa = slab_same_head(glay)
    gmask_a = jnp.asarray((glay.slot == 0)[None], BF16)
    gmask_b = 1 - gmask_a

    a_l, q_l, k_l, v_l = _inproj0(x, mods0, None, row1(l0_norm_mix), w_in0, sh_gqa, qg4, kg4,
                                  cos0, sin0, rope=True, tm=tp_lat)
    a_c, q_c, k_c, v_c = map(unflat, _inproj0(flat(ctx), mods0, b, row1(l0_norm_mix), w_in0, sh_gqa,
                                               qg4, kg4, ones_c, zeros_c, rope=False,
                                               tm=min(n_ctx, tp_lat)))
    n_kv_pairs = GQA_KV_HEADS // 2
    attn_l = _pair_attn(q_l, gmask_a, gmask_b, k_c, v_c, k_l, v_l, dk=LANES, n_pairs=n_kv_pairs,
                        q_per_pair=GQA_GROUP, blocks_per_step=GQA_GROUP, tq=tq_lat, name="gqa_lat")
    attn_c = _pair_attn(q_c, gmask_a, gmask_b, k_c, v_c, None, None, dk=LANES, n_pairs=n_kv_pairs,
                        q_per_pair=GQA_GROUP, blocks_per_step=GQA_GROUP, tq=tc, name="gqa_ctx")

    eye = jnp.eye(POOL_GROUPS, dtype=F32)
    pw_bd = (eye[:, None, :, None] * l0_pool_w[:, :, None, :]).reshape(POOL_DIM, POOL_DIM).astype(BF16)
    pool_l = _pool(a_l, pw_bd, row1(l0_pool_scale))
    pool_c = _pool(a_c, pw_bd, row1(l0_pool_scale))

    w_out0_pool = l0_w_out[:POOL_DIM].astype(BF16)
    w_out0_attn = pair_order(l0_w_out[POOL_DIM:], 0).astype(BF16)
    ffn0 = ffn_weights(l0_ffn_w_gate, l0_ffn_w_up, l0_ffn_w_down)

    x1 = _post(x, mods0, None, pool_l, attn_l, w_out0_pool, w_out0_attn, row1(l0_norm_ffn), *ffn0,
               tm=tm_lat, name="post0_lat")
    xc = _post(flat(ctx), mods0, b, flat(pool_c), flat(attn_c), w_out0_pool, w_out0_attn,
               row1(l0_norm_ffn), *ffn0, tm=min(n_ctx, tm_lat), name="post0_ctx")

    mods1 = mods(l1_ada_w, l1_ada_b)
    mlay = _mla_layout()
    heads = np.arange(MLA_HEADS)[:, None]
    o_kr = MLA_Q_LORA + MLA_KV_LORA
    kr_cols = np.where(mlay.dim >= MLA_NOPE, o_kr + mlay.dim - MLA_NOPE, -1)
    w_in1 = jnp.concatenate([l1_w_in[:, :o_kr].astype(BF16), _take_cols(l1_w_in, kr_cols).astype(BF16),
                             l1_w_in[:, o_kr + MLA_ROPE:].astype(BF16)], axis=1)
    wuq = _take_cols(l1_mla_w_uq, np.where(mlay.dim >= 0, heads * MLA_QK + mlay.dim, -1).reshape(-1)
                     ).astype(BF16)
    kv_w = MLA_NOPE + MLA_V
    k_idx = np.where((mlay.dim >= 0) & (mlay.dim < MLA_NOPE), heads * kv_w + mlay.dim, -1).reshape(-1)
    v_idx = (heads * kv_w + MLA_NOPE + np.arange(MLA_V)[None, :]).reshape(-1)
    wukv = _take_cols(l1_mla_w_ukv, np.concatenate([k_idx, v_idx])).astype(BF16)

    shmla = slab_same_head(mlay)
    mqg = slab_gain(l1_mla_q_gain, mlay, MLA_QK ** 0.5 * MLA_QSCALE)
    mkg = slab_gain(l1_mla_k_gain, mlay, MLA_QK ** 0.5)
    nqg = tile4(l1_na_q_gain, NA_HEAD_DIM ** 0.5 * NA_QSCALE)
    nkg = tile4(l1_na_k_gain, NA_HEAD_DIM ** 0.5)
    cos1, sin1 = mlay.rope_tables(s)
    common = (row1(l1_norm_mix), w_in1, row1(l1_mla_q_a_gain), row1(l1_mla_kv_a_gain), wuq, wukv,
              sh64, shmla, mqg, mkg, nqg, nkg)
    mq, mk, mv, nq, nk, nv = _inproj1(x1, mods1, None, *common, cos1, sin1, rope=True, want_q=True,
                                      tm=tp_lat)
    mkc, mvc, nkc, nvc = map(unflat, _inproj1(xc, mods1, b, *common, ones_c, zeros_c, rope=False,
                                              want_q=False, tm=min(n_ctx, tp_lat)))

    no_mask = jnp.ones((1, MLA_PAIR_W), BF16)
    o_mla = _pair_attn(mq, no_mask, no_mask, mkc, mvc, mk, mv, dk=MLA_PAIR_W,
                       n_pairs=MLA_HEADS // 2, q_per_pair=1, blocks_per_step=1, tq=tq_lat, name="mla")
    o_na = _na_attn(nq, nkc, nvc, nk, nv, _na_band_table(l1_na_rpb))

    w_out1 = l1_w_out.astype(BF16)
    ffn1 = ffn_weights(l1_ffn_w_gate, l1_ffn_w_up, l1_ffn_w_down)
    return _post(x1, mods1, None, o_mla, o_na, w_out1[:MLA_HEADS * MLA_V], w_out1[MLA_HEADS * MLA_V:],
                 row1(l1_norm_ffn), *ffn1, tm=tm_lat, name="post1_lat")
```

```python
import functools

import numpy as np
import jax
import jax.numpy as jnp
from jax import lax
from jax.experimental import pallas as pl
from jax.experimental.pallas import tpu as pltpu

F32 = jnp.float32
BF16 = jnp.bfloat16

D_MODEL = 1024
GRID_W = 64
ROPE_THETA = 10000.0
EPS = 1e-6
LANES = 128
SLAB = 256

POOL_GROUPS = 4
POOL_CH = 64
POOL_DIM = POOL_GROUPS * POOL_CH
POOL_HALO = 16

HEAD_DIM = 64
GQA_HEADS = 12
GQA_KV_HEADS = 4
GQA_GROUP = GQA_HEADS // GQA_KV_HEADS
GQA_Q_DIM = GQA_HEADS * HEAD_DIM
GQA_KV_DIM = GQA_KV_HEADS * HEAD_DIM

MLA_HEADS = 8
MLA_NOPE = 64
MLA_ROPE = 32
MLA_QK = MLA_NOPE + MLA_ROPE
MLA_V = 64
MLA_Q_LORA = 384
MLA_KV_LORA = 256
MLA_PAIR_W = 2 * LANES

NA_HEADS = 8
NA_HEAD_DIM = 64
NA_DIM = NA_HEADS * NA_HEAD_DIM
NA_ROWS = 8
NA_COLS = 16
NA_BLOCK_ROWS = 4
NA_KEY_TILES = 3

FFN_DIM = -(-8 * D_MODEL // (3 * 256)) * 256
FFN_CHUNK = 256

NEG_BIG = -1e30
LOG2E = 1.4426950408889634
GQA_QSCALE = HEAD_DIM ** -0.5 * LOG2E
MLA_QSCALE = MLA_QK ** -0.5 * LOG2E
NA_QSCALE = NA_HEAD_DIM ** -0.5 * LOG2E
POST_SUB = 256
PROJ0_SUB = 128
PROJ1_SUB = 256
ATTN_SUB = 256
ATTN_KEY_CHUNK = 256
ATTN_TILES_IN_FLIGHT = 2
ATTN_LOOKAHEAD = 1

VMEM_LIMIT = 56 * 1024 * 1024

_NT = (((1,), (1,)), ((), ()))


def _params(n_axes):
    return pltpu.CompilerParams(dimension_semantics=("arbitrary",) * n_axes,
                                vmem_limit_bytes=VMEM_LIMIT)


def _modulate(x, gain, shift, scale):
    ms = jnp.mean(x * x, axis=-1, keepdims=True)
    return x * lax.rsqrt(ms + EPS) * (gain * (1.0 + scale)) + shift


def _row_rmsnorm(x, gain):
    ms = jnp.mean(x * x, axis=-1, keepdims=True)
    return x * lax.rsqrt(ms + EPS) * gain


def _lane(shape):
    return lax.broadcasted_iota(jnp.int32, shape, 1)


def _head_rmsnorm(slab, same_head, gain, n):
    ssq = jnp.dot((slab * slab).astype(BF16), same_head, preferred_element_type=F32)
    return slab * lax.rsqrt(ssq + n * EPS) * gain


def _rope(blk, cos, sin_signed):
    return blk * cos + pltpu.roll(blk, LANES // 2, 1) * sin_signed


def _ada_kernel(c_ref, w_ref, b_ref, o_ref):
    cnd = c_ref[...]
    act = cnd * jax.nn.sigmoid(cnd)
    o_ref[...] = jnp.dot(act.astype(BF16), w_ref[...].astype(BF16),
                         preferred_element_type=F32) + b_ref[...]


def _ada(cond, w, b):
    n_rows, d = cond.shape
    n = w.shape[1]
    tn = 2048
    return pl.pallas_call(
        _ada_kernel,
        grid=(n // tn,),
        in_specs=[pl.BlockSpec((n_rows, d), lambda j: (0, 0)),
                  pl.BlockSpec((d, tn), lambda j: (0, j)),
                  pl.BlockSpec((1, tn), lambda j: (0, j))],
        out_specs=pl.BlockSpec((n_rows, tn), lambda j: (0, j)),
        out_shape=jax.ShapeDtypeStruct((n_rows, n), F32),
        compiler_params=_params(1),
        name="ada",
    )(cond, w, b.reshape(1, n))


def _slabs(src, c0, width):
    return [src[:, c0 + j * SLAB:c0 + (j + 1) * SLAB] for j in range(width // SLAB)]


def _norm_rope_store(slabs, same_head, gain, n, dst_ref, rows, rope_tables=None):
    for j, slab in enumerate(slabs):
        slab = _head_rmsnorm(slab, same_head, gain, n)
        for blk in range(SLAB // LANES):
            part = slab[:, blk * LANES:(blk + 1) * LANES]
            if rope_tables is not None:
                part = _rope(part, *rope_tables)
            o = j * SLAB + blk * LANES
            dst_ref[0, rows, o:o + LANES] = part.astype(BF16)


def _software_pipeline(n_rows, sub, stages):
    sub = min(n_rows, sub)
    tiles = [slice(r, r + sub) for r in range(0, n_rows, sub)]
    states = [{} for _ in tiles]
    for it in range(len(tiles) + len(stages) - 1):
        for s, stage in reversed(list(enumerate(stages))):
            t = it - s
            if 0 <= t < len(tiles):
                stage(tiles[t], states[t])


def _inproj0_kernel(x_ref, sh_ref, sc_ref, g_ref, w_ref, same_ref, qg_ref, kg_ref, cos_ref, sin_ref,
                    a_ref, q_ref, k_ref, v_ref, *, rope):
    q0 = POOL_DIM
    k0 = q0 + GQA_Q_DIM
    v0 = k0 + GQA_KV_DIM

    def project(rows, st):
        h = _modulate(x_ref[0, rows, :], g_ref[...], sh_ref[0, 0], sc_ref[0, 0]).astype(BF16)
        st["p"] = jnp.dot(h, w_ref[...], preferred_element_type=F32)

    def epilogue(rows, st):
        p = st["p"]
        a_ref[0, rows, :] = p[:, :POOL_DIM]
        tables = (cos_ref[rows, :], sin_ref[rows, :]) if rope else None
        for c0, width, gain_ref, dst in ((q0, GQA_Q_DIM, qg_ref, q_ref), (k0, GQA_KV_DIM, kg_ref, k_ref)):
            _norm_rope_store(_slabs(p, c0, width), same_ref[...], gain_ref[...], HEAD_DIM, dst, rows,
                             tables)
        v_ref[0, rows, :] = p[:, v0:v0 + GQA_KV_DIM].astype(BF16)

    _software_pipeline(x_ref.shape[1], PROJ0_SUB, [project, epilogue])


def _inproj0(x, mods, ctx_row, gain, w, same_head, q_gain, k_gain, cos, sin, *, rope, tm):
    b, t, d = x.shape
    row = lambda bi, i: (bi, i, 0)
    return pl.pallas_call(
        functools.partial(_inproj0_kernel, rope=rope),
        grid=(b, t // tm),
        in_specs=[pl.BlockSpec((1, tm, d), row),
                  _mod_spec(mods, 0, ctx_row), _mod_spec(mods, 1, ctx_row),
                  _resident(gain), _resident(w), _resident(same_head),
                  _resident(q_gain), _resident(k_gain),
                  pl.BlockSpec((tm, LANES), lambda bi, i: (i, 0)),
                  pl.BlockSpec((tm, LANES), lambda bi, i: (i, 0))],
        out_specs=[pl.BlockSpec((1, tm, POOL_DIM), row),
                   pl.BlockSpec((1, tm, GQA_Q_DIM), row),
                   pl.BlockSpec((1, tm, GQA_KV_DIM), row),
                   pl.BlockSpec((1, tm, GQA_KV_DIM), row)],
        out_shape=[jax.ShapeDtypeStruct((b, t, POOL_DIM), F32),
                   jax.ShapeDtypeStruct((b, t, GQA_Q_DIM), BF16),
                   jax.ShapeDtypeStruct((b, t, GQA_KV_DIM), BF16),
                   jax.ShapeDtypeStruct((b, t, GQA_KV_DIM), BF16)],
        compiler_params=_params(2),
        name="inproj0_rope" if rope else "inproj0_ctx",
    )(x, mods, mods, gain, w, same_head, q_gain, k_gain, cos, sin)


def _inproj1_kernel(x_ref, sh_ref, sc_ref, g_ref, w_ref, qag_ref, kvag_ref, wuq_ref, wukv_ref,
                    sh64_ref, shmla_ref, mqg_ref, mkg_ref, nqg_ref, nkg_ref, cos_ref, sin_ref,
                    *out_refs, rope, want_q):
    if want_q:
        mq_ref, mk_ref, mv_ref, nq_ref, nk_ref, nv_ref = out_refs
    else:
        mk_ref, mv_ref, nk_ref, nv_ref = out_refs
    o_ckv = MLA_Q_LORA
    o_kr = o_ckv + MLA_KV_LORA
    o_nq = o_kr + LANES
    o_nk = o_nq + NA_DIM
    o_nv = o_nk + NA_DIM
    mla_w = MLA_HEADS * LANES

    def project(rows, st):
        h = _modulate(x_ref[0, rows, :], g_ref[...], sh_ref[0, 0], sc_ref[0, 0]).astype(BF16)
        st["p"] = jnp.dot(h, w_ref[...], preferred_element_type=F32)

    def up_project(rows, st):
        p = st["p"]
        if want_q:
            st["qq"] = jnp.dot(_row_rmsnorm(p[:, :MLA_Q_LORA], qag_ref[...]).astype(BF16),
                               wuq_ref[...], preferred_element_type=F32)
        st["kv"] = jnp.dot(_row_rmsnorm(p[:, o_ckv:o_ckv + MLA_KV_LORA], kvag_ref[...]).astype(BF16),
                           wukv_ref[...], preferred_element_type=F32)

    def rope_tables(rows):
        return (cos_ref[rows, :], sin_ref[rows, :]) if rope else None

    def epilogue_q(rows, st):
        _norm_rope_store(_slabs(st["qq"], 0, mla_w), shmla_ref[...], mqg_ref[...], MLA_QK,
                         mq_ref, rows, rope_tables(rows))
        _norm_rope_store(_slabs(st["p"], o_nq, NA_DIM), sh64_ref[...], nqg_ref[...], NA_HEAD_DIM,
                         nq_ref, rows)

    def epilogue_kv(rows, st):
        p, kv = st["p"], st["kv"]
        tables = rope_tables(rows)
        kr_blk = p[:, o_kr:o_kr + LANES]
        kr_slab = jnp.concatenate([kr_blk, kr_blk], axis=1)
        _norm_rope_store([slab + kr_slab for slab in _slabs(kv, 0, mla_w)], shmla_ref[...],
                         mkg_ref[...], MLA_QK, mk_ref, rows, tables)
        mv_ref[0, rows, :] = kv[:, mla_w:mla_w + MLA_HEADS * MLA_V].astype(BF16)
        _norm_rope_store(_slabs(p, o_nk, NA_DIM), sh64_ref[...], nkg_ref[...], NA_HEAD_DIM,
                         nk_ref, rows)
        nv_ref[0, rows, :] = p[:, o_nv:o_nv + NA_DIM].astype(BF16)

    stages = [project, up_project] + ([epilogue_q] if want_q else []) + [epilogue_kv]
    _software_pipeline(x_ref.shape[1], PROJ1_SUB, stages)


def _inproj1(x, mods, ctx_row, gain, w, qag, kvag, wuq, wukv, sh64, shmla, mqg, mkg, nqg, nkg,
             cos, sin, *, rope, want_q, tm):
    b, t, d = x.shape
    row = lambda bi, i: (bi, i, 0)
    full = _resident
    mla_w = (MLA_HEADS // 2) * MLA_PAIR_W
    outs = [(mla_w, "mk"), (MLA_HEADS * MLA_V, "mv"), (NA_DIM, "nk"), (NA_DIM, "nv")]
    if want_q:
        outs = [(mla_w, "mq")] + outs[:2] + [(NA_DIM, "nq")] + outs[2:]
    return pl.pallas_call(
        functools.partial(_inproj1_kernel, rope=rope, want_q=want_q),
        grid=(b, t // tm),
        in_specs=[pl.BlockSpec((1, tm, d), row),
                  _mod_spec(mods, 0, ctx_row), _mod_spec(mods, 1, ctx_row),
                  full(gain), full(w), full(qag), full(kvag), full(wuq), full(wukv),
                  full(sh64), full(shmla), full(mqg), full(mkg), full(nqg), full(nkg),
                  pl.BlockSpec((tm, LANES), lambda bi, i: (i, 0)),
                  pl.BlockSpec((tm, LANES), lambda bi, i: (i, 0))],
        out_specs=[pl.BlockSpec((1, tm, n), row) for n, _ in outs],
        out_shape=[jax.ShapeDtypeStruct((b, t, n), BF16) for n, _ in outs],
        compiler_params=_params(2),
        name="inproj1_lat" if want_q else "inproj1_ctx",
    )(x, mods, mods, gain, w, qag, kvag, wuq, wukv, sh64, shmla, mqg, mkg, nqg, nkg, cos, sin)


def _transpose_on_mxu(x):
    n = x.shape[1]
    eye = lax.broadcasted_iota(jnp.int32, (n, n), 0) == lax.broadcasted_iota(jnp.int32, (n, n), 1)
    return lax.dot_general(jnp.where(eye, 1.0, 0.0).astype(BF16), x, _NT, preferred_element_type=F32)


def _fill_pair_kv_t(k, v, mask_a, mask_b, ka_ref, kb_ref, vat_ref, vbt_ref, r0):
    n = k.shape[0]
    if mask_a is None:
        ka_ref[r0:r0 + n, :] = k[:, :LANES]
        kb_ref[r0:r0 + n, :] = k[:, LANES:]
    else:
        ka_ref[r0:r0 + n, :] = k * mask_a
        kb_ref[r0:r0 + n, :] = k * mask_b
    vt = _transpose_on_mxu(v)
    first = lax.broadcasted_iota(jnp.int32, vt.shape, 0) < HEAD_DIM
    vat_ref[:, r0:r0 + n] = jnp.where(first, vt, 1.0).astype(BF16)
    vbt_ref[:, r0:r0 + n] = jnp.where(first, 1.0, vt).astype(BF16)


def _default_key_chunks(n_keys):
    return [(k0, min(ATTN_KEY_CHUNK, n_keys - k0), None) for k0 in range(0, n_keys, ATTN_KEY_CHUNK)]


def _pair_softmax_pv(groups, ka_ref, kb_ref, vat_ref, vbt_ref, emit):
    heads = ((ka_ref, vat_ref), (kb_ref, vbt_ref))
    jobs = [(g, j) for g, tiles in enumerate(groups) for j in range(len(tiles[0][1]))]
    queries = {}

    def scores(job):
        g, j = job
        if g not in queries:
            queries[g] = [load_q() for load_q, _, _ in groups[g]]
        out = []
        for (_, key_chunks, _), qs in zip(groups[g], queries[g]):
            k0, size, bias = key_chunks[j]
            for head, (q, (k_ref, _)) in enumerate(zip(qs, heads)):
                sc = lax.dot_general(k_ref[k0:k0 + size, :], q, _NT, preferred_element_type=F32)
                out.append((sc if bias is None else sc + bias(head)).astype(BF16))
        return out

    pending = [scores(job) for job in jobs[:ATTN_LOOKAHEAD]]
    for n, (g, j) in enumerate(jobs):
        if n + ATTN_LOOKAHEAD < len(jobs):
            pending.append(scores(jobs[n + ATTN_LOOKAHEAD]))
        s = pending.pop(0)
        if j == 0:
            run_max = [None] * len(s)
            acc = [None] * len(s)
        for t, (_, key_chunks, _) in enumerate(groups[g]):
            k0, size, _ = key_chunks[j]
            for head, (_, vt_ref) in enumerate(heads):
                c = 2 * t + head
                mx = s[c].max(axis=0, keepdims=True)
                if j > 0:
                    mx = jnp.maximum(mx, run_max[c])
                p = jnp.exp2(s[c] - mx)
                pv = jnp.dot(vt_ref[:, k0:k0 + size], p, preferred_element_type=F32)
                if j > 0:
                    acc[c] = acc[c] * jnp.exp2(run_max[c].astype(F32) - mx.astype(F32)) + pv
                else:
                    acc[c] = pv
                run_max[c] = mx
        if j == len(groups[g][0][1]) - 1:
            for t, (_, _, tag) in enumerate(groups[g]):
                oa, ob = acc[2 * t], acc[2 * t + 1]
                top = oa[:HEAD_DIM] / oa[HEAD_DIM:HEAD_DIM + 1]
                bot = ob[HEAD_DIM:] / ob[0:1]
                emit(tag, jnp.concatenate([top, bot], axis=0).T)


def _pair_attn_kernel(*refs, use_lat, steps_per_pair, dk, split):
    if use_lat:
        q_ref, ma_ref, mb_ref, kc_ref, vc_ref, kl_ref, vl_ref, o_ref, ka, kb, vat, vbt = refs
        segs = ((kc_ref, vc_ref), (kl_ref, vl_ref))
    else:
        q_ref, ma_ref, mb_ref, kc_ref, vc_ref, o_ref, ka, kb, vat, vbt = refs
        segs = ((kc_ref, vc_ref),)

    @pl.when(jnp.logical_and(pl.program_id(2) == 0, pl.program_id(1) % steps_per_pair == 0))
    def _():
        r0 = 0
        for k_ref, v_ref in segs:
            masks = (None, None) if split else (ma_ref[...], mb_ref[...])
            _fill_pair_kv_t(k_ref[0], v_ref[0], *masks, ka, kb, vat, vbt, r0)
            r0 += k_ref.shape[1]

    sub = min(q_ref.shape[1], ATTN_SUB)
    key_chunks = _default_key_chunks(ka.shape[0])

    def load_q(r0, blk):
        q = q_ref[0, r0:r0 + sub, blk * dk:(blk + 1) * dk]
        return (q[:, :LANES], q[:, LANES:]) if split else (q, q)

    def emit(tag, out):
        r0, blk = tag
        o_ref[0, r0:r0 + sub, blk * LANES:(blk + 1) * LANES] = out.astype(o_ref.dtype)

    tiles = [(functools.partial(load_q, r0, blk), key_chunks, (r0, blk))
             for blk in range(q_ref.shape[2] // dk) for r0 in range(0, q_ref.shape[1], sub)]
    groups = [tiles[i:i + ATTN_TILES_IN_FLIGHT] for i in range(0, len(tiles), ATTN_TILES_IN_FLIGHT)]
    _pair_softmax_pv(groups, ka, kb, vat, vbt, emit)


def _pair_attn(q, mask_a, mask_b, kc, vc, kl, vl, *, dk, n_pairs, q_per_pair, blocks_per_step, tq,
               name):
    b, t, _ = q.shape
    split = dk == 2 * LANES
    dk_head = dk // 2 if split else dk
    assert q_per_pair % blocks_per_step == 0
    steps_per_pair = q_per_pair // blocks_per_step
    n_qblk = n_pairs * steps_per_pair
    use_lat = kl is not None
    qmap = lambda bi, c, i: (bi, i, c)
    kvmap = lambda bi, c, i: (bi, 0, c // steps_per_pair)
    const = lambda bi, c, i: (0, 0)
    in_specs = [pl.BlockSpec((1, tq, blocks_per_step * dk), qmap),
                pl.BlockSpec((1, dk), const),
                pl.BlockSpec((1, dk), const),
                pl.BlockSpec((1, kc.shape[1], dk), kvmap),
                pl.BlockSpec((1, vc.shape[1], LANES), kvmap)]
    args = [q, mask_a, mask_b, kc, vc]
    if use_lat:
        in_specs += [pl.BlockSpec((1, kl.shape[1], dk), kvmap),
                     pl.BlockSpec((1, vl.shape[1], LANES), kvmap)]
        args += [kl, vl]
    n_keys = kc.shape[1] + (kl.shape[1] if use_lat else 0)
    return pl.pallas_call(
        functools.partial(_pair_attn_kernel, use_lat=use_lat, steps_per_pair=steps_per_pair, dk=dk,
                          split=split),
        grid=(b, n_qblk, t // tq),
        in_specs=in_specs,
        out_specs=pl.BlockSpec((1, tq, blocks_per_step * LANES), qmap),
        out_shape=jax.ShapeDtypeStruct((b, t, n_pairs * q_per_pair * LANES), BF16),
        scratch_shapes=[pltpu.VMEM((n_keys, dk_head), BF16), pltpu.VMEM((n_keys, dk_head), BF16),
                        pltpu.VMEM((LANES, n_keys), BF16), pltpu.VMEM((LANES, n_keys), BF16)],
        compiler_params=_params(3),
        name=name,
    )(*args)


def _na_block_tiles(u, n_blocks):
    t0 = min(max(u - 1, 0), n_blocks - NA_KEY_TILES)
    return t0, (0 if u == 0 else 2 if u == n_blocks - 1 else 1)


def _na_kernel(q_ref, kc_ref, vc_ref, k_ref, v_ref, band_ref, o_ref, ka, kb, vat, vbt, bias_ref,
               *, n_rows):
    j_idx, ok = _na_bias_plan(n_rows)

    @pl.when(pl.program_id(1) == 0)
    def _():
        masked = jnp.full((GRID_W, GRID_W), NEG_BIG, F32)
        for head, cls, ch, i, dr in np.ndindex(2, *j_idx.shape):
            half = (dr % 2) * GRID_W
            piece = (band_ref[head, int(j_idx[cls, ch, i, dr]), :, half:half + GRID_W]
                     if ok[cls, ch, i, dr] else masked)
            bias_ref[head, cls, ch, i * GRID_W:(i + 1) * GRID_W, dr * GRID_W:(dr + 1) * GRID_W] = piece

    tc = kc_ref.shape[1]
    lane = _lane((1, LANES))
    mask_a = jnp.where(lane < NA_HEAD_DIM, 1.0, 0.0).astype(BF16)
    mask_b = jnp.where(lane < NA_HEAD_DIM, 0.0, 1.0).astype(BF16)
    _fill_pair_kv_t(kc_ref[0], vc_ref[0], mask_a, mask_b, ka, kb, vat, vbt, 0)
    _fill_pair_kv_t(k_ref[0], v_ref[0], mask_a, mask_b, ka, kb, vat, vbt, tc)

    n_blocks = n_rows // NA_BLOCK_ROWS
    blk_q = NA_BLOCK_ROWS * GRID_W

    def load_q(u):
        q = q_ref[0, u * blk_q:(u + 1) * blk_q, :]
        return q, q

    def emit(u, out):
        o_ref[0, u * blk_q:(u + 1) * blk_q, :] = out.astype(o_ref.dtype)

    by_len = {}
    for u in range(n_blocks):
        t0, cls = _na_block_tiles(u, n_blocks)
        chunks = [(0, tc, None)]
        for ch in range(NA_KEY_TILES):
            if ok[cls, ch].any():
                chunks.append((tc + (t0 + ch) * blk_q, blk_q,
                               functools.partial(lambda head, cls, ch: bias_ref[head, cls, ch],
                                                 cls=cls, ch=ch)))
        by_len.setdefault(len(chunks), []).append((functools.partial(load_q, u), chunks, u))
    groups = [tiles[i:i + ATTN_TILES_IN_FLIGHT] for tiles in by_len.values()
              for i in range(0, len(tiles), ATTN_TILES_IN_FLIGHT)]
    _pair_softmax_pv(groups, ka, kb, vat, vbt, emit)


def _na_attn(q, kc, vc, k, v, band):
    b, s, _ = q.shape
    n_rows = s // GRID_W
    n_pairs = NA_HEADS // 2
    n_keys = kc.shape[1] + s
    blk_q = NA_BLOCK_ROWS * GRID_W
    assert kc.shape[1] == ATTN_KEY_CHUNK and blk_q == ATTN_KEY_CHUNK
    blk = lambda rows: pl.BlockSpec((1, rows, LANES), lambda m, bi: (bi, 0, m))
    return pl.pallas_call(
        functools.partial(_na_kernel, n_rows=n_rows),
        grid=(n_pairs, b),
        in_specs=[blk(s), blk(kc.shape[1]), blk(vc.shape[1]), blk(s), blk(s),
                  pl.BlockSpec((2,) + band.shape[1:], lambda m, bi: (m, 0, 0, 0))],
        out_specs=blk(s),
        out_shape=jax.ShapeDtypeStruct((b, s, NA_DIM), BF16),
        scratch_shapes=[pltpu.VMEM((n_keys, LANES), BF16), pltpu.VMEM((n_keys, LANES), BF16),
                        pltpu.VMEM((LANES, n_keys), BF16), pltpu.VMEM((LANES, n_keys), BF16),
                        pltpu.VMEM((2, 3, NA_KEY_TILES, blk_q, blk_q), F32)],
        compiler_params=_params(2),
        name="na_attn",
    )(q, kc, vc, k, v, band)


def _pool_kernel(a_ref, pw_ref, ps_ref, o_ref, pad_ref, *, t_len, chunk):
    pad_ref[0:POOL_HALO, :] = jnp.zeros((POOL_HALO, POOL_DIM), F32)
    pad_ref[POOL_HALO + t_len:2 * POOL_HALO + t_len, :] = jnp.zeros((POOL_HALO, POOL_DIM), F32)
    pad_ref[POOL_HALO:POOL_HALO + t_len, :] = a_ref[0]
    lane = _lane((chunk, POOL_DIM))
    g0 = lane < POOL_CH
    g1 = lane < 2 * POOL_CH
    g2 = lane < 3 * POOL_CH
    half_w = jnp.where(g0, 1, jnp.where(g1, 2, jnp.where(g2, 4, 8)))
    for c in range(t_len // chunk):
        base = c * chunk

        def ld(off, base=base):
            return pad_ref[POOL_HALO + base + off:POOL_HALO + base + off + chunk, :]

        a0 = ld(0)
        w2 = ld(-1) + a0
        w4 = w2 + ld(-2) + ld(1)
        w8 = w4 + ld(-4) + ld(-3) + ld(2) + ld(3)
        w16 = w8
        for off in (-8, -7, -6, -5, 4, 5, 6, 7):
            w16 = w16 + ld(off)
        tok = base + lax.broadcasted_iota(jnp.int32, (chunk, POOL_DIM), 0)
        cnt = (jnp.minimum(tok + half_w, t_len) - jnp.maximum(tok - half_w, 0)).astype(F32)
        wsum = jnp.where(g0, w2, jnp.where(g1, w4, jnp.where(g2, w8, w16)))
        dlt = wsum / cnt - a0
        y = jnp.dot(dlt.astype(BF16), pw_ref[...], preferred_element_type=F32) * ps_ref[...]
        o_ref[0, base:base + chunk, :] = y.astype(o_ref.dtype)


def _pool(a, pw_bd, pscale):
    b, t, _ = a.shape
    chunk = min(t, 256)
    return pl.pallas_call(
        functools.partial(_pool_kernel, t_len=t, chunk=chunk),
        grid=(b,),
        in_specs=[pl.BlockSpec((1, t, POOL_DIM), lambda bi: (bi, 0, 0)),
                  pl.BlockSpec((POOL_DIM, POOL_DIM), lambda bi: (0, 0)),
                  pl.BlockSpec((1, POOL_DIM), lambda bi: (0, 0))],
        out_specs=pl.BlockSpec((1, t, POOL_DIM), lambda bi: (bi, 0, 0)),
        out_shape=jax.ShapeDtypeStruct((b, t, POOL_DIM), BF16),
        scratch_shapes=[pltpu.VMEM((t + 2 * POOL_HALO, POOL_DIM), F32)],
        compiler_params=_params(1),
        name="pool",
    )(a, pw_bd, pscale)


def _post_kernel(x_ref, g1_ref, sh_ref, sc_ref, g2_ref, m0_ref, m1_ref, w0_ref, w1_ref, g_ref,
                 wg_ref, wu_ref, wd_ref, o_ref):
    n_rows = x_ref.shape[1]
    sub = min(n_rows, POST_SUB)
    tiles = [slice(r, r + sub) for r in range(0, n_rows, sub)]
    n_chunks = wg_ref.shape[1] // FFN_CHUNK

    mixes = [jnp.dot(m0_ref[0, rows, :], w0_ref[...], preferred_element_type=F32)
             + jnp.dot(m1_ref[0, rows, :], w1_ref[...], preferred_element_type=F32) for rows in tiles]
    xs = [x_ref[0, rows, :] + g1_ref[0, 0] * mix for rows, mix in zip(tiles, mixes)]
    hs = [_modulate(x, g_ref[...], sh_ref[0, 0], sc_ref[0, 0]).astype(BF16) for x in xs]

    def gate_up(c):
        cols = slice(c * FFN_CHUNK, (c + 1) * FFN_CHUNK)
        return [(jnp.dot(h, wg_ref[:, cols], preferred_element_type=F32),
                 jnp.dot(h, wu_ref[:, cols], preferred_element_type=F32)) for h in hs]

    accs = [None] * len(tiles)
    cur = gate_up(0)
    for c in range(n_chunks):
        nxt = gate_up(c + 1) if c + 1 < n_chunks else None
        for t, (gt, up) in enumerate(cur):
            act = (gt * jax.nn.sigmoid(gt) * up).astype(BF16)
            part = jnp.dot(act, wd_ref[c * FFN_CHUNK:(c + 1) * FFN_CHUNK, :],
                           preferred_element_type=F32)
            accs[t] = part if accs[t] is None else accs[t] + part
        cur = nxt
    for rows, x, acc in zip(tiles, xs, accs):
        o_ref[0, rows, :] = x + g2_ref[0, 0] * acc


def _mod_spec(mods, k, ctx_row):
    blk = (1, 1, 1, mods.shape[-1])
    if ctx_row is None:
        return pl.BlockSpec(blk, lambda bi, i: (bi, k, 0, 0))
    return pl.BlockSpec(blk, lambda bi, i: (ctx_row, k, 0, 0))


def _resident(a):
    nd = a.ndim
    return pl.BlockSpec(a.shape, lambda bi, i: (0,) * nd, pipeline_mode=pl.Buffered(1))


def _post(x, mods, ctx_row, m0, m1, w0, w1, gain, wg, wu, wd, *, tm, name):
    b, t, d = x.shape
    row = lambda bi, i: (bi, i, 0)
    return pl.pallas_call(
        _post_kernel,
        grid=(b, t // tm),
        in_specs=[pl.BlockSpec((1, tm, d), row),
                  _mod_spec(mods, 2, ctx_row), _mod_spec(mods, 3, ctx_row),
                  _mod_spec(mods, 4, ctx_row), _mod_spec(mods, 5, ctx_row),
                  pl.BlockSpec((1, tm, m0.shape[2]), row),
                  pl.BlockSpec((1, tm, m1.shape[2]), row),
                  _resident(w0), _resident(w1), _resident(gain),
                  _resident(wg), _resident(wu), _resident(wd)],
        out_specs=pl.BlockSpec((1, tm, d), row),
        out_shape=jax.ShapeDtypeStruct((b, t, d), F32),
        compiler_params=_params(2),
        name=name,
    )(x, mods, mods, mods, mods, m0, m1, w0, w1, gain, wg, wu, wd)


class _LaneLayout:
    def __init__(self, dim, slot, axis, freq, x2, n_freq):
        self.dim, self.slot, self.axis, self.freq, self.x2, self.n_freq = dim, slot, axis, freq, x2, n_freq

    def rope_tables(self, seq):
        tok = np.arange(seq)
        inv = (ROPE_THETA ** (-np.arange(self.n_freq, dtype=np.float32) * 2.0
                              / (2 * self.n_freq))).astype(np.float32).astype(np.float64)
        pos = np.stack([tok // GRID_W, tok % GRID_W]).astype(np.float64)
        rot = self.axis >= 0
        ang = pos[np.maximum(self.axis, 0)].T * inv[self.freq][None, :]
        cos = np.where(rot[None], np.cos(ang), 1.0)
        sin = np.where(rot[None], np.sin(ang) * np.where(self.x2, 1.0, -1.0)[None], 0.0)
        return jnp.asarray(cos, F32), jnp.asarray(sin, F32)


def _gqa_layout():
    lane = np.arange(LANES)
    x2, slot, axis, freq = lane // 64, (lane % 64) // 32, (lane % 32) // 16, lane % 16
    return _LaneLayout((axis * 2 + x2) * 16 + freq, slot, axis, freq, x2.astype(bool), HEAD_DIM // 4)


def _mla_layout():
    lane = np.arange(LANES)
    is_rot = (lane % 64) < 16
    x2 = lane >= 64
    axis = np.where(is_rot, (lane % 64) // 8, -1)
    freq = np.where(is_rot, lane % 8, 0)
    nope = np.where(lane < 64, lane - 16, 48 + lane - 80)
    dim = np.where(is_rot, MLA_NOPE + (np.maximum(axis, 0) * 2 + x2) * 8 + freq, nope)
    dim = np.where(lane >= MLA_QK, -1, dim)
    return _LaneLayout(dim, np.zeros(LANES, int), axis, freq, x2 & is_rot, MLA_ROPE // 4)


def _take_cols(w, idx):
    return jnp.where(jnp.asarray(idx >= 0)[None, :], w[:, np.maximum(idx, 0)], 0.0)


def _na_bias_plan(n_rows):
    n_blocks = n_rows // NA_BLOCK_ROWS
    j_idx = np.zeros((3, NA_KEY_TILES, NA_BLOCK_ROWS, NA_BLOCK_ROWS), np.int32)
    ok = np.zeros(j_idx.shape, bool)
    for cls, u in enumerate((0, 1, n_blocks - 1)):
        t0, cls_u = _na_block_tiles(u, n_blocks)
        assert cls_u == cls
        for ch in range(NA_KEY_TILES):
            for i in range(NA_BLOCK_ROWS):
                for dr in range(NA_BLOCK_ROWS):
                    r, kr = NA_BLOCK_ROWS * u + dr, NA_BLOCK_ROWS * (t0 + ch) + i
                    r0 = min(max(r - NA_ROWS // 2, 0), n_rows - NA_ROWS)
                    ok[cls, ch, i, dr] = r0 <= kr < r0 + NA_ROWS
                    j_idx[cls, ch, i, dr] = kr - r + NA_ROWS - 1
    return j_idx, ok


def _na_band_table(rpb):
    cols = np.arange(GRID_W)
    c0 = np.clip(cols - NA_COLS // 2, 0, GRID_W - NA_COLS)
    kc = np.arange(GRID_W)
    inside = (kc[None, :] >= c0[:, None]) & (kc[None, :] < c0[:, None] + NA_COLS)
    dc = kc[None, :] - cols[:, None] + (NA_COLS - 1)
    onehot = (np.arange(2 * NA_COLS - 1)[:, None, None] == dc[None]) & inside[None]
    sel = jnp.einsum("hjd,dck->hjkc", rpb * LOG2E, jnp.asarray(onehot, F32),
                     precision=lax.Precision.HIGHEST)
    band = jnp.where(jnp.asarray(inside.T)[None, None], sel, NEG_BIG).astype(F32)
    return jnp.concatenate([band, band], axis=-1)


def kernel(x, c, ctx, c_ctx, l0_ada_w, l0_ada_b, l0_norm_mix, l0_norm_ffn, l0_w_in, l0_pool_w, l0_pool_scale, l0_q_gain, l0_k_gain, l0_w_out, l0_ffn_w_gate, l0_ffn_w_up, l0_ffn_w_down, l1_ada_w, l1_ada_b, l1_norm_mix, l1_norm_ffn, l1_w_in, l1_mla_q_a_gain, l1_mla_kv_a_gain, l1_mla_w_uq, l1_mla_w_ukv, l1_mla_q_gain, l1_mla_k_gain, l1_na_q_gain, l1_na_k_gain, l1_na_rpb, l1_w_out, l1_ffn_w_gate, l1_ffn_w_up, l1_ffn_w_down):
    b, s, d = x.shape
    tc = ctx.shape[1]
    tm_lat = 1024
    tp_lat = 1024
    tq_lat = 2048

    cond = jnp.concatenate([c, c_ctx[None, :], jnp.zeros((7, d), F32)], axis=0)

    def mods(ada_w, ada_b):
        return _ada(cond, ada_w, ada_b).reshape(cond.shape[0], 6, 1, d)

    def ffn_weights(wg, wu, wd):
        return wg.astype(BF16), wu.astype(BF16), wd.astype(BF16)

    row1 = lambda v: v.reshape(1, -1)
    tile4 = lambda v, mul: (jnp.tile(v, SLAB // v.shape[0]) * mul).reshape(1, -1)
    sh64 = jnp.asarray(np.kron(np.eye(SLAB // HEAD_DIM), np.ones((HEAD_DIM, HEAD_DIM))), BF16)

    def slab_gain(g, lay, mul):
        per_lane = jnp.where(jnp.asarray(lay.dim >= 0), g[np.maximum(lay.dim, 0)], 0.0) * mul
        return jnp.tile(per_lane, SLAB // LANES).reshape(1, -1)

    def slab_same_head(lay):
        head = np.concatenate([np.where(lay.dim >= 0, blk * 2 + lay.slot, -1)
                               for blk in range(SLAB // LANES)])
        return jnp.asarray((head[:, None] == head[None, :]) & (head[:, None] >= 0), BF16)

    def pair_order(w_q, axis):
        shp = w_q.shape
        split = shp[:axis] + (GQA_KV_HEADS // 2, 2, GQA_GROUP, HEAD_DIM) + shp[axis + 1:]
        perm = list(range(len(split)))
        perm[axis + 1], perm[axis + 2] = perm[axis + 2], perm[axis + 1]
        return w_q.reshape(split).transpose(perm).reshape(shp)

    mods0 = mods(l0_ada_w, l0_ada_b)
    glay = _gqa_layout()
    q_cols = np.concatenate([((2 * m + glay.slot) * GQA_GROUP + g) * HEAD_DIM + glay.dim
                             for m in range(GQA_KV_HEADS // 2) for g in range(GQA_GROUP)])
    k_cols = np.concatenate([(2 * m + glay.slot) * HEAD_DIM + glay.dim
                             for m in range(GQA_KV_HEADS // 2)])
    q0, k0, v0 = POOL_DIM, POOL_DIM + GQA_Q_DIM, POOL_DIM + GQA_Q_DIM + GQA_KV_DIM
    w_in0 = l0_w_in[:, np.concatenate([np.arange(q0), q0 + q_cols, k0 + k_cols,
                                       np.arange(v0, l0_w_in.shape[1])])].astype(BF16)
    cos0, sin0 = glay.rope_tables(s)
    n_ctx = b * tc
    flat = lambda a: a.reshape(1, n_ctx, a.shape[-1])
    unflat = lambda a: a.reshape(b, tc, a.shape[-1])
    ones_c = jnp.ones((n_ctx, LANES), F32)
    zeros_c = jnp.zeros((n_ctx, LANES), F32)
    qg4 = slab_gain(l0_q_gain, glay, HEAD_DIM ** 0.5 * GQA_QSCALE)
    kg4 = slab_gain(l0_k_gain, glay, HEAD_DIM ** 0.5)
    sh_gqa = slab_same_head(glay)
    gmask_a = jnp.asarray((glay.slot == 0)[None], BF16)
    gmask_b = 1 - gmask_a

    a_l, q_l, k_l, v_l = _inproj0(x, mods0, None, row1(l0_norm_mix), w_in0, sh_gqa, qg4, kg4,
                                  cos0, sin0, rope=True, tm=tp_lat)
    a_c, q_c, k_c, v_c = map(unflat, _inproj0(flat(ctx), mods0, b, row1(l0_norm_mix), w_in0, sh_gqa,
                                               qg4, kg4, ones_c, zeros_c, rope=False,
                                               tm=min(n_ctx, tp_lat)))
    n_kv_pairs = GQA_KV_HEADS // 2
    attn_l = _pair_attn(q_l, gmask_a, gmask_b, k_c, v_c, k_l, v_l, dk=LANES, n_pairs=n_kv_pairs,
                        q_per_pair=GQA_GROUP, blocks_per_step=GQA_GROUP, tq=tq_lat, name="gqa_lat")
    attn_c = _pair_attn(q_c, gmask_a, gmask_b, k_c, v_c, None, None, dk=LANES, n_pairs=n_kv_pairs,
                        q_per_pair=GQA_GROUP, blocks_per_step=GQA_GROUP, tq=tc, name="gqa_ctx")

    eye = jnp.eye(POOL_GROUPS, dtype=F32)
    pw_bd = (eye[:, None, :, None] * l0_pool_w[:, :, None, :]).reshape(POOL_DIM, POOL_DIM).astype(BF16)
    pool_l = _pool(a_l, pw_bd, row1(l0_pool_scale))
    pool_c = _pool(a_c, pw_bd, row1(l0_pool_scale))

    w_out0_pool = l0_w_out[:POOL_DIM].astype(BF16)
    w_out0_attn = pair_order(l0_w_out[POOL_DIM:], 0).astype(BF16)
    ffn0 = ffn_weights(l0_ffn_w_gate, l0_ffn_w_up, l0_ffn_w_down)

    x1 = _post(x, mods0, None, pool_l, attn_l, w_out0_pool, w_out0_attn, row1(l0_norm_ffn), *ffn0,
               tm=tm_lat, name="post0_lat")
    xc = _post(flat(ctx), mods0, b, flat(pool_c), flat(attn_c), w_out0_pool, w_out0_attn,
               row1(l0_norm_ffn), *ffn0, tm=min(n_ctx, tm_lat), name="post0_ctx")

    mods1 = mods(l1_ada_w, l1_ada_b)
    mlay = _mla_layout()
    heads = np.arange(MLA_HEADS)[:, None]
    o_kr = MLA_Q_LORA + MLA_KV_LORA
    kr_cols = np.where(mlay.dim >= MLA_NOPE, o_kr + mlay.dim - MLA_NOPE, -1)
    w_in1 = jnp.concatenate([l1_w_in[:, :o_kr].astype(BF16), _take_cols(l1_w_in, kr_cols).astype(BF16),
                             l1_w_in[:, o_kr + MLA_ROPE:].astype(BF16)], axis=1)
    wuq = _take_cols(l1_mla_w_uq, np.where(mlay.dim >= 0, heads * MLA_QK + mlay.dim, -1).reshape(-1)
                     ).astype(BF16)
    kv_w = MLA_NOPE + MLA_V
    k_idx = np.where((mlay.dim >= 0) & (mlay.dim < MLA_NOPE), heads * kv_w + mlay.dim, -1).reshape(-1)
    v_idx = (heads * kv_w + MLA_NOPE + np.arange(MLA_V)[None, :]).reshape(-1)
    wukv = _take_cols(l1_mla_w_ukv, np.concatenate([k_idx, v_idx])).astype(BF16)

    shmla = slab_same_head(mlay)
    mqg = slab_gain(l1_mla_q_gain, mlay, MLA_QK ** 0.5 * MLA_QSCALE)
    mkg = slab_gain(l1_mla_k_gain, mlay, MLA_QK ** 0.5)
    nqg = tile4(l1_na_q_gain, NA_HEAD_DIM ** 0.5 * NA_QSCALE)
    nkg = tile4(l1_na_k_gain, NA_HEAD_DIM ** 0.5)
    cos1, sin1 = mlay.rope_tables(s)
    common = (row1(l1_norm_mix), w_in1, row1(l1_mla_q_a_gain), row1(l1_mla_kv_a_gain), wuq, wukv,
              sh64, shmla, mqg, mkg, nqg, nkg)
    mq, mk, mv, nq, nk, nv = _inproj1(x1, mods1, None, *common, cos1, sin1, rope=True, want_q=True,
                                      tm=tp_lat)
    mkc, mvc, nkc, nvc = map(unflat, _inproj1(xc, mods1, b, *common, ones_c, zeros_c, rope=False,
                                              want_q=False, tm=min(n_ctx, tp_lat)))

    no_mask = jnp.ones((1, MLA_PAIR_W), BF16)
    o_mla = _pair_attn(mq, no_mask, no_mask, mkc, mvc, mk, mv, dk=MLA_PAIR_W,
                       n_pairs=MLA_HEADS // 2, q_per_pair=1, blocks_per_step=1, tq=tq_lat, name="mla")
    o_na = _na_attn(nq, nkc, nvc, nk, nv, _na_band_table(l1_na_rpb))

    w_out1 = l1_w_out.astype(BF16)
    ffn1 = ffn_weights(l1_ffn_w_gate, l1_ffn_w_up, l1_ffn_w_down)
    return _post(x1, mods1, None, o_mla, o_na, w_out1[:MLA_HEADS * MLA_V], w_out1[MLA_HEADS * MLA_V:],
                 row1(l1_norm_ffn), *ffn1, tm=tm_lat, name="post1_lat")
```

```python
import functools

import numpy as np
import jax
import jax.numpy as jnp
from jax import lax
from jax.experimental import pallas as pl
from jax.experimental.pallas import tpu as pltpu

F32 = jnp.float32
BF16 = jnp.bfloat16

D_MODEL = 1024
GRID_W = 64
ROPE_THETA = 10000.0
EPS = 1e-6
LANES = 128
SLAB = 256

POOL_GROUPS = 4
POOL_CH = 64
POOL_DIM = POOL_GROUPS * POOL_CH
POOL_HALO = 16

HEAD_DIM = 64
GQA_HEADS = 12
GQA_KV_HEADS = 4
GQA_GROUP = GQA_HEADS // GQA_KV_HEADS
GQA_Q_DIM = GQA_HEADS * HEAD_DIM
GQA_KV_DIM = GQA_KV_HEADS * HEAD_DIM

MLA_HEADS = 8
MLA_NOPE = 64
MLA_ROPE = 32
MLA_QK = MLA_NOPE + MLA_ROPE
MLA_V = 64
MLA_Q_LORA = 384
MLA_KV_LORA = 256
MLA_PAIR_W = 2 * LANES

NA_HEADS = 8
NA_HEAD_DIM = 64
NA_DIM = NA_HEADS * NA_HEAD_DIM
NA_ROWS = 8
NA_COLS = 16
NA_BLOCK_ROWS = 4
NA_KEY_TILES = 3

FFN_DIM = -(-8 * D_MODEL // (3 * 256)) * 256
FFN_CHUNK = 256

NEG_BIG = -1e30
LOG2E = 1.4426950408889634
GQA_QSCALE = HEAD_DIM ** -0.5 * LOG2E
MLA_QSCALE = MLA_QK ** -0.5 * LOG2E
NA_QSCALE = NA_HEAD_DIM ** -0.5 * LOG2E
POST_SUB = 256
PROJ0_SUB = 128
PROJ1_SUB = 256
ATTN_SUB = 256
ATTN_KEY_CHUNK = 256
ATTN_TILES_IN_FLIGHT = 2
ATTN_LOOKAHEAD = 1

VMEM_LIMIT = 56 * 1024 * 1024

_NT = (((1,), (1,)), ((), ()))


def _params(n_axes):
    return pltpu.CompilerParams(dimension_semantics=("arbitrary",) * n_axes,
                                vmem_limit_bytes=VMEM_LIMIT)


def _modulate(x, gain, shift, scale):
    ms = jnp.mean(x * x, axis=-1, keepdims=True)
    return x * lax.rsqrt(ms + EPS) * (gain * (1.0 + scale)) + shift


def _row_rmsnorm(x, gain):
    ms = jnp.mean(x * x, axis=-1, keepdims=True)
    return x * lax.rsqrt(ms + EPS) * gain


def _lane(shape):
    return lax.broadcasted_iota(jnp.int32, shape, 1)


def _head_rmsnorm(slab, same_head, gain, n):
    ssq = jnp.dot((slab * slab).astype(BF16), same_head, preferred_element_type=F32)
    return slab * lax.rsqrt(ssq + n * EPS) * gain


def _rope(blk, cos, sin_signed):
    return blk * cos + pltpu.roll(blk, LANES // 2, 1) * sin_signed


def _ada_kernel(c_ref, w_ref, b_ref, o_ref):
    cnd = c_ref[...]
    act = cnd * jax.nn.sigmoid(cnd)
    o_ref[...] = jnp.dot(act.astype(BF16), w_ref[...].astype(BF16),
                         preferred_element_type=F32) + b_ref[...]


def _ada(cond, w, b):
    n_rows, d = cond.shape
    n = w.shape[1]
    tn = 2048
    return pl.pallas_call(
        _ada_kernel,
        grid=(n // tn,),
        in_specs=[pl.BlockSpec((n_rows, d), lambda j: (0, 0)),
                  pl.BlockSpec((d, tn), lambda j: (0, j)),
                  pl.BlockSpec((1, tn), lambda j: (0, j))],
        out_specs=pl.BlockSpec((n_rows, tn), lambda j: (0, j)),
        out_shape=jax.ShapeDtypeStruct((n_rows, n), F32),
        compiler_params=_params(1),
        name="ada",
    )(cond, w, b.reshape(1, n))


def _slabs(src, c0, width):
    return [src[:, c0 + j * SLAB:c0 + (j + 1) * SLAB] for j in range(width // SLAB)]


def _norm_rope_store(slabs, same_head, gain, n, dst_ref, rows, rope_tables=None):
    for j, slab in enumerate(slabs):
        slab = _head_rmsnorm(slab, same_head, gain, n)
        for blk in range(SLAB // LANES):
            part = slab[:, blk * LANES:(blk + 1) * LANES]
            if rope_tables is not None:
                part = _rope(part, *rope_tables)
            o = j * SLAB + blk * LANES
            dst_ref[0, rows, o:o + LANES] = part.astype(BF16)


def _software_pipeline(n_rows, sub, stages):
    sub = min(n_rows, sub)
    tiles = [slice(r, r + sub) for r in range(0, n_rows, sub)]
    states = [{} for _ in tiles]
    for it in range(len(tiles) + len(stages) - 1):
        for s, stage in reversed(list(enumerate(stages))):
            t = it - s
            if 0 <= t < len(tiles):
                stage(tiles[t], states[t])


def _inproj0_kernel(x_ref, sh_ref, sc_ref, g_ref, w_ref, same_ref, qg_ref, kg_ref, cos_ref, sin_ref,
                    a_ref, q_ref, k_ref, v_ref, *, rope):
    q0 = POOL_DIM
    k0 = q0 + GQA_Q_DIM
    v0 = k0 + GQA_KV_DIM

    def project(rows, st):
        h = _modulate(x_ref[0, rows, :], g_ref[...], sh_ref[0, 0], sc_ref[0, 0]).astype(BF16)
        part = lambda c0, c1: jnp.dot(h, w_ref[:, c0:c1], preferred_element_type=F32)
        a_ref[0, rows, :] = part(0, q0)
        st["qk"] = part(q0, v0)
        v_ref[0, rows, :] = part(v0, v0 + GQA_KV_DIM).astype(BF16)

    def epilogue(rows, st):
        p = st["qk"]
        tables = (cos_ref[rows, :], sin_ref[rows, :]) if rope else None
        for c0, width, gain_ref, dst in ((0, GQA_Q_DIM, qg_ref, q_ref),
                                         (GQA_Q_DIM, GQA_KV_DIM, kg_ref, k_ref)):
            _norm_rope_store(_slabs(p, c0, width), same_ref[...], gain_ref[...], HEAD_DIM, dst, rows,
                             tables)

    _software_pipeline(x_ref.shape[1], PROJ0_SUB, [project, epilogue])


def _inproj0(x, mods, ctx_row, gain, w, same_head, q_gain, k_gain, cos, sin, *, rope, tm):
    b, t, d = x.shape
    row = lambda bi, i: (bi, i, 0)
    return pl.pallas_call(
        functools.partial(_inproj0_kernel, rope=rope),
        grid=(b, t // tm),
        in_specs=[pl.BlockSpec((1, tm, d), row),
                  _mod_spec(mods, 0, ctx_row), _mod_spec(mods, 1, ctx_row),
                  _resident(gain), _resident(w), _resident(same_head),
                  _resident(q_gain), _resident(k_gain),
                  pl.BlockSpec((tm, LANES), lambda bi, i: (i, 0)),
                  pl.BlockSpec((tm, LANES), lambda bi, i: (i, 0))],
        out_specs=[pl.BlockSpec((1, tm, POOL_DIM), row),
                   pl.BlockSpec((1, tm, GQA_Q_DIM), row),
                   pl.BlockSpec((1, tm, GQA_KV_DIM), row),
                   pl.BlockSpec((1, tm, GQA_KV_DIM), row)],
        out_shape=[jax.ShapeDtypeStruct((b, t, POOL_DIM), F32),
                   jax.ShapeDtypeStruct((b, t, GQA_Q_DIM), BF16),
                   jax.ShapeDtypeStruct((b, t, GQA_KV_DIM), BF16),
                   jax.ShapeDtypeStruct((b, t, GQA_KV_DIM), BF16)],
        compiler_params=_params(2),
        name="inproj0_rope" if rope else "inproj0_ctx",
    )(x, mods, mods, gain, w, same_head, q_gain, k_gain, cos, sin)


def _inproj1_kernel(x_ref, sh_ref, sc_ref, g_ref, w_ref, qag_ref, kvag_ref, wuq_ref, wukv_ref,
                    sh64_ref, shmla_ref, mqg_ref, mkg_ref, nqg_ref, nkg_ref, cos_ref, sin_ref,
                    *out_refs, rope, want_q):
    if want_q:
        mq_ref, mk_ref, mv_ref, nq_ref, nk_ref, nv_ref = out_refs
    else:
        mk_ref, mv_ref, nk_ref, nv_ref = out_refs
    o_ckv = MLA_Q_LORA
    o_kr = o_ckv + MLA_KV_LORA
    o_nq = o_kr + LANES
    o_nk = o_nq + NA_DIM
    o_nv = o_nk + NA_DIM
    mla_w = MLA_HEADS * LANES

    def project(rows, st):
        h = _modulate(x_ref[0, rows, :], g_ref[...], sh_ref[0, 0], sc_ref[0, 0]).astype(BF16)
        part = lambda c0, c1: jnp.dot(h, w_ref[:, c0:c1], preferred_element_type=F32)
        st["p"] = part(0, o_nq)
        if want_q:
            st["nq"] = part(o_nq, o_nk)
        st["nk"] = part(o_nk, o_nv)
        nv_ref[0, rows, :] = part(o_nv, o_nv + NA_DIM).astype(BF16)

    def up_project(rows, st):
        p = st["p"]
        if want_q:
            st["qq"] = jnp.dot(_row_rmsnorm(p[:, :MLA_Q_LORA], qag_ref[...]).astype(BF16),
                               wuq_ref[...], preferred_element_type=F32)
        st["kv"] = jnp.dot(_row_rmsnorm(p[:, o_ckv:o_ckv + MLA_KV_LORA], kvag_ref[...]).astype(BF16),
                           wukv_ref[...], preferred_element_type=F32)

    def rope_tables(rows):
        return (cos_ref[rows, :], sin_ref[rows, :]) if rope else None

    def epilogue_q(rows, st):
        _norm_rope_store(_slabs(st["qq"], 0, mla_w), shmla_ref[...], mqg_ref[...], MLA_QK,
                         mq_ref, rows, rope_tables(rows))
        _norm_rope_store(_slabs(st["nq"], 0, NA_DIM), sh64_ref[...], nqg_ref[...], NA_HEAD_DIM,
                         nq_ref, rows)

    def epilogue_kv(rows, st):
        p, kv = st["p"], st["kv"]
        tables = rope_tables(rows)
        kr_blk = p[:, o_kr:o_kr + LANES]
        kr_slab = jnp.concatenate([kr_blk, kr_blk], axis=1)
        _norm_rope_store([slab + kr_slab for slab in _slabs(kv, 0, mla_w)], shmla_ref[...],
                         mkg_ref[...], MLA_QK, mk_ref, rows, tables)
        mv_ref[0, rows, :] = kv[:, mla_w:mla_w + MLA_HEADS * MLA_V].astype(BF16)
        _norm_rope_store(_slabs(st["nk"], 0, NA_DIM), sh64_ref[...], nkg_ref[...], NA_HEAD_DIM,
                         nk_ref, rows)

    stages = [project, up_project] + ([epilogue_q] if want_q else []) + [epilogue_kv]
    _software_pipeline(x_ref.shape[1], PROJ1_SUB, stages)


def _inproj1(x, mods, ctx_row, gain, w, qag, kvag, wuq, wukv, sh64, shmla, mqg, mkg, nqg, nkg,
             cos, sin, *, rope, want_q, tm):
    b, t, d = x.shape
    row = lambda bi, i: (bi, i, 0)
    full = _resident
    mla_w = (MLA_HEADS // 2) * MLA_PAIR_W
    outs = [(mla_w, "mk"), (MLA_HEADS * MLA_V, "mv"), (NA_DIM, "nk"), (NA_DIM, "nv")]
    if want_q:
        outs = [(mla_w, "mq")] + outs[:2] + [(NA_DIM, "nq")] + outs[2:]
    return pl.pallas_call(
        functools.partial(_inproj1_kernel, rope=rope, want_q=want_q),
        grid=(b, t // tm),
        in_specs=[pl.BlockSpec((1, tm, d), row),
                  _mod_spec(mods, 0, ctx_row), _mod_spec(mods, 1, ctx_row),
                  full(gain), full(w), full(qag), full(kvag), full(wuq), full(wukv),
                  full(sh64), full(shmla), full(mqg), full(mkg), full(nqg), full(nkg),
                  pl.BlockSpec((tm, LANES), lambda bi, i: (i, 0)),
                  pl.BlockSpec((tm, LANES), lambda bi, i: (i, 0))],
        out_specs=[pl.BlockSpec((1, tm, n), row) for n, _ in outs],
        out_shape=[jax.ShapeDtypeStruct((b, t, n), BF16) for n, _ in outs],
        compiler_params=_params(2),
        name="inproj1_lat" if want_q else "inproj1_ctx",
    )(x, mods, mods, gain, w, qag, kvag, wuq, wukv, sh64, shmla, mqg, mkg, nqg, nkg, cos, sin)


def _transpose_on_mxu(x):
    n = x.shape[1]
    eye = lax.broadcasted_iota(jnp.int32, (n, n), 0) == lax.broadcasted_iota(jnp.int32, (n, n), 1)
    return lax.dot_general(jnp.where(eye, 1.0, 0.0).astype(BF16), x, _NT, preferred_element_type=F32)


def _fill_pair_kv_t(k, v, mask_a, mask_b, ka_ref, kb_ref, vat_ref, vbt_ref, r0):
    n = k.shape[0]
    if mask_a is None:
        ka_ref[r0:r0 + n, :] = k[:, :LANES]
        kb_ref[r0:r0 + n, :] = k[:, LANES:]
    else:
        ka_ref[r0:r0 + n, :] = k * mask_a
        kb_ref[r0:r0 + n, :] = k * mask_b
    vt = _transpose_on_mxu(v)
    first = lax.broadcasted_iota(jnp.int32, vt.shape, 0) < HEAD_DIM
    vat_ref[:, r0:r0 + n] = jnp.where(first, vt, 1.0).astype(BF16)
    vbt_ref[:, r0:r0 + n] = jnp.where(first, 1.0, vt).astype(BF16)


def _default_key_chunks(n_keys):
    return [(k0, min(ATTN_KEY_CHUNK, n_keys - k0), None) for k0 in range(0, n_keys, ATTN_KEY_CHUNK)]


def _pair_softmax_pv(groups, ka_ref, kb_ref, vat_ref, vbt_ref, emit):
    heads = ((ka_ref, vat_ref), (kb_ref, vbt_ref))
    jobs = [(g, j) for g, tiles in enumerate(groups) for j in range(len(tiles[0][1]))]
    queries = {}

    def scores(job):
        g, j = job
        if g not in queries:
            queries[g] = [load_q() for load_q, _, _ in groups[g]]
        out = []
        for (_, key_chunks, _), qs in zip(groups[g], queries[g]):
            k0, size, bias = key_chunks[j]
            for head, (q, (k_ref, _)) in enumerate(zip(qs, heads)):
                sc = lax.dot_general(k_ref[k0:k0 + size, :], q, _NT, preferred_element_type=F32)
                out.append((sc if bias is None else sc + bias(head)).astype(BF16))
        return out

    pending = [scores(job) for job in jobs[:ATTN_LOOKAHEAD]]
    for n, (g, j) in enumerate(jobs):
        if n + ATTN_LOOKAHEAD < len(jobs):
            pending.append(scores(jobs[n + ATTN_LOOKAHEAD]))
        s = pending.pop(0)
        if j == 0:
            run_max = [None] * len(s)
            acc = [None] * len(s)
        for t, (_, key_chunks, _) in enumerate(groups[g]):
            k0, size, _ = key_chunks[j]
            for head, (_, vt_ref) in enumerate(heads):
                c = 2 * t + head
                mx = s[c].max(axis=0, keepdims=True)
                if j > 0:
                    mx = jnp.maximum(mx, run_max[c])
                p = jnp.exp2(s[c] - mx)
                pv = jnp.dot(vt_ref[:, k0:k0 + size], p, preferred_element_type=F32)
                if j > 0:
                    acc[c] = acc[c] * jnp.exp2(run_max[c].astype(F32) - mx.astype(F32)) + pv
                else:
                    acc[c] = pv
                run_max[c] = mx
        if j == len(groups[g][0][1]) - 1:
            for t, (_, _, tag) in enumerate(groups[g]):
                oa, ob = acc[2 * t], acc[2 * t + 1]
                top = oa[:HEAD_DIM] / oa[HEAD_DIM:HEAD_DIM + 1]
                bot = ob[HEAD_DIM:] / ob[0:1]
                emit(tag, jnp.concatenate([top, bot], axis=0).T)


def _pair_attn_kernel(*refs, use_lat, steps_per_pair, dk, split):
    if use_lat:
        q_ref, ma_ref, mb_ref, kc_ref, vc_ref, kl_ref, vl_ref, o_ref, ka, kb, vat, vbt = refs
        segs = ((kc_ref, vc_ref), (kl_ref, vl_ref))
    else:
        q_ref, ma_ref, mb_ref, kc_ref, vc_ref, o_ref, ka, kb, vat, vbt = refs
        segs = ((kc_ref, vc_ref),)

    @pl.when(jnp.logical_and(pl.program_id(2) == 0, pl.program_id(1) % steps_per_pair == 0))
    def _():
        r0 = 0
        for k_ref, v_ref in segs:
            masks = (None, None) if split else (ma_ref[...], mb_ref[...])
            _fill_pair_kv_t(k_ref[0], v_ref[0], *masks, ka, kb, vat, vbt, r0)
            r0 += k_ref.shape[1]

    sub = min(q_ref.shape[1], ATTN_SUB)
    key_chunks = _default_key_chunks(ka.shape[0])

    def load_q(r0, blk):
        q = q_ref[0, r0:r0 + sub, blk * dk:(blk + 1) * dk]
        return (q[:, :LANES], q[:, LANES:]) if split else (q, q)

    def emit(tag, out):
        r0, blk = tag
        o_ref[0, r0:r0 + sub, blk * LANES:(blk + 1) * LANES] = out.astype(o_ref.dtype)

    tiles = [(functools.partial(load_q, r0, blk), key_chunks, (r0, blk))
             for blk in range(q_ref.shape[2] // dk) for r0 in range(0, q_ref.shape[1], sub)]
    groups = [tiles[i:i + ATTN_TILES_IN_FLIGHT] for i in range(0, len(tiles), ATTN_TILES_IN_FLIGHT)]
    _pair_softmax_pv(groups, ka, kb, vat, vbt, emit)


def _pair_attn(q, mask_a, mask_b, kc, vc, kl, vl, *, dk, n_pairs, q_per_pair, blocks_per_step, tq,
               name):
    b, t, _ = q.shape
    split = dk == 2 * LANES
    dk_head = dk // 2 if split else dk
    assert q_per_pair % blocks_per_step == 0
    steps_per_pair = q_per_pair // blocks_per_step
    n_qblk = n_pairs * steps_per_pair
    use_lat = kl is not None
    qmap = lambda bi, c, i: (bi, i, c)
    kvmap = lambda bi, c, i: (bi, 0, c // steps_per_pair)
    const = lambda bi, c, i: (0, 0)
    in_specs = [pl.BlockSpec((1, tq, blocks_per_step * dk), qmap),
                pl.BlockSpec((1, dk), const),
                pl.BlockSpec((1, dk), const),
                pl.BlockSpec((1, kc.shape[1], dk), kvmap),
                pl.BlockSpec((1, vc.shape[1], LANES), kvmap)]
    args = [q, mask_a, mask_b, kc, vc]
    if use_lat:
        in_specs += [pl.BlockSpec((1, kl.shape[1], dk), kvmap),
                     pl.BlockSpec((1, vl.shape[1], LANES), kvmap)]
        args += [kl, vl]
    n_keys = kc.shape[1] + (kl.shape[1] if use_lat else 0)
    return pl.pallas_call(
        functools.partial(_pair_attn_kernel, use_lat=use_lat, steps_per_pair=steps_per_pair, dk=dk,
                          split=split),
        grid=(b, n_qblk, t // tq),
        in_specs=in_specs,
        out_specs=pl.BlockSpec((1, tq, blocks_per_step * LANES), qmap),
        out_shape=jax.ShapeDtypeStruct((b, t, n_pairs * q_per_pair * LANES), BF16),
        scratch_shapes=[pltpu.VMEM((n_keys, dk_head), BF16), pltpu.VMEM((n_keys, dk_head), BF16),
                        pltpu.VMEM((LANES, n_keys), BF16), pltpu.VMEM((LANES, n_keys), BF16)],
        compiler_params=_params(3),
        name=name,
    )(*args)


def _na_block_tiles(u, n_blocks):
    t0 = min(max(u - 1, 0), n_blocks - NA_KEY_TILES)
    return t0, (0 if u == 0 else 2 if u == n_blocks - 1 else 1)


def _na_kernel(q_ref, kc_ref, vc_ref, k_ref, v_ref, band_ref, o_ref, ka, kb, vat, vbt, bias_ref,
               *, n_rows):
    j_idx, ok = _na_bias_plan(n_rows)

    @pl.when(pl.program_id(1) == 0)
    def _():
        masked = jnp.full((GRID_W, GRID_W), NEG_BIG, F32)
        for head, cls, ch, i, dr in np.ndindex(2, *j_idx.shape):
            half = (dr % 2) * GRID_W
            piece = (band_ref[head, int(j_idx[cls, ch, i, dr]), :, half:half + GRID_W]
                     if ok[cls, ch, i, dr] else masked)
            bias_ref[head, cls, ch, i * GRID_W:(i + 1) * GRID_W, dr * GRID_W:(dr + 1) * GRID_W] = piece

    tc = kc_ref.shape[1]
    lane = _lane((1, LANES))
    mask_a = jnp.where(lane < NA_HEAD_DIM, 1.0, 0.0).astype(BF16)
    mask_b = jnp.where(lane < NA_HEAD_DIM, 0.0, 1.0).astype(BF16)
    _fill_pair_kv_t(kc_ref[0], vc_ref[0], mask_a, mask_b, ka, kb, vat, vbt, 0)
    _fill_pair_kv_t(k_ref[0], v_ref[0], mask_a, mask_b, ka, kb, vat, vbt, tc)

    n_blocks = n_rows // NA_BLOCK_ROWS
    blk_q = NA_BLOCK_ROWS * GRID_W

    def load_q(u):
        q = q_ref[0, u * blk_q:(u + 1) * blk_q, :]
        return q, q

    def emit(u, out):
        o_ref[0, u * blk_q:(u + 1) * blk_q, :] = out.astype(o_ref.dtype)

    by_len = {}
    for u in range(n_blocks):
        t0, cls = _na_block_tiles(u, n_blocks)
        chunks = [(0, tc, None)]
        for ch in range(NA_KEY_TILES):
            if ok[cls, ch].any():
                chunks.append((tc + (t0 + ch) * blk_q, blk_q,
                               functools.partial(lambda head, cls, ch: bias_ref[head, cls, ch],
                                                 cls=cls, ch=ch)))
        by_len.setdefault(len(chunks), []).append((functools.partial(load_q, u), chunks, u))
    groups = [tiles[i:i + ATTN_TILES_IN_FLIGHT] for tiles in by_len.values()
              for i in range(0, len(tiles), ATTN_TILES_IN_FLIGHT)]
    _pair_softmax_pv(groups, ka, kb, vat, vbt, emit)


def _na_attn(q, kc, vc, k, v, band):
    b, s, _ = q.shape
    n_rows = s // GRID_W
    n_pairs = NA_HEADS // 2
    n_keys = kc.shape[1] + s
    blk_q = NA_BLOCK_ROWS * GRID_W
    assert kc.shape[1] == ATTN_KEY_CHUNK and blk_q == ATTN_KEY_CHUNK
    blk = lambda rows: pl.BlockSpec((1, rows, LANES), lambda m, bi: (bi, 0, m))
    return pl.pallas_call(
        functools.partial(_na_kernel, n_rows=n_rows),
        grid=(n_pairs, b),
        in_specs=[blk(s), blk(kc.shape[1]), blk(vc.shape[1]), blk(s), blk(s),
                  pl.BlockSpec((2,) + band.shape[1:], lambda m, bi: (m, 0, 0, 0))],
        out_specs=blk(s),
        out_shape=jax.ShapeDtypeStruct((b, s, NA_DIM), BF16),
        scratch_shapes=[pltpu.VMEM((n_keys, LANES), BF16), pltpu.VMEM((n_keys, LANES), BF16),
                        pltpu.VMEM((LANES, n_keys), BF16), pltpu.VMEM((LANES, n_keys), BF16),
                        pltpu.VMEM((2, 3, NA_KEY_TILES, blk_q, blk_q), F32)],
        compiler_params=_params(2),
        name="na_attn",
    )(q, kc, vc, k, v, band)


def _pool_kernel(a_ref, pw_ref, ps_ref, o_ref, pad_ref, *, t_len, chunk):
    pad_ref[0:POOL_HALO, :] = jnp.zeros((POOL_HALO, POOL_DIM), F32)
    pad_ref[POOL_HALO + t_len:2 * POOL_HALO + t_len, :] = jnp.zeros((POOL_HALO, POOL_DIM), F32)
    pad_ref[POOL_HALO:POOL_HALO + t_len, :] = a_ref[0]
    lane = _lane((chunk, POOL_DIM))
    g0 = lane < POOL_CH
    g1 = lane < 2 * POOL_CH
    g2 = lane < 3 * POOL_CH
    half_w = jnp.where(g0, 1, jnp.where(g1, 2, jnp.where(g2, 4, 8)))
    for c in range(t_len // chunk):
        base = c * chunk

        def ld(off, base=base):
            return pad_ref[POOL_HALO + base + off:POOL_HALO + base + off + chunk, :]

        a0 = ld(0)
        w2 = ld(-1) + a0
        w4 = w2 + ld(-2) + ld(1)
        w8 = w4 + ld(-4) + ld(-3) + ld(2) + ld(3)
        w16 = w8
        for off in (-8, -7, -6, -5, 4, 5, 6, 7):
            w16 = w16 + ld(off)
        tok = base + lax.broadcasted_iota(jnp.int32, (chunk, POOL_DIM), 0)
        cnt = (jnp.minimum(tok + half_w, t_len) - jnp.maximum(tok - half_w, 0)).astype(F32)
        wsum = jnp.where(g0, w2, jnp.where(g1, w4, jnp.where(g2, w8, w16)))
        dlt = wsum / cnt - a0
        y = jnp.dot(dlt.astype(BF16), pw_ref[...], preferred_element_type=F32) * ps_ref[...]
        o_ref[0, base:base + chunk, :] = y.astype(o_ref.dtype)


def _pool(a, pw_bd, pscale):
    b, t, _ = a.shape
    chunk = min(t, 256)
    return pl.pallas_call(
        functools.partial(_pool_kernel, t_len=t, chunk=chunk),
        grid=(b,),
        in_specs=[pl.BlockSpec((1, t, POOL_DIM), lambda bi: (bi, 0, 0)),
                  pl.BlockSpec((POOL_DIM, POOL_DIM), lambda bi: (0, 0)),
                  pl.BlockSpec((1, POOL_DIM), lambda bi: (0, 0))],
        out_specs=pl.BlockSpec((1, t, POOL_DIM), lambda bi: (bi, 0, 0)),
        out_shape=jax.ShapeDtypeStruct((b, t, POOL_DIM), BF16),
        scratch_shapes=[pltpu.VMEM((t + 2 * POOL_HALO, POOL_DIM), F32)],
        compiler_params=_params(1),
        name="pool",
    )(a, pw_bd, pscale)


def _post_kernel(x_ref, g1_ref, sh_ref, sc_ref, g2_ref, m0_ref, m1_ref, w0_ref, w1_ref, g_ref,
                 wg_ref, wu_ref, wd_ref, o_ref):
    n_rows = x_ref.shape[1]
    sub = min(n_rows, POST_SUB)
    tiles = [slice(r, r + sub) for r in range(0, n_rows, sub)]
    n_chunks = wg_ref.shape[1] // FFN_CHUNK

    mixes = [jnp.dot(m0_ref[0, rows, :], w0_ref[...], preferred_element_type=F32)
             + jnp.dot(m1_ref[0, rows, :], w1_ref[...], preferred_element_type=F32) for rows in tiles]
    xs = [x_ref[0, rows, :] + g1_ref[0, 0] * mix for rows, mix in zip(tiles, mixes)]
    hs = [_modulate(x, g_ref[...], sh_ref[0, 0], sc_ref[0, 0]).astype(BF16) for x in xs]

    def gate_up(c):
        cols = slice(c * FFN_CHUNK, (c + 1) * FFN_CHUNK)
        return [(jnp.dot(h, wg_ref[:, cols], preferred_element_type=F32),
                 jnp.dot(h, wu_ref[:, cols], preferred_element_type=F32)) for h in hs]

    accs = [None] * len(tiles)
    cur = gate_up(0)
    for c in range(n_chunks):
        nxt = gate_up(c + 1) if c + 1 < n_chunks else None
        for t, (gt, up) in enumerate(cur):
            act = (gt * jax.nn.sigmoid(gt) * up).astype(BF16)
            part = jnp.dot(act, wd_ref[c * FFN_CHUNK:(c + 1) * FFN_CHUNK, :],
                           preferred_element_type=F32)
            accs[t] = part if accs[t] is None else accs[t] + part
        cur = nxt
    for rows, x, acc in zip(tiles, xs, accs):
        o_ref[0, rows, :] = x + g2_ref[0, 0] * acc


def _mod_spec(mods, k, ctx_row):
    blk = (1, 1, 1, mods.shape[-1])
    if ctx_row is None:
        return pl.BlockSpec(blk, lambda bi, i: (bi, k, 0, 0))
    return pl.BlockSpec(blk, lambda bi, i: (ctx_row, k, 0, 0))


def _resident(a):
    nd = a.ndim
    return pl.BlockSpec(a.shape, lambda bi, i: (0,) * nd, pipeline_mode=pl.Buffered(1))


def _post(x, mods, ctx_row, m0, m1, w0, w1, gain, wg, wu, wd, *, tm, name):
    b, t, d = x.shape
    row = lambda bi, i: (bi, i, 0)
    return pl.pallas_call(
        _post_kernel,
        grid=(b, t // tm),
        in_specs=[pl.BlockSpec((1, tm, d), row),
                  _mod_spec(mods, 2, ctx_row), _mod_spec(mods, 3, ctx_row),
                  _mod_spec(mods, 4, ctx_row), _mod_spec(mods, 5, ctx_row),
                  pl.BlockSpec((1, tm, m0.shape[2]), row),
                  pl.BlockSpec((1, tm, m1.shape[2]), row),
                  _resident(w0), _resident(w1), _resident(gain),
                  _resident(wg), _resident(wu), _resident(wd)],
        out_specs=pl.BlockSpec((1, tm, d), row),
        out_shape=jax.ShapeDtypeStruct((b, t, d), F32),
        compiler_params=_params(2),
        name=name,
    )(x, mods, mods, mods, mods, m0, m1, w0, w1, gain, wg, wu, wd)


class _LaneLayout:
    def __init__(self, dim, slot, axis, freq, x2, n_freq):
        self.dim, self.slot, self.axis, self.freq, self.x2, self.n_freq = dim, slot, axis, freq, x2, n_freq

    def rope_tables(self, seq):
        tok = np.arange(seq)
        inv = (ROPE_THETA ** (-np.arange(self.n_freq, dtype=np.float32) * 2.0
                              / (2 * self.n_freq))).astype(np.float32).astype(np.float64)
        pos = np.stack([tok // GRID_W, tok % GRID_W]).astype(np.float64)
        rot = self.axis >= 0
        ang = pos[np.maximum(self.axis, 0)].T * inv[self.freq][None, :]
        cos = np.where(rot[None], np.cos(ang), 1.0)
        sin = np.where(rot[None], np.sin(ang) * np.where(self.x2, 1.0, -1.0)[None], 0.0)
        return jnp.asarray(cos, F32), jnp.asarray(sin, F32)


def _gqa_layout():
    lane = np.arange(LANES)
    x2, slot, axis, freq = lane // 64, (lane % 64) // 32, (lane % 32) // 16, lane % 16
    return _LaneLayout((axis * 2 + x2) * 16 + freq, slot, axis, freq, x2.astype(bool), HEAD_DIM // 4)


def _mla_layout():
    lane = np.arange(LANES)
    is_rot = (lane % 64) < 16
    x2 = lane >= 64
    axis = np.where(is_rot, (lane % 64) // 8, -1)
    freq = np.where(is_rot, lane % 8, 0)
    nope = np.where(lane < 64, lane - 16, 48 + lane - 80)
    dim = np.where(is_rot, MLA_NOPE + (np.maximum(axis, 0) * 2 + x2) * 8 + freq, nope)
    dim = np.where(lane >= MLA_QK, -1, dim)
    return _LaneLayout(dim, np.zeros(LANES, int), axis, freq, x2 & is_rot, MLA_ROPE // 4)


def _take_cols(w, idx):
    return jnp.where(jnp.asarray(idx >= 0)[None, :], w[:, np.maximum(idx, 0)], 0.0)


def _na_bias_plan(n_rows):
    n_blocks = n_rows // NA_BLOCK_ROWS
    j_idx = np.zeros((3, NA_KEY_TILES, NA_BLOCK_ROWS, NA_BLOCK_ROWS), np.int32)
    ok = np.zeros(j_idx.shape, bool)
    for cls, u in enumerate((0, 1, n_blocks - 1)):
        t0, cls_u = _na_block_tiles(u, n_blocks)
        assert cls_u == cls
        for ch in range(NA_KEY_TILES):
            for i in range(NA_BLOCK_ROWS):
                for dr in range(NA_BLOCK_ROWS):
                    r, kr = NA_BLOCK_ROWS * u + dr, NA_BLOCK_ROWS * (t0 + ch) + i
                    r0 = min(max(r - NA_ROWS // 2, 0), n_rows - NA_ROWS)
                    ok[cls, ch, i, dr] = r0 <= kr < r0 + NA_ROWS
                    j_idx[cls, ch, i, dr] = kr - r + NA_ROWS - 1
    return j_idx, ok


def _na_band_table(rpb):
    cols = np.arange(GRID_W)
    c0 = np.clip(cols - NA_COLS // 2, 0, GRID_W - NA_COLS)
    kc = np.arange(GRID_W)
    inside = (kc[None, :] >= c0[:, None]) & (kc[None, :] < c0[:, None] + NA_COLS)
    dc = kc[None, :] - cols[:, None] + (NA_COLS - 1)
    onehot = (np.arange(2 * NA_COLS - 1)[:, None, None] == dc[None]) & inside[None]
    sel = jnp.einsum("hjd,dck->hjkc", rpb * LOG2E, jnp.asarray(onehot, F32),
                     precision=lax.Precision.HIGHEST)
    band = jnp.where(jnp.asarray(inside.T)[None, None], sel, NEG_BIG).astype(F32)
    return jnp.concatenate([band, band], axis=-1)


def kernel(x, c, ctx, c_ctx, l0_ada_w, l0_ada_b, l0_norm_mix, l0_norm_ffn, l0_w_in, l0_pool_w, l0_pool_scale, l0_q_gain, l0_k_gain, l0_w_out, l0_ffn_w_gate, l0_ffn_w_up, l0_ffn_w_down, l1_ada_w, l1_ada_b, l1_norm_mix, l1_norm_ffn, l1_w_in, l1_mla_q_a_gain, l1_mla_kv_a_gain, l1_mla_w_uq, l1_mla_w_ukv, l1_mla_q_gain, l1_mla_k_gain, l1_na_q_gain, l1_na_k_gain, l1_na_rpb, l1_w_out, l1_ffn_w_gate, l1_ffn_w_up, l1_ffn_w_down):
    b, s, d = x.shape
    tc = ctx.shape[1]
    tm_lat = 1024
    tp_lat = 1024
    tq_lat = 2048

    cond = jnp.concatenate([c, c_ctx[None, :], jnp.zeros((7, d), F32)], axis=0)

    def mods(ada_w, ada_b):
        return _ada(cond, ada_w, ada_b).reshape(cond.shape[0], 6, 1, d)

    def ffn_weights(wg, wu, wd):
        return wg.astype(BF16), wu.astype(BF16), wd.astype(BF16)

    row1 = lambda v: v.reshape(1, -1)
    tile4 = lambda v, mul: (jnp.tile(v, SLAB // v.shape[0]) * mul).reshape(1, -1)
    sh64 = jnp.asarray(np.kron(np.eye(SLAB // HEAD_DIM), np.ones((HEAD_DIM, HEAD_DIM))), BF16)

    def slab_gain(g, lay, mul):
        per_lane = jnp.where(jnp.asarray(lay.dim >= 0), g[np.maximum(lay.dim, 0)], 0.0) * mul
        return jnp.tile(per_lane, SLAB // LANES).reshape(1, -1)

    def slab_same_head(lay):
        head = np.concatenate([np.where(lay.dim >= 0, blk * 2 + lay.slot, -1)
                               for blk in range(SLAB // LANES)])
        return jnp.asarray((head[:, None] == head[None, :]) & (head[:, None] >= 0), BF16)

    def pair_order(w_q, axis):
        shp = w_q.shape
        split = shp[:axis] + (GQA_KV_HEADS // 2, 2, GQA_GROUP, HEAD_DIM) + shp[axis + 1:]
        perm = list(range(len(split)))
        perm[axis + 1], perm[axis + 2] = perm[axis + 2], perm[axis + 1]
        return w_q.reshape(split).transpose(perm).reshape(shp)

    mods0 = mods(l0_ada_w, l0_ada_b)
    glay = _gqa_layout()
    q_cols = np.concatenate([((2 * m + glay.slot) * GQA_GROUP + g) * HEAD_DIM + glay.dim
                             for m in range(GQA_KV_HEADS // 2) for g in range(GQA_GROUP)])
    k_cols = np.concatenate([(2 * m + glay.slot) * HEAD_DIM + glay.dim
                             for m in range(GQA_KV_HEADS // 2)])
    q0, k0, v0 = POOL_DIM, POOL_DIM + GQA_Q_DIM, POOL_DIM + GQA_Q_DIM + GQA_KV_DIM
    w_in0 = l0_w_in[:, np.concatenate([np.arange(q0), q0 + q_cols, k0 + k_cols,
                                       np.arange(v0, l0_w_in.shape[1])])].astype(BF16)
    cos0, sin0 = glay.rope_tables(s)
    n_ctx = b * tc
    flat = lambda a: a.reshape(1, n_ctx, a.shape[-1])
    unflat = lambda a: a.reshape(b, tc, a.shape[-1])
    ones_c = jnp.ones((n_ctx, LANES), F32)
    zeros_c = jnp.zeros((n_ctx, LANES), F32)
    qg4 = slab_gain(l0_q_gain, glay, HEAD_DIM ** 0.5 * GQA_QSCALE)
    kg4 = slab_gain(l0_k_gain, glay, HEAD_DIM ** 0.5)
    sh_gqa = slab_same_head(glay)
    gmask_a = jnp.asarray((glay.slot == 0)[None], BF16)
    gmask_b = 1 - gmask_a

    a_l, q_l, k_l, v_l = _inproj0(x, mods0, None, row1(l0_norm_mix), w_in0, sh_gqa, qg4, kg4,
                                  cos0, sin0, rope=True, tm=tp_lat)
    a_c, q_c, k_c, v_c = map(unflat, _inproj0(flat(ctx), mods0, b, row1(l0_norm_mix), w_in0, sh_gqa,
                                               qg4, kg4, ones_c, zeros_c, rope=False,
                                               tm=min(n_ctx, tp_lat)))
    n_kv_pairs = GQA_KV_HEADS // 2
    attn_l = _pair_attn(q_l, gmask_a, gmask_b, k_c, v_c, k_l, v_l, dk=LANES, n_pairs=n_kv_pairs,
                        q_per_pair=GQA_GROUP, blocks_per_step=GQA_GROUP, tq=tq_lat, name="gqa_lat")
    attn_c = _pair_attn(q_c, gmask_a, gmask_b, k_c, v_c, None, None, dk=LANES, n_pairs=n_kv_pairs,
                        q_per_pair=GQA_GROUP, blocks_per_step=GQA_GROUP, tq=tc, name="gqa_ctx")

    eye = jnp.eye(POOL_GROUPS, dtype=F32)
    pw_bd = (eye[:, None, :, None] * l0_pool_w[:, :, None, :]).reshape(POOL_DIM, POOL_DIM).astype(BF16)
    pool_l = _pool(a_l, pw_bd, row1(l0_pool_scale))
    pool_c = _pool(a_c, pw_bd, row1(l0_pool_scale))

    w_out0_pool = l0_w_out[:POOL_DIM].astype(BF16)
    w_out0_attn = pair_order(l0_w_out[POOL_DIM:], 0).astype(BF16)
    ffn0 = ffn_weights(l0_ffn_w_gate, l0_ffn_w_up, l0_ffn_w_down)

    x1 = _post(x, mods0, None, pool_l, attn_l, w_out0_pool, w_out0_attn, row1(l0_norm_ffn), *ffn0,
               tm=tm_lat, name="post0_lat")
    xc = _post(flat(ctx), mods0, b, flat(pool_c), flat(attn_c), w_out0_pool, w_out0_attn,
               row1(l0_norm_ffn), *ffn0, tm=min(n_ctx, tm_lat), name="post0_ctx")

    mods1 = mods(l1_ada_w, l1_ada_b)
    mlay = _mla_layout()
    heads = np.arange(MLA_HEADS)[:, None]
    o_kr = MLA_Q_LORA + MLA_KV_LORA
    kr_cols = np.where(mlay.dim >= MLA_NOPE, o_kr + mlay.dim - MLA_NOPE, -1)
    w_in1 = jnp.concatenate([l1_w_in[:, :o_kr].astype(BF16), _take_cols(l1_w_in, kr_cols).astype(BF16),
                             l1_w_in[:, o_kr + MLA_ROPE:].astype(BF16)], axis=1)
    wuq = _take_cols(l1_mla_w_uq, np.where(mlay.dim >= 0, heads * MLA_QK + mlay.dim, -1).reshape(-1)
                     ).astype(BF16)
    kv_w = MLA_NOPE + MLA_V
    k_idx = np.where((mlay.dim >= 0) & (mlay.dim < MLA_NOPE), heads * kv_w + mlay.dim, -1).reshape(-1)
    v_idx = (heads * kv_w + MLA_NOPE + np.arange(MLA_V)[None, :]).reshape(-1)
    wukv = _take_cols(l1_mla_w_ukv, np.concatenate([k_idx, v_idx])).astype(BF16)

    shmla = slab_same_head(mlay)
    mqg = slab_gain(l1_mla_q_gain, mlay, MLA_QK ** 0.5 * MLA_QSCALE)
    mkg = slab_gain(l1_mla_k_gain, mlay, MLA_QK ** 0.5)
    nqg = tile4(l1_na_q_gain, NA_HEAD_DIM ** 0.5 * NA_QSCALE)
    nkg = tile4(l1_na_k_gain, NA_HEAD_DIM ** 0.5)
    cos1, sin1 = mlay.rope_tables(s)
    common = (row1(l1_norm_mix), w_in1, row1(l1_mla_q_a_gain), row1(l1_mla_kv_a_gain), wuq, wukv,
              sh64, shmla, mqg, mkg, nqg, nkg)
    mq, mk, mv, nq, nk, nv = _inproj1(x1, mods1, None, *common, cos1, sin1, rope=True, want_q=True,
                                      tm=tp_lat)
    mkc, mvc, nkc, nvc = map(unflat, _inproj1(xc, mods1, b, *common, ones_c, zeros_c, rope=False,
                                              want_q=False, tm=min(n_ctx, tp_lat)))

    no_mask = jnp.ones((1, MLA_PAIR_W), BF16)
    o_mla = _pair_attn(mq, no_mask, no_mask, mkc, mvc, mk, mv, dk=MLA_PAIR_W,
                       n_pairs=MLA_HEADS // 2, q_per_pair=1, blocks_per_step=1, tq=tq_lat, name="mla")
    o_na = _na_attn(nq, nkc, nvc, nk, nv, _na_band_table(l1_na_rpb))

    w_out1 = l1_w_out.astype(BF16)
    ffn1 = ffn_weights(l1_ffn_w_gate, l1_ffn_w_up, l1_ffn_w_down)
    return _post(x1, mods1, None, o_mla, o_na, w_out1[:MLA_HEADS * MLA_V], w_out1[MLA_HEADS * MLA_V:],
                 row1(l1_norm_ffn), *ffn1, tm=tm_lat, name="post1_lat")
```
